```python
import jax, jax.numpy as jnp
from jax import lax
import numpy as np

D_MODEL = 1024
BATCH = 2
SEQ = 16384
DEPTH = 1

ATT_HEAD_DIM = 64
ATT_HEADS = (D_MODEL // 2) // ATT_HEAD_DIM
ATT_W = ATT_HEADS * ATT_HEAD_DIM
IDX_HEADS = 4
IDX_DIM = 64
TOPK_MAX = 256
Q_BLOCK = 128
HG_KDIM = 128
HG_VDIM = 128
HG_HEADS = (D_MODEL // 2) // HG_VDIM
HG_W = HG_HEADS * HG_VDIM
HG_CHUNK = 64
MIX_W = ATT_W + HG_W
D_FF = 4 * D_MODEL
ROPE_THETA = 10000.0
ALPHA = (2.0 * DEPTH) ** 0.25
BETA = (8.0 * DEPTH) ** -0.25
LN_EPS = 1e-5
RMS_EPS = 1e-6
SPLIT_SIZES = (ATT_W, ATT_W, ATT_W,
               IDX_HEADS * IDX_DIM, IDX_DIM, IDX_HEADS,
               HG_HEADS * HG_KDIM, HG_HEADS * HG_KDIM,
               HG_W, HG_W)
N_IN_COLS = sum(SPLIT_SIZES)

kernel_name = "hymba_dsa_hgrn2_deepnorm"


def _split_points():
    pts, acc = [], 0
    for s in SPLIT_SIZES[:-1]:
        acc += s
        pts.append(acc)
    return pts


def layer_norm(x, g, b):
    xf = x.astype(jnp.float32)
    mu = jnp.mean(xf, axis=-1, keepdims=True)
    var = jnp.mean(jnp.square(xf - mu), axis=-1, keepdims=True)
    y = (xf - mu) * lax.rsqrt(var + LN_EPS) * g.astype(jnp.float32) + b.astype(jnp.float32)
    return y.astype(x.dtype)


def rope(x, pos):
    half = x.shape[-1] // 2
    inv = jnp.power(ROPE_THETA, -jnp.arange(half, dtype=jnp.float32) / half)
    ang = pos[:, None] * inv[None, :]
    cos = jnp.cos(ang)[None, :, None, :]
    sin = jnp.sin(ang)[None, :, None, :]
    xf = x.astype(jnp.float32)
    x1, x2 = xf[..., :half], xf[..., half:]
    return jnp.concatenate([x1 * cos - x2 * sin, x2 * cos + x1 * sin], axis=-1).astype(x.dtype)


def dsa_attention(q, k, v, iq, ik, iw):
    B, S, H, D = q.shape
    topk = min(TOPK_MAX, S // 4)
    n_blocks = S // Q_BLOCK
    kpos = jnp.arange(S, dtype=jnp.int32)
    bidx = jnp.arange(B, dtype=jnp.int32)[:, None, None]
    scale = D ** -0.5

    def block(i):
        start = i * Q_BLOCK
        qb = lax.dynamic_slice_in_dim(q, start, Q_BLOCK, axis=1)
        iqb = lax.dynamic_slice_in_dim(iq, start, Q_BLOCK, axis=1)
        iwb = lax.dynamic_slice_in_dim(iw, start, Q_BLOCK, axis=1)
        qpos = start + jnp.arange(Q_BLOCK, dtype=jnp.int32)
        logits = jnp.einsum('bqhd,bsd->bqhs', iqb, ik).astype(jnp.float32)
        score = jnp.einsum('bqh,bqhs->bqs', iwb.astype(jnp.float32), jax.nn.relu(logits))
        causal = kpos[None, :] <= qpos[:, None]
        score = jnp.where(causal[None], score, -jnp.inf)
        _, idx = lax.top_k(score, topk)
        valid = idx <= qpos[None, :, None]
        ks = k[bidx, idx]
        vs = v[bidx, idx]
        s = jnp.einsum('bqhd,bqkhd->bhqk', qb, ks).astype(jnp.float32) * scale
        s = jnp.where(valid[:, None], s, -jnp.inf)
        p = jax.nn.softmax(s, axis=-1).astype(v.dtype)
        return jnp.einsum('bhqk,bqkhd->bqhd', p, vs)

    out = lax.map(block, jnp.arange(n_blocks, dtype=jnp.int32))
    return out.transpose(1, 0, 2, 3, 4).reshape(B, S, H * D)


def hgrn2(q, f_logit, inp, gate, lb, norm_g):
    B, S, Hh, K = q.shape
    V = inp.shape[-1]
    C = HG_CHUNK
    N = S // C
    f = lb[None, None] + (1.0 - lb[None, None]) * jax.nn.sigmoid(f_logit.astype(jnp.float32))
    logf = jnp.log(f)
    kk = 1.0 - f

    def chunks(a):
        return a.reshape(B, N, C, Hh, a.shape[-1]).transpose(1, 0, 3, 2, 4)

    qc, kc, vc, lc = chunks(q.astype(jnp.float32)), chunks(kk), chunks(inp.astype(jnp.float32)), chunks(logf)
    tri = jnp.arange(C)[:, None] >= jnp.arange(C)[None, :]

    def step(state, xs):
        qh, kh, vh, lh = xs
        b = jnp.cumsum(lh, axis=-2)
        o_inter = jnp.einsum('bhck,bhkv->bhcv', qh * jnp.exp(b), state)
        diff = b[:, :, :, None, :] - b[:, :, None, :, :]
        decay = jnp.exp(jnp.where(tri[None, None, :, :, None], diff, -jnp.inf))
        a = jnp.einsum('bhtk,bhtsk,bhsk->bhts', qh, decay, kh)
        o = o_inter + jnp.einsum('bhts,bhsv->bhtv', a, vh)
        b_last = b[:, :, -1:, :]
        new_state = jnp.exp(b_last[:, :, 0, :])[..., None] * state + \
            jnp.einsum('bhsk,bhsv->bhkv', kh * jnp.exp(b_last - b), vh)
        return new_state, o

    s0 = jnp.zeros((B, Hh, K, V), jnp.float32)
    _, o = lax.scan(step, s0, (qc, kc, vc, lc))
    o = o.transpose(1, 0, 3, 2, 4).reshape(B, S, Hh, V)
    o = o * lax.rsqrt(jnp.mean(jnp.square(o), axis=-1, keepdims=True) + RMS_EPS) * norm_g.astype(jnp.float32)
    o = o * jax.nn.silu(gate.astype(jnp.float32))
    return o.reshape(B, S, Hh * V).astype(inp.dtype)


def setup_inputs(seed: int = 0) -> dict:
    key = jax.random.key(seed)
    ks = jax.random.split(key, 12)
    f32 = jnp.float32
    x = jax.random.normal(ks[0], (BATCH, SEQ, D_MODEL), f32)
    w_in = jax.random.normal(ks[1], (DEPTH, D_MODEL, N_IN_COLS), f32) * D_MODEL ** -0.5
    w_o = jax.random.normal(ks[2], (DEPTH, MIX_W, D_MODEL), f32) * (MIX_W ** -0.5 * BETA)
    lb_logits = jax.random.normal(ks[3], (DEPTH + 1, HG_HEADS, HG_KDIM), f32) * 0.5
    hg_norm_g = 1.0 + 0.02 * jax.random.normal(ks[4], (DEPTH, HG_HEADS, HG_VDIM), f32)
    ln1_g = 1.0 + 0.02 * jax.random.normal(ks[5], (DEPTH, D_MODEL), f32)
    ln1_b = 0.02 * jax.random.normal(ks[6], (DEPTH, D_MODEL), f32)
    w_up = jax.random.normal(ks[7], (DEPTH, D_MODEL, D_FF), f32) * D_MODEL ** -0.5
    w_down = jax.random.normal(ks[8], (DEPTH, D_FF, D_MODEL), f32) * (D_FF ** -0.5 * BETA)
    ln2_g = 1.0 + 0.02 * jax.random.normal(ks[9], (DEPTH, D_MODEL), f32)
    ln2_b = 0.02 * jax.random.normal(ks[10], (DEPTH, D_MODEL), f32)
    return {"x": x, "w_in": w_in, "w_o": w_o, "lb_logits": lb_logits, "hg_norm_g": hg_norm_g,
            "ln1_g": ln1_g, "ln1_b": ln1_b, "w_up": w_up, "w_down": w_down,
            "ln2_g": ln2_g, "ln2_b": ln2_b}


def reference(x, w_in, w_o, lb_logits, hg_norm_g, ln1_g, ln1_b, w_up, w_down, ln2_g, ln2_b):
    B, S, _ = x.shape
    pos = jnp.arange(S, dtype=jnp.float32)
    lower_bounds = jnp.cumsum(jax.nn.softmax(lb_logits.astype(jnp.float32), axis=0), axis=0)
    idx_w_scale = (IDX_HEADS ** -0.5) * (IDX_DIM ** -0.5)
    for l in range(DEPTH):
        proj = x @ w_in[l]
        q, k, v, iq, ik, iw, hq, hf, hi, hg = jnp.split(proj, _split_points(), axis=-1)
        q = rope(q.reshape(B, S, ATT_HEADS, ATT_HEAD_DIM), pos)
        k = rope(k.reshape(B, S, ATT_HEADS, ATT_HEAD_DIM), pos)
        v = v.reshape(B, S, ATT_HEADS, ATT_HEAD_DIM)
        iq = rope(iq.reshape(B, S, IDX_HEADS, IDX_DIM), pos)
        ik = rope(ik.reshape(B, S, 1, IDX_DIM), pos)[:, :, 0, :]
        att = dsa_attention(q, k, v, iq, ik, iw * idx_w_scale)
        hgo = hgrn2(hq.reshape(B, S, HG_HEADS, HG_KDIM),
                    hf.reshape(B, S, HG_HEADS, HG_KDIM),
                    hi.reshape(B, S, HG_HEADS, HG_VDIM),
                    hg.reshape(B, S, HG_HEADS, HG_VDIM),
                    lower_bounds[l], hg_norm_g[l])
        mix = jnp.concatenate([att, hgo], axis=-1) @ w_o[l]
        x = layer_norm(ALPHA * x + mix, ln1_g[l], ln1_b[l])
        h = jnp.square(jax.nn.relu(x @ w_up[l])) @ w_down[l]
        x = layer_norm(ALPHA * x + h, ln2_g[l], ln2_b[l])
    return x
```

```python
import functools

import numpy as np
import jax
import jax.numpy as jnp
from jax import lax
from jax.experimental import pallas as pl
from jax.experimental.pallas import tpu as pltpu

ATT_HEAD_DIM = 64
ATT_HEADS = 8
ATT_W = ATT_HEADS * ATT_HEAD_DIM
IDX_HEADS = 4
IDX_DIM = 64
IDX_W = IDX_HEADS * IDX_DIM
TOPK_MAX = 256
HG_KDIM = 128
HG_HEADS = 4
HG_W = HG_HEADS * HG_KDIM
ROPE_THETA = 10000.0
LN_EPS = 1e-5
RMS_EPS = 1e-6

LANES = 128
VMEM_LIMIT_BYTES = 56 * 1024 * 1024

PROJ_ROWS = 256
DSA_QB = 128
DSA_KB = 512
HG_ROWS = 256
HG_CHUNK = 64
HG_SUB = 16
FFN_ROWS = 256
FFN_COLS = 1024

_F32 = jnp.float32
_BF16 = jnp.bfloat16
_I32 = jnp.int32
_INT_MIN = -(2 ** 31)
_NEG = -1e30


def _nt_dot(a, b):
    return lax.dot_general(a, b, (((1,), (1,)), ((), ())), preferred_element_type=_F32)


def _tn_dot(a, b):
    return lax.dot_general(a, b, (((0,), (0,)), ((), ())), preferred_element_type=_F32)


def _dot(a, b):
    return jnp.dot(a, b, preferred_element_type=_F32)


def _rope_group(z, cos, sin_signed):
    lane = lax.broadcasted_iota(_I32, z.shape, 1)
    first_half = (lane % ATT_HEAD_DIM) < (ATT_HEAD_DIM // 2)
    upper = pltpu.roll(z, LANES - ATT_HEAD_DIM // 2, 1)
    lower = pltpu.roll(z, ATT_HEAD_DIM // 2, 1)
    return z * cos + jnp.where(first_half, upper, lower) * sin_signed


def _proj_kernel(x_ref, wa_ref, wh_ref, cos_ref, sin_ref,
                 q_ref, k_ref, v_ref, iq_ref, ik_ref, iw_ref,
                 hq_ref, hf_ref, hi_ref, hg_ref, *, q_scale, iw_scale):
    xb = x_ref[...].astype(_BF16)
    cos = cos_ref[...]
    sin = sin_ref[...]
    pa = _dot(xb, wa_ref[...])

    def roped(col0, width):
        return [_rope_group(pa[:, col0 + g * LANES: col0 + (g + 1) * LANES], cos, sin)
                for g in range(width // LANES)]

    for g, z in enumerate(roped(0, ATT_W)):
        q_ref[:, g * LANES:(g + 1) * LANES] = (z * q_scale).astype(_BF16)
    for g, z in enumerate(roped(ATT_W, ATT_W)):
        k_ref[:, g * LANES:(g + 1) * LANES] = z.astype(_BF16)
    v_ref[...] = pa[:, 2 * ATT_W:3 * ATT_W].astype(_BF16)
    for g, z in enumerate(roped(3 * ATT_W, IDX_W)):
        iq_ref[:, g * LANES:(g + 1) * LANES] = z.astype(_BF16)
    ik = roped(3 * ATT_W + IDX_W, LANES)[0]
    ik_ref[...] = ik[:, :IDX_DIM].astype(_BF16)
    iw_ref[...] = pa[:, 3 * ATT_W + IDX_W + LANES:] * iw_scale

    ph = _dot(xb, wh_ref[...])
    hq_ref[...] = ph[:, 0 * HG_W:1 * HG_W]
    hf_ref[...] = ph[:, 1 * HG_W:2 * HG_W]
    hi_ref[...] = ph[:, 2 * HG_W:3 * HG_W]
    hg_ref[...] = ph[:, 3 * HG_W:4 * HG_W]


def _rope_tables(seq):
    half = ATT_HEAD_DIM // 2
    inv = np.power(np.float64(ROPE_THETA), -np.arange(half, dtype=np.float64) / half)
    ang = np.arange(seq, dtype=np.float64)[:, None] * inv[None, :]
    cos = np.cos(ang)
    sin = np.sin(ang)
    cos_t = np.tile(np.concatenate([cos, cos], axis=1), (1, LANES // ATT_HEAD_DIM))
    sin_t = np.tile(np.concatenate([-sin, sin], axis=1), (1, LANES // ATT_HEAD_DIM))
    return jnp.asarray(cos_t, _F32), jnp.asarray(sin_t, _F32)


def _project(x2d, w_in, seq):
    rows, d = x2d.shape
    c = 3 * ATT_W + IDX_W
    pad = lambda w, n: jnp.pad(w, ((0, 0), (0, n - w.shape[1])))
    wa = jnp.concatenate([w_in[:, :c],
                          pad(w_in[:, c:c + IDX_DIM], LANES),
                          pad(w_in[:, c + IDX_DIM:c + IDX_DIM + IDX_HEADS], LANES)],
                         axis=1).astype(_BF16)
    wh = w_in[:, c + IDX_DIM + IDX_HEADS:].astype(_BF16)
    cos_t, sin_t = _rope_tables(seq)
    tm = PROJ_ROWS
    n_seq_blocks = seq // tm
    row_spec = lambda w: pl.BlockSpec((tm, w), lambda i: (i, 0))
    full_spec = lambda a: pl.BlockSpec(a.shape, lambda i: (0, 0))
    pos_spec = pl.BlockSpec((tm, LANES), lambda i: (i % n_seq_blocks, 0))
    out_shapes = [
        jax.ShapeDtypeStruct((rows, ATT_W), _BF16),
        jax.ShapeDtypeStruct((rows, ATT_W), _BF16),
        jax.ShapeDtypeStruct((rows, ATT_W), _BF16),
        jax.ShapeDtypeStruct((rows, IDX_W), _BF16),
        jax.ShapeDtypeStruct((rows, IDX_DIM), _BF16),
        jax.ShapeDtypeStruct((rows, LANES), _F32),
    ] + [jax.ShapeDtypeStruct((rows, HG_W), _F32)] * 4
    out_specs = [row_spec(s.shape[1]) for s in out_shapes]
    kern = functools.partial(_proj_kernel, q_scale=ATT_HEAD_DIM ** -0.5,
                             iw_scale=(IDX_HEADS ** -0.5) * (IDX_DIM ** -0.5))
    return pl.pallas_call(
        kern,
        grid=(rows // tm,),
        in_specs=[row_spec(d), full_spec(wa), full_spec(wh), pos_spec, pos_spec],
        out_specs=out_specs,
        out_shape=out_shapes,
        compiler_params=pltpu.CompilerParams(
            dimension_semantics=("arbitrary",), vmem_limit_bytes=VMEM_LIMIT_BYTES),
    )(x2d, wa, wh, cos_t, sin_t)


def _dsa_kernel(qi_ref, kj_ref,
                q_ref, iq_ref, iw_ref, ik_ref, k_ref, v_ref,
                o_ref,
                keys_scr, thr_scr, rem_scr, tie_scr, m_scr, l_scr, acc_scr, *, topk):
    p = pl.program_id(1)
    i = qi_ref[p]
    j = kj_ref[p]
    qb, kb = DSA_QB, DSA_KB
    n_kb = (i * qb + qb - 1) // kb + 1

    @pl.when(j == 0)
    def _select():
        iw = iw_ref[0]
        qpos = i * qb + lax.broadcasted_iota(_I32, (qb, kb), 0)

        def score_block(c, carry):
            col0 = pl.multiple_of(c * kb, kb)
            ikc = ik_ref[0, pl.ds(col0, kb), :]
            score = jnp.zeros((qb, kb), _F32)
            for h in range(IDX_HEADS):
                logits = _nt_dot(iq_ref[0, :, h * IDX_DIM:(h + 1) * IDX_DIM], ikc)
                score = score + iw[:, h:h + 1] * jnp.maximum(logits, 0.0)
            bits = lax.bitcast_convert_type(score, _I32)
            key = bits ^ ((bits >> 31) & 0x7FFFFFFF)
            kpos = col0 + lax.broadcasted_iota(_I32, (qb, kb), 1)
            keys_scr[c] = jnp.where(kpos <= qpos, key, _INT_MIN)
            return carry

        lax.fori_loop(0, n_kb, score_block, 0)

        def count(pred):
            def body(c, acc):
                hit = pred(keys_scr[c])
                for g in range(kb // LANES):
                    acc = acc + jnp.where(hit[:, g * LANES:(g + 1) * LANES], 1.0, 0.0)
                return acc
            acc = lax.fori_loop(0, n_kb, body, jnp.zeros((qb, LANES), _F32))
            return jnp.sum(acc, axis=1, keepdims=True)

        def bisect(step, thr):
            bit = jnp.left_shift(jnp.int32(1), 31 - step)
            cand = jnp.where(step == 0, 0, thr | bit)
            ok = count(lambda kc: kc >= cand) >= topk
            return jnp.where(ok, cand, thr)

        thr = lax.fori_loop(0, 32, bisect, jnp.full((qb, 1), _INT_MIN, _I32))
        thr = jnp.maximum(thr, _INT_MIN + 1)
        n_gt = count(lambda kc: kc > thr)
        thr_scr[...] = thr
        rem_scr[...] = topk - n_gt
        tie_scr[...] = jnp.zeros_like(tie_scr)
        m_scr[...] = jnp.full_like(m_scr, _NEG)
        l_scr[...] = jnp.zeros_like(l_scr)
        acc_scr[...] = jnp.zeros_like(acc_scr)

    keys = keys_scr[j]
    thr = thr_scr[...]
    tie = keys == thr
    tie_b = jnp.where(tie, 1.0, 0.0).astype(_BF16)
    row = lax.broadcasted_iota(_I32, (kb, kb), 0)
    col = lax.broadcasted_iota(_I32, (kb, kb), 1)
    before = jnp.where(row < col, 1.0, 0.0).astype(_BF16)
    ties_before = tie_scr[...] + _dot(tie_b, before)
    mask = (keys > thr) | (tie & (ties_before < rem_scr[...]))
    tie_scr[...] = ties_before[:, kb - 1:kb] + jnp.where(tie[:, kb - 1:kb], 1.0, 0.0)

    kblk = k_ref[0]
    vblk = v_ref[0]
    lane = lax.broadcasted_iota(_I32, (qb, LANES), 1)
    low_head = lane < ATT_HEAD_DIM
    for pair in range(ATT_HEADS // 2):
        pv = []
        alphas = []
        for sub in range(2):
            h = 2 * pair + sub
            hs = slice(h * ATT_HEAD_DIM, (h + 1) * ATT_HEAD_DIM)
            s = _nt_dot(q_ref[0, :, hs], kblk[:, hs])
            s = jnp.where(mask, s, _NEG)
            m_old = m_scr[h]
            m_new = jnp.maximum(m_old, jnp.max(s, axis=1, keepdims=True))
            alpha = jnp.exp(m_old - m_new)
            pr = jnp.exp(s - m_new)
            l_scr[h] = alpha * l_scr[h] + jnp.sum(pr, axis=1, keepdims=True)
            m_scr[h] = m_new
            pv.append(_dot(pr.astype(_BF16), vblk[:, pair * LANES:(pair + 1) * LANES]))
            alphas.append(alpha)
        ps = slice(pair * LANES, (pair + 1) * LANES)
        acc_scr[:, ps] = (jnp.where(low_head, alphas[0], alphas[1]) * acc_scr[:, ps]
                          + jnp.where(low_head, pv[0], pv[1]))

    @pl.when(j == n_kb - 1)
    def _finish():
        for pair in range(ATT_HEADS // 2):
            ps = slice(pair * LANES, (pair + 1) * LANES)
            denom = jnp.where(low_head, l_scr[2 * pair], l_scr[2 * pair + 1])
            o_ref[0, :, ps] = (acc_scr[:, ps] / denom).astype(o_ref.dtype)


def _dsa_attention(q, k, v, iq, ik, iw):
    bsz, seq, _ = q.shape
    qb, kb = DSA_QB, DSA_KB
    topk = min(TOPK_MAX, seq // 4)
    n_qb = seq // qb
    pairs = [(i, j) for i in range(n_qb) for j in range((i * qb + qb - 1) // kb + 1)]
    qi = jnp.asarray(np.array([p[0] for p in pairs], np.int32))
    kj = jnp.asarray(np.array([p[1] for p in pairs], np.int32))
    q_map = lambda b, p, qi_r, kj_r: (b, qi_r[p], 0)
    k_map = lambda b, p, qi_r, kj_r: (b, kj_r[p], 0)
    all_map = lambda b, p, qi_r, kj_r: (b, 0, 0)
    grid_spec = pltpu.PrefetchScalarGridSpec(
        num_scalar_prefetch=2,
        grid=(bsz, len(pairs)),
        in_specs=[
            pl.BlockSpec((1, qb, ATT_W), q_map),
            pl.BlockSpec((1, qb, IDX_W), q_map),
            pl.BlockSpec((1, qb, LANES), q_map),
            pl.BlockSpec((1, seq, IDX_DIM), all_map),
            pl.BlockSpec((1, kb, ATT_W), k_map),
            pl.BlockSpec((1, kb, ATT_W), k_map),
        ],
        out_specs=pl.BlockSpec((1, qb, ATT_W), q_map),
        scratch_shapes=[
            pltpu.VMEM((seq // kb, qb, kb), _I32),
            pltpu.VMEM((qb, 1), _I32),
            pltpu.VMEM((qb, 1), _F32),
            pltpu.VMEM((qb, 1), _F32),
            pltpu.VMEM((ATT_HEADS, qb, 1), _F32),
            pltpu.VMEM((ATT_HEADS, qb, 1), _F32),
            pltpu.VMEM((qb, ATT_W), _F32),
        ],
    )
    return pl.pallas_call(
        functools.partial(_dsa_kernel, topk=topk),
        grid_spec=grid_spec,
        out_shape=jax.ShapeDtypeStruct((bsz, seq, ATT_W), _BF16),
        compiler_params=pltpu.CompilerParams(
            dimension_semantics=("arbitrary", "arbitrary"), vmem_limit_bytes=VMEM_LIMIT_BYTES),
    )(qi, kj, q, iq, iw, ik, k, v)


def _split3(a):
    hi = a.astype(_BF16)
    r1 = a - hi.astype(_F32)
    mid = r1.astype(_BF16)
    lo = (r1 - mid.astype(_F32)).astype(_BF16)
    return hi, mid, lo


def _hgrn_kernel(lbl_ref, g_ref, hq_ref, hf_ref, hi_ref, hg_ref, o_ref,
                 state_scr, kk_scr, b_scr, o_scr, *, layer):
    rows, ch, sb = HG_ROWS, HG_CHUNK, HG_SUB
    n_sub = ch // sb

    @pl.when(pl.program_id(1) == 0)
    def _reset():
        state_scr[...] = jnp.zeros_like(state_scr)

    lbl = lbl_ref[...]
    e = jnp.exp(lbl - jnp.max(lbl, axis=0, keepdims=True))
    lb = jnp.sum(e[:layer + 1], axis=0, keepdims=True) / jnp.sum(e, axis=0, keepdims=True)

    f = lb + (1.0 - lb) * jax.nn.sigmoid(hf_ref[0])
    kk_scr[...] = 1.0 - f
    logf = jnp.log(f)
    r_i = lax.broadcasted_iota(_I32, (ch, ch), 0)
    c_i = lax.broadcasted_iota(_I32, (ch, ch), 1)
    lower = jnp.where(c_i <= r_i, 1.0, 0.0).astype(_BF16)
    for c in range(rows // ch):
        parts = _split3(logf[c * ch:(c + 1) * ch])
        b_scr[c * ch:(c + 1) * ch, :] = sum(_dot(lower, part) for part in parts)

    t_idx = lax.broadcasted_iota(_I32, (sb, 1), 0)
    row_idx = lax.broadcasted_iota(_I32, (ch, 1), 0)

    def chunk(c, carry):
        r0 = pl.multiple_of(c * ch, ch)
        cs = pl.ds(r0, ch)
        b = b_scr[cs, :]
        kk = kk_scr[cs, :]
        qv = hq_ref[0, cs, :]
        vv = hi_ref[0, cs, :]
        b_last = b[ch - 1:ch]
        q_in = (qv * jnp.exp(b)).astype(_BF16)
        k_out = kk * jnp.exp(b_last - b)
        vb = vv.astype(_BF16)

        a_rows = [jnp.zeros((sb, ch), _F32)] * HG_HEADS
        a_off = [[a_rows[h]] for h in range(HG_HEADS)]
        for s_i in range(1, n_sub):
            ref_b = b[s_i * sb - 1:s_i * sb]
            q_s = (qv[s_i * sb:(s_i + 1) * sb] * jnp.exp(b[s_i * sb:(s_i + 1) * sb] - ref_b))
            k_s = jnp.where(row_idx < s_i * sb, kk * jnp.exp(jnp.minimum(ref_b - b, 0.0)), 0.0)
            q_s = q_s.astype(_BF16)
            k_s = k_s.astype(_BF16)
            for h in range(HG_HEADS):
                hs = slice(h * HG_KDIM, (h + 1) * HG_KDIM)
                a_off[h].append(_nt_dot(q_s[:, hs], k_s[:, hs]))

        for h in range(HG_HEADS):
            hs = slice(h * HG_KDIM, (h + 1) * HG_KDIM)
            st = state_scr[h]
            o_h = _nt_dot(q_in[:, hs], st.astype(_BF16))
            a_h = jnp.concatenate(a_off[h], axis=0).astype(_BF16)
            o_scr[cs, hs] = o_h + _dot(a_h, vb[:, hs])
            state_scr[h] = (st * jnp.exp(b_last[:, hs])
                            + _tn_dot(vb[:, hs], k_out[:, hs].astype(_BF16)))

        for s_i in range(n_sub):
            rs = pl.ds(r0 + s_i * sb, sb)
            q_s = hq_ref[0, rs, :]
            b_s = b_scr[rs, :]
            acc = o_scr[rs, :]
            for t in range(sb):
                one = pl.ds(r0 + s_i * sb + t, 1)
                w = q_s * jnp.exp(jnp.minimum(b_s - b_scr[one, :], 0.0)) * kk_scr[one, :]
                v_row = hi_ref[0, one, :]
                parts = []
                for h in range(HG_HEADS):
                    hs = slice(h * HG_KDIM, (h + 1) * HG_KDIM)
                    a = jnp.sum(w[:, hs], axis=1, keepdims=True)
                    parts.append(jnp.where(t_idx >= t, a, 0.0) * v_row[:, hs])
                acc = acc + jnp.concatenate(parts, axis=1)
            o_scr[rs, :] = acc
        return carry

    lax.fori_loop(0, rows // ch, chunk, 0)

    o = o_scr[...]
    gate = hg_ref[0]
    gain = g_ref[...]
    for h in range(HG_HEADS):
        hs = slice(h * HG_KDIM, (h + 1) * HG_KDIM)
        oh = o[:, hs]
        oh = oh * lax.rsqrt(jnp.mean(oh * oh, axis=1, keepdims=True) + RMS_EPS) * gain[:, hs]
        gh = gate[:, hs]
        o_ref[0, :, hs] = (oh * (gh * jax.nn.sigmoid(gh))).astype(o_ref.dtype)


def _hgrn2(hq, hf, hi, hg, lb_logits, norm_g, layer):
    bsz, seq, _ = hq.shape
    rows = HG_ROWS
    blk = pl.BlockSpec((1, rows, HG_W), lambda b, t: (b, t, 0))
    lbl = lb_logits.reshape(lb_logits.shape[0], HG_W)
    gain = norm_g.reshape(1, HG_W)
    return pl.pallas_call(
        functools.partial(_hgrn_kernel, layer=layer),
        grid=(bsz, seq // rows),
        in_specs=[pl.BlockSpec(lbl.shape, lambda b, t: (0, 0)),
                  pl.BlockSpec(gain.shape, lambda b, t: (0, 0)),
                  blk, blk, blk, blk],
        out_specs=blk,
        out_shape=jax.ShapeDtypeStruct((bsz, seq, HG_W), _BF16),
        scratch_shapes=[
            pltpu.VMEM((HG_HEADS, HG_KDIM, HG_KDIM), _F32),
            pltpu.VMEM((rows, HG_W), _F32),
            pltpu.VMEM((rows, HG_W), _F32),
            pltpu.VMEM((rows, HG_W), _F32),
        ],
        compiler_params=pltpu.CompilerParams(
            dimension_semantics=("arbitrary", "arbitrary"), vmem_limit_bytes=VMEM_LIMIT_BYTES),
    )(lbl, gain, hq, hf, hi, hg)


def _layer_norm(y, g, b):
    mu = jnp.mean(y, axis=1, keepdims=True)
    yc = y - mu
    var = jnp.mean(yc * yc, axis=1, keepdims=True)
    return yc * lax.rsqrt(var + LN_EPS) * g + b


def _ffn_kernel(x_ref, att_ref, hgo_ref, wo_ref, g1_ref, b1_ref, wu_ref, wd_ref,
                g2_ref, b2_ref, o_ref, *, alpha):
    mix = (_dot(att_ref[...], wo_ref[:ATT_W, :]) + _dot(hgo_ref[...], wo_ref[ATT_W:, :]))
    y1 = _layer_norm(alpha * x_ref[...] + mix, g1_ref[...], b1_ref[...])
    y1b = y1.astype(_BF16)
    h = jnp.zeros_like(y1)
    for c in range(wu_ref.shape[1] // FFN_COLS):
        cs = slice(c * FFN_COLS, (c + 1) * FFN_COLS)
        u = jnp.maximum(_dot(y1b, wu_ref[:, cs]), 0.0)
        h = h + _dot((u * u).astype(_BF16), wd_ref[cs, :])
    o_ref[...] = _layer_norm(alpha * y1 + h, g2_ref[...], b2_ref[...])


def _out_ffn(x2d, att2d, hgo2d, w_o, g1, b1, w_up, w_down, g2, b2, alpha):
    rows, d = x2d.shape
    tm = FFN_ROWS
    row_spec = lambda w: pl.BlockSpec((tm, w), lambda i: (i, 0))
    full_spec = lambda a: pl.BlockSpec(a.shape, lambda i: (0, 0))
    vec = lambda a: a.reshape(1, d).astype(_F32)
    args = (x2d, att2d, hgo2d, w_o.astype(_BF16), vec(g1), vec(b1),
            w_up.astype(_BF16), w_down.astype(_BF16), vec(g2), vec(b2))
    in_specs = [row_spec(d), row_spec(ATT_W), row_spec(HG_W)] + [full_spec(a) for a in args[3:]]
    return pl.pallas_call(
        functools.partial(_ffn_kernel, alpha=alpha),
        grid=(rows // tm,),
        in_specs=in_specs,
        out_specs=row_spec(d),
        out_shape=jax.ShapeDtypeStruct((rows, d), _F32),
        compiler_params=pltpu.CompilerParams(
            dimension_semantics=("arbitrary",), vmem_limit_bytes=VMEM_LIMIT_BYTES),
    )(*args)


def kernel(x, w_in, w_o, lb_logits, hg_norm_g, ln1_g, ln1_b, w_up, w_down, ln2_g, ln2_b):
    bsz, seq, d = x.shape
    depth = w_in.shape[0]
    alpha = (2.0 * depth) ** 0.25
    x2d = x.reshape(bsz * seq, d)
    for l in range(depth):
        q, k, v, iq, ik, iw, hq, hf, hi, hg = _project(x2d, w_in[l], seq)
        r3 = lambda a: a.reshape(bsz, seq, a.shape[-1])
        att = _dsa_attention(r3(q), r3(k), r3(v), r3(iq), r3(ik), r3(iw))
        hgo = _hgrn2(r3(hq), r3(hf), r3(hi), r3(hg), lb_logits, hg_norm_g[l], l)
        x2d = _out_ffn(x2d, att.reshape(bsz * seq, ATT_W), hgo.reshape(bsz * seq, HG_W),
                       w_o[l], ln1_g[l], ln1_b[l], w_up[l], w_down[l], ln2_g[l], ln2_b[l], alpha)
    return x2d.reshape(bsz, seq, d)
```

```python
import functools
import math

import numpy as np
import jax
import jax.numpy as jnp
from jax import lax
from jax.experimental import pallas as pl
from jax.experimental.pallas import tpu as pltpu

ATT_HEAD_DIM = 64
ATT_HEADS = 8
ATT_W = ATT_HEADS * ATT_HEAD_DIM
IDX_HEADS = 4
IDX_DIM = 64
IDX_W = IDX_HEADS * IDX_DIM
TOPK_MAX = 256
HG_KDIM = 128
HG_HEADS = 4
HG_W = HG_HEADS * HG_KDIM
ROPE_THETA = 10000.0
LN_EPS = 1e-5
RMS_EPS = 1e-6

LANES = 128
SUBLANES = 8
BF16_ROWS = 16
VMEM_LIMIT_BYTES = 56 * 1024 * 1024

PROJ_ROWS = 256
DSA_QB = 256
DSA_KB = 512
V_ROWS = ATT_HEAD_DIM + BF16_ROWS
HG_ROWS = 256
HG_CHUNK = 64
HG_SUB = 16
FFN_ROWS = 256
FFN_COLS = 1024

_F32 = jnp.float32
_BF16 = jnp.bfloat16
_I32 = jnp.int32
_INT_MIN = -(2 ** 31)
_NEG = -1e30


def _nt_dot(a, b):
    return lax.dot_general(a, b, (((1,), (1,)), ((), ())), preferred_element_type=_F32)


def _tn_dot(a, b):
    return lax.dot_general(a, b, (((0,), (0,)), ((), ())), preferred_element_type=_F32)


def _dot(a, b):
    return jnp.dot(a, b, preferred_element_type=_F32)


def _tree_sum(parts):
    while len(parts) > 1:
        parts = [parts[n] + parts[n + 1] for n in range(0, len(parts) - 1, 2)] + (
            [parts[-1]] if len(parts) % 2 else [])
    return parts[0]


def _rope_group(z, cos, sin_signed):
    lane = lax.broadcasted_iota(_I32, z.shape, 1)
    first_half = (lane % ATT_HEAD_DIM) < (ATT_HEAD_DIM // 2)
    upper = pltpu.roll(z, LANES - ATT_HEAD_DIM // 2, 1)
    lower = pltpu.roll(z, ATT_HEAD_DIM // 2, 1)
    return z * cos + jnp.where(first_half, upper, lower) * sin_signed


def _proj_kernel(x_ref, wa_ref, wh_ref, cos_ref, sin_ref,
                 q_ref, k_ref, v_ref, iq_ref, ik_ref, iw_ref,
                 hq_ref, hf_ref, hi_ref, hg_ref, *, q_scale, iw_scale):
    xb = x_ref[...].astype(_BF16)
    cos = cos_ref[...]
    sin = sin_ref[...]
    pa = _dot(xb, wa_ref[...])

    def roped(col0, width):
        return [_rope_group(pa[:, col0 + g * LANES: col0 + (g + 1) * LANES], cos, sin)
                for g in range(width // LANES)]

    for g, z in enumerate(roped(0, ATT_W)):
        q_ref[:, g * LANES:(g + 1) * LANES] = (z * q_scale).astype(_BF16)
    for g, z in enumerate(roped(ATT_W, ATT_W)):
        k_ref[:, g * LANES:(g + 1) * LANES] = z.astype(_BF16)
    v_ref[...] = pa[:, 2 * ATT_W:3 * ATT_W].astype(_BF16)
    for g, z in enumerate(roped(3 * ATT_W, IDX_W)):
        iq_ref[:, g * LANES:(g + 1) * LANES] = z.astype(_BF16)
    ik_ref[...] = roped(3 * ATT_W + IDX_W, LANES)[0].astype(_BF16)
    iw_ref[...] = pa[:, 3 * ATT_W + IDX_W + LANES:] * iw_scale

    ph = _dot(xb, wh_ref[...])
    hq_ref[...] = ph[:, 0 * HG_W:1 * HG_W]
    hf_ref[...] = ph[:, 1 * HG_W:2 * HG_W]
    hi_ref[...] = ph[:, 2 * HG_W:3 * HG_W]
    hg_ref[...] = ph[:, 3 * HG_W:4 * HG_W]


def _rope_tables(seq):
    half = ATT_HEAD_DIM // 2
    inv = np.power(np.float64(ROPE_THETA), -np.arange(half, dtype=np.float64) / half)
    ang = np.arange(seq, dtype=np.float64)[:, None] * inv[None, :]
    cos = np.cos(ang)
    sin = np.sin(ang)
    cos_t = np.tile(np.concatenate([cos, cos], axis=1), (1, LANES // ATT_HEAD_DIM))
    sin_t = np.tile(np.concatenate([-sin, sin], axis=1), (1, LANES // ATT_HEAD_DIM))
    return jnp.asarray(cos_t, _F32), jnp.asarray(sin_t, _F32)


def _project(x2d, w_in, seq):
    rows, d = x2d.shape
    c = 3 * ATT_W + IDX_W
    w_ik = w_in[:, c:c + IDX_DIM]
    w_iw = w_in[:, c + IDX_DIM:c + IDX_DIM + IDX_HEADS]
    wa = jnp.concatenate([w_in[:, :c], w_ik, w_ik,
                          jnp.pad(w_iw, ((0, 0), (0, LANES - IDX_HEADS)))],
                         axis=1).astype(_BF16)
    wh = w_in[:, c + IDX_DIM + IDX_HEADS:].astype(_BF16)
    cos_t, sin_t = _rope_tables(seq)
    tm = PROJ_ROWS
    n_seq_blocks = seq // tm
    row_spec = lambda w: pl.BlockSpec((tm, w), lambda i: (i, 0))
    full_spec = lambda a: pl.BlockSpec(a.shape, lambda i: (0, 0))
    pos_spec = pl.BlockSpec((tm, LANES), lambda i: (i % n_seq_blocks, 0))
    out_shapes = [
        jax.ShapeDtypeStruct((rows, ATT_W), _BF16),
        jax.ShapeDtypeStruct((rows, ATT_W), _BF16),
        jax.ShapeDtypeStruct((rows, ATT_W), _BF16),
        jax.ShapeDtypeStruct((rows, IDX_W), _BF16),
        jax.ShapeDtypeStruct((rows, LANES), _BF16),
        jax.ShapeDtypeStruct((rows, LANES), _F32),
    ] + [jax.ShapeDtypeStruct((rows, HG_W), _F32)] * 4
    out_specs = [row_spec(s.shape[1]) for s in out_shapes]
    kern = functools.partial(_proj_kernel, q_scale=ATT_HEAD_DIM ** -0.5 * math.log2(math.e),
                             iw_scale=(IDX_HEADS ** -0.5) * (IDX_DIM ** -0.5))
    return pl.pallas_call(
        kern,
        grid=(rows // tm,),
        in_specs=[row_spec(d), full_spec(wa), full_spec(wh), pos_spec, pos_spec],
        out_specs=out_specs,
        out_shape=out_shapes,
        compiler_params=pltpu.CompilerParams(
            dimension_semantics=("arbitrary",), vmem_limit_bytes=VMEM_LIMIT_BYTES),
    )(x2d, wa, wh, cos_t, sin_t)


def _dsa_kernel(qi_ref, kj_ref,
                qt_ref, iqt_ref, iwt_ref, ik_ref, before_ref, k_ref, vt_ref,
                o_ref,
                keys_scr, m_scr, acc_scr, *, topk):
    p = pl.program_id(1)
    i = qi_ref[p]
    j = kj_ref[p]
    qb, kb = DSA_QB, DSA_KB
    n_kb = (i * qb + qb - 1) // kb + 1
    first_head = lax.broadcasted_iota(_I32, (LANES, qb), 0) < ATT_HEAD_DIM

    def one_head(pair_rows, h):
        keep = first_head if h % 2 == 0 else jnp.logical_not(first_head)
        return jnp.where(keep, pair_rows, jnp.zeros_like(pair_rows))

    @pl.when(j == 0)
    def _select():
        iwt = iwt_ref[0]
        qpos = i * qb + lax.broadcasted_iota(_I32, (kb, qb), 1)
        krow = lax.broadcasted_iota(_I32, (kb, qb), 0)
        iq_heads = [one_head(iqt_ref[0, (h // 2) * LANES:(h // 2 + 1) * LANES, :], h)
                    for h in range(IDX_HEADS)]

        def score_block(c, carry):
            row0 = pl.multiple_of(c * kb, kb)
            ikc = ik_ref[0, pl.ds(row0, kb), :]
            score = jnp.zeros((kb, qb), _F32)
            for h in range(IDX_HEADS):
                logits = _dot(ikc, iq_heads[h])
                score = score + iwt[h:h + 1, :] * jnp.maximum(logits, 0.0)
            bits = lax.bitcast_convert_type(score, _I32)
            key = bits ^ ((bits >> 31) & 0x7FFFFFFF)
            keys_scr[c] = jnp.where(row0 + krow <= qpos, key, _INT_MIN)
            return carry

        lax.fori_loop(0, n_kb, score_block, 0)

        def count(pred):
            def body(c, acc):
                hit = jnp.where(pred(keys_scr[c]), 1.0, 0.0)
                return acc + _tree_sum([hit[r * SUBLANES:(r + 1) * SUBLANES]
                                        for r in range(kb // SUBLANES)])
            acc = lax.fori_loop(0, n_kb, body, jnp.zeros((SUBLANES, qb), _F32))
            return jnp.sum(acc, axis=0, keepdims=True)

        def bisect(step, thr):
            bit = jnp.left_shift(jnp.int32(1), 31 - step)
            cand = jnp.where(step == 0, 0, thr | bit)
            ok = count(lambda kc: kc >= cand) >= topk
            return jnp.where(ok, cand, thr)

        thr = lax.fori_loop(0, 32, bisect, jnp.full((1, qb), _INT_MIN, _I32))
        thr = jnp.maximum(thr, _INT_MIN + 1)
        rem = topk - count(lambda kc: kc > thr)

        def mask_block(c, seen):
            keys = keys_scr[c]
            tie = keys == thr
            ties_before = seen + _dot(before_ref[...], jnp.where(tie, 1.0, 0.0).astype(_BF16))
            chosen = (keys > thr) | (tie & (ties_before < rem))
            keys_scr[c] = lax.bitcast_convert_type(jnp.where(chosen, 0.0, _NEG), _I32)
            return ties_before[kb - 1:kb] + jnp.where(tie[kb - 1:kb], 1.0, 0.0)

        lax.fori_loop(0, n_kb, mask_block, jnp.zeros((1, qb), _F32))

        m_scr[...] = jnp.full_like(m_scr, _NEG)
        acc_scr[...] = jnp.zeros_like(acc_scr)

    bias = lax.bitcast_convert_type(keys_scr[j], _F32)
    def logits(h):
        pair = slice((h // 2) * LANES, (h // 2 + 1) * LANES)
        return _dot(k_ref[0, :, pair], one_head(qt_ref[0, pair, :], h))

    s_next = logits(0)
    for h in range(ATT_HEADS):
        s = s_next + bias
        if h + 1 < ATT_HEADS:
            s_next = logits(h + 1)
        m_old = m_scr[h]
        m_new = jnp.maximum(m_old, jnp.max(s, axis=0, keepdims=True))
        pr = jnp.exp2(s - m_new).astype(_BF16)
        acc_scr[h] = jnp.exp2(m_old - m_new) * acc_scr[h] + _dot(vt_ref[0, h], pr)
        m_scr[h] = m_new

    @pl.when(j == n_kb - 1)
    def _finish():
        for h in range(ATT_HEADS):
            a = acc_scr[h]
            o_ref[0, h * ATT_HEAD_DIM:(h + 1) * ATT_HEAD_DIM, :] = (
                a[:ATT_HEAD_DIM] / a[ATT_HEAD_DIM:ATT_HEAD_DIM + 1]).astype(o_ref.dtype)


def _dsa_attention(q, k, v, iq, ik, iw):
    bsz, seq, _ = q.shape
    qb, kb = DSA_QB, DSA_KB
    topk = min(TOPK_MAX, seq // 4)
    n_qb = seq // qb
    pairs = [(i, j) for i in range(n_qb) for j in range((i * qb + qb - 1) // kb + 1)]
    qi = jnp.asarray(np.array([p[0] for p in pairs], np.int32))
    kj = jnp.asarray(np.array([p[1] for p in pairs], np.int32))
    qt = jnp.swapaxes(q, 1, 2)
    iqt = jnp.swapaxes(iq, 1, 2)
    iwt = jnp.swapaxes(iw[:, :, :SUBLANES], 1, 2)
    vt = jnp.swapaxes(v.reshape(bsz, seq, ATT_HEADS, ATT_HEAD_DIM), 1, 3)
    vt = jnp.swapaxes(vt, 1, 2)
    extra = jnp.zeros((bsz, ATT_HEADS, V_ROWS - ATT_HEAD_DIM, seq), v.dtype).at[:, :, 0].set(1)
    vt = jnp.concatenate([vt, extra], axis=2)
    before = jnp.asarray(np.tril(np.ones((kb, kb), np.float32), -1), _BF16)
    q_map = lambda b, p, qi_r, kj_r: (b, 0, qi_r[p])
    k_map = lambda b, p, qi_r, kj_r: (b, kj_r[p], 0)
    grid_spec = pltpu.PrefetchScalarGridSpec(
        num_scalar_prefetch=2,
        grid=(bsz, len(pairs)),
        in_specs=[
            pl.BlockSpec((1, ATT_W, qb), q_map),
            pl.BlockSpec((1, IDX_W, qb), q_map),
            pl.BlockSpec((1, SUBLANES, qb), q_map),
            pl.BlockSpec((1, seq, LANES), lambda b, p, qi_r, kj_r: (b, 0, 0)),
            pl.BlockSpec((kb, kb), lambda b, p, qi_r, kj_r: (0, 0)),
            pl.BlockSpec((1, kb, ATT_W), k_map),
            pl.BlockSpec((1, ATT_HEADS, V_ROWS, kb), lambda b, p, qi_r, kj_r: (b, 0, 0, kj_r[p])),
        ],
        out_specs=pl.BlockSpec((1, ATT_W, qb), q_map),
        scratch_shapes=[
            pltpu.VMEM((seq // kb, kb, qb), _I32),
            pltpu.VMEM((ATT_HEADS, 1, qb), _F32),
            pltpu.VMEM((ATT_HEADS, V_ROWS, qb), _F32),
        ],
    )
    return pl.pallas_call(
        functools.partial(_dsa_kernel, topk=topk),
        grid_spec=grid_spec,
        out_shape=jax.ShapeDtypeStruct((bsz, ATT_W, seq), _BF16),
        compiler_params=pltpu.CompilerParams(
            dimension_semantics=("arbitrary", "arbitrary"), vmem_limit_bytes=VMEM_LIMIT_BYTES),
    )(qi, kj, qt, iqt, iwt, ik, before, k, vt)


def _split3(a):
    hi = a.astype(_BF16)
    r1 = a - hi.astype(_F32)
    mid = r1.astype(_BF16)
    lo = (r1 - mid.astype(_F32)).astype(_BF16)
    return hi, mid, lo


def _hgrn_kernel(lbl_ref, g_ref, hq_ref, hf_ref, hi_ref, hg_ref, o_ref,
                 state_scr, kk_scr, b_scr, o_scr, *, layer):
    rows, ch, sb = HG_ROWS, HG_CHUNK, HG_SUB
    n_sub = ch // sb

    @pl.when(pl.program_id(1) == 0)
    def _reset():
        state_scr[...] = jnp.zeros_like(state_scr)

    lbl = lbl_ref[...]
    e = jnp.exp(lbl - jnp.max(lbl, axis=0, keepdims=True))
    lb = jnp.sum(e[:layer + 1], axis=0, keepdims=True) / jnp.sum(e, axis=0, keepdims=True)

    f = lb + (1.0 - lb) * jax.nn.sigmoid(hf_ref[0])
    kk_scr[...] = 1.0 - f
    logf = jnp.log(f)
    r_i = lax.broadcasted_iota(_I32, (ch, ch), 0)
    c_i = lax.broadcasted_iota(_I32, (ch, ch), 1)
    lower = jnp.where(c_i <= r_i, 1.0, 0.0).astype(_BF16)
    for c in range(rows // ch):
        parts = _split3(logf[c * ch:(c + 1) * ch])
        b_scr[c * ch:(c + 1) * ch, :] = sum(_dot(lower, part) for part in parts)

    t_idx = lax.broadcasted_iota(_I32, (sb, 1), 0)
    row_idx = lax.broadcasted_iota(_I32, (ch, 1), 0)

    def chunk(c, carry):
        r0 = pl.multiple_of(c * ch, ch)
        cs = pl.ds(r0, ch)
        b = b_scr[cs, :]
        kk = kk_scr[cs, :]
        qv = hq_ref[0, cs, :]
        vv = hi_ref[0, cs, :]
        b_last = b[ch - 1:ch]
        q_in = (qv * jnp.exp(b)).astype(_BF16)
        k_out = kk * jnp.exp(b_last - b)
        vb = vv.astype(_BF16)

        a_off = [[jnp.zeros((sb, ch), _F32)] for _ in range(HG_HEADS)]
        for s_i in range(1, n_sub):
            ref_b = b[s_i * sb - 1:s_i * sb]
            q_s = (qv[s_i * sb:(s_i + 1) * sb] * jnp.exp(b[s_i * sb:(s_i + 1) * sb] - ref_b))
            k_s = jnp.where(row_idx < s_i * sb, kk * jnp.exp(jnp.minimum(ref_b - b, 0.0)), 0.0)
            q_s = q_s.astype(_BF16)
            k_s = k_s.astype(_BF16)
            for h in range(HG_HEADS):
                hs = slice(h * HG_KDIM, (h + 1) * HG_KDIM)
                a_off[h].append(_nt_dot(q_s[:, hs], k_s[:, hs]))

        for h in range(HG_HEADS):
            hs = slice(h * HG_KDIM, (h + 1) * HG_KDIM)
            st = state_scr[h]
            o_h = _nt_dot(q_in[:, hs], st.astype(_BF16))
            a_h = jnp.concatenate(a_off[h], axis=0).astype(_BF16)
            o_scr[cs, hs] = o_h + _dot(a_h, vb[:, hs])
            state_scr[h] = (st * jnp.exp(b_last[:, hs])
                            + _tn_dot(vb[:, hs], k_out[:, hs].astype(_BF16)))

        for s_i in range(n_sub):
            rs = pl.ds(r0 + s_i * sb, sb)
            q_s = hq_ref[0, rs, :]
            b_s = b_scr[rs, :]
            acc = o_scr[rs, :]
            for t in range(sb):
                one = pl.ds(r0 + s_i * sb + t, 1)
                w = q_s * jnp.exp(jnp.minimum(b_s - b_scr[one, :], 0.0)) * kk_scr[one, :]
                v_row = hi_ref[0, one, :]
                parts = []
                for h in range(HG_HEADS):
                    hs = slice(h * HG_KDIM, (h + 1) * HG_KDIM)
                    a = jnp.sum(w[:, hs], axis=1, keepdims=True)
                    parts.append(jnp.where(t_idx >= t, a, 0.0) * v_row[:, hs])
                acc = acc + jnp.concatenate(parts, axis=1)
            o_scr[rs, :] = acc
        return carry

    lax.fori_loop(0, rows // ch, chunk, 0)

    o = o_scr[...]
    gate = hg_ref[0]
    gain = g_ref[...]
    for h in range(HG_HEADS):
        hs = slice(h * HG_KDIM, (h + 1) * HG_KDIM)
        oh = o[:, hs]
        oh = oh * lax.rsqrt(jnp.mean(oh * oh, axis=1, keepdims=True) + RMS_EPS) * gain[:, hs]
        gh = gate[:, hs]
        o_ref[0, :, hs] = (oh * (gh * jax.nn.sigmoid(gh))).astype(o_ref.dtype)


def _hgrn2(hq, hf, hi, hg, lb_logits, norm_g, layer):
    bsz, seq, _ = hq.shape
    rows = HG_ROWS
    blk = pl.BlockSpec((1, rows, HG_W), lambda b, t: (b, t, 0))
    lbl = lb_logits.reshape(lb_logits.shape[0], HG_W)
    gain = norm_g.reshape(1, HG_W)
    return pl.pallas_call(
        functools.partial(_hgrn_kernel, layer=layer),
        grid=(bsz, seq // rows),
        in_specs=[pl.BlockSpec(lbl.shape, lambda b, t: (0, 0)),
                  pl.BlockSpec(gain.shape, lambda b, t: (0, 0)),
                  blk, blk, blk, blk],
        out_specs=blk,
        out_shape=jax.ShapeDtypeStruct((bsz, seq, HG_W), _BF16),
        scratch_shapes=[
            pltpu.VMEM((HG_HEADS, HG_KDIM, HG_KDIM), _F32),
            pltpu.VMEM((rows, HG_W), _F32),
            pltpu.VMEM((rows, HG_W), _F32),
            pltpu.VMEM((rows, HG_W), _F32),
        ],
        compiler_params=pltpu.CompilerParams(
            dimension_semantics=("arbitrary", "arbitrary"), vmem_limit_bytes=VMEM_LIMIT_BYTES),
    )(lbl, gain, hq, hf, hi, hg)


def _layer_norm(y, g, b):
    mu = jnp.mean(y, axis=1, keepdims=True)
    yc = y - mu
    var = jnp.mean(yc * yc, axis=1, keepdims=True)
    return yc * lax.rsqrt(var + LN_EPS) * g + b


def _ffn_kernel(x_ref, att_ref, hgo_ref, wo_ref, g1_ref, b1_ref, wu_ref, wd_ref,
                g2_ref, b2_ref, o_ref, *, alpha):
    mix = (_tn_dot(att_ref[0], wo_ref[:ATT_W, :]) + _dot(hgo_ref[...], wo_ref[ATT_W:, :]))
    y1 = _layer_norm(alpha * x_ref[...] + mix, g1_ref[...], b1_ref[...])
    y1b = y1.astype(_BF16)
    h = jnp.zeros_like(y1)
    for c in range(wu_ref.shape[1] // FFN_COLS):
        cs = slice(c * FFN_COLS, (c + 1) * FFN_COLS)
        u = jnp.maximum(_dot(y1b, wu_ref[:, cs]), 0.0)
        h = h + _dot((u * u).astype(_BF16), wd_ref[cs, :])
    o_ref[...] = _layer_norm(alpha * y1 + h, g2_ref[...], b2_ref[...])


def _out_ffn(x2d, att_t, hgo2d, w_o, g1, b1, w_up, w_down, g2, b2, alpha):
    rows, d = x2d.shape
    seq = att_t.shape[2]
    tm = FFN_ROWS
    n_seq_blocks = seq // tm
    row_spec = lambda w: pl.BlockSpec((tm, w), lambda i: (i, 0))
    full_spec = lambda a: pl.BlockSpec(a.shape, lambda i: (0, 0))
    att_spec = pl.BlockSpec((1, ATT_W, tm), lambda i: (i // n_seq_blocks, 0, i % n_seq_blocks))
    vec = lambda a: a.reshape(1, d).astype(_F32)
    args = (x2d, att_t, hgo2d, w_o.astype(_BF16), vec(g1), vec(b1),
            w_up.astype(_BF16), w_down.astype(_BF16), vec(g2), vec(b2))
    in_specs = [row_spec(d), att_spec, row_spec(HG_W)] + [full_spec(a) for a in args[3:]]
    return pl.pallas_call(
        functools.partial(_ffn_kernel, alpha=alpha),
        grid=(rows // tm,),
        in_specs=in_specs,
        out_specs=row_spec(d),
        out_shape=jax.ShapeDtypeStruct((rows, d), _F32),
        compiler_params=pltpu.CompilerParams(
            dimension_semantics=("arbitrary",), vmem_limit_bytes=VMEM_LIMIT_BYTES),
    )(*args)


def kernel(x, w_in, w_o, lb_logits, hg_norm_g, ln1_g, ln1_b, w_up, w_down, ln2_g, ln2_b):
    bsz, seq, d = x.shape
    depth = w_in.shape[0]
    alpha = (2.0 * depth) ** 0.25
    x2d = x.reshape(bsz * seq, d)
    for l in range(depth):
        q, k, v, iq, ik, iw, hq, hf, hi, hg = _project(x2d, w_in[l], seq)
        r3 = lambda a: a.reshape(bsz, seq, a.shape[-1])
        att_t = _dsa_attention(r3(q), r3(k), r3(v), r3(iq), r3(ik), r3(iw))
        hgo = _hgrn2(r3(hq), r3(hf), r3(hi), r3(hg), lb_logits, hg_norm_g[l], l)
        x2d = _out_ffn(x2d, att_t, hgo.reshape(bsz * seq, HG_W),
                       w_o[l], ln1_g[l], ln1_b[l], w_up[l], w_down[l], ln2_g[l], ln2_b[l], alpha)
    return x2d.reshape(bsz, seq, d)
```

```python
import functools
import math

import numpy as np
import jax
import jax.numpy as jnp
from jax import lax
from jax.experimental import pallas as pl
from jax.experimental.pallas import tpu as pltpu

ATT_HEAD_DIM = 64
ATT_HEADS = 8
ATT_W = ATT_HEADS * ATT_HEAD_DIM
IDX_HEADS = 4
IDX_DIM = 64
IDX_W = IDX_HEADS * IDX_DIM
TOPK_MAX = 256
HG_KDIM = 128
HG_HEADS = 4
HG_W = HG_HEADS * HG_KDIM
ROPE_THETA = 10000.0
LN_EPS = 1e-5
RMS_EPS = 1e-6

LANES = 128
SUBLANES = 8
BF16_ROWS = 16
VMEM_LIMIT_BYTES = 56 * 1024 * 1024

PROJ_ROWS = 256
DSA_QB = 256
DSA_KB = 512
V_ROWS = ATT_HEAD_DIM + BF16_ROWS
HG_ROWS = 256
HG_CHUNK = 64
HG_SUB = 16
FFN_ROWS = 256
FFN_COLS = 1024

_F32 = jnp.float32
_BF16 = jnp.bfloat16
_I32 = jnp.int32
_INT_MIN = -(2 ** 31)
_NEG = -1e30


def _nt_dot(a, b):
    return lax.dot_general(a, b, (((1,), (1,)), ((), ())), preferred_element_type=_F32)


def _tn_dot(a, b):
    return lax.dot_general(a, b, (((0,), (0,)), ((), ())), preferred_element_type=_F32)


def _dot(a, b):
    return jnp.dot(a, b, preferred_element_type=_F32)


def _tree_sum(parts):
    while len(parts) > 1:
        parts = [parts[n] + parts[n + 1] for n in range(0, len(parts) - 1, 2)] + (
            [parts[-1]] if len(parts) % 2 else [])
    return parts[0]


def _rope_group(z, cos, sin_signed):
    lane = lax.broadcasted_iota(_I32, z.shape, 1)
    first_half = (lane % ATT_HEAD_DIM) < (ATT_HEAD_DIM // 2)
    upper = pltpu.roll(z, LANES - ATT_HEAD_DIM // 2, 1)
    lower = pltpu.roll(z, ATT_HEAD_DIM // 2, 1)
    return z * cos + jnp.where(first_half, upper, lower) * sin_signed


def _proj_kernel(x_ref, wa_ref, wh_ref, cos_ref, sin_ref,
                 q_ref, k_ref, v_ref, iq_ref, ik_ref, iw_ref,
                 hq_ref, hf_ref, hi_ref, hg_ref, *, q_scale, iw_scale):
    xb = x_ref[...].astype(_BF16)
    cos = cos_ref[...]
    sin = sin_ref[...]
    pa = _dot(xb, wa_ref[...])

    def roped(col0, width):
        return [_rope_group(pa[:, col0 + g * LANES: col0 + (g + 1) * LANES], cos, sin)
                for g in range(width // LANES)]

    for g, z in enumerate(roped(0, ATT_W)):
        q_ref[:, g * LANES:(g + 1) * LANES] = (z * q_scale).astype(_BF16)
    for g, z in enumerate(roped(ATT_W, ATT_W)):
        k_ref[:, g * LANES:(g + 1) * LANES] = z.astype(_BF16)
    v_ref[...] = pa[:, 2 * ATT_W:3 * ATT_W].astype(_BF16)
    for g, z in enumerate(roped(3 * ATT_W, IDX_W)):
        iq_ref[:, g * LANES:(g + 1) * LANES] = z.astype(_BF16)
    ik_ref[...] = roped(3 * ATT_W + IDX_W, LANES)[0].astype(_BF16)
    iw_ref[...] = pa[:, 3 * ATT_W + IDX_W + LANES:] * iw_scale

    ph = _dot(xb, wh_ref[...])
    hq_ref[...] = ph[:, 0 * HG_W:1 * HG_W]
    hf_ref[...] = ph[:, 1 * HG_W:2 * HG_W]
    hi_ref[...] = ph[:, 2 * HG_W:3 * HG_W]
    hg_ref[...] = ph[:, 3 * HG_W:4 * HG_W]


def _rope_tables(seq):
    half = ATT_HEAD_DIM // 2
    inv = np.power(np.float64(ROPE_THETA), -np.arange(half, dtype=np.float64) / half)
    ang = np.arange(seq, dtype=np.float64)[:, None] * inv[None, :]
    cos = np.cos(ang)
    sin = np.sin(ang)
    cos_t = np.tile(np.concatenate([cos, cos], axis=1), (1, LANES // ATT_HEAD_DIM))
    sin_t = np.tile(np.concatenate([-sin, sin], axis=1), (1, LANES // ATT_HEAD_DIM))
    return jnp.asarray(cos_t, _F32), jnp.asarray(sin_t, _F32)


def _project(x2d, w_in, seq):
    rows, d = x2d.shape
    c = 3 * ATT_W + IDX_W
    w_ik = w_in[:, c:c + IDX_DIM]
    w_iw = w_in[:, c + IDX_DIM:c + IDX_DIM + IDX_HEADS]
    wa = jnp.concatenate([w_in[:, :c], w_ik, w_ik,
                          jnp.pad(w_iw, ((0, 0), (0, LANES - IDX_HEADS)))],
                         axis=1).astype(_BF16)
    wh = w_in[:, c + IDX_DIM + IDX_HEADS:].astype(_BF16)
    cos_t, sin_t = _rope_tables(seq)
    tm = PROJ_ROWS
    n_seq_blocks = seq // tm
    row_spec = lambda w: pl.BlockSpec((tm, w), lambda i: (i, 0))
    full_spec = lambda a: pl.BlockSpec(a.shape, lambda i: (0, 0))
    pos_spec = pl.BlockSpec((tm, LANES), lambda i: (i % n_seq_blocks, 0))
    out_shapes = [
        jax.ShapeDtypeStruct((rows, ATT_W), _BF16),
        jax.ShapeDtypeStruct((rows, ATT_W), _BF16),
        jax.ShapeDtypeStruct((rows, ATT_W), _BF16),
        jax.ShapeDtypeStruct((rows, IDX_W), _BF16),
        jax.ShapeDtypeStruct((rows, LANES), _BF16),
        jax.ShapeDtypeStruct((rows, LANES), _F32),
    ] + [jax.ShapeDtypeStruct((rows, HG_W), _F32)] * 4
    out_specs = [row_spec(s.shape[1]) for s in out_shapes]
    kern = functools.partial(_proj_kernel, q_scale=ATT_HEAD_DIM ** -0.5 * math.log2(math.e),
                             iw_scale=(IDX_HEADS ** -0.5) * (IDX_DIM ** -0.5))
    return pl.pallas_call(
        kern,
        grid=(rows // tm,),
        in_specs=[row_spec(d), full_spec(wa), full_spec(wh), pos_spec, pos_spec],
        out_specs=out_specs,
        out_shape=out_shapes,
        compiler_params=pltpu.CompilerParams(
            dimension_semantics=("arbitrary",), vmem_limit_bytes=VMEM_LIMIT_BYTES),
    )(x2d, wa, wh, cos_t, sin_t)


def _dsa_kernel(qi_ref, kj_ref, ka_ref, kb_ref,
                qt_ref, iqt_ref, iwt_ref, ik_ref, before_ref, k_ref, vt_ref,
                o_ref,
                keys_scr, m_scr, acc_scr, s_even, s_odd, bm_even, bm_odd, *, topk):
    del ka_ref, kb_ref
    p = pl.program_id(1)
    i = qi_ref[p]
    j = kj_ref[p]
    qb, kb = DSA_QB, DSA_KB
    n_kb = (i * qb + qb - 1) // kb + 1
    first_head = lax.broadcasted_iota(_I32, (LANES, qb), 0) < ATT_HEAD_DIM

    def one_head(pair_rows, h):
        keep = first_head if h % 2 == 0 else jnp.logical_not(first_head)
        return jnp.where(keep, pair_rows, jnp.zeros_like(pair_rows))

    @pl.when(j == 0)
    def _select():
        iwt = iwt_ref[0]
        qpos = i * qb + lax.broadcasted_iota(_I32, (kb, qb), 1)
        krow = lax.broadcasted_iota(_I32, (kb, qb), 0)
        iq_heads = [one_head(iqt_ref[0, (h // 2) * LANES:(h // 2 + 1) * LANES, :], h)
                    for h in range(IDX_HEADS)]

        def score_block(c, carry):
            row0 = pl.multiple_of(c * kb, kb)
            ikc = ik_ref[0, pl.ds(row0, kb), :]
            score = jnp.zeros((kb, qb), _F32)
            for h in range(IDX_HEADS):
                logits = _dot(ikc, iq_heads[h])
                score = score + iwt[h:h + 1, :] * jnp.maximum(logits, 0.0)
            bits = lax.bitcast_convert_type(score, _I32)
            key = bits ^ ((bits >> 31) & 0x7FFFFFFF)
            keys_scr[c] = jnp.where(row0 + krow <= qpos, key, _INT_MIN)
            return carry

        lax.fori_loop(0, n_kb, score_block, 0)

        def count(pred):
            def body(c, acc):
                hit = jnp.where(pred(keys_scr[c]), 1.0, 0.0)
                return acc + _tree_sum([hit[r * SUBLANES:(r + 1) * SUBLANES]
                                        for r in range(kb // SUBLANES)])
            acc = lax.fori_loop(0, n_kb, body, jnp.zeros((SUBLANES, qb), _F32))
            return jnp.sum(acc, axis=0, keepdims=True)

        def bisect(step, thr):
            bit = jnp.left_shift(jnp.int32(1), 31 - step)
            cand = jnp.where(step == 0, 0, thr | bit)
            ok = count(lambda kc: kc >= cand) >= topk
            return jnp.where(ok, cand, thr)

        thr = lax.fori_loop(0, 32, bisect, jnp.full((1, qb), _INT_MIN, _I32))
        thr = jnp.maximum(thr, _INT_MIN + 1)
        rem = topk - count(lambda kc: kc > thr)

        def mask_block(c, seen):
            keys = keys_scr[c]
            tie = keys == thr
            ties_before = seen + _dot(before_ref[...], jnp.where(tie, 1.0, 0.0).astype(_BF16))
            chosen = (keys > thr) | (tie & (ties_before < rem))
            keys_scr[c] = lax.bitcast_convert_type(jnp.where(chosen, 0.0, _NEG), _I32)
            return ties_before[kb - 1:kb] + jnp.where(tie[kb - 1:kb], 1.0, 0.0)

        lax.fori_loop(0, n_kb, mask_block, jnp.zeros((1, qb), _F32))

        m_scr[...] = jnp.full_like(m_scr, _NEG)
        acc_scr[...] = jnp.zeros_like(acc_scr)
        s_odd[...] = jnp.full_like(s_odd, _NEG)
        bm_odd[...] = jnp.full_like(bm_odd, _NEG)

    def step(s_write, bm_write, s_read, bm_read):
        bias = lax.bitcast_convert_type(keys_scr[jnp.minimum(j, n_kb - 1)], _F32)
        for h in range(ATT_HEADS):
            pair = slice((h // 2) * LANES, (h // 2 + 1) * LANES)
            s = _dot(k_ref[0, :, pair], one_head(qt_ref[0, pair, :], h)) + bias
            s_write[h] = s
            bm_write[h] = jnp.max(s, axis=0, keepdims=True)

            m_old = m_scr[h]
            m_new = jnp.maximum(m_old, bm_read[h])
            pr = jnp.exp2(s_read[h] - m_new).astype(_BF16)
            acc_scr[h] = jnp.exp2(m_old - m_new) * acc_scr[h] + _dot(vt_ref[0, h], pr)
            m_scr[h] = m_new

    @pl.when(j % 2 == 0)
    def _even():
        step(s_even, bm_even, s_odd, bm_odd)

    @pl.when(j % 2 == 1)
    def _odd():
        step(s_odd, bm_odd, s_even, bm_even)

    @pl.when(j == n_kb)
    def _finish():
        for h in range(ATT_HEADS):
            a = acc_scr[h]
            o_ref[0, h * ATT_HEAD_DIM:(h + 1) * ATT_HEAD_DIM, :] = (
                a[:ATT_HEAD_DIM] / a[ATT_HEAD_DIM:ATT_HEAD_DIM + 1]).astype(o_ref.dtype)


def _dsa_attention(q, k, v, iq, ik, iw):
    bsz, seq, _ = q.shape
    qb, kb = DSA_QB, DSA_KB
    topk = min(TOPK_MAX, seq // 4)
    n_qb = seq // qb
    n_kb = lambda i: (i * qb + qb - 1) // kb + 1
    pairs = [(i, j) for i in range(n_qb) for j in range(n_kb(i) + 1)]
    as_i32 = lambda vals: jnp.asarray(np.array(vals, np.int32))
    qi = as_i32([i for i, j in pairs])
    kj = as_i32([j for i, j in pairs])
    k_stage_a = as_i32([min(j, n_kb(i) - 1) for i, j in pairs])
    k_stage_b = as_i32([max(j - 1, 0) for i, j in pairs])
    qt = jnp.swapaxes(q, 1, 2)
    iqt = jnp.swapaxes(iq, 1, 2)
    iwt = jnp.swapaxes(iw[:, :, :SUBLANES], 1, 2)
    vt = jnp.swapaxes(v.reshape(bsz, seq, ATT_HEADS, ATT_HEAD_DIM), 1, 3)
    vt = jnp.swapaxes(vt, 1, 2)
    extra = jnp.zeros((bsz, ATT_HEADS, V_ROWS - ATT_HEAD_DIM, seq), v.dtype).at[:, :, 0].set(1)
    vt = jnp.concatenate([vt, extra], axis=2)
    before = jnp.asarray(np.tril(np.ones((kb, kb), np.float32), -1), _BF16)
    q_map = lambda b, p, qi_r, kj_r, ka_r, kb_r: (b, 0, qi_r[p])
    grid_spec = pltpu.PrefetchScalarGridSpec(
        num_scalar_prefetch=4,
        grid=(bsz, len(pairs)),
        in_specs=[
            pl.BlockSpec((1, ATT_W, qb), q_map),
            pl.BlockSpec((1, IDX_W, qb), q_map),
            pl.BlockSpec((1, SUBLANES, qb), q_map),
            pl.BlockSpec((1, seq, LANES), lambda b, p, *_: (b, 0, 0)),
            pl.BlockSpec((kb, kb), lambda b, p, *_: (0, 0)),
            pl.BlockSpec((1, kb, ATT_W), lambda b, p, qi_r, kj_r, ka_r, kb_r: (b, ka_r[p], 0)),
            pl.BlockSpec((1, ATT_HEADS, V_ROWS, kb),
                         lambda b, p, qi_r, kj_r, ka_r, kb_r: (b, 0, 0, kb_r[p])),
        ],
        out_specs=pl.BlockSpec((1, ATT_W, qb), q_map),
        scratch_shapes=[
            pltpu.VMEM((seq // kb, kb, qb), _I32),
            pltpu.VMEM((ATT_HEADS, 1, qb), _F32),
            pltpu.VMEM((ATT_HEADS, V_ROWS, qb), _F32),
            pltpu.VMEM((ATT_HEADS, kb, qb), _F32),
            pltpu.VMEM((ATT_HEADS, kb, qb), _F32),
            pltpu.VMEM((ATT_HEADS, 1, qb), _F32),
            pltpu.VMEM((ATT_HEADS, 1, qb), _F32),
        ],
    )
    return pl.pallas_call(
        functools.partial(_dsa_kernel, topk=topk),
        grid_spec=grid_spec,
        out_shape=jax.ShapeDtypeStruct((bsz, ATT_W, seq), _BF16),
        compiler_params=pltpu.CompilerParams(
            dimension_semantics=("arbitrary", "arbitrary"), vmem_limit_bytes=VMEM_LIMIT_BYTES),
    )(qi, kj, k_stage_a, k_stage_b, qt, iqt, iwt, ik, before, k, vt)


def _split3(a):
    hi = a.astype(_BF16)
    r1 = a - hi.astype(_F32)
    mid = r1.astype(_BF16)
    lo = (r1 - mid.astype(_F32)).astype(_BF16)
    return hi, mid, lo


def _hgrn_kernel(lbl_ref, g_ref, hq_ref, hf_ref, hi_ref, hg_ref, o_ref,
                 state_scr, kk_scr, b_scr, o_scr, *, layer):
    rows, ch, sb = HG_ROWS, HG_CHUNK, HG_SUB
    n_sub = ch // sb

    @pl.when(pl.program_id(1) == 0)
    def _reset():
        state_scr[...] = jnp.zeros_like(state_scr)

    lbl = lbl_ref[...]
    e = jnp.exp(lbl - jnp.max(lbl, axis=0, keepdims=True))
    lb = jnp.sum(e[:layer + 1], axis=0, keepdims=True) / jnp.sum(e, axis=0, keepdims=True)

    f = lb + (1.0 - lb) * jax.nn.sigmoid(hf_ref[0])
    kk_scr[...] = 1.0 - f
    logf = jnp.log(f)
    r_i = lax.broadcasted_iota(_I32, (ch, ch), 0)
    c_i = lax.broadcasted_iota(_I32, (ch, ch), 1)
    lower = jnp.where(c_i <= r_i, 1.0, 0.0).astype(_BF16)
    for c in range(rows // ch):
        parts = _split3(logf[c * ch:(c + 1) * ch])
        b_scr[c * ch:(c + 1) * ch, :] = sum(_dot(lower, part) for part in parts)

    t_idx = lax.broadcasted_iota(_I32, (sb, 1), 0)
    row_idx = lax.broadcasted_iota(_I32, (ch, 1), 0)

    def chunk(c, carry):
        r0 = pl.multiple_of(c * ch, ch)
        cs = pl.ds(r0, ch)
        b = b_scr[cs, :]
        kk = kk_scr[cs, :]
        qv = hq_ref[0, cs, :]
        vv = hi_ref[0, cs, :]
        b_last = b[ch - 1:ch]
        q_in = (qv * jnp.exp(b)).astype(_BF16)
        k_out = kk * jnp.exp(b_last - b)
        vb = vv.astype(_BF16)

        a_off = [[jnp.zeros((sb, ch), _F32)] for _ in range(HG_HEADS)]
        for s_i in range(1, n_sub):
            ref_b = b[s_i * sb - 1:s_i * sb]
            q_s = (qv[s_i * sb:(s_i + 1) * sb] * jnp.exp(b[s_i * sb:(s_i + 1) * sb] - ref_b))
            k_s = jnp.where(row_idx < s_i * sb, kk * jnp.exp(jnp.minimum(ref_b - b, 0.0)), 0.0)
            q_s = q_s.astype(_BF16)
            k_s = k_s.astype(_BF16)
            for h in range(HG_HEADS):
                hs = slice(h * HG_KDIM, (h + 1) * HG_KDIM)
                a_off[h].append(_nt_dot(q_s[:, hs], k_s[:, hs]))

        for h in range(HG_HEADS):
            hs = slice(h * HG_KDIM, (h + 1) * HG_KDIM)
            st = state_scr[h]
            o_h = _nt_dot(q_in[:, hs], st.astype(_BF16))
            a_h = jnp.concatenate(a_off[h], axis=0).astype(_BF16)
            o_scr[cs, hs] = o_h + _dot(a_h, vb[:, hs])
            state_scr[h] = (st * jnp.exp(b_last[:, hs])
                            + _tn_dot(vb[:, hs], k_out[:, hs].astype(_BF16)))

        for s_i in range(n_sub):
            rs = pl.ds(r0 + s_i * sb, sb)
            q_s = hq_ref[0, rs, :]
            b_s = b_scr[rs, :]
            acc = o_scr[rs, :]
            for t in range(sb):
                one = pl.ds(r0 + s_i * sb + t, 1)
                w = q_s * jnp.exp(jnp.minimum(b_s - b_scr[one, :], 0.0)) * kk_scr[one, :]
                v_row = hi_ref[0, one, :]
                parts = []
                for h in range(HG_HEADS):
                    hs = slice(h * HG_KDIM, (h + 1) * HG_KDIM)
                    a = jnp.sum(w[:, hs], axis=1, keepdims=True)
                    parts.append(jnp.where(t_idx >= t, a, 0.0) * v_row[:, hs])
                acc = acc + jnp.concatenate(parts, axis=1)
            o_scr[rs, :] = acc
        return carry

    lax.fori_loop(0, rows // ch, chunk, 0)

    o = o_scr[...]
    gate = hg_ref[0]
    gain = g_ref[...]
    for h in range(HG_HEADS):
        hs = slice(h * HG_KDIM, (h + 1) * HG_KDIM)
        oh = o[:, hs]
        oh = oh * lax.rsqrt(jnp.mean(oh * oh, axis=1, keepdims=True) + RMS_EPS) * gain[:, hs]
        gh = gate[:, hs]
        o_ref[0, :, hs] = (oh * (gh * jax.nn.sigmoid(gh))).astype(o_ref.dtype)


def _hgrn2(hq, hf, hi, hg, lb_logits, norm_g, layer):
    bsz, seq, _ = hq.shape
    rows = HG_ROWS
    blk = pl.BlockSpec((1, rows, HG_W), lambda b, t: (b, t, 0))
    lbl = lb_logits.reshape(lb_logits.shape[0], HG_W)
    gain = norm_g.reshape(1, HG_W)
    return pl.pallas_call(
        functools.partial(_hgrn_kernel, layer=layer),
        grid=(bsz, seq // rows),
        in_specs=[pl.BlockSpec(lbl.shape, lambda b, t: (0, 0)),
                  pl.BlockSpec(gain.shape, lambda b, t: (0, 0)),
                  blk, blk, blk, blk],
        out_specs=blk,
        out_shape=jax.ShapeDtypeStruct((bsz, seq, HG_W), _BF16),
        scratch_shapes=[
            pltpu.VMEM((HG_HEADS, HG_KDIM, HG_KDIM), _F32),
            pltpu.VMEM((rows, HG_W), _F32),
            pltpu.VMEM((rows, HG_W), _F32),
            pltpu.VMEM((rows, HG_W), _F32),
        ],
        compiler_params=pltpu.CompilerParams(
            dimension_semantics=("arbitrary", "arbitrary"), vmem_limit_bytes=VMEM_LIMIT_BYTES),
    )(lbl, gain, hq, hf, hi, hg)


def _layer_norm(y, g, b):
    mu = jnp.mean(y, axis=1, keepdims=True)
    yc = y - mu
    var = jnp.mean(yc * yc, axis=1, keepdims=True)
    return yc * lax.rsqrt(var + LN_EPS) * g + b


def _ffn_kernel(x_ref, att_ref, hgo_ref, wo_ref, g1_ref, b1_ref, wu_ref, wd_ref,
                g2_ref, b2_ref, o_ref, *, alpha):
    mix = (_tn_dot(att_ref[0], wo_ref[:ATT_W, :]) + _dot(hgo_ref[...], wo_ref[ATT_W:, :]))
    y1 = _layer_norm(alpha * x_ref[...] + mix, g1_ref[...], b1_ref[...])
    y1b = y1.astype(_BF16)
    h = jnp.zeros_like(y1)
    for c in range(wu_ref.shape[1] // FFN_COLS):
        cs = slice(c * FFN_COLS, (c + 1) * FFN_COLS)
        u = jnp.maximum(_dot(y1b, wu_ref[:, cs]), 0.0)
        h = h + _dot((u * u).astype(_BF16), wd_ref[cs, :])
    o_ref[...] = _layer_norm(alpha * y1 + h, g2_ref[...], b2_ref[...])


def _out_ffn(x2d, att_t, hgo2d, w_o, g1, b1, w_up, w_down, g2, b2, alpha):
    rows, d = x2d.shape
    seq = att_t.shape[2]
    tm = FFN_ROWS
    n_seq_blocks = seq // tm
    row_spec = lambda w: pl.BlockSpec((tm, w), lambda i: (i, 0))
    full_spec = lambda a: pl.BlockSpec(a.shape, lambda i: (0, 0))
    att_spec = pl.BlockSpec((1, ATT_W, tm), lambda i: (i // n_seq_blocks, 0, i % n_seq_blocks))
    vec = lambda a: a.reshape(1, d).astype(_F32)
    args = (x2d, att_t, hgo2d, w_o.astype(_BF16), vec(g1), vec(b1),
            w_up.astype(_BF16), w_down.astype(_BF16), vec(g2), vec(b2))
    in_specs = [row_spec(d), att_spec, row_spec(HG_W)] + [full_spec(a) for a in args[3:]]
    return pl.pallas_call(
        functools.partial(_ffn_kernel, alpha=alpha),
        grid=(rows // tm,),
        in_specs=in_specs,
        out_specs=row_spec(d),
        out_shape=jax.ShapeDtypeStruct((rows, d), _F32),
        compiler_params=pltpu.CompilerParams(
            dimension_semantics=("arbitrary",), vmem_limit_bytes=VMEM_LIMIT_BYTES),
    )(*args)


def kernel(x, w_in, w_o, lb_logits, hg_norm_g, ln1_g, ln1_b, w_up, w_down, ln2_g, ln2_b):
    bsz, seq, d = x.shape
    depth = w_in.shape[0]
    alpha = (2.0 * depth) ** 0.25
    x2d = x.reshape(bsz * seq, d)
    for l in range(depth):
        q, k, v, iq, ik, iw, hq, hf, hi, hg = _project(x2d, w_in[l], seq)
        r3 = lambda a: a.reshape(bsz, seq, a.shape[-1])
        att_t = _dsa_attention(r3(q), r3(k), r3(v), r3(iq), r3(ik), r3(iw))
        hgo = _hgrn2(r3(hq), r3(hf), r3(hi), r3(hg), lb_logits, hg_norm_g[l], l)
        x2d = _out_ffn(x2d, att_t, hgo.reshape(bsz * seq, HG_W),
                       w_o[l], ln1_g[l], ln1_b[l], w_up[l], w_down[l], ln2_g[l], ln2_b[l], alpha)
    return x2d.reshape(bsz, seq, d)
```

```python
import functools
import math

import numpy as np
import jax
import jax.numpy as jnp
from jax import lax
from jax.experimental import pallas as pl
from jax.experimental.pallas import tpu as pltpu

ATT_HEAD_DIM = 64
ATT_HEADS = 8
ATT_W = ATT_HEADS * ATT_HEAD_DIM
IDX_HEADS = 4
IDX_DIM = 64
IDX_W = IDX_HEADS * IDX_DIM
TOPK_MAX = 256
HG_KDIM = 128
HG_HEADS = 4
HG_W = HG_HEADS * HG_KDIM
ROPE_THETA = 10000.0
LN_EPS = 1e-5
RMS_EPS = 1e-6

LANES = 128
SUBLANES = 8
BF16_ROWS = 16
VMEM_LIMIT_BYTES = 56 * 1024 * 1024

PROJ_ROWS = 256
DSA_QB = 256
DSA_KB = 512
V_ROWS = ATT_HEAD_DIM + BF16_ROWS
COUNT_ACCS = 4
HG_ROWS = 256
HG_CHUNK = 64
HG_SUB = 16
FFN_ROWS = 256
FFN_COLS = 1024

_F32 = jnp.float32
_BF16 = jnp.bfloat16
_I32 = jnp.int32
_I16 = jnp.int16
_INT_MIN = -(2 ** 31)
_I16_MIN = -(2 ** 15)
_NEG = -1e30


def _nt_dot(a, b):
    return lax.dot_general(a, b, (((1,), (1,)), ((), ())), preferred_element_type=_F32)


def _tn_dot(a, b):
    return lax.dot_general(a, b, (((0,), (0,)), ((), ())), preferred_element_type=_F32)


def _dot(a, b):
    return jnp.dot(a, b, preferred_element_type=_F32)


def _tree_sum(parts):
    while len(parts) > 1:
        parts = [parts[n] + parts[n + 1] for n in range(0, len(parts) - 1, 2)] + (
            [parts[-1]] if len(parts) % 2 else [])
    return parts[0]


def _rope_group(z, cos, sin_signed):
    lane = lax.broadcasted_iota(_I32, z.shape, 1)
    first_half = (lane % ATT_HEAD_DIM) < (ATT_HEAD_DIM // 2)
    upper = pltpu.roll(z, LANES - ATT_HEAD_DIM // 2, 1)
    lower = pltpu.roll(z, ATT_HEAD_DIM // 2, 1)
    return z * cos + jnp.where(first_half, upper, lower) * sin_signed


def _proj_kernel(x_ref, wa_ref, wh_ref, cos_ref, sin_ref,
                 q_ref, k_ref, v_ref, iq_ref, ik_ref, iw_ref,
                 hq_ref, hf_ref, hi_ref, hg_ref, *, q_scale, iw_scale):
    xb = x_ref[...].astype(_BF16)
    cos = cos_ref[...]
    sin = sin_ref[...]
    pa = _dot(xb, wa_ref[...])

    def roped(col0, width):
        return [_rope_group(pa[:, col0 + g * LANES: col0 + (g + 1) * LANES], cos, sin)
                for g in range(width // LANES)]

    for g, z in enumerate(roped(0, ATT_W)):
        q_ref[:, g * LANES:(g + 1) * LANES] = (z * q_scale).astype(_BF16)
    for g, z in enumerate(roped(ATT_W, ATT_W)):
        k_ref[:, g * LANES:(g + 1) * LANES] = z.astype(_BF16)
    v_ref[...] = pa[:, 2 * ATT_W:3 * ATT_W].astype(_BF16)
    for g, z in enumerate(roped(3 * ATT_W, IDX_W)):
        iq_ref[:, g * LANES:(g + 1) * LANES] = z.astype(_BF16)
    ik_ref[...] = roped(3 * ATT_W + IDX_W, LANES)[0].astype(_BF16)
    iw_ref[...] = pa[:, 3 * ATT_W + IDX_W + LANES:] * iw_scale

    ph = _dot(xb, wh_ref[...])
    hq_ref[...] = ph[:, 0 * HG_W:1 * HG_W]
    hf_ref[...] = ph[:, 1 * HG_W:2 * HG_W]
    hi_ref[...] = ph[:, 2 * HG_W:3 * HG_W]
    hg_ref[...] = ph[:, 3 * HG_W:4 * HG_W]


def _rope_tables(seq):
    half = ATT_HEAD_DIM // 2
    inv = np.power(np.float64(ROPE_THETA), -np.arange(half, dtype=np.float64) / half)
    ang = np.arange(seq, dtype=np.float64)[:, None] * inv[None, :]
    cos = np.cos(ang)
    sin = np.sin(ang)
    cos_t = np.tile(np.concatenate([cos, cos], axis=1), (1, LANES // ATT_HEAD_DIM))
    sin_t = np.tile(np.concatenate([-sin, sin], axis=1), (1, LANES // ATT_HEAD_DIM))
    return jnp.asarray(cos_t, _F32), jnp.asarray(sin_t, _F32)


def _project(x2d, w_in, seq):
    rows, d = x2d.shape
    c = 3 * ATT_W + IDX_W
    w_ik = w_in[:, c:c + IDX_DIM]
    w_iw = w_in[:, c + IDX_DIM:c + IDX_DIM + IDX_HEADS]
    wa = jnp.concatenate([w_in[:, :c], w_ik, w_ik,
                          jnp.pad(w_iw, ((0, 0), (0, LANES - IDX_HEADS)))],
                         axis=1).astype(_BF16)
    wh = w_in[:, c + IDX_DIM + IDX_HEADS:].astype(_BF16)
    cos_t, sin_t = _rope_tables(seq)
    tm = PROJ_ROWS
    n_seq_blocks = seq // tm
    row_spec = lambda w: pl.BlockSpec((tm, w), lambda i: (i, 0))
    full_spec = lambda a: pl.BlockSpec(a.shape, lambda i: (0, 0))
    pos_spec = pl.BlockSpec((tm, LANES), lambda i: (i % n_seq_blocks, 0))
    out_shapes = [
        jax.ShapeDtypeStruct((rows, ATT_W), _BF16),
        jax.ShapeDtypeStruct((rows, ATT_W), _BF16),
        jax.ShapeDtypeStruct((rows, ATT_W), _BF16),
        jax.ShapeDtypeStruct((rows, IDX_W), _BF16),
        jax.ShapeDtypeStruct((rows, LANES), _BF16),
        jax.ShapeDtypeStruct((rows, LANES), _F32),
    ] + [jax.ShapeDtypeStruct((rows, HG_W), _F32)] * 4
    out_specs = [row_spec(s.shape[1]) for s in out_shapes]
    kern = functools.partial(_proj_kernel, q_scale=ATT_HEAD_DIM ** -0.5 * math.log2(math.e),
                             iw_scale=(IDX_HEADS ** -0.5) * (IDX_DIM ** -0.5))
    return pl.pallas_call(
        kern,
        grid=(rows // tm,),
        in_specs=[row_spec(d), full_spec(wa), full_spec(wh), pos_spec, pos_spec],
        out_specs=out_specs,
        out_shape=out_shapes,
        compiler_params=pltpu.CompilerParams(
            dimension_semantics=("arbitrary",), vmem_limit_bytes=VMEM_LIMIT_BYTES),
    )(x2d, wa, wh, cos_t, sin_t)


def _dsa_kernel(qi_ref, kj_ref, ka_ref, kb_ref,
                qt_ref, iqt_ref, iwt_ref, ik_ref, before_ref, k_ref, vt_ref,
                o_ref,
                hi_scr, lo_scr, bias_scr, m_scr, acc_scr, s_even, s_odd, bm_even, bm_odd,
                *, topk):
    del ka_ref, kb_ref
    p = pl.program_id(1)
    i = qi_ref[p]
    j = kj_ref[p]
    qb, kb = DSA_QB, DSA_KB
    n_kb = (i * qb + qb - 1) // kb + 1
    first_head = lax.broadcasted_iota(_I32, (LANES, qb), 0) < ATT_HEAD_DIM

    def one_head(pair_rows, h):
        keep = first_head if h % 2 == 0 else jnp.logical_not(first_head)
        return jnp.where(keep, pair_rows, jnp.zeros_like(pair_rows))

    @pl.when(j == 0)
    def _select():
        iwt = iwt_ref[0]
        qpos = i * qb + lax.broadcasted_iota(_I32, (kb, qb), 1)
        krow = lax.broadcasted_iota(_I32, (kb, qb), 0)
        iq_heads = [one_head(iqt_ref[0, (h // 2) * LANES:(h // 2 + 1) * LANES, :], h)
                    for h in range(IDX_HEADS)]

        def score_block(c, carry):
            row0 = pl.multiple_of(c * kb, kb)
            ikc = ik_ref[0, pl.ds(row0, kb), :]
            score = jnp.zeros((kb, qb), _F32)
            for h in range(IDX_HEADS):
                logits = _dot(ikc, iq_heads[h])
                score = score + iwt[h:h + 1, :] * jnp.maximum(logits, 0.0)
            bits = lax.bitcast_convert_type(score, _I32)
            key = bits ^ ((bits >> 31) & 0x7FFFFFFF)
            key = jnp.where(row0 + krow <= qpos, key, _INT_MIN)
            hi_scr[c] = (key >> 16).astype(_I16)
            lo_scr[c] = ((key & 0xFFFF) + _I16_MIN).astype(_I16)
            return carry

        lax.fori_loop(0, n_kb, score_block, 0)

        def count_ge(ref, cand):
            cand_rows = jnp.broadcast_to(cand.astype(_I16), (BF16_ROWS, qb))
            one, zero = jnp.int16(1), jnp.int16(0)

            def body(c, accs):
                accs = list(accs)
                for r in range(kb // BF16_ROWS):
                    tile = ref[c, r * BF16_ROWS:(r + 1) * BF16_ROWS, :]
                    accs[r % len(accs)] = accs[r % len(accs)] + jnp.where(tile >= cand_rows, one, zero)
                return tuple(accs)

            accs = lax.fori_loop(0, n_kb, body, (jnp.zeros((BF16_ROWS, qb), _I16),) * COUNT_ACCS)
            return jnp.sum(_tree_sum(list(accs)).astype(_I32), axis=0, keepdims=True)

        def bisect(ref, n_above):
            def body(step, carry):
                val, n_next = carry
                bit = jnp.left_shift(jnp.int32(1), 15 - step)
                cand = jnp.where(step == 0, 0, val | bit)
                cnt = n_above + count_ge(ref, cand)
                ok = cnt >= topk
                return jnp.where(ok, cand, val), jnp.where(ok, n_next, cnt)
            return lax.fori_loop(0, 16, body, (jnp.full((1, qb), _I16_MIN, _I32), n_above))

        t_hi, n_gt_hi = bisect(hi_scr, jnp.zeros((1, qb), _I32))
        t_hi_rows = jnp.broadcast_to(t_hi.astype(_I16), (kb, qb))

        def bucket_block(c, carry):
            lo_scr[c] = jnp.where(hi_scr[c] == t_hi_rows, lo_scr[c], jnp.int16(_I16_MIN))
            return carry

        lax.fori_loop(0, n_kb, bucket_block, 0)
        t_lo, n_gt = bisect(lo_scr, n_gt_hi)
        t_lo = jnp.where((t_hi == _I16_MIN) & (t_lo == _I16_MIN), _I16_MIN + 1, t_lo)
        t_lo_rows = jnp.broadcast_to(t_lo.astype(_I16), (kb, qb))
        rem_rows = jnp.broadcast_to((topk - n_gt).astype(_I16), (kb, qb))

        def mask_block(c, seen):
            hi = hi_scr[c]
            lo = lo_scr[c]
            tie = (hi == t_hi_rows) & (lo == t_lo_rows)
            tie_count = jnp.where(tie, jnp.asarray(1, _BF16), jnp.asarray(0, _BF16))
            ties_before = seen + _dot(before_ref[...], tie_count)
            allowed = ties_before.astype(_I32).astype(_I16) < rem_rows
            chosen = (hi > t_hi_rows) | (lo > t_lo_rows) | (tie & allowed)
            bias_scr[c] = jnp.where(chosen, jnp.asarray(0, bias_scr.dtype),
                                    jnp.asarray(_NEG, bias_scr.dtype))
            last = ((hi[kb - 1:kb].astype(_I32) == t_hi) & (lo[kb - 1:kb].astype(_I32) == t_lo))
            return ties_before[kb - 1:kb] + jnp.where(last, 1.0, 0.0)

        lax.fori_loop(0, n_kb, mask_block, jnp.zeros((1, qb), _F32))

        m_scr[...] = jnp.full_like(m_scr, _NEG)
        acc_scr[...] = jnp.zeros_like(acc_scr)
        s_odd[...] = jnp.full_like(s_odd, _NEG)
        bm_odd[...] = jnp.full_like(bm_odd, _NEG)

    def step(s_write, bm_write, s_read, bm_read):
        bias = bias_scr[jnp.minimum(j, n_kb - 1)].astype(_F32)
        for h in range(ATT_HEADS):
            pair = slice((h // 2) * LANES, (h // 2 + 1) * LANES)
            s = _dot(k_ref[0, :, pair], one_head(qt_ref[0, pair, :], h)) + bias
            s_write[h] = s
            bm_write[h] = jnp.max(s, axis=0, keepdims=True)

            m_old = m_scr[h]
            m_new = jnp.maximum(m_old, bm_read[h])
            pr = jnp.exp2(s_read[h] - m_new).astype(_BF16)
            acc_scr[h] = jnp.exp2(m_old - m_new) * acc_scr[h] + _dot(vt_ref[0, h], pr)
            m_scr[h] = m_new

    @pl.when(j % 2 == 0)
    def _even():
        step(s_even, bm_even, s_odd, bm_odd)

    @pl.when(j % 2 == 1)
    def _odd():
        step(s_odd, bm_odd, s_even, bm_even)

    @pl.when(j == n_kb)
    def _finish():
        for h in range(ATT_HEADS):
            a = acc_scr[h]
            o_ref[0, h * ATT_HEAD_DIM:(h + 1) * ATT_HEAD_DIM, :] = (
                a[:ATT_HEAD_DIM] / a[ATT_HEAD_DIM:ATT_HEAD_DIM + 1]).astype(o_ref.dtype)


def _dsa_attention(q, k, v, iq, ik, iw):
    bsz, seq, _ = q.shape
    qb, kb = DSA_QB, DSA_KB
    topk = min(TOPK_MAX, seq // 4)
    n_qb = seq // qb
    n_kb = lambda i: (i * qb + qb - 1) // kb + 1
    pairs = [(i, j) for i in range(n_qb) for j in range(n_kb(i) + 1)]
    as_i32 = lambda vals: jnp.asarray(np.array(vals, np.int32))
    qi = as_i32([i for i, j in pairs])
    kj = as_i32([j for i, j in pairs])
    k_stage_a = as_i32([min(j, n_kb(i) - 1) for i, j in pairs])
    k_stage_b = as_i32([max(j - 1, 0) for i, j in pairs])
    qt = jnp.swapaxes(q, 1, 2)
    iqt = jnp.swapaxes(iq, 1, 2)
    iwt = jnp.swapaxes(iw[:, :, :SUBLANES], 1, 2)
    vt = jnp.swapaxes(v.reshape(bsz, seq, ATT_HEADS, ATT_HEAD_DIM), 1, 3)
    vt = jnp.swapaxes(vt, 1, 2)
    extra = jnp.zeros((bsz, ATT_HEADS, V_ROWS - ATT_HEAD_DIM, seq), v.dtype).at[:, :, 0].set(1)
    vt = jnp.concatenate([vt, extra], axis=2)
    before = jnp.asarray(np.tril(np.ones((kb, kb), np.float32), -1), _BF16)
    q_map = lambda b, p, qi_r, kj_r, ka_r, kb_r: (b, 0, qi_r[p])
    grid_spec = pltpu.PrefetchScalarGridSpec(
        num_scalar_prefetch=4,
        grid=(bsz, len(pairs)),
        in_specs=[
            pl.BlockSpec((1, ATT_W, qb), q_map),
            pl.BlockSpec((1, IDX_W, qb), q_map),
            pl.BlockSpec((1, SUBLANES, qb), q_map),
            pl.BlockSpec((1, seq, LANES), lambda b, p, *_: (b, 0, 0)),
            pl.BlockSpec((kb, kb), lambda b, p, *_: (0, 0)),
            pl.BlockSpec((1, kb, ATT_W), lambda b, p, qi_r, kj_r, ka_r, kb_r: (b, ka_r[p], 0)),
            pl.BlockSpec((1, ATT_HEADS, V_ROWS, kb),
                         lambda b, p, qi_r, kj_r, ka_r, kb_r: (b, 0, 0, kb_r[p])),
        ],
        out_specs=pl.BlockSpec((1, ATT_W, qb), q_map),
        scratch_shapes=[
            pltpu.VMEM((seq // kb, kb, qb), _I16),
            pltpu.VMEM((seq // kb, kb, qb), _I16),
            pltpu.VMEM((seq // kb, kb, qb), _BF16),
            pltpu.VMEM((ATT_HEADS, 1, qb), _F32),
            pltpu.VMEM((ATT_HEADS, V_ROWS, qb), _F32),
            pltpu.VMEM((ATT_HEADS, kb, qb), _F32),
            pltpu.VMEM((ATT_HEADS, kb, qb), _F32),
            pltpu.VMEM((ATT_HEADS, 1, qb), _F32),
            pltpu.VMEM((ATT_HEADS, 1, qb), _F32),
        ],
    )
    return pl.pallas_call(
        functools.partial(_dsa_kernel, topk=topk),
        grid_spec=grid_spec,
        out_shape=jax.ShapeDtypeStruct((bsz, ATT_W, seq), _BF16),
        compiler_params=pltpu.CompilerParams(
            dimension_semantics=("arbitrary", "arbitrary"), vmem_limit_bytes=VMEM_LIMIT_BYTES),
    )(qi, kj, k_stage_a, k_stage_b, qt, iqt, iwt, ik, before, k, vt)


def _split3(a):
    hi = a.astype(_BF16)
    r1 = a - hi.astype(_F32)
    mid = r1.astype(_BF16)
    lo = (r1 - mid.astype(_F32)).astype(_BF16)
    return hi, mid, lo


def _hgrn_kernel(lbl_ref, g_ref, hq_ref, hf_ref, hi_ref, hg_ref, o_ref,
                 state_scr, kk_scr, b_scr, o_scr, *, layer):
    rows, ch, sb = HG_ROWS, HG_CHUNK, HG_SUB
    n_sub = ch // sb

    @pl.when(pl.program_id(1) == 0)
    def _reset():
        state_scr[...] = jnp.zeros_like(state_scr)

    lbl = lbl_ref[...]
    e = jnp.exp(lbl - jnp.max(lbl, axis=0, keepdims=True))
    lb = jnp.sum(e[:layer + 1], axis=0, keepdims=True) / jnp.sum(e, axis=0, keepdims=True)

    f = lb + (1.0 - lb) * jax.nn.sigmoid(hf_ref[0])
    kk_scr[...] = 1.0 - f
    logf = jnp.log(f)
    r_i = lax.broadcasted_iota(_I32, (ch, ch), 0)
    c_i = lax.broadcasted_iota(_I32, (ch, ch), 1)
    lower = jnp.where(c_i <= r_i, 1.0, 0.0).astype(_BF16)
    for c in range(rows // ch):
        parts = _split3(logf[c * ch:(c + 1) * ch])
        b_scr[c * ch:(c + 1) * ch, :] = sum(_dot(lower, part) for part in parts)

    t_idx = lax.broadcasted_iota(_I32, (sb, 1), 0)
    row_idx = lax.broadcasted_iota(_I32, (ch, 1), 0)

    def chunk(c, carry):
        r0 = pl.multiple_of(c * ch, ch)
        cs = pl.ds(r0, ch)
        b = b_scr[cs, :]
        kk = kk_scr[cs, :]
        qv = hq_ref[0, cs, :]
        vv = hi_ref[0, cs, :]
        b_last = b[ch - 1:ch]
        q_in = (qv * jnp.exp(b)).astype(_BF16)
        k_out = kk * jnp.exp(b_last - b)
        vb = vv.astype(_BF16)

        a_off = [[jnp.zeros((sb, ch), _F32)] for _ in range(HG_HEADS)]
        for s_i in range(1, n_sub):
            ref_b = b[s_i * sb - 1:s_i * sb]
            q_s = (qv[s_i * sb:(s_i + 1) * sb] * jnp.exp(b[s_i * sb:(s_i + 1) * sb] - ref_b))
            k_s = jnp.where(row_idx < s_i * sb, kk * jnp.exp(jnp.minimum(ref_b - b, 0.0)), 0.0)
            q_s = q_s.astype(_BF16)
            k_s = k_s.astype(_BF16)
            for h in range(HG_HEADS):
                hs = slice(h * HG_KDIM, (h + 1) * HG_KDIM)
                a_off[h].append(_nt_dot(q_s[:, hs], k_s[:, hs]))

        for h in range(HG_HEADS):
            hs = slice(h * HG_KDIM, (h + 1) * HG_KDIM)
            st = state_scr[h]
            o_h = _nt_dot(q_in[:, hs], st.astype(_BF16))
            a_h = jnp.concatenate(a_off[h], axis=0).astype(_BF16)
            o_scr[cs, hs] = o_h + _dot(a_h, vb[:, hs])
            state_scr[h] = (st * jnp.exp(b_last[:, hs])
                            + _tn_dot(vb[:, hs], k_out[:, hs].astype(_BF16)))

        for s_i in range(n_sub):
            rs = pl.ds(r0 + s_i * sb, sb)
            q_s = hq_ref[0, rs, :]
            b_s = b_scr[rs, :]
            acc = o_scr[rs, :]
            for t in range(sb):
                one = pl.ds(r0 + s_i * sb + t, 1)
                w = q_s * jnp.exp(jnp.minimum(b_s - b_scr[one, :], 0.0)) * kk_scr[one, :]
                v_row = hi_ref[0, one, :]
                parts = []
                for h in range(HG_HEADS):
                    hs = slice(h * HG_KDIM, (h + 1) * HG_KDIM)
                    a = jnp.sum(w[:, hs], axis=1, keepdims=True)
                    parts.append(jnp.where(t_idx >= t, a, 0.0) * v_row[:, hs])
                acc = acc + jnp.concatenate(parts, axis=1)
            o_scr[rs, :] = acc
        return carry

    lax.fori_loop(0, rows // ch, chunk, 0)

    o = o_scr[...]
    gate = hg_ref[0]
    gain = g_ref[...]
    for h in range(HG_HEADS):
        hs = slice(h * HG_KDIM, (h + 1) * HG_KDIM)
        oh = o[:, hs]
        oh = oh * lax.rsqrt(jnp.mean(oh * oh, axis=1, keepdims=True) + RMS_EPS) * gain[:, hs]
        gh = gate[:, hs]
        o_ref[0, :, hs] = (oh * (gh * jax.nn.sigmoid(gh))).astype(o_ref.dtype)


def _hgrn2(hq, hf, hi, hg, lb_logits, norm_g, layer):
    bsz, seq, _ = hq.shape
    rows = HG_ROWS
    blk = pl.BlockSpec((1, rows, HG_W), lambda b, t: (b, t, 0))
    lbl = lb_logits.reshape(lb_logits.shape[0], HG_W)
    gain = norm_g.reshape(1, HG_W)
    return pl.pallas_call(
        functools.partial(_hgrn_kernel, layer=layer),
        grid=(bsz, seq // rows),
        in_specs=[pl.BlockSpec(lbl.shape, lambda b, t: (0, 0)),
                  pl.BlockSpec(gain.shape, lambda b, t: (0, 0)),
                  blk, blk, blk, blk],
        out_specs=blk,
        out_shape=jax.ShapeDtypeStruct((bsz, seq, HG_W), _BF16),
        scratch_shapes=[
            pltpu.VMEM((HG_HEADS, HG_KDIM, HG_KDIM), _F32),
            pltpu.VMEM((rows, HG_W), _F32),
            pltpu.VMEM((rows, HG_W), _F32),
            pltpu.VMEM((rows, HG_W), _F32),
        ],
        compiler_params=pltpu.CompilerParams(
            dimension_semantics=("arbitrary", "arbitrary"), vmem_limit_bytes=VMEM_LIMIT_BYTES),
    )(lbl, gain, hq, hf, hi, hg)


def _layer_norm(y, g, b):
    mu = jnp.mean(y, axis=1, keepdims=True)
    yc = y - mu
    var = jnp.mean(yc * yc, axis=1, keepdims=True)
    return yc * lax.rsqrt(var + LN_EPS) * g + b


def _ffn_kernel(x_ref, att_ref, hgo_ref, wo_ref, g1_ref, b1_ref, wu_ref, wd_ref,
                g2_ref, b2_ref, o_ref, *, alpha):
    mix = (_tn_dot(att_ref[0], wo_ref[:ATT_W, :]) + _dot(hgo_ref[...], wo_ref[ATT_W:, :]))
    y1 = _layer_norm(alpha * x_ref[...] + mix, g1_ref[...], b1_ref[...])
    y1b = y1.astype(_BF16)
    h = jnp.zeros_like(y1)
    for c in range(wu_ref.shape[1] // FFN_COLS):
        cs = slice(c * FFN_COLS, (c + 1) * FFN_COLS)
        u = jnp.maximum(_dot(y1b, wu_ref[:, cs]), 0.0)
        h = h + _dot((u * u).astype(_BF16), wd_ref[cs, :])
    o_ref[...] = _layer_norm(alpha * y1 + h, g2_ref[...], b2_ref[...])


def _out_ffn(x2d, att_t, hgo2d, w_o, g1, b1, w_up, w_down, g2, b2, alpha):
    rows, d = x2d.shape
    seq = att_t.shape[2]
    tm = FFN_ROWS
    n_seq_blocks = seq // tm
    row_spec = lambda w: pl.BlockSpec((tm, w), lambda i: (i, 0))
    full_spec = lambda a: pl.BlockSpec(a.shape, lambda i: (0, 0))
    att_spec = pl.BlockSpec((1, ATT_W, tm), lambda i: (i // n_seq_blocks, 0, i % n_seq_blocks))
    vec = lambda a: a.reshape(1, d).astype(_F32)
    args = (x2d, att_t, hgo2d, w_o.astype(_BF16), vec(g1), vec(b1),
            w_up.astype(_BF16), w_down.astype(_BF16), vec(g2), vec(b2))
    in_specs = [row_spec(d), att_spec, row_spec(HG_W)] + [full_spec(a) for a in args[3:]]
    return pl.pallas_call(
        functools.partial(_ffn_kernel, alpha=alpha),
        grid=(rows // tm,),
        in_specs=in_specs,
        out_specs=row_spec(d),
        out_shape=jax.ShapeDtypeStruct((rows, d), _F32),
        compiler_params=pltpu.CompilerParams(
            dimension_semantics=("arbitrary",), vmem_limit_bytes=VMEM_LIMIT_BYTES),
    )(*args)


def kernel(x, w_in, w_o, lb_logits, hg_norm_g, ln1_g, ln1_b, w_up, w_down, ln2_g, ln2_b):
    bsz, seq, d = x.shape
    depth = w_in.shape[0]
    alpha = (2.0 * depth) ** 0.25
    x2d = x.reshape(bsz * seq, d)
    for l in range(depth):
        q, k, v, iq, ik, iw, hq, hf, hi, hg = _project(x2d, w_in[l], seq)
        r3 = lambda a: a.reshape(bsz, seq, a.shape[-1])
        att_t = _dsa_attention(r3(q), r3(k), r3(v), r3(iq), r3(ik), r3(iw))
        hgo = _hgrn2(r3(hq), r3(hf), r3(hi), r3(hg), lb_logits, hg_norm_g[l], l)
        x2d = _out_ffn(x2d, att_t, hgo.reshape(bsz * seq, HG_W),
                       w_o[l], ln1_g[l], ln1_b[l], w_up[l], w_down[l], ln2_g[l], ln2_b[l], alpha)
    return x2d.reshape(bsz, seq, d)
```

```python
import functools
import math

import numpy as np
import jax
import jax.numpy as jnp
from jax import lax
from jax.experimental import pallas as pl
from jax.experimental.pallas import tpu as pltpu

ATT_HEAD_DIM = 64
ATT_HEADS = 8
ATT_W = ATT_HEADS * ATT_HEAD_DIM
IDX_HEADS = 4
IDX_DIM = 64
IDX_W = IDX_HEADS * IDX_DIM
TOPK_MAX = 256
HG_KDIM = 128
HG_HEADS = 4
HG_W = HG_HEADS * HG_KDIM
ROPE_THETA = 10000.0
LN_EPS = 1e-5
RMS_EPS = 1e-6

LANES = 128
SUBLANES = 8
BF16_ROWS = 16
VMEM_LIMIT_BYTES = 56 * 1024 * 1024

PROJ_ROWS = 256
DSA_QB = 256
DSA_KB = 512
V_ROWS = ATT_HEAD_DIM + BF16_ROWS
COUNT_ACCS = 4
HG_ROWS = 256
HG_CHUNK = 64
HG_SUB = 16
FFN_ROWS = 256
FFN_COLS = 1024

_F32 = jnp.float32
_BF16 = jnp.bfloat16
_I32 = jnp.int32
_I16 = jnp.int16
_INT_MIN = -(2 ** 31)
_I16_MIN = -(2 ** 15)
_NEG = -1e30


def _nt_dot(a, b):
    return lax.dot_general(a, b, (((1,), (1,)), ((), ())), preferred_element_type=_F32)


def _tn_dot(a, b):
    return lax.dot_general(a, b, (((0,), (0,)), ((), ())), preferred_element_type=_F32)


def _dot(a, b):
    return jnp.dot(a, b, preferred_element_type=_F32)


def _tree_sum(parts):
    while len(parts) > 1:
        parts = [parts[n] + parts[n + 1] for n in range(0, len(parts) - 1, 2)] + (
            [parts[-1]] if len(parts) % 2 else [])
    return parts[0]


def _rope_group(z, cos, sin_signed):
    lane = lax.broadcasted_iota(_I32, z.shape, 1)
    first_half = (lane % ATT_HEAD_DIM) < (ATT_HEAD_DIM // 2)
    upper = pltpu.roll(z, LANES - ATT_HEAD_DIM // 2, 1)
    lower = pltpu.roll(z, ATT_HEAD_DIM // 2, 1)
    return z * cos + jnp.where(first_half, upper, lower) * sin_signed


def _proj_kernel(x_ref, wa_ref, wh_ref, cos_ref, sin_ref,
                 q_ref, k_ref, v_ref, iq_ref, ik_ref, iw_ref,
                 hq_ref, hf_ref, hi_ref, hg_ref, *, q_scale, iw_scale):
    xb = x_ref[...].astype(_BF16)
    cos = cos_ref[...]
    sin = sin_ref[...]
    pa = _dot(xb, wa_ref[...])

    def roped(col0, width):
        return [_rope_group(pa[:, col0 + g * LANES: col0 + (g + 1) * LANES], cos, sin)
                for g in range(width // LANES)]

    for g, z in enumerate(roped(0, ATT_W)):
        q_ref[:, g * LANES:(g + 1) * LANES] = (z * q_scale).astype(_BF16)
    for g, z in enumerate(roped(ATT_W, ATT_W)):
        k_ref[:, g * LANES:(g + 1) * LANES] = z.astype(_BF16)
    v_ref[...] = pa[:, 2 * ATT_W:3 * ATT_W].astype(_BF16)
    for g, z in enumerate(roped(3 * ATT_W, IDX_W)):
        iq_ref[:, g * LANES:(g + 1) * LANES] = z.astype(_BF16)
    ik_ref[...] = roped(3 * ATT_W + IDX_W, LANES)[0].astype(_BF16)
    iw_ref[...] = pa[:, 3 * ATT_W + IDX_W + LANES:] * iw_scale

    ph = _dot(xb, wh_ref[...])
    hq_ref[...] = ph[:, 0 * HG_W:1 * HG_W]
    hf_ref[...] = ph[:, 1 * HG_W:2 * HG_W]
    hi_ref[...] = ph[:, 2 * HG_W:3 * HG_W]
    hg_ref[...] = ph[:, 3 * HG_W:4 * HG_W]


def _rope_tables(seq):
    half = ATT_HEAD_DIM // 2
    inv = np.power(np.float64(ROPE_THETA), -np.arange(half, dtype=np.float64) / half)
    ang = np.arange(seq, dtype=np.float64)[:, None] * inv[None, :]
    cos = np.cos(ang)
    sin = np.sin(ang)
    cos_t = np.tile(np.concatenate([cos, cos], axis=1), (1, LANES // ATT_HEAD_DIM))
    sin_t = np.tile(np.concatenate([-sin, sin], axis=1), (1, LANES // ATT_HEAD_DIM))
    return jnp.asarray(cos_t, _F32), jnp.asarray(sin_t, _F32)


def _project(x2d, w_in, seq):
    rows, d = x2d.shape
    c = 3 * ATT_W + IDX_W
    w_ik = w_in[:, c:c + IDX_DIM]
    w_iw = w_in[:, c + IDX_DIM:c + IDX_DIM + IDX_HEADS]
    wa = jnp.concatenate([w_in[:, :c], w_ik, w_ik,
                          jnp.pad(w_iw, ((0, 0), (0, LANES - IDX_HEADS)))],
                         axis=1).astype(_BF16)
    wh = w_in[:, c + IDX_DIM + IDX_HEADS:].astype(_BF16)
    cos_t, sin_t = _rope_tables(seq)
    tm = PROJ_ROWS
    n_seq_blocks = seq // tm
    row_spec = lambda w: pl.BlockSpec((tm, w), lambda i: (i, 0))
    full_spec = lambda a: pl.BlockSpec(a.shape, lambda i: (0, 0))
    pos_spec = pl.BlockSpec((tm, LANES), lambda i: (i % n_seq_blocks, 0))
    out_shapes = [
        jax.ShapeDtypeStruct((rows, ATT_W), _BF16),
        jax.ShapeDtypeStruct((rows, ATT_W), _BF16),
        jax.ShapeDtypeStruct((rows, ATT_W), _BF16),
        jax.ShapeDtypeStruct((rows, IDX_W), _BF16),
        jax.ShapeDtypeStruct((rows, LANES), _BF16),
        jax.ShapeDtypeStruct((rows, LANES), _F32),
    ] + [jax.ShapeDtypeStruct((rows, HG_W), _F32)] * 4
    out_specs = [row_spec(s.shape[1]) for s in out_shapes]
    kern = functools.partial(_proj_kernel, q_scale=ATT_HEAD_DIM ** -0.5 * math.log2(math.e),
                             iw_scale=(IDX_HEADS ** -0.5) * (IDX_DIM ** -0.5))
    return pl.pallas_call(
        kern,
        grid=(rows // tm,),
        in_specs=[row_spec(d), full_spec(wa), full_spec(wh), pos_spec, pos_spec],
        out_specs=out_specs,
        out_shape=out_shapes,
        compiler_params=pltpu.CompilerParams(
            dimension_semantics=("arbitrary",), vmem_limit_bytes=VMEM_LIMIT_BYTES),
    )(x2d, wa, wh, cos_t, sin_t)


def _dsa_kernel(qi_ref, kj_ref, ka_ref, vp_ref, vc_ref,
                qt_ref, iqt_ref, iwt_ref, ik_ref, before_ref, k_ref, vt_prev_ref, vt_ref,
                o_ref,
                hi_scr, lo_scr, bias_scr, m_scr, acc_scr, s_even, s_odd, bm_even, bm_odd,
                *, topk):
    del ka_ref, vp_ref, vc_ref
    p = pl.program_id(1)
    i = qi_ref[p]
    j = kj_ref[p]
    qb, kb = DSA_QB, DSA_KB
    n_kb = (i * qb + qb - 1) // kb + 1
    n_steps = (i * qb + qb - 1) // (2 * kb) + 1
    first_head = lax.broadcasted_iota(_I32, (LANES, qb), 0) < ATT_HEAD_DIM

    def one_head(pair_rows, h):
        keep = first_head if h % 2 == 0 else jnp.logical_not(first_head)
        return jnp.where(keep, pair_rows, jnp.zeros_like(pair_rows))

    @pl.when(j == 0)
    def _select():
        iwt = iwt_ref[0]
        qpos = i * qb + lax.broadcasted_iota(_I32, (kb, qb), 1)
        krow = lax.broadcasted_iota(_I32, (kb, qb), 0)
        iq_heads = [one_head(iqt_ref[0, (h // 2) * LANES:(h // 2 + 1) * LANES, :], h)
                    for h in range(IDX_HEADS)]

        def score_chunk(c):
            row0 = pl.multiple_of(c * kb, kb)
            ikc = ik_ref[0, pl.ds(row0, kb), :]
            score = jnp.zeros((kb, qb), _F32)
            for h in range(IDX_HEADS):
                logits = _dot(ikc, iq_heads[h])
                score = score + iwt[h:h + 1, :] * jnp.maximum(logits, 0.0)
            bits = lax.bitcast_convert_type(score, _I32)
            key = bits ^ ((bits >> 31) & 0x7FFFFFFF)
            key = jnp.where(row0 + krow <= qpos, key, _INT_MIN)
            hi_scr[c] = (key >> 16).astype(_I16)
            lo_scr[c] = ((key & 0xFFFF) + _I16_MIN).astype(_I16)

        def score_pair(c2, carry):
            score_chunk(2 * c2)
            score_chunk(2 * c2 + 1)
            return carry

        lax.fori_loop(0, n_steps, score_pair, 0)

        def count_ge(ref, cand):
            cand_rows = jnp.broadcast_to(cand.astype(_I16), (BF16_ROWS, qb))
            one, zero = jnp.int16(1), jnp.int16(0)

            def body(c, accs):
                accs = list(accs)
                for r in range(kb // BF16_ROWS):
                    tile = ref[c, r * BF16_ROWS:(r + 1) * BF16_ROWS, :]
                    accs[r % len(accs)] = accs[r % len(accs)] + jnp.where(tile >= cand_rows, one, zero)
                return tuple(accs)

            accs = lax.fori_loop(0, n_kb, body, (jnp.zeros((BF16_ROWS, qb), _I16),) * COUNT_ACCS)
            return jnp.sum(_tree_sum(list(accs)).astype(_I32), axis=0, keepdims=True)

        def bisect(ref, n_above):
            def body(step, carry):
                val, n_next = carry
                bit = jnp.left_shift(jnp.int32(1), 15 - step)
                cand = jnp.where(step == 0, 0, val | bit)
                cnt = n_above + count_ge(ref, cand)
                ok = cnt >= topk
                return jnp.where(ok, cand, val), jnp.where(ok, n_next, cnt)
            return lax.fori_loop(0, 16, body, (jnp.full((1, qb), _I16_MIN, _I32), n_above))

        t_hi, n_gt_hi = bisect(hi_scr, jnp.zeros((1, qb), _I32))
        t_hi_rows = jnp.broadcast_to(t_hi.astype(_I16), (kb, qb))

        def bucket_block(c, carry):
            lo_scr[c] = jnp.where(hi_scr[c] == t_hi_rows, lo_scr[c], jnp.int16(_I16_MIN))
            return carry

        lax.fori_loop(0, n_kb, bucket_block, 0)
        t_lo, n_gt = bisect(lo_scr, n_gt_hi)
        t_lo = jnp.where((t_hi == _I16_MIN) & (t_lo == _I16_MIN), _I16_MIN + 1, t_lo)
        t_lo_rows = jnp.broadcast_to(t_lo.astype(_I16), (kb, qb))
        rem_rows = jnp.broadcast_to((topk - n_gt).astype(_I16), (kb, qb))

        def mask_block(c, seen):
            hi = hi_scr[c]
            lo = lo_scr[c]
            tie = (hi == t_hi_rows) & (lo == t_lo_rows)
            tie_count = jnp.where(tie, jnp.asarray(1, _BF16), jnp.asarray(0, _BF16))
            ties_before = seen + _dot(before_ref[...], tie_count)
            allowed = ties_before.astype(_I32).astype(_I16) < rem_rows
            chosen = (hi > t_hi_rows) | (lo > t_lo_rows) | (tie & allowed)
            bias_scr[c] = jnp.where(chosen, jnp.asarray(0, bias_scr.dtype),
                                    jnp.asarray(_NEG, bias_scr.dtype))
            last = ((hi[kb - 1:kb].astype(_I32) == t_hi) & (lo[kb - 1:kb].astype(_I32) == t_lo))
            return ties_before[kb - 1:kb] + jnp.where(last, 1.0, 0.0)

        lax.fori_loop(0, 2 * n_steps, mask_block, jnp.zeros((1, qb), _F32))

        m_scr[...] = jnp.full_like(m_scr, _NEG)
        acc_scr[...] = jnp.zeros_like(acc_scr)
        s_odd[...] = jnp.full_like(s_odd, _NEG)
        bm_odd[...] = jnp.full_like(bm_odd, _NEG)

    def phase(k_rows, chunk, s_write, bm_write, s_read, bm_read, vt_read):
        bias = bias_scr[chunk].astype(_F32)
        for h in range(ATT_HEADS):
            pair = slice((h // 2) * LANES, (h // 2 + 1) * LANES)
            s = _dot(k_ref[0, k_rows, pair], one_head(qt_ref[0, pair, :], h)) + bias
            s_write[h] = s
            bm_write[h] = jnp.max(s, axis=0, keepdims=True)

            m_old = m_scr[h]
            m_new = jnp.maximum(m_old, bm_read[h])
            pr = jnp.exp2(s_read[h] - m_new).astype(_BF16)
            acc_scr[h] = jnp.exp2(m_old - m_new) * acc_scr[h] + _dot(vt_read[0, h], pr)
            m_scr[h] = m_new

    last_chunk = 2 * n_steps - 1
    phase(slice(0, kb), jnp.minimum(2 * j, last_chunk), s_even, bm_even, s_odd, bm_odd, vt_prev_ref)

    @pl.when(j < n_steps)
    def _second():
        phase(slice(kb, 2 * kb), 2 * j + 1, s_odd, bm_odd, s_even, bm_even, vt_ref)

    @pl.when(j == n_steps)
    def _finish():
        for h in range(ATT_HEADS):
            a = acc_scr[h]
            o_ref[0, h * ATT_HEAD_DIM:(h + 1) * ATT_HEAD_DIM, :] = (
                a[:ATT_HEAD_DIM] / a[ATT_HEAD_DIM:ATT_HEAD_DIM + 1]).astype(o_ref.dtype)


def _dsa_attention(q, k, v, iq, ik, iw):
    bsz, seq, _ = q.shape
    qb, kb = DSA_QB, DSA_KB
    topk = min(TOPK_MAX, seq // 4)
    n_qb = seq // qb
    n_steps = lambda i: (i * qb + qb - 1) // (2 * kb) + 1
    pairs = [(i, j) for i in range(n_qb) for j in range(n_steps(i) + 1)]
    as_i32 = lambda vals: jnp.asarray(np.array(vals, np.int32))
    qi = as_i32([i for i, j in pairs])
    kj = as_i32([j for i, j in pairs])
    k_pair = as_i32([min(j, n_steps(i) - 1) for i, j in pairs])
    v_prev = as_i32([max(2 * j - 1, 0) for i, j in pairs])
    v_this = as_i32([min(2 * j, 2 * n_steps(i) - 1) for i, j in pairs])
    qt = jnp.swapaxes(q, 1, 2)
    iqt = jnp.swapaxes(iq, 1, 2)
    iwt = jnp.swapaxes(iw[:, :, :SUBLANES], 1, 2)
    vt = jnp.swapaxes(v.reshape(bsz, seq, ATT_HEADS, ATT_HEAD_DIM), 1, 3)
    vt = jnp.swapaxes(vt, 1, 2)
    extra = jnp.zeros((bsz, ATT_HEADS, V_ROWS - ATT_HEAD_DIM, seq), v.dtype).at[:, :, 0].set(1)
    vt = jnp.concatenate([vt, extra], axis=2)
    before = jnp.asarray(np.tril(np.ones((kb, kb), np.float32), -1), _BF16)
    q_map = lambda b, p, qi_r, *_: (b, 0, qi_r[p])
    vt_spec = lambda which: pl.BlockSpec(
        (1, ATT_HEADS, V_ROWS, kb), lambda b, p, *refs: (b, 0, 0, refs[which][p]))
    grid_spec = pltpu.PrefetchScalarGridSpec(
        num_scalar_prefetch=5,
        grid=(bsz, len(pairs)),
        in_specs=[
            pl.BlockSpec((1, ATT_W, qb), q_map),
            pl.BlockSpec((1, IDX_W, qb), q_map),
            pl.BlockSpec((1, SUBLANES, qb), q_map),
            pl.BlockSpec((1, seq, LANES), lambda b, p, *_: (b, 0, 0)),
            pl.BlockSpec((kb, kb), lambda b, p, *_: (0, 0)),
            pl.BlockSpec((1, 2 * kb, ATT_W), lambda b, p, *refs: (b, refs[2][p], 0)),
            vt_spec(3),
            vt_spec(4),
        ],
        out_specs=pl.BlockSpec((1, ATT_W, qb), q_map),
        scratch_shapes=[
            pltpu.VMEM((seq // kb, kb, qb), _I16),
            pltpu.VMEM((seq // kb, kb, qb), _I16),
            pltpu.VMEM((seq // kb, kb, qb), _BF16),
            pltpu.VMEM((ATT_HEADS, 1, qb), _F32),
            pltpu.VMEM((ATT_HEADS, V_ROWS, qb), _F32),
            pltpu.VMEM((ATT_HEADS, kb, qb), _F32),
            pltpu.VMEM((ATT_HEADS, kb, qb), _F32),
            pltpu.VMEM((ATT_HEADS, 1, qb), _F32),
            pltpu.VMEM((ATT_HEADS, 1, qb), _F32),
        ],
    )
    return pl.pallas_call(
        functools.partial(_dsa_kernel, topk=topk),
        grid_spec=grid_spec,
        out_shape=jax.ShapeDtypeStruct((bsz, ATT_W, seq), _BF16),
        compiler_params=pltpu.CompilerParams(
            dimension_semantics=("arbitrary", "arbitrary"), vmem_limit_bytes=VMEM_LIMIT_BYTES),
    )(qi, kj, k_pair, v_prev, v_this, qt, iqt, iwt, ik, before, k, vt, vt)


def _split3(a):
    hi = a.astype(_BF16)
    r1 = a - hi.astype(_F32)
    mid = r1.astype(_BF16)
    lo = (r1 - mid.astype(_F32)).astype(_BF16)
    return hi, mid, lo


def _hgrn_kernel(lbl_ref, g_ref, hq_ref, hf_ref, hi_ref, hg_ref, o_ref,
                 state_scr, kk_scr, b_scr, o_scr, *, layer):
    rows, ch, sb = HG_ROWS, HG_CHUNK, HG_SUB
    n_sub = ch // sb

    @pl.when(pl.program_id(1) == 0)
    def _reset():
        state_scr[...] = jnp.zeros_like(state_scr)

    lbl = lbl_ref[...]
    e = jnp.exp(lbl - jnp.max(lbl, axis=0, keepdims=True))
    lb = jnp.sum(e[:layer + 1], axis=0, keepdims=True) / jnp.sum(e, axis=0, keepdims=True)

    f = lb + (1.0 - lb) * jax.nn.sigmoid(hf_ref[0])
    kk_scr[...] = 1.0 - f
    logf = jnp.log(f)
    r_i = lax.broadcasted_iota(_I32, (ch, ch), 0)
    c_i = lax.broadcasted_iota(_I32, (ch, ch), 1)
    lower = jnp.where(c_i <= r_i, 1.0, 0.0).astype(_BF16)
    for c in range(rows // ch):
        parts = _split3(logf[c * ch:(c + 1) * ch])
        b_scr[c * ch:(c + 1) * ch, :] = sum(_dot(lower, part) for part in parts)

    t_idx = lax.broadcasted_iota(_I32, (sb, 1), 0)
    row_idx = lax.broadcasted_iota(_I32, (ch, 1), 0)

    def chunk(c, carry):
        r0 = pl.multiple_of(c * ch, ch)
        cs = pl.ds(r0, ch)
        b = b_scr[cs, :]
        kk = kk_scr[cs, :]
        qv = hq_ref[0, cs, :]
        vv = hi_ref[0, cs, :]
        b_last = b[ch - 1:ch]
        q_in = (qv * jnp.exp(b)).astype(_BF16)
        k_out = kk * jnp.exp(b_last - b)
        vb = vv.astype(_BF16)

        a_off = [[jnp.zeros((sb, ch), _F32)] for _ in range(HG_HEADS)]
        for s_i in range(1, n_sub):
            ref_b = b[s_i * sb - 1:s_i * sb]
            q_s = (qv[s_i * sb:(s_i + 1) * sb] * jnp.exp(b[s_i * sb:(s_i + 1) * sb] - ref_b))
            k_s = jnp.where(row_idx < s_i * sb, kk * jnp.exp(jnp.minimum(ref_b - b, 0.0)), 0.0)
            q_s = q_s.astype(_BF16)
            k_s = k_s.astype(_BF16)
            for h in range(HG_HEADS):
                hs = slice(h * HG_KDIM, (h + 1) * HG_KDIM)
                a_off[h].append(_nt_dot(q_s[:, hs], k_s[:, hs]))

        for h in range(HG_HEADS):
            hs = slice(h * HG_KDIM, (h + 1) * HG_KDIM)
            st = state_scr[h]
            o_h = _nt_dot(q_in[:, hs], st.astype(_BF16))
            a_h = jnp.concatenate(a_off[h], axis=0).astype(_BF16)
            o_scr[cs, hs] = o_h + _dot(a_h, vb[:, hs])
            state_scr[h] = (st * jnp.exp(b_last[:, hs])
                            + _tn_dot(vb[:, hs], k_out[:, hs].astype(_BF16)))

        for s_i in range(n_sub):
            rs = pl.ds(r0 + s_i * sb, sb)
            q_s = hq_ref[0, rs, :]
            b_s = b_scr[rs, :]
            acc = o_scr[rs, :]
            for t in range(sb):
                one = pl.ds(r0 + s_i * sb + t, 1)
                w = q_s * jnp.exp(jnp.minimum(b_s - b_scr[one, :], 0.0)) * kk_scr[one, :]
                v_row = hi_ref[0, one, :]
                parts = []
                for h in range(HG_HEADS):
                    hs = slice(h * HG_KDIM, (h + 1) * HG_KDIM)
                    a = jnp.sum(w[:, hs], axis=1, keepdims=True)
                    parts.append(jnp.where(t_idx >= t, a, 0.0) * v_row[:, hs])
                acc = acc + jnp.concatenate(parts, axis=1)
            o_scr[rs, :] = acc
        return carry

    lax.fori_loop(0, rows // ch, chunk, 0)

    o = o_scr[...]
    gate = hg_ref[0]
    gain = g_ref[...]
    for h in range(HG_HEADS):
        hs = slice(h * HG_KDIM, (h + 1) * HG_KDIM)
        oh = o[:, hs]
        oh = oh * lax.rsqrt(jnp.mean(oh * oh, axis=1, keepdims=True) + RMS_EPS) * gain[:, hs]
        gh = gate[:, hs]
        o_ref[0, :, hs] = (oh * (gh * jax.nn.sigmoid(gh))).astype(o_ref.dtype)


def _hgrn2(hq, hf, hi, hg, lb_logits, norm_g, layer):
    bsz, seq, _ = hq.shape
    rows = HG_ROWS
    blk = pl.BlockSpec((1, rows, HG_W), lambda b, t: (b, t, 0))
    lbl = lb_logits.reshape(lb_logits.shape[0], HG_W)
    gain = norm_g.reshape(1, HG_W)
    return pl.pallas_call(
        functools.partial(_hgrn_kernel, layer=layer),
        grid=(bsz, seq // rows),
        in_specs=[pl.BlockSpec(lbl.shape, lambda b, t: (0, 0)),
                  pl.BlockSpec(gain.shape, lambda b, t: (0, 0)),
                  blk, blk, blk, blk],
        out_specs=blk,
        out_shape=jax.ShapeDtypeStruct((bsz, seq, HG_W), _BF16),
        scratch_shapes=[
            pltpu.VMEM((HG_HEADS, HG_KDIM, HG_KDIM), _F32),
            pltpu.VMEM((rows, HG_W), _F32),
            pltpu.VMEM((rows, HG_W), _F32),
            pltpu.VMEM((rows, HG_W), _F32),
        ],
        compiler_params=pltpu.CompilerParams(
            dimension_semantics=("arbitrary", "arbitrary"), vmem_limit_bytes=VMEM_LIMIT_BYTES),
    )(lbl, gain, hq, hf, hi, hg)


def _layer_norm(y, g, b):
    mu = jnp.mean(y, axis=1, keepdims=True)
    yc = y - mu
    var = jnp.mean(yc * yc, axis=1, keepdims=True)
    return yc * lax.rsqrt(var + LN_EPS) * g + b


def _ffn_kernel(x_ref, att_ref, hgo_ref, wo_ref, g1_ref, b1_ref, wu_ref, wd_ref,
                g2_ref, b2_ref, o_ref, *, alpha):
    mix = (_tn_dot(att_ref[0], wo_ref[:ATT_W, :]) + _dot(hgo_ref[...], wo_ref[ATT_W:, :]))
    y1 = _layer_norm(alpha * x_ref[...] + mix, g1_ref[...], b1_ref[...])
    y1b = y1.astype(_BF16)
    h = jnp.zeros_like(y1)
    for c in range(wu_ref.shape[1] // FFN_COLS):
        cs = slice(c * FFN_COLS, (c + 1) * FFN_COLS)
        u = jnp.maximum(_dot(y1b, wu_ref[:, cs]), 0.0)
        h = h + _dot((u * u).astype(_BF16), wd_ref[cs, :])
    o_ref[...] = _layer_norm(alpha * y1 + h, g2_ref[...], b2_ref[...])


def _out_ffn(x2d, att_t, hgo2d, w_o, g1, b1, w_up, w_down, g2, b2, alpha):
    rows, d = x2d.shape
    seq = att_t.shape[2]
    tm = FFN_ROWS
    n_seq_blocks = seq // tm
    row_spec = lambda w: pl.BlockSpec((tm, w), lambda i: (i, 0))
    full_spec = lambda a: pl.BlockSpec(a.shape, lambda i: (0, 0))
    att_spec = pl.BlockSpec((1, ATT_W, tm), lambda i: (i // n_seq_blocks, 0, i % n_seq_blocks))
    vec = lambda a: a.reshape(1, d).astype(_F32)
    args = (x2d, att_t, hgo2d, w_o.astype(_BF16), vec(g1), vec(b1),
            w_up.astype(_BF16), w_down.astype(_BF16), vec(g2), vec(b2))
    in_specs = [row_spec(d), att_spec, row_spec(HG_W)] + [full_spec(a) for a in args[3:]]
    return pl.pallas_call(
        functools.partial(_ffn_kernel, alpha=alpha),
        grid=(rows // tm,),
        in_specs=in_specs,
        out_specs=row_spec(d),
        out_shape=jax.ShapeDtypeStruct((rows, d), _F32),
        compiler_params=pltpu.CompilerParams(
            dimension_semantics=("arbitrary",), vmem_limit_bytes=VMEM_LIMIT_BYTES),
    )(*args)


def kernel(x, w_in, w_o, lb_logits, hg_norm_g, ln1_g, ln1_b, w_up, w_down, ln2_g, ln2_b):
    bsz, seq, d = x.shape
    depth = w_in.shape[0]
    alpha = (2.0 * depth) ** 0.25
    x2d = x.reshape(bsz * seq, d)
    for l in range(depth):
        q, k, v, iq, ik, iw, hq, hf, hi, hg = _project(x2d, w_in[l], seq)
        r3 = lambda a: a.reshape(bsz, seq, a.shape[-1])
        att_t = _dsa_attention(r3(q), r3(k), r3(v), r3(iq), r3(ik), r3(iw))
        hgo = _hgrn2(r3(hq), r3(hf), r3(hi), r3(hg), lb_logits, hg_norm_g[l], l)
        x2d = _out_ffn(x2d, att_t, hgo.reshape(bsz * seq, HG_W),
                       w_o[l], ln1_g[l], ln1_b[l], w_up[l], w_down[l], ln2_g[l], ln2_b[l], alpha)
    return x2d.reshape(bsz, seq, d)
```

```python
import functools
import math

import numpy as np
import jax
import jax.numpy as jnp
from jax import lax
from jax.experimental import pallas as pl
from jax.experimental.pallas import tpu as pltpu

ATT_HEAD_DIM = 64
ATT_HEADS = 8
ATT_W = ATT_HEADS * ATT_HEAD_DIM
IDX_HEADS = 4
IDX_DIM = 64
IDX_W = IDX_HEADS * IDX_DIM
TOPK_MAX = 256
HG_KDIM = 128
HG_HEADS = 4
HG_W = HG_HEADS * HG_KDIM
ROPE_THETA = 10000.0
LN_EPS = 1e-5
RMS_EPS = 1e-6

LANES = 128
SUBLANES = 8
BF16_ROWS = 16
VMEM_LIMIT_BYTES = 56 * 1024 * 1024

PROJ_ROWS = 256
DSA_QB = 256
DSA_KB = 512
V_ROWS = ATT_HEAD_DIM + BF16_ROWS
COUNT_ACCS = 4
CAP_GROUPS = 2
CAP_ROWS = 2 * CAP_GROUPS * BF16_ROWS
CAP_CHUNKS_PER_BLOCK = DSA_KB // CAP_ROWS
HG_ROWS = 256
HG_CHUNK = 64
HG_SUB = 16
FFN_ROWS = 256
FFN_COLS = 1024

_F32 = jnp.float32
_BF16 = jnp.bfloat16
_I32 = jnp.int32
_I16 = jnp.int16
_INT_MIN = -(2 ** 31)
_I16_MIN = -(2 ** 15)
_NEG = -1e30


def _nt_dot(a, b):
    return lax.dot_general(a, b, (((1,), (1,)), ((), ())), preferred_element_type=_F32)


def _tn_dot(a, b):
    return lax.dot_general(a, b, (((0,), (0,)), ((), ())), preferred_element_type=_F32)


def _dot(a, b):
    return jnp.dot(a, b, preferred_element_type=_F32)


def _tree_sum(parts):
    while len(parts) > 1:
        parts = [parts[n] + parts[n + 1] for n in range(0, len(parts) - 1, 2)] + (
            [parts[-1]] if len(parts) % 2 else [])
    return parts[0]


def _rope_group(z, cos, sin_signed):
    lane = lax.broadcasted_iota(_I32, z.shape, 1)
    first_half = (lane % ATT_HEAD_DIM) < (ATT_HEAD_DIM // 2)
    upper = pltpu.roll(z, LANES - ATT_HEAD_DIM // 2, 1)
    lower = pltpu.roll(z, ATT_HEAD_DIM // 2, 1)
    return z * cos + jnp.where(first_half, upper, lower) * sin_signed


def _proj_kernel(x_ref, wa_ref, wh_ref, cos_ref, sin_ref,
                 q_ref, k_ref, v_ref, iq_ref, ik_ref, iw_ref,
                 hq_ref, hf_ref, hi_ref, hg_ref, *, q_scale, iw_scale):
    xb = x_ref[...].astype(_BF16)
    cos = cos_ref[...]
    sin = sin_ref[...]
    pa = _dot(xb, wa_ref[...])

    def roped(col0, width):
        return [_rope_group(pa[:, col0 + g * LANES: col0 + (g + 1) * LANES], cos, sin)
                for g in range(width // LANES)]

    for g, z in enumerate(roped(0, ATT_W)):
        q_ref[:, g * LANES:(g + 1) * LANES] = (z * q_scale).astype(_BF16)
    for g, z in enumerate(roped(ATT_W, ATT_W)):
        k_ref[:, g * LANES:(g + 1) * LANES] = z.astype(_BF16)
    v_ref[...] = pa[:, 2 * ATT_W:3 * ATT_W].astype(_BF16)
    for g, z in enumerate(roped(3 * ATT_W, IDX_W)):
        iq_ref[:, g * LANES:(g + 1) * LANES] = z.astype(_BF16)
    ik_ref[...] = roped(3 * ATT_W + IDX_W, LANES)[0].astype(_BF16)
    iw_ref[...] = pa[:, 3 * ATT_W + IDX_W + LANES:] * iw_scale

    ph = _dot(xb, wh_ref[...])
    hq_ref[...] = ph[:, 0 * HG_W:1 * HG_W]
    hf_ref[...] = ph[:, 1 * HG_W:2 * HG_W]
    hi_ref[...] = ph[:, 2 * HG_W:3 * HG_W]
    hg_ref[...] = ph[:, 3 * HG_W:4 * HG_W]


def _rope_tables(seq):
    half = ATT_HEAD_DIM // 2
    inv = np.power(np.float64(ROPE_THETA), -np.arange(half, dtype=np.float64) / half)
    ang = np.arange(seq, dtype=np.float64)[:, None] * inv[None, :]
    cos = np.cos(ang)
    sin = np.sin(ang)
    cos_t = np.tile(np.concatenate([cos, cos], axis=1), (1, LANES // ATT_HEAD_DIM))
    sin_t = np.tile(np.concatenate([-sin, sin], axis=1), (1, LANES // ATT_HEAD_DIM))
    return jnp.asarray(cos_t, _F32), jnp.asarray(sin_t, _F32)


def _project(x2d, w_in, seq):
    rows, d = x2d.shape
    c = 3 * ATT_W + IDX_W
    w_ik = w_in[:, c:c + IDX_DIM]
    w_iw = w_in[:, c + IDX_DIM:c + IDX_DIM + IDX_HEADS]
    wa = jnp.concatenate([w_in[:, :c], w_ik, w_ik,
                          jnp.pad(w_iw, ((0, 0), (0, LANES - IDX_HEADS)))],
                         axis=1).astype(_BF16)
    wh = w_in[:, c + IDX_DIM + IDX_HEADS:].astype(_BF16)
    cos_t, sin_t = _rope_tables(seq)
    tm = PROJ_ROWS
    n_seq_blocks = seq // tm
    row_spec = lambda w: pl.BlockSpec((tm, w), lambda i: (i, 0))
    full_spec = lambda a: pl.BlockSpec(a.shape, lambda i: (0, 0))
    pos_spec = pl.BlockSpec((tm, LANES), lambda i: (i % n_seq_blocks, 0))
    out_shapes = [
        jax.ShapeDtypeStruct((rows, ATT_W), _BF16),
        jax.ShapeDtypeStruct((rows, ATT_W), _BF16),
        jax.ShapeDtypeStruct((rows, ATT_W), _BF16),
        jax.ShapeDtypeStruct((rows, IDX_W), _BF16),
        jax.ShapeDtypeStruct((rows, LANES), _BF16),
        jax.ShapeDtypeStruct((rows, LANES), _F32),
    ] + [jax.ShapeDtypeStruct((rows, HG_W), _F32)] * 4
    out_specs = [row_spec(s.shape[1]) for s in out_shapes]
    kern = functools.partial(_proj_kernel, q_scale=ATT_HEAD_DIM ** -0.5 * math.log2(math.e),
                             iw_scale=(IDX_HEADS ** -0.5) * (IDX_DIM ** -0.5))
    return pl.pallas_call(
        kern,
        grid=(rows // tm,),
        in_specs=[row_spec(d), full_spec(wa), full_spec(wh), pos_spec, pos_spec],
        out_specs=out_specs,
        out_shape=out_shapes,
        compiler_params=pltpu.CompilerParams(
            dimension_semantics=("arbitrary",), vmem_limit_bytes=VMEM_LIMIT_BYTES),
    )(x2d, wa, wh, cos_t, sin_t)


def _dsa_kernel(qi_ref, kj_ref, ka_ref, vp_ref, vc_ref,
                qt_ref, iqt_ref, iwt_ref, ik_ref, before_ref, k_ref, vt_prev_ref, vt_ref,
                o_ref,
                hi_scr, lo_scr, cap_scr, bias_scr, m_scr, acc_scr, s_even, s_odd, bm_even, bm_odd,
                *, topk):
    del ka_ref, vp_ref, vc_ref
    p = pl.program_id(1)
    i = qi_ref[p]
    j = kj_ref[p]
    qb, kb = DSA_QB, DSA_KB
    n_kb = (i * qb + qb - 1) // kb + 1
    n_steps = (i * qb + qb - 1) // (2 * kb) + 1
    first_head = lax.broadcasted_iota(_I32, (LANES, qb), 0) < ATT_HEAD_DIM

    def one_head(pair_rows, h):
        keep = first_head if h % 2 == 0 else jnp.logical_not(first_head)
        return jnp.where(keep, pair_rows, jnp.zeros_like(pair_rows))

    @pl.when(j == 0)
    def _select():
        iwt = iwt_ref[0]
        qpos = i * qb + lax.broadcasted_iota(_I32, (kb, qb), 1)
        krow = lax.broadcasted_iota(_I32, (kb, qb), 0)
        iq_heads = [one_head(iqt_ref[0, (h // 2) * LANES:(h // 2 + 1) * LANES, :], h)
                    for h in range(IDX_HEADS)]

        def score_chunk(c):
            row0 = pl.multiple_of(c * kb, kb)
            ikc = ik_ref[0, pl.ds(row0, kb), :]
            score = jnp.zeros((kb, qb), _F32)
            for h in range(IDX_HEADS):
                logits = _dot(ikc, iq_heads[h])
                score = score + iwt[h:h + 1, :] * jnp.maximum(logits, 0.0)
            bits = lax.bitcast_convert_type(score, _I32)
            key = bits ^ ((bits >> 31) & 0x7FFFFFFF)
            key = jnp.where(row0 + krow <= qpos, key, _INT_MIN)
            hi_scr[c] = (key >> 16).astype(_I16)
            lo_scr[c] = ((key & 0xFFFF) + _I16_MIN).astype(_I16)

        def score_pair(c2, carry):
            score_chunk(2 * c2)
            score_chunk(2 * c2 + 1)
            return carry

        lax.fori_loop(0, n_steps, score_pair, 0)

        tile_rows = lambda r: slice(r * BF16_ROWS, (r + 1) * BF16_ROWS)
        i16_min = jnp.int16(_I16_MIN)

        def count(ref, cand, n_trips, blocks_per_trip, strict=False):
            cand_rows = jnp.broadcast_to(cand.astype(_I16), (BF16_ROWS, qb))
            one, zero = jnp.int16(1), jnp.int16(0)

            def body(t, accs):
                accs = list(accs)
                for u in range(blocks_per_trip):
                    for r in range(kb // BF16_ROWS):
                        tile = ref[blocks_per_trip * t + u, tile_rows(r), :]
                        hit = tile > cand_rows if strict else tile >= cand_rows
                        accs[r % len(accs)] = accs[r % len(accs)] + jnp.where(hit, one, zero)
                return tuple(accs)

            accs = lax.fori_loop(0, n_trips, body, (jnp.zeros((BF16_ROWS, qb), _I16),) * COUNT_ACCS)
            return jnp.sum(_tree_sum(list(accs)).astype(_I32), axis=0, keepdims=True)

        def bisect(ref, n_above, n_trips, blocks_per_trip):
            def body(step, carry):
                val, n_next = carry
                bit = jnp.left_shift(jnp.int32(1), 15 - step)
                cand = jnp.where(step == 0, 0, val | bit)
                cnt = n_above + count(ref, cand, n_trips, blocks_per_trip)
                ok = cnt >= topk
                return jnp.where(ok, cand, val), jnp.where(ok, n_next, cnt)
            return lax.fori_loop(0, 16, body, (jnp.full((1, qb), _I16_MIN, _I32), n_above))

        t_hi, n_gt_hi = bisect(hi_scr, jnp.zeros((1, qb), _I32), n_steps, 2)
        t_hi_tile = jnp.broadcast_to(t_hi.astype(_I16), (BF16_ROWS, qb))

        cap_scr[...] = jnp.full_like(cap_scr, _I16_MIN)

        def capture_block(c, carry):
            tops = [[jnp.full((BF16_ROWS, qb), _I16_MIN, _I16)] * 2 for _ in range(CAP_GROUPS)]
            for r in range(kb // BF16_ROWS):
                x = jnp.where(hi_scr[c, tile_rows(r), :] == t_hi_tile, lo_scr[c, tile_rows(r), :], i16_min)
                lo_scr[c, tile_rows(r), :] = x
                first, second = tops[r % CAP_GROUPS]
                above = x > first
                tops[r % CAP_GROUPS] = [jnp.where(above, x, first),
                                        jnp.where(above, first, jnp.where(x > second, x, second))]
            row0 = pl.multiple_of((c % CAP_CHUNKS_PER_BLOCK) * CAP_ROWS, CAP_ROWS)
            for g in range(CAP_GROUPS):
                for t in range(2):
                    cap_scr[c // CAP_CHUNKS_PER_BLOCK,
                            pl.ds(row0 + (2 * g + t) * BF16_ROWS, BF16_ROWS), :] = tops[g][t]
            return carry

        lax.fori_loop(0, n_kb, capture_block, 0)
        n_cap = (n_kb + CAP_CHUNKS_PER_BLOCK - 1) // CAP_CHUNKS_PER_BLOCK
        t_lo, _ = bisect(cap_scr, n_gt_hi, n_cap, 1)
        n_gt = n_gt_hi + count(lo_scr, t_lo, n_steps, 2, strict=True)
        n_wrong = jnp.sum(jnp.where(n_gt >= topk, 1.0, 0.0))
        t_lo, n_gt = lax.cond(n_wrong > 0.0,
                              lambda: bisect(lo_scr, n_gt_hi, n_steps, 2),
                              lambda: (t_lo, n_gt))
        t_hi_rows = jnp.broadcast_to(t_hi.astype(_I16), (kb, qb))
        t_lo = jnp.where((t_hi == _I16_MIN) & (t_lo == _I16_MIN), _I16_MIN + 1, t_lo)
        t_lo_rows = jnp.broadcast_to(t_lo.astype(_I16), (kb, qb))
        rem_rows = jnp.broadcast_to((topk - n_gt).astype(_I16), (kb, qb))

        def mask_block(c, seen):
            hi = hi_scr[c]
            lo = lo_scr[c]
            tie = (hi == t_hi_rows) & (lo == t_lo_rows)
            tie_count = jnp.where(tie, jnp.asarray(1, _BF16), jnp.asarray(0, _BF16))
            ties_before = seen + _dot(before_ref[...], tie_count)
            allowed = ties_before.astype(_I32).astype(_I16) < rem_rows
            chosen = (hi > t_hi_rows) | (lo > t_lo_rows) | (tie & allowed)
            bias_scr[c] = jnp.where(chosen, jnp.asarray(0, bias_scr.dtype),
                                    jnp.asarray(_NEG, bias_scr.dtype))
            last = ((hi[kb - 1:kb].astype(_I32) == t_hi) & (lo[kb - 1:kb].astype(_I32) == t_lo))
            return ties_before[kb - 1:kb] + jnp.where(last, 1.0, 0.0)

        lax.fori_loop(0, 2 * n_steps, mask_block, jnp.zeros((1, qb), _F32))

        m_scr[...] = jnp.full_like(m_scr, _NEG)
        acc_scr[...] = jnp.zeros_like(acc_scr)
        s_odd[...] = jnp.full_like(s_odd, _NEG)
        bm_odd[...] = jnp.full_like(bm_odd, _NEG)

    def phase(k_rows, chunk, s_write, bm_write, s_read, bm_read, vt_read):
        bias = bias_scr[chunk].astype(_F32)
        for h in range(ATT_HEADS):
            pair = slice((h // 2) * LANES, (h // 2 + 1) * LANES)
            s = _dot(k_ref[0, k_rows, pair], one_head(qt_ref[0, pair, :], h)) + bias
            s_write[h] = s
            bm_write[h] = jnp.max(s, axis=0, keepdims=True)

            m_old = m_scr[h]
            m_new = jnp.maximum(m_old, bm_read[h])
            pr = jnp.exp2(s_read[h] - m_new).astype(_BF16)
            acc_scr[h] = jnp.exp2(m_old - m_new) * acc_scr[h] + _dot(vt_read[0, h], pr)
            m_scr[h] = m_new

    last_chunk = 2 * n_steps - 1
    phase(slice(0, kb), jnp.minimum(2 * j, last_chunk), s_even, bm_even, s_odd, bm_odd, vt_prev_ref)

    @pl.when(j < n_steps)
    def _second():
        phase(slice(kb, 2 * kb), 2 * j + 1, s_odd, bm_odd, s_even, bm_even, vt_ref)

    @pl.when(j == n_steps)
    def _finish():
        for h in range(ATT_HEADS):
            a = acc_scr[h]
            o_ref[0, h * ATT_HEAD_DIM:(h + 1) * ATT_HEAD_DIM, :] = (
                a[:ATT_HEAD_DIM] / a[ATT_HEAD_DIM:ATT_HEAD_DIM + 1]).astype(o_ref.dtype)


def _dsa_attention(q, k, v, iq, ik, iw):
    bsz, seq, _ = q.shape
    qb, kb = DSA_QB, DSA_KB
    topk = min(TOPK_MAX, seq // 4)
    n_qb = seq // qb
    n_steps = lambda i: (i * qb + qb - 1) // (2 * kb) + 1
    pairs = [(i, j) for i in range(n_qb) for j in range(n_steps(i) + 1)]
    as_i32 = lambda vals: jnp.asarray(np.array(vals, np.int32))
    qi = as_i32([i for i, j in pairs])
    kj = as_i32([j for i, j in pairs])
    k_pair = as_i32([min(j, n_steps(i) - 1) for i, j in pairs])
    v_prev = as_i32([max(2 * j - 1, 0) for i, j in pairs])
    v_this = as_i32([min(2 * j, 2 * n_steps(i) - 1) for i, j in pairs])
    qt = jnp.swapaxes(q, 1, 2)
    iqt = jnp.swapaxes(iq, 1, 2)
    iwt = jnp.swapaxes(iw[:, :, :SUBLANES], 1, 2)
    vt = jnp.swapaxes(v.reshape(bsz, seq, ATT_HEADS, ATT_HEAD_DIM), 1, 3)
    vt = jnp.swapaxes(vt, 1, 2)
    extra = jnp.zeros((bsz, ATT_HEADS, V_ROWS - ATT_HEAD_DIM, seq), v.dtype).at[:, :, 0].set(1)
    vt = jnp.concatenate([vt, extra], axis=2)
    before = jnp.asarray(np.tril(np.ones((kb, kb), np.float32), -1), _BF16)
    q_map = lambda b, p, qi_r, *_: (b, 0, qi_r[p])
    vt_spec = lambda which: pl.BlockSpec(
        (1, ATT_HEADS, V_ROWS, kb), lambda b, p, *refs: (b, 0, 0, refs[which][p]))
    grid_spec = pltpu.PrefetchScalarGridSpec(
        num_scalar_prefetch=5,
        grid=(bsz, len(pairs)),
        in_specs=[
            pl.BlockSpec((1, ATT_W, qb), q_map),
            pl.BlockSpec((1, IDX_W, qb), q_map),
            pl.BlockSpec((1, SUBLANES, qb), q_map),
            pl.BlockSpec((1, seq, LANES), lambda b, p, *_: (b, 0, 0)),
            pl.BlockSpec((kb, kb), lambda b, p, *_: (0, 0)),
            pl.BlockSpec((1, 2 * kb, ATT_W), lambda b, p, *refs: (b, refs[2][p], 0)),
            vt_spec(3),
            vt_spec(4),
        ],
        out_specs=pl.BlockSpec((1, ATT_W, qb), q_map),
        scratch_shapes=[
            pltpu.VMEM((seq // kb, kb, qb), _I16),
            pltpu.VMEM((seq // kb, kb, qb), _I16),
            pltpu.VMEM((-(-(seq // kb) // CAP_CHUNKS_PER_BLOCK), kb, qb), _I16),
            pltpu.VMEM((seq // kb, kb, qb), _BF16),
            pltpu.VMEM((ATT_HEADS, 1, qb), _F32),
            pltpu.VMEM((ATT_HEADS, V_ROWS, qb), _F32),
            pltpu.VMEM((ATT_HEADS, kb, qb), _F32),
            pltpu.VMEM((ATT_HEADS, kb, qb), _F32),
            pltpu.VMEM((ATT_HEADS, 1, qb), _F32),
            pltpu.VMEM((ATT_HEADS, 1, qb), _F32),
        ],
    )
    return pl.pallas_call(
        functools.partial(_dsa_kernel, topk=topk),
        grid_spec=grid_spec,
        out_shape=jax.ShapeDtypeStruct((bsz, ATT_W, seq), _BF16),
        compiler_params=pltpu.CompilerParams(
            dimension_semantics=("arbitrary", "arbitrary"), vmem_limit_bytes=VMEM_LIMIT_BYTES),
    )(qi, kj, k_pair, v_prev, v_this, qt, iqt, iwt, ik, before, k, vt, vt)


def _split3(a):
    hi = a.astype(_BF16)
    r1 = a - hi.astype(_F32)
    mid = r1.astype(_BF16)
    lo = (r1 - mid.astype(_F32)).astype(_BF16)
    return hi, mid, lo


def _hgrn_kernel(lbl_ref, g_ref, hq_ref, hf_ref, hi_ref, hg_ref, o_ref,
                 state_scr, kk_scr, b_scr, o_scr, *, layer):
    rows, ch, sb = HG_ROWS, HG_CHUNK, HG_SUB
    n_sub = ch // sb

    @pl.when(pl.program_id(1) == 0)
    def _reset():
        state_scr[...] = jnp.zeros_like(state_scr)

    lbl = lbl_ref[...]
    e = jnp.exp(lbl - jnp.max(lbl, axis=0, keepdims=True))
    lb = jnp.sum(e[:layer + 1], axis=0, keepdims=True) / jnp.sum(e, axis=0, keepdims=True)

    f = lb + (1.0 - lb) * jax.nn.sigmoid(hf_ref[0])
    kk_scr[...] = 1.0 - f
    logf = jnp.log(f)
    r_i = lax.broadcasted_iota(_I32, (ch, ch), 0)
    c_i = lax.broadcasted_iota(_I32, (ch, ch), 1)
    lower = jnp.where(c_i <= r_i, 1.0, 0.0).astype(_BF16)
    for c in range(rows // ch):
        parts = _split3(logf[c * ch:(c + 1) * ch])
        b_scr[c * ch:(c + 1) * ch, :] = sum(_dot(lower, part) for part in parts)

    t_idx = lax.broadcasted_iota(_I32, (sb, 1), 0)
    row_idx = lax.broadcasted_iota(_I32, (ch, 1), 0)

    def chunk(c, carry):
        r0 = pl.multiple_of(c * ch, ch)
        cs = pl.ds(r0, ch)
        b = b_scr[cs, :]
        kk = kk_scr[cs, :]
        qv = hq_ref[0, cs, :]
        vv = hi_ref[0, cs, :]
        b_last = b[ch - 1:ch]
        q_in = (qv * jnp.exp(b)).astype(_BF16)
        k_out = kk * jnp.exp(b_last - b)
        vb = vv.astype(_BF16)

        a_off = [[jnp.zeros((sb, ch), _F32)] for _ in range(HG_HEADS)]
        for s_i in range(1, n_sub):
            ref_b = b[s_i * sb - 1:s_i * sb]
            q_s = (qv[s_i * sb:(s_i + 1) * sb] * jnp.exp(b[s_i * sb:(s_i + 1) * sb] - ref_b))
            k_s = jnp.where(row_idx < s_i * sb, kk * jnp.exp(jnp.minimum(ref_b - b, 0.0)), 0.0)
            q_s = q_s.astype(_BF16)
            k_s = k_s.astype(_BF16)
            for h in range(HG_HEADS):
                hs = slice(h * HG_KDIM, (h + 1) * HG_KDIM)
                a_off[h].append(_nt_dot(q_s[:, hs], k_s[:, hs]))

        for h in range(HG_HEADS):
            hs = slice(h * HG_KDIM, (h + 1) * HG_KDIM)
            st = state_scr[h]
            o_h = _nt_dot(q_in[:, hs], st.astype(_BF16))
            a_h = jnp.concatenate(a_off[h], axis=0).astype(_BF16)
            o_scr[cs, hs] = o_h + _dot(a_h, vb[:, hs])
            state_scr[h] = (st * jnp.exp(b_last[:, hs])
                            + _tn_dot(vb[:, hs], k_out[:, hs].astype(_BF16)))

        for s_i in range(n_sub):
            rs = pl.ds(r0 + s_i * sb, sb)
            q_s = hq_ref[0, rs, :]
            b_s = b_scr[rs, :]
            acc = o_scr[rs, :]
            for t in range(sb):
                one = pl.ds(r0 + s_i * sb + t, 1)
                w = q_s * jnp.exp(jnp.minimum(b_s - b_scr[one, :], 0.0)) * kk_scr[one, :]
                v_row = hi_ref[0, one, :]
                parts = []
                for h in range(HG_HEADS):
                    hs = slice(h * HG_KDIM, (h + 1) * HG_KDIM)
                    a = jnp.sum(w[:, hs], axis=1, keepdims=True)
                    parts.append(jnp.where(t_idx >= t, a, 0.0) * v_row[:, hs])
                acc = acc + jnp.concatenate(parts, axis=1)
            o_scr[rs, :] = acc
        return carry

    lax.fori_loop(0, rows // ch, chunk, 0)

    o = o_scr[...]
    gate = hg_ref[0]
    gain = g_ref[...]
    for h in range(HG_HEADS):
        hs = slice(h * HG_KDIM, (h + 1) * HG_KDIM)
        oh = o[:, hs]
        oh = oh * lax.rsqrt(jnp.mean(oh * oh, axis=1, keepdims=True) + RMS_EPS) * gain[:, hs]
        gh = gate[:, hs]
        o_ref[0, :, hs] = (oh * (gh * jax.nn.sigmoid(gh))).astype(o_ref.dtype)


def _hgrn2(hq, hf, hi, hg, lb_logits, norm_g, layer):
    bsz, seq, _ = hq.shape
    rows = HG_ROWS
    blk = pl.BlockSpec((1, rows, HG_W), lambda b, t: (b, t, 0))
    lbl = lb_logits.reshape(lb_logits.shape[0], HG_W)
    gain = norm_g.reshape(1, HG_W)
    return pl.pallas_call(
        functools.partial(_hgrn_kernel, layer=layer),
        grid=(bsz, seq // rows),
        in_specs=[pl.BlockSpec(lbl.shape, lambda b, t: (0, 0)),
                  pl.BlockSpec(gain.shape, lambda b, t: (0, 0)),
                  blk, blk, blk, blk],
        out_specs=blk,
        out_shape=jax.ShapeDtypeStruct((bsz, seq, HG_W), _BF16),
        scratch_shapes=[
            pltpu.VMEM((HG_HEADS, HG_KDIM, HG_KDIM), _F32),
            pltpu.VMEM((rows, HG_W), _F32),
            pltpu.VMEM((rows, HG_W), _F32),
            pltpu.VMEM((rows, HG_W), _F32),
        ],
        compiler_params=pltpu.CompilerParams(
            dimension_semantics=("arbitrary", "arbitrary"), vmem_limit_bytes=VMEM_LIMIT_BYTES),
    )(lbl, gain, hq, hf, hi, hg)


def _layer_norm(y, g, b):
    mu = jnp.mean(y, axis=1, keepdims=True)
    yc = y - mu
    var = jnp.mean(yc * yc, axis=1, keepdims=True)
    return yc * lax.rsqrt(var + LN_EPS) * g + b


def _ffn_kernel(x_ref, att_ref, hgo_ref, wo_ref, g1_ref, b1_ref, wu_ref, wd_ref,
                g2_ref, b2_ref, o_ref, *, alpha):
    mix = (_tn_dot(att_ref[0], wo_ref[:ATT_W, :]) + _dot(hgo_ref[...], wo_ref[ATT_W:, :]))
    y1 = _layer_norm(alpha * x_ref[...] + mix, g1_ref[...], b1_ref[...])
    y1b = y1.astype(_BF16)
    h = jnp.zeros_like(y1)
    for c in range(wu_ref.shape[1] // FFN_COLS):
        cs = slice(c * FFN_COLS, (c + 1) * FFN_COLS)
        u = jnp.maximum(_dot(y1b, wu_ref[:, cs]), 0.0)
        h = h + _dot((u * u).astype(_BF16), wd_ref[cs, :])
    o_ref[...] = _layer_norm(alpha * y1 + h, g2_ref[...], b2_ref[...])


def _out_ffn(x2d, att_t, hgo2d, w_o, g1, b1, w_up, w_down, g2, b2, alpha):
    rows, d = x2d.shape
    seq = att_t.shape[2]
    tm = FFN_ROWS
    n_seq_blocks = seq // tm
    row_spec = lambda w: pl.BlockSpec((tm, w), lambda i: (i, 0))
    full_spec = lambda a: pl.BlockSpec(a.shape, lambda i: (0, 0))
    att_spec = pl.BlockSpec((1, ATT_W, tm), lambda i: (i // n_seq_blocks, 0, i % n_seq_blocks))
    vec = lambda a: a.reshape(1, d).astype(_F32)
    args = (x2d, att_t, hgo2d, w_o.astype(_BF16), vec(g1), vec(b1),
            w_up.astype(_BF16), w_down.astype(_BF16), vec(g2), vec(b2))
    in_specs = [row_spec(d), att_spec, row_spec(HG_W)] + [full_spec(a) for a in args[3:]]
    return pl.pallas_call(
        functools.partial(_ffn_kernel, alpha=alpha),
        grid=(rows // tm,),
        in_specs=in_specs,
        out_specs=row_spec(d),
        out_shape=jax.ShapeDtypeStruct((rows, d), _F32),
        compiler_params=pltpu.CompilerParams(
            dimension_semantics=("arbitrary",), vmem_limit_bytes=VMEM_LIMIT_BYTES),
    )(*args)


def kernel(x, w_in, w_o, lb_logits, hg_norm_g, ln1_g, ln1_b, w_up, w_down, ln2_g, ln2_b):
    bsz, seq, d = x.shape
    depth = w_in.shape[0]
    alpha = (2.0 * depth) ** 0.25
    x2d = x.reshape(bsz * seq, d)
    for l in range(depth):
        q, k, v, iq, ik, iw, hq, hf, hi, hg = _project(x2d, w_in[l], seq)
        r3 = lambda a: a.reshape(bsz, seq, a.shape[-1])
        att_t = _dsa_attention(r3(q), r3(k), r3(v), r3(iq), r3(ik), r3(iw))
        hgo = _hgrn2(r3(hq), r3(hf), r3(hi), r3(hg), lb_logits, hg_norm_g[l], l)
        x2d = _out_ffn(x2d, att_t, hgo.reshape(bsz * seq, HG_W),
                       w_o[l], ln1_g[l], ln1_b[l], w_up[l], w_down[l], ln2_g[l], ln2_b[l], alpha)
    return x2d.reshape(bsz, seq, d)
```

```python
import functools
import math

import numpy as np
import jax
import jax.numpy as jnp
from jax import lax
from jax.experimental import pallas as pl
from jax.experimental.pallas import tpu as pltpu

ATT_HEAD_DIM = 64
ATT_HEADS = 8
ATT_W = ATT_HEADS * ATT_HEAD_DIM
IDX_HEADS = 4
IDX_DIM = 64
IDX_W = IDX_HEADS * IDX_DIM
TOPK_MAX = 256
HG_KDIM = 128
HG_HEADS = 4
HG_W = HG_HEADS * HG_KDIM
ROPE_THETA = 10000.0
LN_EPS = 1e-5
RMS_EPS = 1e-6

LANES = 128
SUBLANES = 8
BF16_ROWS = 16
VMEM_LIMIT_BYTES = 56 * 1024 * 1024

PROJ_ROWS = 256
DSA_QB = 256
DSA_KB = 512
V_ROWS = ATT_HEAD_DIM + BF16_ROWS
COUNT_ACCS = 4
CAP_GROUPS = 2
CAP_ROWS = 2 * CAP_GROUPS * BF16_ROWS
CAP_CHUNKS_PER_BLOCK = DSA_KB // CAP_ROWS
HG_ROWS = 256
HG_CHUNK = 64
HG_SUB = 8
FFN_ROWS = 256
FFN_COLS = 1024

_F32 = jnp.float32
_BF16 = jnp.bfloat16
_I32 = jnp.int32
_I16 = jnp.int16
_INT_MIN = -(2 ** 31)
_I16_MIN = -(2 ** 15)
_NEG = -1e30


def _nt_dot(a, b):
    return lax.dot_general(a, b, (((1,), (1,)), ((), ())), preferred_element_type=_F32)


def _tn_dot(a, b):
    return lax.dot_general(a, b, (((0,), (0,)), ((), ())), preferred_element_type=_F32)


def _dot(a, b):
    return jnp.dot(a, b, preferred_element_type=_F32)


def _tree_sum(parts):
    while len(parts) > 1:
        parts = [parts[n] + parts[n + 1] for n in range(0, len(parts) - 1, 2)] + (
            [parts[-1]] if len(parts) % 2 else [])
    return parts[0]


def _rope_group(z, cos, sin_signed):
    lane = lax.broadcasted_iota(_I32, z.shape, 1)
    first_half = (lane % ATT_HEAD_DIM) < (ATT_HEAD_DIM // 2)
    upper = pltpu.roll(z, LANES - ATT_HEAD_DIM // 2, 1)
    lower = pltpu.roll(z, ATT_HEAD_DIM // 2, 1)
    return z * cos + jnp.where(first_half, upper, lower) * sin_signed


def _proj_kernel(x_ref, wa_ref, wh_ref, cos_ref, sin_ref,
                 q_ref, k_ref, v_ref, iq_ref, ik_ref, iw_ref,
                 hq_ref, hf_ref, hi_ref, hg_ref, *, q_scale, iw_scale):
    xb = x_ref[...].astype(_BF16)
    cos = cos_ref[...]
    sin = sin_ref[...]
    pa = _dot(xb, wa_ref[...])

    def roped(col0, width):
        return [_rope_group(pa[:, col0 + g * LANES: col0 + (g + 1) * LANES], cos, sin)
                for g in range(width // LANES)]

    for g, z in enumerate(roped(0, ATT_W)):
        q_ref[:, g * LANES:(g + 1) * LANES] = (z * q_scale).astype(_BF16)
    for g, z in enumerate(roped(ATT_W, ATT_W)):
        k_ref[:, g * LANES:(g + 1) * LANES] = z.astype(_BF16)
    v_ref[...] = pa[:, 2 * ATT_W:3 * ATT_W].astype(_BF16)
    for g, z in enumerate(roped(3 * ATT_W, IDX_W)):
        iq_ref[:, g * LANES:(g + 1) * LANES] = z.astype(_BF16)
    ik_ref[...] = roped(3 * ATT_W + IDX_W, LANES)[0].astype(_BF16)
    iw_ref[...] = pa[:, 3 * ATT_W + IDX_W + LANES:] * iw_scale

    ph = _dot(xb, wh_ref[...])
    hq_ref[...] = ph[:, 0 * HG_W:1 * HG_W]
    hf_ref[...] = ph[:, 1 * HG_W:2 * HG_W]
    hi_ref[...] = ph[:, 2 * HG_W:3 * HG_W]
    hg_ref[...] = ph[:, 3 * HG_W:4 * HG_W]


def _rope_tables(seq):
    half = ATT_HEAD_DIM // 2
    inv = np.power(np.float64(ROPE_THETA), -np.arange(half, dtype=np.float64) / half)
    ang = np.arange(seq, dtype=np.float64)[:, None] * inv[None, :]
    cos = np.cos(ang)
    sin = np.sin(ang)
    cos_t = np.tile(np.concatenate([cos, cos], axis=1), (1, LANES // ATT_HEAD_DIM))
    sin_t = np.tile(np.concatenate([-sin, sin], axis=1), (1, LANES // ATT_HEAD_DIM))
    return jnp.asarray(cos_t, _F32), jnp.asarray(sin_t, _F32)


def _project(x2d, w_in, seq):
    rows, d = x2d.shape
    c = 3 * ATT_W + IDX_W
    w_ik = w_in[:, c:c + IDX_DIM]
    w_iw = w_in[:, c + IDX_DIM:c + IDX_DIM + IDX_HEADS]
    wa = jnp.concatenate([w_in[:, :c], w_ik, w_ik,
                          jnp.pad(w_iw, ((0, 0), (0, LANES - IDX_HEADS)))],
                         axis=1).astype(_BF16)
    wh = w_in[:, c + IDX_DIM + IDX_HEADS:].astype(_BF16)
    cos_t, sin_t = _rope_tables(seq)
    tm = PROJ_ROWS
    n_seq_blocks = seq // tm
    row_spec = lambda w: pl.BlockSpec((tm, w), lambda i: (i, 0))
    full_spec = lambda a: pl.BlockSpec(a.shape, lambda i: (0, 0))
    pos_spec = pl.BlockSpec((tm, LANES), lambda i: (i % n_seq_blocks, 0))
    out_shapes = [
        jax.ShapeDtypeStruct((rows, ATT_W), _BF16),
        jax.ShapeDtypeStruct((rows, ATT_W), _BF16),
        jax.ShapeDtypeStruct((rows, ATT_W), _BF16),
        jax.ShapeDtypeStruct((rows, IDX_W), _BF16),
        jax.ShapeDtypeStruct((rows, LANES), _BF16),
        jax.ShapeDtypeStruct((rows, LANES), _F32),
    ] + [jax.ShapeDtypeStruct((rows, HG_W), _F32)] * 4
    out_specs = [row_spec(s.shape[1]) for s in out_shapes]
    kern = functools.partial(_proj_kernel, q_scale=ATT_HEAD_DIM ** -0.5 * math.log2(math.e),
                             iw_scale=(IDX_HEADS ** -0.5) * (IDX_DIM ** -0.5))
    return pl.pallas_call(
        kern,
        grid=(rows // tm,),
        in_specs=[row_spec(d), full_spec(wa), full_spec(wh), pos_spec, pos_spec],
        out_specs=out_specs,
        out_shape=out_shapes,
        compiler_params=pltpu.CompilerParams(
            dimension_semantics=("arbitrary",), vmem_limit_bytes=VMEM_LIMIT_BYTES),
    )(x2d, wa, wh, cos_t, sin_t)


def _dsa_kernel(qi_ref, kj_ref, ka_ref, vp_ref, vc_ref,
                qt_ref, iqt_ref, iwt_ref, ik_ref, before_ref, k_ref, vt_prev_ref, vt_ref,
                o_ref,
                hi_scr, lo_scr, cap_scr, bias_scr, m_scr, acc_scr, s_even, s_odd, bm_even, bm_odd,
                *, topk):
    del ka_ref, vp_ref, vc_ref
    p = pl.program_id(1)
    i = qi_ref[p]
    j = kj_ref[p]
    qb, kb = DSA_QB, DSA_KB
    n_kb = (i * qb + qb - 1) // kb + 1
    n_steps = (i * qb + qb - 1) // (2 * kb) + 1
    first_head = lax.broadcasted_iota(_I32, (LANES, qb), 0) < ATT_HEAD_DIM

    def one_head(pair_rows, h):
        keep = first_head if h % 2 == 0 else jnp.logical_not(first_head)
        return jnp.where(keep, pair_rows, jnp.zeros_like(pair_rows))

    @pl.when(j == 0)
    def _select():
        iwt = iwt_ref[0]
        qpos = i * qb + lax.broadcasted_iota(_I32, (kb, qb), 1)
        krow = lax.broadcasted_iota(_I32, (kb, qb), 0)
        iq_heads = [one_head(iqt_ref[0, (h // 2) * LANES:(h // 2 + 1) * LANES, :], h)
                    for h in range(IDX_HEADS)]

        def score_chunk(c, causal):
            row0 = pl.multiple_of(c * kb, kb)
            ikc = ik_ref[0, pl.ds(row0, kb), :]
            score = jnp.zeros((kb, qb), _F32)
            for h in range(IDX_HEADS):
                logits = _dot(ikc, iq_heads[h])
                score = score + iwt[h:h + 1, :] * jnp.maximum(logits, 0.0)
            bits = lax.bitcast_convert_type(score, _I32)
            key = bits ^ ((bits >> 31) & 0x7FFFFFFF)
            if causal:
                key = jnp.where(row0 + krow <= qpos, key, _INT_MIN)
            hi_scr[c] = (key >> 16).astype(_I16)
            lo_scr[c] = ((key & 0xFFFF) + _I16_MIN).astype(_I16)

        def score_pair(c2, carry, causal=False):
            score_chunk(2 * c2, causal)
            score_chunk(2 * c2 + 1, causal)
            return carry

        lax.fori_loop(0, n_steps - 1, score_pair, 0)
        score_pair(n_steps - 1, 0, causal=True)

        tile_rows = lambda r: slice(r * BF16_ROWS, (r + 1) * BF16_ROWS)
        i16_min = jnp.int16(_I16_MIN)

        def count(ref, cand, n_trips, blocks_per_trip, strict=False):
            cand_rows = jnp.broadcast_to(cand.astype(_I16), (BF16_ROWS, qb))
            one, zero = jnp.int16(1), jnp.int16(0)

            def body(t, accs):
                accs = list(accs)
                for u in range(blocks_per_trip):
                    for r in range(kb // BF16_ROWS):
                        tile = ref[blocks_per_trip * t + u, tile_rows(r), :]
                        hit = tile > cand_rows if strict else tile >= cand_rows
                        accs[r % len(accs)] = accs[r % len(accs)] + jnp.where(hit, one, zero)
                return tuple(accs)

            accs = lax.fori_loop(0, n_trips, body, (jnp.zeros((BF16_ROWS, qb), _I16),) * COUNT_ACCS)
            return jnp.sum(_tree_sum(list(accs)).astype(_I32), axis=0, keepdims=True)

        def bisect(ref, n_above, n_trips, blocks_per_trip):
            def body(step, carry):
                val, n_next = carry
                bit = jnp.left_shift(jnp.int32(1), 15 - step)
                cand = jnp.where(step == 0, 0, val | bit)
                cnt = n_above + count(ref, cand, n_trips, blocks_per_trip)
                ok = cnt >= topk
                return jnp.where(ok, cand, val), jnp.where(ok, n_next, cnt)
            return lax.fori_loop(0, 16, body, (jnp.full((1, qb), _I16_MIN, _I32), n_above))

        t_hi, n_gt_hi = bisect(hi_scr, jnp.zeros((1, qb), _I32), n_steps, 2)
        t_hi_tile = jnp.broadcast_to(t_hi.astype(_I16), (BF16_ROWS, qb))

        cap_scr[...] = jnp.full_like(cap_scr, _I16_MIN)

        def capture_block(c, carry):
            tops = [[jnp.full((BF16_ROWS, qb), _I16_MIN, _I16)] * 2 for _ in range(CAP_GROUPS)]
            for r in range(kb // BF16_ROWS):
                x = jnp.where(hi_scr[c, tile_rows(r), :] == t_hi_tile, lo_scr[c, tile_rows(r), :], i16_min)
                lo_scr[c, tile_rows(r), :] = x
                first, second = tops[r % CAP_GROUPS]
                above = x > first
                tops[r % CAP_GROUPS] = [jnp.where(above, x, first),
                                        jnp.where(above, first, jnp.where(x > second, x, second))]
            row0 = pl.multiple_of((c % CAP_CHUNKS_PER_BLOCK) * CAP_ROWS, CAP_ROWS)
            for g in range(CAP_GROUPS):
                for t in range(2):
                    cap_scr[c // CAP_CHUNKS_PER_BLOCK,
                            pl.ds(row0 + (2 * g + t) * BF16_ROWS, BF16_ROWS), :] = tops[g][t]
            return carry

        lax.fori_loop(0, n_kb, capture_block, 0)
        n_cap = (n_kb + CAP_CHUNKS_PER_BLOCK - 1) // CAP_CHUNKS_PER_BLOCK
        t_lo, _ = bisect(cap_scr, n_gt_hi, n_cap, 1)
        n_gt = n_gt_hi + count(lo_scr, t_lo, n_steps, 2, strict=True)
        n_wrong = jnp.sum(jnp.where(n_gt >= topk, 1.0, 0.0))
        t_lo, n_gt = lax.cond(n_wrong > 0.0,
                              lambda: bisect(lo_scr, n_gt_hi, n_steps, 2),
                              lambda: (t_lo, n_gt))
        t_hi_rows = jnp.broadcast_to(t_hi.astype(_I16), (kb, qb))
        t_lo = jnp.where((t_hi == _I16_MIN) & (t_lo == _I16_MIN), _I16_MIN + 1, t_lo)
        t_lo_rows = jnp.broadcast_to(t_lo.astype(_I16), (kb, qb))
        rem_rows = jnp.broadcast_to((topk - n_gt).astype(_I16), (kb, qb))

        def mask_block(c, seen):
            hi = hi_scr[c]
            lo = lo_scr[c]
            tie = (hi == t_hi_rows) & (lo == t_lo_rows)
            tie_count = jnp.where(tie, jnp.asarray(1, _BF16), jnp.asarray(0, _BF16))
            ties_before = seen + _dot(before_ref[...], tie_count)
            allowed = ties_before.astype(_I32).astype(_I16) < rem_rows
            chosen = (hi > t_hi_rows) | (lo > t_lo_rows) | (tie & allowed)
            bias_scr[c] = jnp.where(chosen, jnp.asarray(0, bias_scr.dtype),
                                    jnp.asarray(_NEG, bias_scr.dtype))
            last = ((hi[kb - 1:kb].astype(_I32) == t_hi) & (lo[kb - 1:kb].astype(_I32) == t_lo))
            return ties_before[kb - 1:kb] + jnp.where(last, 1.0, 0.0)

        lax.fori_loop(0, n_steps, lambda c2, seen: mask_block(2 * c2 + 1, mask_block(2 * c2, seen)),
                      jnp.zeros((1, qb), _F32))

        m_scr[...] = jnp.full_like(m_scr, _NEG)
        acc_scr[...] = jnp.zeros_like(acc_scr)
        s_odd[...] = jnp.full_like(s_odd, _NEG)
        bm_odd[...] = jnp.full_like(bm_odd, _NEG)

    def phase(k_rows, chunk, s_write, bm_write, s_read, bm_read, vt_read):
        bias = bias_scr[chunk].astype(_F32)
        def stage_a(h):
            pair = slice((h // 2) * LANES, (h // 2 + 1) * LANES)
            s = _dot(k_ref[0, k_rows, pair], one_head(qt_ref[0, pair, :], h)) + bias
            s_write[h] = s
            bm_write[h] = jnp.max(s, axis=0, keepdims=True)

        def stage_b(h):
            m_old = m_scr[h]
            m_new = jnp.maximum(m_old, bm_read[h])
            pr = jnp.exp2(s_read[h] - m_new).astype(_BF16)
            acc_scr[h] = jnp.exp2(m_old - m_new) * acc_scr[h] + _dot(vt_read[0, h], pr)
            m_scr[h] = m_new

        stage_a(0)
        for h in range(ATT_HEADS):
            if h + 1 < ATT_HEADS:
                stage_a(h + 1)
            stage_b(h)

    last_chunk = 2 * n_steps - 1
    phase(slice(0, kb), jnp.minimum(2 * j, last_chunk), s_even, bm_even, s_odd, bm_odd, vt_prev_ref)

    @pl.when(j < n_steps)
    def _second():
        phase(slice(kb, 2 * kb), 2 * j + 1, s_odd, bm_odd, s_even, bm_even, vt_ref)

    @pl.when(j == n_steps)
    def _finish():
        for h in range(ATT_HEADS):
            a = acc_scr[h]
            o_ref[0, h * ATT_HEAD_DIM:(h + 1) * ATT_HEAD_DIM, :] = (
                a[:ATT_HEAD_DIM] / a[ATT_HEAD_DIM:ATT_HEAD_DIM + 1]).astype(o_ref.dtype)


def _dsa_attention(q, k, v, iq, ik, iw):
    bsz, seq, _ = q.shape
    qb, kb = DSA_QB, DSA_KB
    topk = min(TOPK_MAX, seq // 4)
    n_qb = seq // qb
    n_steps = lambda i: (i * qb + qb - 1) // (2 * kb) + 1
    pairs = [(i, j) for i in range(n_qb) for j in range(n_steps(i) + 1)]
    as_i32 = lambda vals: jnp.asarray(np.array(vals, np.int32))
    qi = as_i32([i for i, j in pairs])
    kj = as_i32([j for i, j in pairs])
    k_pair = as_i32([min(j, n_steps(i) - 1) for i, j in pairs])
    v_prev = as_i32([max(2 * j - 1, 0) for i, j in pairs])
    v_this = as_i32([min(2 * j, 2 * n_steps(i) - 1) for i, j in pairs])
    qt = jnp.swapaxes(q, 1, 2)
    iqt = jnp.swapaxes(iq, 1, 2)
    iwt = jnp.swapaxes(iw[:, :, :SUBLANES], 1, 2)
    vt = jnp.swapaxes(v.reshape(bsz, seq, ATT_HEADS, ATT_HEAD_DIM), 1, 3)
    vt = jnp.swapaxes(vt, 1, 2)
    extra = jnp.zeros((bsz, ATT_HEADS, V_ROWS - ATT_HEAD_DIM, seq), v.dtype).at[:, :, 0].set(1)
    vt = jnp.concatenate([vt, extra], axis=2)
    before = jnp.asarray(np.tril(np.ones((kb, kb), np.float32), -1), _BF16)
    q_map = lambda b, p, qi_r, *_: (b, 0, qi_r[p])
    vt_spec = lambda which: pl.BlockSpec(
        (1, ATT_HEADS, V_ROWS, kb), lambda b, p, *refs: (b, 0, 0, refs[which][p]))
    grid_spec = pltpu.PrefetchScalarGridSpec(
        num_scalar_prefetch=5,
        grid=(bsz, len(pairs)),
        in_specs=[
            pl.BlockSpec((1, ATT_W, qb), q_map),
            pl.BlockSpec((1, IDX_W, qb), q_map),
            pl.BlockSpec((1, SUBLANES, qb), q_map),
            pl.BlockSpec((1, seq, LANES), lambda b, p, *_: (b, 0, 0)),
            pl.BlockSpec((kb, kb), lambda b, p, *_: (0, 0)),
            pl.BlockSpec((1, 2 * kb, ATT_W), lambda b, p, *refs: (b, refs[2][p], 0)),
            vt_spec(3),
            vt_spec(4),
        ],
        out_specs=pl.BlockSpec((1, ATT_W, qb), q_map),
        scratch_shapes=[
            pltpu.VMEM((seq // kb, kb, qb), _I16),
            pltpu.VMEM((seq // kb, kb, qb), _I16),
            pltpu.VMEM((-(-(seq // kb) // CAP_CHUNKS_PER_BLOCK), kb, qb), _I16),
            pltpu.VMEM((seq // kb, kb, qb), _BF16),
            pltpu.VMEM((ATT_HEADS, 1, qb), _F32),
            pltpu.VMEM((ATT_HEADS, V_ROWS, qb), _F32),
            pltpu.VMEM((ATT_HEADS, kb, qb), _F32),
            pltpu.VMEM((ATT_HEADS, kb, qb), _F32),
            pltpu.VMEM((ATT_HEADS, 1, qb), _F32),
            pltpu.VMEM((ATT_HEADS, 1, qb), _F32),
        ],
    )
    return pl.pallas_call(
        functools.partial(_dsa_kernel, topk=topk),
        grid_spec=grid_spec,
        out_shape=jax.ShapeDtypeStruct((bsz, ATT_W, seq), _BF16),
        compiler_params=pltpu.CompilerParams(
            dimension_semantics=("arbitrary", "arbitrary"), vmem_limit_bytes=VMEM_LIMIT_BYTES),
    )(qi, kj, k_pair, v_prev, v_this, qt, iqt, iwt, ik, before, k, vt, vt)


def _split3(a):
    hi = a.astype(_BF16)
    r1 = a - hi.astype(_F32)
    mid = r1.astype(_BF16)
    lo = (r1 - mid.astype(_F32)).astype(_BF16)
    return hi, mid, lo


def _hgrn_kernel(lbl_ref, g_ref, hq_ref, hf_ref, hi_ref, hg_ref, o_ref,
                 state_scr, kk_scr, b_scr, o_scr, *, layer):
    rows, ch, sb = HG_ROWS, HG_CHUNK, HG_SUB
    n_sub = ch // sb

    @pl.when(pl.program_id(1) == 0)
    def _reset():
        state_scr[...] = jnp.zeros_like(state_scr)

    lbl = lbl_ref[...]
    e = jnp.exp(lbl - jnp.max(lbl, axis=0, keepdims=True))
    lb = jnp.sum(e[:layer + 1], axis=0, keepdims=True) / jnp.sum(e, axis=0, keepdims=True)

    f = lb + (1.0 - lb) * jax.nn.sigmoid(hf_ref[0])
    kk_scr[...] = 1.0 - f
    logf = jnp.log(f)
    r_i = lax.broadcasted_iota(_I32, (ch, ch), 0)
    c_i = lax.broadcasted_iota(_I32, (ch, ch), 1)
    lower = jnp.where(c_i <= r_i, 1.0, 0.0).astype(_BF16)
    for c in range(rows // ch):
        parts = _split3(logf[c * ch:(c + 1) * ch])
        b_scr[c * ch:(c + 1) * ch, :] = sum(_dot(lower, part) for part in parts)

    t_idx = lax.broadcasted_iota(_I32, (sb, 1), 0)
    row_idx = lax.broadcasted_iota(_I32, (ch, 1), 0)

    def chunk(c, carry):
        r0 = pl.multiple_of(c * ch, ch)
        cs = pl.ds(r0, ch)
        b = b_scr[cs, :]
        kk = kk_scr[cs, :]
        qv = hq_ref[0, cs, :]
        vv = hi_ref[0, cs, :]
        b_last = b[ch - 1:ch]
        q_in = (qv * jnp.exp(b)).astype(_BF16)
        k_out = kk * jnp.exp(b_last - b)
        vb = vv.astype(_BF16)

        a_off = [[jnp.zeros((sb, ch), _F32)] for _ in range(HG_HEADS)]
        for s_i in range(1, n_sub):
            ref_b = b[s_i * sb - 1:s_i * sb]
            q_s = (qv[s_i * sb:(s_i + 1) * sb] * jnp.exp(b[s_i * sb:(s_i + 1) * sb] - ref_b))
            k_s = jnp.where(row_idx < s_i * sb, kk * jnp.exp(jnp.minimum(ref_b - b, 0.0)), 0.0)
            q_s = q_s.astype(_BF16)
            k_s = k_s.astype(_BF16)
            for h in range(HG_HEADS):
                hs = slice(h * HG_KDIM, (h + 1) * HG_KDIM)
                a_off[h].append(_nt_dot(q_s[:, hs], k_s[:, hs]))

        for h in range(HG_HEADS):
            hs = slice(h * HG_KDIM, (h + 1) * HG_KDIM)
            st = state_scr[h]
            o_h = _nt_dot(q_in[:, hs], st.astype(_BF16))
            a_h = jnp.concatenate(a_off[h], axis=0).astype(_BF16)
            o_scr[cs, hs] = o_h + _dot(a_h, vb[:, hs])
            state_scr[h] = (st * jnp.exp(b_last[:, hs])
                            + _tn_dot(vb[:, hs], k_out[:, hs].astype(_BF16)))

        for s_i in range(n_sub):
            rs = pl.ds(r0 + s_i * sb, sb)
            q_s = hq_ref[0, rs, :]
            b_s = b_scr[rs, :]
            acc = o_scr[rs, :]
            for t in range(sb):
                one = pl.ds(r0 + s_i * sb + t, 1)
                w = q_s * jnp.exp(jnp.minimum(b_s - b_scr[one, :], 0.0)) * kk_scr[one, :]
                v_row = hi_ref[0, one, :]
                parts = []
                for h in range(HG_HEADS):
                    hs = slice(h * HG_KDIM, (h + 1) * HG_KDIM)
                    a = jnp.sum(w[:, hs], axis=1, keepdims=True)
                    parts.append(jnp.where(t_idx >= t, a, 0.0) * v_row[:, hs])
                acc = acc + jnp.concatenate(parts, axis=1)
            o_scr[rs, :] = acc
        return carry

    lax.fori_loop(0, rows // ch, chunk, 0)

    o = o_scr[...]
    gate = hg_ref[0]
    gain = g_ref[...]
    for h in range(HG_HEADS):
        hs = slice(h * HG_KDIM, (h + 1) * HG_KDIM)
        oh = o[:, hs]
        oh = oh * lax.rsqrt(jnp.mean(oh * oh, axis=1, keepdims=True) + RMS_EPS) * gain[:, hs]
        gh = gate[:, hs]
        o_ref[0, :, hs] = (oh * (gh * jax.nn.sigmoid(gh))).astype(o_ref.dtype)


def _hgrn2(hq, hf, hi, hg, lb_logits, norm_g, layer):
    bsz, seq, _ = hq.shape
    rows = HG_ROWS
    blk = pl.BlockSpec((1, rows, HG_W), lambda b, t: (b, t, 0))
    lbl = lb_logits.reshape(lb_logits.shape[0], HG_W)
    gain = norm_g.reshape(1, HG_W)
    return pl.pallas_call(
        functools.partial(_hgrn_kernel, layer=layer),
        grid=(bsz, seq // rows),
        in_specs=[pl.BlockSpec(lbl.shape, lambda b, t: (0, 0)),
                  pl.BlockSpec(gain.shape, lambda b, t: (0, 0)),
                  blk, blk, blk, blk],
        out_specs=blk,
        out_shape=jax.ShapeDtypeStruct((bsz, seq, HG_W), _BF16),
        scratch_shapes=[
            pltpu.VMEM((HG_HEADS, HG_KDIM, HG_KDIM), _F32),
            pltpu.VMEM((rows, HG_W), _F32),
            pltpu.VMEM((rows, HG_W), _F32),
            pltpu.VMEM((rows, HG_W), _F32),
        ],
        compiler_params=pltpu.CompilerParams(
            dimension_semantics=("arbitrary", "arbitrary"), vmem_limit_bytes=VMEM_LIMIT_BYTES),
    )(lbl, gain, hq, hf, hi, hg)


def _layer_norm(y, g, b):
    mu = jnp.mean(y, axis=1, keepdims=True)
    yc = y - mu
    var = jnp.mean(yc * yc, axis=1, keepdims=True)
    return yc * lax.rsqrt(var + LN_EPS) * g + b


def _ffn_kernel(x_ref, att_ref, hgo_ref, wo_ref, g1_ref, b1_ref, wu_ref, wd_ref,
                g2_ref, b2_ref, o_ref, *, alpha):
    mix = (_tn_dot(att_ref[0], wo_ref[:ATT_W, :]) + _dot(hgo_ref[...], wo_ref[ATT_W:, :]))
    y1 = _layer_norm(alpha * x_ref[...] + mix, g1_ref[...], b1_ref[...])
    y1b = y1.astype(_BF16)
    h = jnp.zeros_like(y1)
    for c in range(wu_ref.shape[1] // FFN_COLS):
        cs = slice(c * FFN_COLS, (c + 1) * FFN_COLS)
        u = jnp.maximum(_dot(y1b, wu_ref[:, cs]), 0.0)
        h = h + _dot((u * u).astype(_BF16), wd_ref[cs, :])
    o_ref[...] = _layer_norm(alpha * y1 + h, g2_ref[...], b2_ref[...])


def _out_ffn(x2d, att_t, hgo2d, w_o, g1, b1, w_up, w_down, g2, b2, alpha):
    rows, d = x2d.shape
    seq = att_t.shape[2]
    tm = FFN_ROWS
    n_seq_blocks = seq // tm
    row_spec = lambda w: pl.BlockSpec((tm, w), lambda i: (i, 0))
    full_spec = lambda a: pl.BlockSpec(a.shape, lambda i: (0, 0))
    att_spec = pl.BlockSpec((1, ATT_W, tm), lambda i: (i // n_seq_blocks, 0, i % n_seq_blocks))
    vec = lambda a: a.reshape(1, d).astype(_F32)
    args = (x2d, att_t, hgo2d, w_o.astype(_BF16), vec(g1), vec(b1),
            w_up.astype(_BF16), w_down.astype(_BF16), vec(g2), vec(b2))
    in_specs = [row_spec(d), att_spec, row_spec(HG_W)] + [full_spec(a) for a in args[3:]]
    return pl.pallas_call(
        functools.partial(_ffn_kernel, alpha=alpha),
        grid=(rows // tm,),
        in_specs=in_specs,
        out_specs=row_spec(d),
        out_shape=jax.ShapeDtypeStruct((rows, d), _F32),
        compiler_params=pltpu.CompilerParams(
            dimension_semantics=("arbitrary",), vmem_limit_bytes=VMEM_LIMIT_BYTES),
    )(*args)


def kernel(x, w_in, w_o, lb_logits, hg_norm_g, ln1_g, ln1_b, w_up, w_down, ln2_g, ln2_b):
    bsz, seq, d = x.shape
    depth = w_in.shape[0]
    alpha = (2.0 * depth) ** 0.25
    x2d = x.reshape(bsz * seq, d)
    for l in range(depth):
        q, k, v, iq, ik, iw, hq, hf, hi, hg = _project(x2d, w_in[l], seq)
        r3 = lambda a: a.reshape(bsz, seq, a.shape[-1])
        att_t = _dsa_attention(r3(q), r3(k), r3(v), r3(iq), r3(ik), r3(iw))
        hgo = _hgrn2(r3(hq), r3(hf), r3(hi), r3(hg), lb_logits, hg_norm_g[l], l)
        x2d = _out_ffn(x2d, att_t, hgo.reshape(bsz * seq, HG_W),
                       w_o[l], ln1_g[l], ln1_b[l], w_up[l], w_down[l], ln2_g[l], ln2_b[l], alpha)
    return x2d.reshape(bsz, seq, d)
```

```python
import functools
import math

import numpy as np
import jax
import jax.numpy as jnp
from jax import lax
from jax.experimental import pallas as pl
from jax.experimental.pallas import tpu as pltpu

ATT_HEAD_DIM = 64
ATT_HEADS = 8
ATT_W = ATT_HEADS * ATT_HEAD_DIM
IDX_HEADS = 4
IDX_DIM = 64
IDX_W = IDX_HEADS * IDX_DIM
TOPK_MAX = 256
HG_KDIM = 128
HG_HEADS = 4
HG_W = HG_HEADS * HG_KDIM
ROPE_THETA = 10000.0
LN_EPS = 1e-5
RMS_EPS = 1e-6

LANES = 128
SUBLANES = 8
BF16_ROWS = 16
VMEM_LIMIT_BYTES = 56 * 1024 * 1024

PROJ_ROWS = 256
DSA_QB = 256
DSA_KB = 512
V_ROWS = ATT_HEAD_DIM + BF16_ROWS
COUNT_ACCS = 4
CAP_GROUPS = 2
CAP_ROWS = 2 * CAP_GROUPS * BF16_ROWS
CAP_CHUNKS_PER_BLOCK = DSA_KB // CAP_ROWS
FOLD_GROUPS = CAP_ROWS // BF16_ROWS
REFINE_BITS = 5
FOLD_MIN_CHUNKS = 8
HG_ROWS = 256
HG_CHUNK = 64
HG_SUB = 8
FFN_ROWS = 256
FFN_COLS = 1024

_F32 = jnp.float32
_BF16 = jnp.bfloat16
_I32 = jnp.int32
_I16 = jnp.int16
_INT_MIN = -(2 ** 31)
_I16_MIN = -(2 ** 15)
_I16_MAX = 2 ** 15 - 1
_NEG = -1e30


def _nt_dot(a, b):
    return lax.dot_general(a, b, (((1,), (1,)), ((), ())), preferred_element_type=_F32)


def _tn_dot(a, b):
    return lax.dot_general(a, b, (((0,), (0,)), ((), ())), preferred_element_type=_F32)


def _dot(a, b):
    return jnp.dot(a, b, preferred_element_type=_F32)


def _tree_sum(parts):
    while len(parts) > 1:
        parts = [parts[n] + parts[n + 1] for n in range(0, len(parts) - 1, 2)] + (
            [parts[-1]] if len(parts) % 2 else [])
    return parts[0]


def _rope_group(z, cos, sin_signed):
    lane = lax.broadcasted_iota(_I32, z.shape, 1)
    first_half = (lane % ATT_HEAD_DIM) < (ATT_HEAD_DIM // 2)
    upper = pltpu.roll(z, LANES - ATT_HEAD_DIM // 2, 1)
    lower = pltpu.roll(z, ATT_HEAD_DIM // 2, 1)
    return z * cos + jnp.where(first_half, upper, lower) * sin_signed


def _proj_kernel(x_ref, wa_ref, wh_ref, cos_ref, sin_ref,
                 q_ref, k_ref, v_ref, iq_ref, ik_ref, iw_ref,
                 hq_ref, hf_ref, hi_ref, hg_ref, *, q_scale, iw_scale):
    xb = x_ref[...].astype(_BF16)
    cos = cos_ref[...]
    sin = sin_ref[...]
    pa = _dot(xb, wa_ref[...])

    def roped(col0, width):
        return [_rope_group(pa[:, col0 + g * LANES: col0 + (g + 1) * LANES], cos, sin)
                for g in range(width // LANES)]

    for g, z in enumerate(roped(0, ATT_W)):
        q_ref[:, g * LANES:(g + 1) * LANES] = (z * q_scale).astype(_BF16)
    for g, z in enumerate(roped(ATT_W, ATT_W)):
        k_ref[:, g * LANES:(g + 1) * LANES] = z.astype(_BF16)
    v_ref[...] = pa[:, 2 * ATT_W:3 * ATT_W].astype(_BF16)
    for g, z in enumerate(roped(3 * ATT_W, IDX_W)):
        iq_ref[:, g * LANES:(g + 1) * LANES] = z.astype(_BF16)
    ik_ref[...] = roped(3 * ATT_W + IDX_W, LANES)[0].astype(_BF16)
    iw_ref[...] = pa[:, 3 * ATT_W + IDX_W + LANES:] * iw_scale

    ph = _dot(xb, wh_ref[...])
    hq_ref[...] = ph[:, 0 * HG_W:1 * HG_W]
    hf_ref[...] = ph[:, 1 * HG_W:2 * HG_W]
    hi_ref[...] = ph[:, 2 * HG_W:3 * HG_W]
    hg_ref[...] = ph[:, 3 * HG_W:4 * HG_W]


def _rope_tables(seq):
    half = ATT_HEAD_DIM // 2
    inv = np.power(np.float64(ROPE_THETA), -np.arange(half, dtype=np.float64) / half)
    ang = np.arange(seq, dtype=np.float64)[:, None] * inv[None, :]
    cos = np.cos(ang)
    sin = np.sin(ang)
    cos_t = np.tile(np.concatenate([cos, cos], axis=1), (1, LANES // ATT_HEAD_DIM))
    sin_t = np.tile(np.concatenate([-sin, sin], axis=1), (1, LANES // ATT_HEAD_DIM))
    return jnp.asarray(cos_t, _F32), jnp.asarray(sin_t, _F32)


def _project(x2d, w_in, seq):
    rows, d = x2d.shape
    c = 3 * ATT_W + IDX_W
    w_ik = w_in[:, c:c + IDX_DIM]
    w_iw = w_in[:, c + IDX_DIM:c + IDX_DIM + IDX_HEADS]
    wa = jnp.concatenate([w_in[:, :c], w_ik, w_ik,
                          jnp.pad(w_iw, ((0, 0), (0, LANES - IDX_HEADS)))],
                         axis=1).astype(_BF16)
    wh = w_in[:, c + IDX_DIM + IDX_HEADS:].astype(_BF16)
    cos_t, sin_t = _rope_tables(seq)
    tm = PROJ_ROWS
    n_seq_blocks = seq // tm
    row_spec = lambda w: pl.BlockSpec((tm, w), lambda i: (i, 0))
    full_spec = lambda a: pl.BlockSpec(a.shape, lambda i: (0, 0))
    pos_spec = pl.BlockSpec((tm, LANES), lambda i: (i % n_seq_blocks, 0))
    out_shapes = [
        jax.ShapeDtypeStruct((rows, ATT_W), _BF16),
        jax.ShapeDtypeStruct((rows, ATT_W), _BF16),
        jax.ShapeDtypeStruct((rows, ATT_W), _BF16),
        jax.ShapeDtypeStruct((rows, IDX_W), _BF16),
        jax.ShapeDtypeStruct((rows, LANES), _BF16),
        jax.ShapeDtypeStruct((rows, LANES), _F32),
    ] + [jax.ShapeDtypeStruct((rows, HG_W), _F32)] * 4
    out_specs = [row_spec(s.shape[1]) for s in out_shapes]
    kern = functools.partial(_proj_kernel, q_scale=ATT_HEAD_DIM ** -0.5 * math.log2(math.e),
                             iw_scale=(IDX_HEADS ** -0.5) * (IDX_DIM ** -0.5))
    return pl.pallas_call(
        kern,
        grid=(rows // tm,),
        in_specs=[row_spec(d), full_spec(wa), full_spec(wh), pos_spec, pos_spec],
        out_specs=out_specs,
        out_shape=out_shapes,
        compiler_params=pltpu.CompilerParams(
            dimension_semantics=("arbitrary",), vmem_limit_bytes=VMEM_LIMIT_BYTES),
    )(x2d, wa, wh, cos_t, sin_t)


def _dsa_kernel(qi_ref, kj_ref, ka_ref, vp_ref, vc_ref,
                qt_ref, iqt_ref, iwt_ref, ik_ref, before_ref, k_ref, vt_prev_ref, vt_ref,
                o_ref,
                hi_scr, lo_scr, fold_scr, cap_scr, bias_scr, m_scr, acc_scr,
                s_even, s_odd, bm_even, bm_odd, *, topk):
    del ka_ref, vp_ref, vc_ref
    p = pl.program_id(1)
    i = qi_ref[p]
    j = kj_ref[p]
    qb, kb = DSA_QB, DSA_KB
    n_kb = (i * qb + qb - 1) // kb + 1
    n_steps = (i * qb + qb - 1) // (2 * kb) + 1
    first_head = lax.broadcasted_iota(_I32, (LANES, qb), 0) < ATT_HEAD_DIM

    def one_head(pair_rows, h):
        keep = first_head if h % 2 == 0 else jnp.logical_not(first_head)
        return jnp.where(keep, pair_rows, jnp.zeros_like(pair_rows))

    @pl.when(j == 0)
    def _select():
        iwt = iwt_ref[0]
        qpos = i * qb + lax.broadcasted_iota(_I32, (kb, qb), 1)
        krow = lax.broadcasted_iota(_I32, (kb, qb), 0)
        iq_heads = [one_head(iqt_ref[0, (h // 2) * LANES:(h // 2 + 1) * LANES, :], h)
                    for h in range(IDX_HEADS)]

        def score_chunk(c, causal):
            row0 = pl.multiple_of(c * kb, kb)
            ikc = ik_ref[0, pl.ds(row0, kb), :]
            score = jnp.zeros((kb, qb), _F32)
            for h in range(IDX_HEADS):
                logits = _dot(ikc, iq_heads[h])
                score = score + iwt[h:h + 1, :] * jnp.maximum(logits, 0.0)
            bits = lax.bitcast_convert_type(score, _I32)
            key = bits ^ ((bits >> 31) & 0x7FFFFFFF)
            if causal:
                key = jnp.where(row0 + krow <= qpos, key, _INT_MIN)
            hi = (key >> 16).astype(_I16)
            hi_scr[c] = hi
            lo_scr[c] = ((key & 0xFFFF) + _I16_MIN).astype(_I16)
            n_tiles = kb // BF16_ROWS
            row0_fold = pl.multiple_of((c % CAP_CHUNKS_PER_BLOCK) * CAP_ROWS, CAP_ROWS)
            for g in range(FOLD_GROUPS):
                tiles = [hi[r * BF16_ROWS:(r + 1) * BF16_ROWS] for r in range(g, n_tiles, FOLD_GROUPS)]
                while len(tiles) > 1:
                    tiles = [jnp.where(tiles[n] > tiles[n + 1], tiles[n], tiles[n + 1])
                             for n in range(0, len(tiles), 2)]
                fold_scr[c // CAP_CHUNKS_PER_BLOCK, pl.ds(row0_fold + g * BF16_ROWS, BF16_ROWS), :] = tiles[0]

        def score_pair(c2, carry, causal=False):
            score_chunk(2 * c2, causal)
            score_chunk(2 * c2 + 1, causal)
            return carry

        fold_scr[...] = jnp.full_like(fold_scr, _I16_MIN)
        lax.fori_loop(0, n_steps - 1, score_pair, 0)
        score_pair(n_steps - 1, 0, causal=True)

        tile_rows = lambda r: slice(r * BF16_ROWS, (r + 1) * BF16_ROWS)
        i16_min = jnp.int16(_I16_MIN)

        def count(ref, cand, n_trips, blocks_per_trip, strict=False):
            cand_rows = jnp.broadcast_to(cand.astype(_I16), (BF16_ROWS, qb))
            one, zero = jnp.int16(1), jnp.int16(0)

            def body(t, accs):
                accs = list(accs)
                for u in range(blocks_per_trip):
                    for r in range(kb // BF16_ROWS):
                        tile = ref[blocks_per_trip * t + u, tile_rows(r), :]
                        hit = tile > cand_rows if strict else tile >= cand_rows
                        accs[r % len(accs)] = accs[r % len(accs)] + jnp.where(hit, one, zero)
                return tuple(accs)

            accs = lax.fori_loop(0, n_trips, body, (jnp.zeros((BF16_ROWS, qb), _I16),) * COUNT_ACCS)
            return jnp.sum(_tree_sum(list(accs)).astype(_I32), axis=0, keepdims=True)

        def bisect(ref, n_above, n_trips, blocks_per_trip):
            def body(step, carry):
                val, n_next = carry
                bit = jnp.left_shift(jnp.int32(1), 15 - step)
                cand = jnp.where(step == 0, 0, val | bit)
                cnt = n_above + count(ref, cand, n_trips, blocks_per_trip)
                ok = cnt >= topk
                return jnp.where(ok, cand, val), jnp.where(ok, n_next, cnt)
            return lax.fori_loop(0, 16, body, (jnp.full((1, qb), _I16_MIN, _I32), n_above))

        n_cap = (n_kb + CAP_CHUNKS_PER_BLOCK - 1) // CAP_CHUNKS_PER_BLOCK
        no_keys = jnp.zeros((1, qb), _I32)

        def high_full():
            return bisect(hi_scr, no_keys, n_steps, 2)

        def high_from_fold():
            base, _ = bisect(fold_scr, no_keys, n_cap, 1)

            def count_from(cand):
                cnt = count(hi_scr, jnp.minimum(cand, _I16_MAX), n_steps, 2)
                return jnp.where(cand > _I16_MAX, 0, cnt)

            n_beyond = count_from(base + (1 << REFINE_BITS))

            def refine():
                def body(step, carry):
                    off, n_next = carry
                    cand_off = off | jnp.left_shift(jnp.int32(1), REFINE_BITS - 1 - step)
                    cnt = count_from(base + cand_off)
                    ok = cnt >= topk
                    return jnp.where(ok, cand_off, off), jnp.where(ok, n_next, cnt)
                off, n_next = lax.fori_loop(0, REFINE_BITS, body, (no_keys, n_beyond))
                return base + off, n_next

            n_outside = jnp.sum(jnp.where(n_beyond >= topk, 1.0, 0.0))
            return lax.cond(n_outside > 0.0, high_full, refine)

        t_hi, n_gt_hi = lax.cond(n_kb >= FOLD_MIN_CHUNKS, high_from_fold, high_full)
        t_hi_tile = jnp.broadcast_to(t_hi.astype(_I16), (BF16_ROWS, qb))

        cap_scr[...] = jnp.full_like(cap_scr, _I16_MIN)

        def capture_block(c, carry):
            tops = [[jnp.full((BF16_ROWS, qb), _I16_MIN, _I16)] * 2 for _ in range(CAP_GROUPS)]
            for r in range(kb // BF16_ROWS):
                x = jnp.where(hi_scr[c, tile_rows(r), :] == t_hi_tile, lo_scr[c, tile_rows(r), :], i16_min)
                lo_scr[c, tile_rows(r), :] = x
                first, second = tops[r % CAP_GROUPS]
                above = x > first
                tops[r % CAP_GROUPS] = [jnp.where(above, x, first),
                                        jnp.where(above, first, jnp.where(x > second, x, second))]
            row0 = pl.multiple_of((c % CAP_CHUNKS_PER_BLOCK) * CAP_ROWS, CAP_ROWS)
            for g in range(CAP_GROUPS):
                for t in range(2):
                    cap_scr[c // CAP_CHUNKS_PER_BLOCK,
                            pl.ds(row0 + (2 * g + t) * BF16_ROWS, BF16_ROWS), :] = tops[g][t]
            return carry

        lax.fori_loop(0, n_kb, capture_block, 0)
        t_lo, _ = bisect(cap_scr, n_gt_hi, n_cap, 1)
        n_gt = n_gt_hi + count(lo_scr, t_lo, n_steps, 2, strict=True)
        n_wrong = jnp.sum(jnp.where(n_gt >= topk, 1.0, 0.0))
        t_lo, n_gt = lax.cond(n_wrong > 0.0,
                              lambda: bisect(lo_scr, n_gt_hi, n_steps, 2),
                              lambda: (t_lo, n_gt))
        t_hi_rows = jnp.broadcast_to(t_hi.astype(_I16), (kb, qb))
        t_lo = jnp.where((t_hi == _I16_MIN) & (t_lo == _I16_MIN), _I16_MIN + 1, t_lo)
        t_lo_rows = jnp.broadcast_to(t_lo.astype(_I16), (kb, qb))
        rem_rows = jnp.broadcast_to((topk - n_gt).astype(_I16), (kb, qb))

        def mask_block(c, seen):
            hi = hi_scr[c]
            lo = lo_scr[c]
            tie = (hi == t_hi_rows) & (lo == t_lo_rows)
            tie_count = jnp.where(tie, jnp.asarray(1, _BF16), jnp.asarray(0, _BF16))
            ties_before = seen + _dot(before_ref[...], tie_count)
            allowed = ties_before.astype(_I32).astype(_I16) < rem_rows
            chosen = (hi > t_hi_rows) | (lo > t_lo_rows) | (tie & allowed)
            bias_scr[c] = jnp.where(chosen, jnp.asarray(0, bias_scr.dtype),
                                    jnp.asarray(_NEG, bias_scr.dtype))
            last = ((hi[kb - 1:kb].astype(_I32) == t_hi) & (lo[kb - 1:kb].astype(_I32) == t_lo))
            return ties_before[kb - 1:kb] + jnp.where(last, 1.0, 0.0)

        lax.fori_loop(0, n_steps, lambda c2, seen: mask_block(2 * c2 + 1, mask_block(2 * c2, seen)),
                      jnp.zeros((1, qb), _F32))

        m_scr[...] = jnp.full_like(m_scr, _NEG)
        acc_scr[...] = jnp.zeros_like(acc_scr)
        s_odd[...] = jnp.full_like(s_odd, _NEG)
        bm_odd[...] = jnp.full_like(bm_odd, _NEG)

    def phase(k_rows, chunk, s_write, bm_write, s_read, bm_read, vt_read):
        bias = bias_scr[chunk].astype(_F32)
        def stage_a(h):
            pair = slice((h // 2) * LANES, (h // 2 + 1) * LANES)
            s = _dot(k_ref[0, k_rows, pair], one_head(qt_ref[0, pair, :], h)) + bias
            s_write[h] = s
            bm_write[h] = jnp.max(s, axis=0, keepdims=True)

        def stage_b(h):
            m_old = m_scr[h]
            m_new = jnp.maximum(m_old, bm_read[h])
            pr = jnp.exp2(s_read[h] - m_new).astype(_BF16)
            acc_scr[h] = jnp.exp2(m_old - m_new) * acc_scr[h] + _dot(vt_read[0, h], pr)
            m_scr[h] = m_new

        stage_a(0)
        for h in range(ATT_HEADS):
            if h + 1 < ATT_HEADS:
                stage_a(h + 1)
            stage_b(h)

    last_chunk = 2 * n_steps - 1
    phase(slice(0, kb), jnp.minimum(2 * j, last_chunk), s_even, bm_even, s_odd, bm_odd, vt_prev_ref)

    @pl.when(j < n_steps)
    def _second():
        phase(slice(kb, 2 * kb), 2 * j + 1, s_odd, bm_odd, s_even, bm_even, vt_ref)

    @pl.when(j == n_steps)
    def _finish():
        for h in range(ATT_HEADS):
            a = acc_scr[h]
            o_ref[0, h * ATT_HEAD_DIM:(h + 1) * ATT_HEAD_DIM, :] = (
                a[:ATT_HEAD_DIM] / a[ATT_HEAD_DIM:ATT_HEAD_DIM + 1]).astype(o_ref.dtype)


def _dsa_attention(q, k, v, iq, ik, iw):
    bsz, seq, _ = q.shape
    qb, kb = DSA_QB, DSA_KB
    topk = min(TOPK_MAX, seq // 4)
    n_qb = seq // qb
    n_steps = lambda i: (i * qb + qb - 1) // (2 * kb) + 1
    pairs = [(i, j) for i in range(n_qb) for j in range(n_steps(i) + 1)]
    as_i32 = lambda vals: jnp.asarray(np.array(vals, np.int32))
    qi = as_i32([i for i, j in pairs])
    kj = as_i32([j for i, j in pairs])
    k_pair = as_i32([min(j, n_steps(i) - 1) for i, j in pairs])
    v_prev = as_i32([max(2 * j - 1, 0) for i, j in pairs])
    v_this = as_i32([min(2 * j, 2 * n_steps(i) - 1) for i, j in pairs])
    qt = jnp.swapaxes(q, 1, 2)
    iqt = jnp.swapaxes(iq, 1, 2)
    iwt = jnp.swapaxes(iw[:, :, :SUBLANES], 1, 2)
    vt = jnp.swapaxes(v.reshape(bsz, seq, ATT_HEADS, ATT_HEAD_DIM), 1, 3)
    vt = jnp.swapaxes(vt, 1, 2)
    extra = jnp.zeros((bsz, ATT_HEADS, V_ROWS - ATT_HEAD_DIM, seq), v.dtype).at[:, :, 0].set(1)
    vt = jnp.concatenate([vt, extra], axis=2)
    before = jnp.asarray(np.tril(np.ones((kb, kb), np.float32), -1), _BF16)
    q_map = lambda b, p, qi_r, *_: (b, 0, qi_r[p])
    vt_spec = lambda which: pl.BlockSpec(
        (1, ATT_HEADS, V_ROWS, kb), lambda b, p, *refs: (b, 0, 0, refs[which][p]))
    grid_spec = pltpu.PrefetchScalarGridSpec(
        num_scalar_prefetch=5,
        grid=(bsz, len(pairs)),
        in_specs=[
            pl.BlockSpec((1, ATT_W, qb), q_map),
            pl.BlockSpec((1, IDX_W, qb), q_map),
            pl.BlockSpec((1, SUBLANES, qb), q_map),
            pl.BlockSpec((1, seq, LANES), lambda b, p, *_: (b, 0, 0)),
            pl.BlockSpec((kb, kb), lambda b, p, *_: (0, 0)),
            pl.BlockSpec((1, 2 * kb, ATT_W), lambda b, p, *refs: (b, refs[2][p], 0)),
            vt_spec(3),
            vt_spec(4),
        ],
        out_specs=pl.BlockSpec((1, ATT_W, qb), q_map),
        scratch_shapes=[
            pltpu.VMEM((seq // kb, kb, qb), _I16),
            pltpu.VMEM((seq // kb, kb, qb), _I16),
            pltpu.VMEM((-(-(seq // kb) // CAP_CHUNKS_PER_BLOCK), kb, qb), _I16),
            pltpu.VMEM((-(-(seq // kb) // CAP_CHUNKS_PER_BLOCK), kb, qb), _I16),
            pltpu.VMEM((seq // kb, kb, qb), _BF16),
            pltpu.VMEM((ATT_HEADS, 1, qb), _F32),
            pltpu.VMEM((ATT_HEADS, V_ROWS, qb), _F32),
            pltpu.VMEM((ATT_HEADS, kb, qb), _F32),
            pltpu.VMEM((ATT_HEADS, kb, qb), _F32),
            pltpu.VMEM((ATT_HEADS, 1, qb), _F32),
            pltpu.VMEM((ATT_HEADS, 1, qb), _F32),
        ],
    )
    return pl.pallas_call(
        functools.partial(_dsa_kernel, topk=topk),
        grid_spec=grid_spec,
        out_shape=jax.ShapeDtypeStruct((bsz, ATT_W, seq), _BF16),
        compiler_params=pltpu.CompilerParams(
            dimension_semantics=("arbitrary", "arbitrary"), vmem_limit_bytes=VMEM_LIMIT_BYTES),
    )(qi, kj, k_pair, v_prev, v_this, qt, iqt, iwt, ik, before, k, vt, vt)


def _split3(a):
    hi = a.astype(_BF16)
    r1 = a - hi.astype(_F32)
    mid = r1.astype(_BF16)
    lo = (r1 - mid.astype(_F32)).astype(_BF16)
    return hi, mid, lo


def _hgrn_kernel(lbl_ref, g_ref, hq_ref, hf_ref, hi_ref, hg_ref, o_ref,
                 state_scr, kk_scr, b_scr, o_scr, *, layer):
    rows, ch, sb = HG_ROWS, HG_CHUNK, HG_SUB
    n_sub = ch // sb

    @pl.when(pl.program_id(1) == 0)
    def _reset():
        state_scr[...] = jnp.zeros_like(state_scr)

    lbl = lbl_ref[...]
    e = jnp.exp(lbl - jnp.max(lbl, axis=0, keepdims=True))
    lb = jnp.sum(e[:layer + 1], axis=0, keepdims=True) / jnp.sum(e, axis=0, keepdims=True)

    f = lb + (1.0 - lb) * jax.nn.sigmoid(hf_ref[0])
    kk_scr[...] = 1.0 - f
    logf = jnp.log(f)
    r_i = lax.broadcasted_iota(_I32, (ch, ch), 0)
    c_i = lax.broadcasted_iota(_I32, (ch, ch), 1)
    lower = jnp.where(c_i <= r_i, 1.0, 0.0).astype(_BF16)
    for c in range(rows // ch):
        parts = _split3(logf[c * ch:(c + 1) * ch])
        b_scr[c * ch:(c + 1) * ch, :] = sum(_dot(lower, part) for part in parts)

    t_idx = lax.broadcasted_iota(_I32, (sb, 1), 0)
    row_idx = lax.broadcasted_iota(_I32, (ch, 1), 0)

    def chunk(c, carry):
        r0 = pl.multiple_of(c * ch, ch)
        cs = pl.ds(r0, ch)
        b = b_scr[cs, :]
        kk = kk_scr[cs, :]
        qv = hq_ref[0, cs, :]
        vv = hi_ref[0, cs, :]
        b_last = b[ch - 1:ch]
        q_in = (qv * jnp.exp(b)).astype(_BF16)
        k_out = kk * jnp.exp(b_last - b)
        vb = vv.astype(_BF16)

        a_off = [[jnp.zeros((sb, ch), _F32)] for _ in range(HG_HEADS)]
        for s_i in range(1, n_sub):
            ref_b = b[s_i * sb - 1:s_i * sb]
            q_s = (qv[s_i * sb:(s_i + 1) * sb] * jnp.exp(b[s_i * sb:(s_i + 1) * sb] - ref_b))
            k_s = jnp.where(row_idx < s_i * sb, kk * jnp.exp(jnp.minimum(ref_b - b, 0.0)), 0.0)
            q_s = q_s.astype(_BF16)
            k_s = k_s.astype(_BF16)
            for h in range(HG_HEADS):
                hs = slice(h * HG_KDIM, (h + 1) * HG_KDIM)
                a_off[h].append(_nt_dot(q_s[:, hs], k_s[:, hs]))

        for h in range(HG_HEADS):
            hs = slice(h * HG_KDIM, (h + 1) * HG_KDIM)
            st = state_scr[h]
            o_h = _nt_dot(q_in[:, hs], st.astype(_BF16))
            a_h = jnp.concatenate(a_off[h], axis=0).astype(_BF16)
            o_scr[cs, hs] = o_h + _dot(a_h, vb[:, hs])
            state_scr[h] = (st * jnp.exp(b_last[:, hs])
                            + _tn_dot(vb[:, hs], k_out[:, hs].astype(_BF16)))

        for s_i in range(n_sub):
            rs = pl.ds(r0 + s_i * sb, sb)
            q_s = hq_ref[0, rs, :]
            b_s = b_scr[rs, :]
            acc = o_scr[rs, :]
            for t in range(sb):
                one = pl.ds(r0 + s_i * sb + t, 1)
                w = q_s * jnp.exp(jnp.minimum(b_s - b_scr[one, :], 0.0)) * kk_scr[one, :]
                v_row = hi_ref[0, one, :]
                parts = []
                for h in range(HG_HEADS):
                    hs = slice(h * HG_KDIM, (h + 1) * HG_KDIM)
                    a = jnp.sum(w[:, hs], axis=1, keepdims=True)
                    parts.append(jnp.where(t_idx >= t, a, 0.0) * v_row[:, hs])
                acc = acc + jnp.concatenate(parts, axis=1)
            o_scr[rs, :] = acc
        return carry

    lax.fori_loop(0, rows // ch, chunk, 0)

    o = o_scr[...]
    gate = hg_ref[0]
    gain = g_ref[...]
    for h in range(HG_HEADS):
        hs = slice(h * HG_KDIM, (h + 1) * HG_KDIM)
        oh = o[:, hs]
        oh = oh * lax.rsqrt(jnp.mean(oh * oh, axis=1, keepdims=True) + RMS_EPS) * gain[:, hs]
        gh = gate[:, hs]
        o_ref[0, :, hs] = (oh * (gh * jax.nn.sigmoid(gh))).astype(o_ref.dtype)


def _hgrn2(hq, hf, hi, hg, lb_logits, norm_g, layer):
    bsz, seq, _ = hq.shape
    rows = HG_ROWS
    blk = pl.BlockSpec((1, rows, HG_W), lambda b, t: (b, t, 0))
    lbl = lb_logits.reshape(lb_logits.shape[0], HG_W)
    gain = norm_g.reshape(1, HG_W)
    return pl.pallas_call(
        functools.partial(_hgrn_kernel, layer=layer),
        grid=(bsz, seq // rows),
        in_specs=[pl.BlockSpec(lbl.shape, lambda b, t: (0, 0)),
                  pl.BlockSpec(gain.shape, lambda b, t: (0, 0)),
                  blk, blk, blk, blk],
        out_specs=blk,
        out_shape=jax.ShapeDtypeStruct((bsz, seq, HG_W), _BF16),
        scratch_shapes=[
            pltpu.VMEM((HG_HEADS, HG_KDIM, HG_KDIM), _F32),
            pltpu.VMEM((rows, HG_W), _F32),
            pltpu.VMEM((rows, HG_W), _F32),
            pltpu.VMEM((rows, HG_W), _F32),
        ],
        compiler_params=pltpu.CompilerParams(
            dimension_semantics=("arbitrary", "arbitrary"), vmem_limit_bytes=VMEM_LIMIT_BYTES),
    )(lbl, gain, hq, hf, hi, hg)


def _layer_norm(y, g, b):
    mu = jnp.mean(y, axis=1, keepdims=True)
    yc = y - mu
    var = jnp.mean(yc * yc, axis=1, keepdims=True)
    return yc * lax.rsqrt(var + LN_EPS) * g + b


def _ffn_kernel(x_ref, att_ref, hgo_ref, wo_ref, g1_ref, b1_ref, wu_ref, wd_ref,
                g2_ref, b2_ref, o_ref, *, alpha):
    mix = (_tn_dot(att_ref[0], wo_ref[:ATT_W, :]) + _dot(hgo_ref[...], wo_ref[ATT_W:, :]))
    y1 = _layer_norm(alpha * x_ref[...] + mix, g1_ref[...], b1_ref[...])
    y1b = y1.astype(_BF16)
    h = jnp.zeros_like(y1)
    for c in range(wu_ref.shape[1] // FFN_COLS):
        cs = slice(c * FFN_COLS, (c + 1) * FFN_COLS)
        u = jnp.maximum(_dot(y1b, wu_ref[:, cs]), 0.0)
        h = h + _dot((u * u).astype(_BF16), wd_ref[cs, :])
    o_ref[...] = _layer_norm(alpha * y1 + h, g2_ref[...], b2_ref[...])


def _out_ffn(x2d, att_t, hgo2d, w_o, g1, b1, w_up, w_down, g2, b2, alpha):
    rows, d = x2d.shape
    seq = att_t.shape[2]
    tm = FFN_ROWS
    n_seq_blocks = seq // tm
    row_spec = lambda w: pl.BlockSpec((tm, w), lambda i: (i, 0))
    full_spec = lambda a: pl.BlockSpec(a.shape, lambda i: (0, 0))
    att_spec = pl.BlockSpec((1, ATT_W, tm), lambda i: (i // n_seq_blocks, 0, i % n_seq_blocks))
    vec = lambda a: a.reshape(1, d).astype(_F32)
    args = (x2d, att_t, hgo2d, w_o.astype(_BF16), vec(g1), vec(b1),
            w_up.astype(_BF16), w_down.astype(_BF16), vec(g2), vec(b2))
    in_specs = [row_spec(d), att_spec, row_spec(HG_W)] + [full_spec(a) for a in args[3:]]
    return pl.pallas_call(
        functools.partial(_ffn_kernel, alpha=alpha),
        grid=(rows // tm,),
        in_specs=in_specs,
        out_specs=row_spec(d),
        out_shape=jax.ShapeDtypeStruct((rows, d), _F32),
        compiler_params=pltpu.CompilerParams(
            dimension_semantics=("arbitrary",), vmem_limit_bytes=VMEM_LIMIT_BYTES),
    )(*args)


def kernel(x, w_in, w_o, lb_logits, hg_norm_g, ln1_g, ln1_b, w_up, w_down, ln2_g, ln2_b):
    bsz, seq, d = x.shape
    depth = w_in.shape[0]
    alpha = (2.0 * depth) ** 0.25
    x2d = x.reshape(bsz * seq, d)
    for l in range(depth):
        q, k, v, iq, ik, iw, hq, hf, hi, hg = _project(x2d, w_in[l], seq)
        r3 = lambda a: a.reshape(bsz, seq, a.shape[-1])
        att_t = _dsa_attention(r3(q), r3(k), r3(v), r3(iq), r3(ik), r3(iw))
        hgo = _hgrn2(r3(hq), r3(hf), r3(hi), r3(hg), lb_logits, hg_norm_g[l], l)
        x2d = _out_ffn(x2d, att_t, hgo.reshape(bsz * seq, HG_W),
                       w_o[l], ln1_g[l], ln1_b[l], w_up[l], w_down[l], ln2_g[l], ln2_b[l], alpha)
    return x2d.reshape(bsz, seq, d)
```

```python
import functools
import math

import numpy as np
import jax
import jax.numpy as jnp
from jax import lax
from jax.experimental import pallas as pl
from jax.experimental.pallas import tpu as pltpu

ATT_HEAD_DIM = 64
ATT_HEADS = 8
ATT_W = ATT_HEADS * ATT_HEAD_DIM
IDX_HEADS = 4
IDX_DIM = 64
IDX_W = IDX_HEADS * IDX_DIM
TOPK_MAX = 256
HG_KDIM = 128
HG_HEADS = 4
HG_W = HG_HEADS * HG_KDIM
ROPE_THETA = 10000.0
LN_EPS = 1e-5
RMS_EPS = 1e-6

LANES = 128
SUBLANES = 8
BF16_ROWS = 16
VMEM_LIMIT_BYTES = 56 * 1024 * 1024

PROJ_ROWS = 256
DSA_QB = 256
DSA_KB = 512
V_ROWS = ATT_HEAD_DIM + BF16_ROWS
COUNT_ACCS = 4
CAP_GROUPS = 2
CAP_ROWS = 2 * CAP_GROUPS * BF16_ROWS
CAP_CHUNKS_PER_BLOCK = DSA_KB // CAP_ROWS
FOLD_GROUPS = CAP_ROWS // BF16_ROWS
REFINE_BITS = 6
FOLD_MIN_CHUNKS = 14
HG_ROWS = 256
HG_CHUNK = 64
HG_SUB = 8
FFN_ROWS = 256
FFN_COLS = 1024

_F32 = jnp.float32
_BF16 = jnp.bfloat16
_I32 = jnp.int32
_I16 = jnp.int16
_INT_MIN = -(2 ** 31)
_I16_MIN = -(2 ** 15)
_I16_MAX = 2 ** 15 - 1
_NEG = -1e30


def _nt_dot(a, b):
    return lax.dot_general(a, b, (((1,), (1,)), ((), ())), preferred_element_type=_F32)


def _tn_dot(a, b):
    return lax.dot_general(a, b, (((0,), (0,)), ((), ())), preferred_element_type=_F32)


def _dot(a, b):
    return jnp.dot(a, b, preferred_element_type=_F32)


def _tree_sum(parts):
    while len(parts) > 1:
        parts = [parts[n] + parts[n + 1] for n in range(0, len(parts) - 1, 2)] + (
            [parts[-1]] if len(parts) % 2 else [])
    return parts[0]


def _rope_group(z, cos, sin_signed):
    lane = lax.broadcasted_iota(_I32, z.shape, 1)
    first_half = (lane % ATT_HEAD_DIM) < (ATT_HEAD_DIM // 2)
    upper = pltpu.roll(z, LANES - ATT_HEAD_DIM // 2, 1)
    lower = pltpu.roll(z, ATT_HEAD_DIM // 2, 1)
    return z * cos + jnp.where(first_half, upper, lower) * sin_signed


def _proj_kernel(x_ref, wa_ref, wh_ref, cos_ref, sin_ref,
                 q_ref, k_ref, v_ref, iq_ref, ik_ref, iw_ref,
                 hq_ref, hf_ref, hi_ref, hg_ref, *, q_scale, iw_scale):
    xb = x_ref[...].astype(_BF16)
    cos = cos_ref[...]
    sin = sin_ref[...]
    pa = _dot(xb, wa_ref[...])

    def roped(col0, width):
        return [_rope_group(pa[:, col0 + g * LANES: col0 + (g + 1) * LANES], cos, sin)
                for g in range(width // LANES)]

    for g, z in enumerate(roped(0, ATT_W)):
        q_ref[:, g * LANES:(g + 1) * LANES] = (z * q_scale).astype(_BF16)
    for g, z in enumerate(roped(ATT_W, ATT_W)):
        k_ref[:, g * LANES:(g + 1) * LANES] = z.astype(_BF16)
    v_ref[...] = pa[:, 2 * ATT_W:3 * ATT_W].astype(_BF16)
    for g, z in enumerate(roped(3 * ATT_W, IDX_W)):
        iq_ref[:, g * LANES:(g + 1) * LANES] = z.astype(_BF16)
    ik_ref[...] = roped(3 * ATT_W + IDX_W, LANES)[0].astype(_BF16)
    iw_ref[...] = pa[:, 3 * ATT_W + IDX_W + LANES:] * iw_scale

    ph = _dot(xb, wh_ref[...])
    hq_ref[...] = ph[:, 0 * HG_W:1 * HG_W]
    hf_ref[...] = ph[:, 1 * HG_W:2 * HG_W]
    hi_ref[...] = ph[:, 2 * HG_W:3 * HG_W]
    hg_ref[...] = ph[:, 3 * HG_W:4 * HG_W]


def _rope_tables(seq):
    half = ATT_HEAD_DIM // 2
    inv = np.power(np.float64(ROPE_THETA), -np.arange(half, dtype=np.float64) / half)
    ang = np.arange(seq, dtype=np.float64)[:, None] * inv[None, :]
    cos = np.cos(ang)
    sin = np.sin(ang)
    cos_t = np.tile(np.concatenate([cos, cos], axis=1), (1, LANES // ATT_HEAD_DIM))
    sin_t = np.tile(np.concatenate([-sin, sin], axis=1), (1, LANES // ATT_HEAD_DIM))
    return jnp.asarray(cos_t, _F32), jnp.asarray(sin_t, _F32)


def _project(x2d, w_in, seq):
    rows, d = x2d.shape
    c = 3 * ATT_W + IDX_W
    w_ik = w_in[:, c:c + IDX_DIM]
    w_iw = w_in[:, c + IDX_DIM:c + IDX_DIM + IDX_HEADS]
    wa = jnp.concatenate([w_in[:, :c], w_ik, w_ik,
                          jnp.pad(w_iw, ((0, 0), (0, LANES - IDX_HEADS)))],
                         axis=1).astype(_BF16)
    wh = w_in[:, c + IDX_DIM + IDX_HEADS:].astype(_BF16)
    cos_t, sin_t = _rope_tables(seq)
    tm = PROJ_ROWS
    n_seq_blocks = seq // tm
    row_spec = lambda w: pl.BlockSpec((tm, w), lambda i: (i, 0))
    full_spec = lambda a: pl.BlockSpec(a.shape, lambda i: (0, 0))
    pos_spec = pl.BlockSpec((tm, LANES), lambda i: (i % n_seq_blocks, 0))
    out_shapes = [
        jax.ShapeDtypeStruct((rows, ATT_W), _BF16),
        jax.ShapeDtypeStruct((rows, ATT_W), _BF16),
        jax.ShapeDtypeStruct((rows, ATT_W), _BF16),
        jax.ShapeDtypeStruct((rows, IDX_W), _BF16),
        jax.ShapeDtypeStruct((rows, LANES), _BF16),
        jax.ShapeDtypeStruct((rows, LANES), _F32),
    ] + [jax.ShapeDtypeStruct((rows, HG_W), _F32)] * 4
    out_specs = [row_spec(s.shape[1]) for s in out_shapes]
    kern = functools.partial(_proj_kernel, q_scale=ATT_HEAD_DIM ** -0.5 * math.log2(math.e),
                             iw_scale=(IDX_HEADS ** -0.5) * (IDX_DIM ** -0.5))
    return pl.pallas_call(
        kern,
        grid=(rows // tm,),
        in_specs=[row_spec(d), full_spec(wa), full_spec(wh), pos_spec, pos_spec],
        out_specs=out_specs,
        out_shape=out_shapes,
        compiler_params=pltpu.CompilerParams(
            dimension_semantics=("arbitrary",), vmem_limit_bytes=VMEM_LIMIT_BYTES),
    )(x2d, wa, wh, cos_t, sin_t)


def _dsa_kernel(qi_ref, kj_ref, ka_ref, vp_ref, vc_ref,
                qt_ref, iqt_ref, iwt_ref, ik_ref, before_ref, k_ref, vt_prev_ref, vt_ref,
                o_ref,
                hi_scr, lo_scr, fold_scr, cap_scr, bias_scr, m_scr, acc_scr,
                s_even, s_odd, bm_even, bm_odd, *, topk):
    del ka_ref, vp_ref, vc_ref
    p = pl.program_id(1)
    i = qi_ref[p]
    j = kj_ref[p]
    qb, kb = DSA_QB, DSA_KB
    n_kb = (i * qb + qb - 1) // kb + 1
    n_steps = (i * qb + qb - 1) // (2 * kb) + 1
    first_head = lax.broadcasted_iota(_I32, (LANES, qb), 0) < ATT_HEAD_DIM

    def one_head(pair_rows, h):
        keep = first_head if h % 2 == 0 else jnp.logical_not(first_head)
        return jnp.where(keep, pair_rows, jnp.zeros_like(pair_rows))

    @pl.when(j == 0)
    def _select():
        iwt = iwt_ref[0]
        qpos = i * qb + lax.broadcasted_iota(_I32, (kb, qb), 1)
        krow = lax.broadcasted_iota(_I32, (kb, qb), 0)
        iq_heads = [one_head(iqt_ref[0, (h // 2) * LANES:(h // 2 + 1) * LANES, :], h)
                    for h in range(IDX_HEADS)]

        def score_chunk(c, causal):
            row0 = pl.multiple_of(c * kb, kb)
            ikc = ik_ref[0, pl.ds(row0, kb), :]
            score = jnp.zeros((kb, qb), _F32)
            for h in range(IDX_HEADS):
                logits = _dot(ikc, iq_heads[h])
                score = score + iwt[h:h + 1, :] * jnp.maximum(logits, 0.0)
            bits = lax.bitcast_convert_type(score, _I32)
            key = bits ^ ((bits >> 31) & 0x7FFFFFFF)
            if causal:
                key = jnp.where(row0 + krow <= qpos, key, _INT_MIN)
            hi = (key >> 16).astype(_I16)
            hi_scr[c] = hi
            lo_scr[c] = ((key & 0xFFFF) + _I16_MIN).astype(_I16)
            n_tiles = kb // BF16_ROWS
            row0_fold = pl.multiple_of((c % CAP_CHUNKS_PER_BLOCK) * CAP_ROWS, CAP_ROWS)
            for g in range(FOLD_GROUPS):
                tiles = [hi[r * BF16_ROWS:(r + 1) * BF16_ROWS] for r in range(g, n_tiles, FOLD_GROUPS)]
                while len(tiles) > 1:
                    tiles = [jnp.where(tiles[n] > tiles[n + 1], tiles[n], tiles[n + 1])
                             for n in range(0, len(tiles), 2)]
                fold_scr[c // CAP_CHUNKS_PER_BLOCK, pl.ds(row0_fold + g * BF16_ROWS, BF16_ROWS), :] = tiles[0]

        def score_pair(c2, carry, causal=False):
            score_chunk(2 * c2, causal)
            score_chunk(2 * c2 + 1, causal)
            return carry

        fold_scr[...] = jnp.full_like(fold_scr, _I16_MIN)
        lax.fori_loop(0, n_steps - 1, score_pair, 0)
        score_pair(n_steps - 1, 0, causal=True)

        tile_rows = lambda r: slice(r * BF16_ROWS, (r + 1) * BF16_ROWS)
        i16_min = jnp.int16(_I16_MIN)

        def count(ref, cand, n_trips, blocks_per_trip, strict=False):
            cand_rows = jnp.broadcast_to(cand.astype(_I16), (BF16_ROWS, qb))
            one, zero = jnp.int16(1), jnp.int16(0)

            def body(t, accs):
                accs = list(accs)
                for u in range(blocks_per_trip):
                    for r in range(kb // BF16_ROWS):
                        tile = ref[blocks_per_trip * t + u, tile_rows(r), :]
                        hit = tile > cand_rows if strict else tile >= cand_rows
                        accs[r % len(accs)] = accs[r % len(accs)] + jnp.where(hit, one, zero)
                return tuple(accs)

            accs = lax.fori_loop(0, n_trips, body, (jnp.zeros((BF16_ROWS, qb), _I16),) * COUNT_ACCS)
            return jnp.sum(_tree_sum(list(accs)).astype(_I32), axis=0, keepdims=True)

        def bisect(ref, n_above, n_trips, blocks_per_trip):
            def body(step, carry):
                val, n_next = carry
                bit = jnp.left_shift(jnp.int32(1), 15 - step)
                cand = jnp.where(step == 0, 0, val | bit)
                cnt = n_above + count(ref, cand, n_trips, blocks_per_trip)
                ok = cnt >= topk
                return jnp.where(ok, cand, val), jnp.where(ok, n_next, cnt)
            return lax.fori_loop(0, 16, body, (jnp.full((1, qb), _I16_MIN, _I32), n_above))

        n_cap = (n_kb + CAP_CHUNKS_PER_BLOCK - 1) // CAP_CHUNKS_PER_BLOCK
        no_keys = jnp.zeros((1, qb), _I32)

        def high_full():
            return bisect(hi_scr, no_keys, n_steps, 2)

        def high_from_fold():
            base, _ = bisect(fold_scr, no_keys, n_cap, 1)

            def count_from(cand):
                cnt = count(hi_scr, jnp.minimum(cand, _I16_MAX), n_steps, 2)
                return jnp.where(cand > _I16_MAX, 0, cnt)

            n_beyond = count_from(base + (1 << REFINE_BITS))

            def refine():
                def body(step, carry):
                    off, n_next = carry
                    cand_off = off | jnp.left_shift(jnp.int32(1), REFINE_BITS - 1 - step)
                    cnt = count_from(base + cand_off)
                    ok = cnt >= topk
                    return jnp.where(ok, cand_off, off), jnp.where(ok, n_next, cnt)
                off, n_next = lax.fori_loop(0, REFINE_BITS, body, (no_keys, n_beyond))
                return base + off, n_next

            n_outside = jnp.sum(jnp.where(n_beyond >= topk, 1.0, 0.0))
            return lax.cond(n_outside > 0.0, high_full, refine)

        t_hi, n_gt_hi = lax.cond(n_kb >= FOLD_MIN_CHUNKS, high_from_fold, high_full)
        t_hi_tile = jnp.broadcast_to(t_hi.astype(_I16), (BF16_ROWS, qb))

        cap_scr[...] = jnp.full_like(cap_scr, _I16_MIN)

        def capture_block(c, carry):
            tops = [[jnp.full((BF16_ROWS, qb), _I16_MIN, _I16)] * 2 for _ in range(CAP_GROUPS)]
            for r in range(kb // BF16_ROWS):
                x = jnp.where(hi_scr[c, tile_rows(r), :] == t_hi_tile, lo_scr[c, tile_rows(r), :], i16_min)
                lo_scr[c, tile_rows(r), :] = x
                first, second = tops[r % CAP_GROUPS]
                above = x > first
                tops[r % CAP_GROUPS] = [jnp.where(above, x, first),
                                        jnp.where(above, first, jnp.where(x > second, x, second))]
            row0 = pl.multiple_of((c % CAP_CHUNKS_PER_BLOCK) * CAP_ROWS, CAP_ROWS)
            for g in range(CAP_GROUPS):
                for t in range(2):
                    cap_scr[c // CAP_CHUNKS_PER_BLOCK,
                            pl.ds(row0 + (2 * g + t) * BF16_ROWS, BF16_ROWS), :] = tops[g][t]
            return carry

        lax.fori_loop(0, n_kb, capture_block, 0)
        t_lo, _ = bisect(cap_scr, n_gt_hi, n_cap, 1)
        n_gt = n_gt_hi + count(lo_scr, t_lo, n_steps, 2, strict=True)
        n_wrong = jnp.sum(jnp.where(n_gt >= topk, 1.0, 0.0))
        t_lo, n_gt = lax.cond(n_wrong > 0.0,
                              lambda: bisect(lo_scr, n_gt_hi, n_steps, 2),
                              lambda: (t_lo, n_gt))
        t_hi_rows = jnp.broadcast_to(t_hi.astype(_I16), (kb, qb))
        t_lo = jnp.where((t_hi == _I16_MIN) & (t_lo == _I16_MIN), _I16_MIN + 1, t_lo)
        t_lo_rows = jnp.broadcast_to(t_lo.astype(_I16), (kb, qb))
        rem_rows = jnp.broadcast_to((topk - n_gt).astype(_I16), (kb, qb))

        def mask_block(c, seen):
            hi = hi_scr[c]
            lo = lo_scr[c]
            tie = (hi == t_hi_rows) & (lo == t_lo_rows)
            tie_count = jnp.where(tie, jnp.asarray(1, _BF16), jnp.asarray(0, _BF16))

            def tie_at(r):
                hit = (hi[r:r + 1].astype(_I32) == t_hi) & (lo[r:r + 1].astype(_I32) == t_lo)
                return jnp.where(hit, 1.0, 0.0)

            half = kb // 2
            before_top = seen + _dot(before_ref[...], tie_count[:half])
            seen_mid = before_top[half - 1:half] + tie_at(half - 1)
            before_bot = seen_mid + _dot(before_ref[...], tie_count[half:])
            ties_before = jnp.concatenate([before_top, before_bot], axis=0)
            allowed = ties_before.astype(_I32).astype(_I16) < rem_rows
            chosen = (hi > t_hi_rows) | (lo > t_lo_rows) | (tie & allowed)
            bias_scr[c] = jnp.where(chosen, jnp.asarray(0, bias_scr.dtype),
                                    jnp.asarray(_NEG, bias_scr.dtype))
            return before_bot[half - 1:half] + tie_at(kb - 1)

        lax.fori_loop(0, n_steps, lambda c2, seen: mask_block(2 * c2 + 1, mask_block(2 * c2, seen)),
                      jnp.zeros((1, qb), _F32))

        m_scr[...] = jnp.full_like(m_scr, _NEG)
        acc_scr[...] = jnp.zeros_like(acc_scr)
        s_odd[...] = jnp.full_like(s_odd, _NEG)
        bm_odd[...] = jnp.full_like(bm_odd, _NEG)

    def phase(k_rows, chunk, s_write, bm_write, s_read, bm_read, vt_read):
        bias = bias_scr[chunk].astype(_F32)
        def stage_a(h):
            pair = slice((h // 2) * LANES, (h // 2 + 1) * LANES)
            s = _dot(k_ref[0, k_rows, pair], one_head(qt_ref[0, pair, :], h)) + bias
            s_write[h] = s
            bm_write[h] = jnp.max(s, axis=0, keepdims=True)

        def stage_b(h):
            m_old = m_scr[h]
            m_new = jnp.maximum(m_old, bm_read[h])
            pr = jnp.exp2(s_read[h] - m_new).astype(_BF16)
            acc_scr[h] = jnp.exp2(m_old - m_new) * acc_scr[h] + _dot(vt_read[0, h], pr)
            m_scr[h] = m_new

        stage_a(0)
        for h in range(ATT_HEADS):
            if h + 1 < ATT_HEADS:
                stage_a(h + 1)
            stage_b(h)

    last_chunk = 2 * n_steps - 1
    phase(slice(0, kb), jnp.minimum(2 * j, last_chunk), s_even, bm_even, s_odd, bm_odd, vt_prev_ref)

    @pl.when(j < n_steps)
    def _second():
        phase(slice(kb, 2 * kb), 2 * j + 1, s_odd, bm_odd, s_even, bm_even, vt_ref)

    @pl.when(j == n_steps)
    def _finish():
        for h in range(ATT_HEADS):
            a = acc_scr[h]
            o_ref[0, h * ATT_HEAD_DIM:(h + 1) * ATT_HEAD_DIM, :] = (
                a[:ATT_HEAD_DIM] / a[ATT_HEAD_DIM:ATT_HEAD_DIM + 1]).astype(o_ref.dtype)


def _dsa_attention(q, k, v, iq, ik, iw):
    bsz, seq, _ = q.shape
    qb, kb = DSA_QB, DSA_KB
    topk = min(TOPK_MAX, seq // 4)
    n_qb = seq // qb
    n_steps = lambda i: (i * qb + qb - 1) // (2 * kb) + 1
    pairs = [(i, j) for i in range(n_qb) for j in range(n_steps(i) + 1)]
    as_i32 = lambda vals: jnp.asarray(np.array(vals, np.int32))
    qi = as_i32([i for i, j in pairs])
    kj = as_i32([j for i, j in pairs])
    k_pair = as_i32([min(j, n_steps(i) - 1) for i, j in pairs])
    v_prev = as_i32([max(2 * j - 1, 0) for i, j in pairs])
    v_this = as_i32([min(2 * j, 2 * n_steps(i) - 1) for i, j in pairs])
    qt = jnp.swapaxes(q, 1, 2)
    iqt = jnp.swapaxes(iq, 1, 2)
    iwt = jnp.swapaxes(iw[:, :, :SUBLANES], 1, 2)
    vt = jnp.swapaxes(v.reshape(bsz, seq, ATT_HEADS, ATT_HEAD_DIM), 1, 3)
    vt = jnp.swapaxes(vt, 1, 2)
    extra = jnp.zeros((bsz, ATT_HEADS, V_ROWS - ATT_HEAD_DIM, seq), v.dtype).at[:, :, 0].set(1)
    vt = jnp.concatenate([vt, extra], axis=2)
    before = jnp.asarray(np.tril(np.ones((kb // 2, kb // 2), np.float32), -1), _BF16)
    q_map = lambda b, p, qi_r, *_: (b, 0, qi_r[p])
    vt_spec = lambda which: pl.BlockSpec(
        (1, ATT_HEADS, V_ROWS, kb), lambda b, p, *refs: (b, 0, 0, refs[which][p]))
    grid_spec = pltpu.PrefetchScalarGridSpec(
        num_scalar_prefetch=5,
        grid=(bsz, len(pairs)),
        in_specs=[
            pl.BlockSpec((1, ATT_W, qb), q_map),
            pl.BlockSpec((1, IDX_W, qb), q_map),
            pl.BlockSpec((1, SUBLANES, qb), q_map),
            pl.BlockSpec((1, seq, LANES), lambda b, p, *_: (b, 0, 0)),
            pl.BlockSpec(before.shape, lambda b, p, *_: (0, 0)),
            pl.BlockSpec((1, 2 * kb, ATT_W), lambda b, p, *refs: (b, refs[2][p], 0)),
            vt_spec(3),
            vt_spec(4),
        ],
        out_specs=pl.BlockSpec((1, ATT_W, qb), q_map),
        scratch_shapes=[
            pltpu.VMEM((seq // kb, kb, qb), _I16),
            pltpu.VMEM((seq // kb, kb, qb), _I16),
            pltpu.VMEM((-(-(seq // kb) // CAP_CHUNKS_PER_BLOCK), kb, qb), _I16),
            pltpu.VMEM((-(-(seq // kb) // CAP_CHUNKS_PER_BLOCK), kb, qb), _I16),
            pltpu.VMEM((seq // kb, kb, qb), _BF16),
            pltpu.VMEM((ATT_HEADS, 1, qb), _F32),
            pltpu.VMEM((ATT_HEADS, V_ROWS, qb), _F32),
            pltpu.VMEM((ATT_HEADS, kb, qb), _F32),
            pltpu.VMEM((ATT_HEADS, kb, qb), _F32),
            pltpu.VMEM((ATT_HEADS, 1, qb), _F32),
            pltpu.VMEM((ATT_HEADS, 1, qb), _F32),
        ],
    )
    return pl.pallas_call(
        functools.partial(_dsa_kernel, topk=topk),
        grid_spec=grid_spec,
        out_shape=jax.ShapeDtypeStruct((bsz, ATT_W, seq), _BF16),
        compiler_params=pltpu.CompilerParams(
            dimension_semantics=("arbitrary", "arbitrary"), vmem_limit_bytes=VMEM_LIMIT_BYTES),
    )(qi, kj, k_pair, v_prev, v_this, qt, iqt, iwt, ik, before, k, vt, vt)


def _split3(a):
    hi = a.astype(_BF16)
    r1 = a - hi.astype(_F32)
    mid = r1.astype(_BF16)
    lo = (r1 - mid.astype(_F32)).astype(_BF16)
    return hi, mid, lo


def _hgrn_kernel(lbl_ref, g_ref, hq_ref, hf_ref, hi_ref, hg_ref, o_ref,
                 state_scr, kk_scr, b_scr, o_scr, *, layer):
    rows, ch, sb = HG_ROWS, HG_CHUNK, HG_SUB
    n_sub = ch // sb

    @pl.when(pl.program_id(1) == 0)
    def _reset():
        state_scr[...] = jnp.zeros_like(state_scr)

    lbl = lbl_ref[...]
    e = jnp.exp(lbl - jnp.max(lbl, axis=0, keepdims=True))
    lb = jnp.sum(e[:layer + 1], axis=0, keepdims=True) / jnp.sum(e, axis=0, keepdims=True)

    f = lb + (1.0 - lb) * jax.nn.sigmoid(hf_ref[0])
    kk_scr[...] = 1.0 - f
    logf = jnp.log(f)
    r_i = lax.broadcasted_iota(_I32, (ch, ch), 0)
    c_i = lax.broadcasted_iota(_I32, (ch, ch), 1)
    lower = jnp.where(c_i <= r_i, 1.0, 0.0).astype(_BF16)
    for c in range(rows // ch):
        parts = _split3(logf[c * ch:(c + 1) * ch])
        b_scr[c * ch:(c + 1) * ch, :] = sum(_dot(lower, part) for part in parts)

    t_idx = lax.broadcasted_iota(_I32, (sb, 1), 0)
    row_idx = lax.broadcasted_iota(_I32, (ch, 1), 0)

    def chunk(c, carry):
        r0 = pl.multiple_of(c * ch, ch)
        cs = pl.ds(r0, ch)
        b = b_scr[cs, :]
        kk = kk_scr[cs, :]
        qv = hq_ref[0, cs, :]
        vv = hi_ref[0, cs, :]
        b_last = b[ch - 1:ch]
        q_in = (qv * jnp.exp(b)).astype(_BF16)
        k_out = kk * jnp.exp(b_last - b)
        vb = vv.astype(_BF16)

        a_off = [[jnp.zeros((sb, ch), _F32)] for _ in range(HG_HEADS)]
        for s_i in range(1, n_sub):
            ref_b = b[s_i * sb - 1:s_i * sb]
            q_s = (qv[s_i * sb:(s_i + 1) * sb] * jnp.exp(b[s_i * sb:(s_i + 1) * sb] - ref_b))
            k_s = jnp.where(row_idx < s_i * sb, kk * jnp.exp(jnp.minimum(ref_b - b, 0.0)), 0.0)
            q_s = q_s.astype(_BF16)
            k_s = k_s.astype(_BF16)
            for h in range(HG_HEADS):
                hs = slice(h * HG_KDIM, (h + 1) * HG_KDIM)
                a_off[h].append(_nt_dot(q_s[:, hs], k_s[:, hs]))

        for h in range(HG_HEADS):
            hs = slice(h * HG_KDIM, (h + 1) * HG_KDIM)
            st = state_scr[h]
            o_h = _nt_dot(q_in[:, hs], st.astype(_BF16))
            a_h = jnp.concatenate(a_off[h], axis=0).astype(_BF16)
            o_scr[cs, hs] = o_h + _dot(a_h, vb[:, hs])
            state_scr[h] = (st * jnp.exp(b_last[:, hs])
                            + _tn_dot(vb[:, hs], k_out[:, hs].astype(_BF16)))

        for s_i in range(n_sub):
            rs = pl.ds(r0 + s_i * sb, sb)
            q_s = hq_ref[0, rs, :]
            b_s = b_scr[rs, :]
            acc = o_scr[rs, :]
            for t in range(sb):
                one = pl.ds(r0 + s_i * sb + t, 1)
                w = q_s * jnp.exp(jnp.minimum(b_s - b_scr[one, :], 0.0)) * kk_scr[one, :]
                v_row = hi_ref[0, one, :]
                parts = []
                for h in range(HG_HEADS):
                    hs = slice(h * HG_KDIM, (h + 1) * HG_KDIM)
                    a = jnp.sum(w[:, hs], axis=1, keepdims=True)
                    parts.append(jnp.where(t_idx >= t, a, 0.0) * v_row[:, hs])
                acc = acc + jnp.concatenate(parts, axis=1)
            o_scr[rs, :] = acc
        return carry

    lax.fori_loop(0, rows // ch, chunk, 0)

    o = o_scr[...]
    gate = hg_ref[0]
    gain = g_ref[...]
    for h in range(HG_HEADS):
        hs = slice(h * HG_KDIM, (h + 1) * HG_KDIM)
        oh = o[:, hs]
        oh = oh * lax.rsqrt(jnp.mean(oh * oh, axis=1, keepdims=True) + RMS_EPS) * gain[:, hs]
        gh = gate[:, hs]
        o_ref[0, :, hs] = (oh * (gh * jax.nn.sigmoid(gh))).astype(o_ref.dtype)


def _hgrn2(hq, hf, hi, hg, lb_logits, norm_g, layer):
    bsz, seq, _ = hq.shape
    rows = HG_ROWS
    blk = pl.BlockSpec((1, rows, HG_W), lambda b, t: (b, t, 0))
    lbl = lb_logits.reshape(lb_logits.shape[0], HG_W)
    gain = norm_g.reshape(1, HG_W)
    return pl.pallas_call(
        functools.partial(_hgrn_kernel, layer=layer),
        grid=(bsz, seq // rows),
        in_specs=[pl.BlockSpec(lbl.shape, lambda b, t: (0, 0)),
                  pl.BlockSpec(gain.shape, lambda b, t: (0, 0)),
                  blk, blk, blk, blk],
        out_specs=blk,
        out_shape=jax.ShapeDtypeStruct((bsz, seq, HG_W), _BF16),
        scratch_shapes=[
            pltpu.VMEM((HG_HEADS, HG_KDIM, HG_KDIM), _F32),
            pltpu.VMEM((rows, HG_W), _F32),
            pltpu.VMEM((rows, HG_W), _F32),
            pltpu.VMEM((rows, HG_W), _F32),
        ],
        compiler_params=pltpu.CompilerParams(
            dimension_semantics=("arbitrary", "arbitrary"), vmem_limit_bytes=VMEM_LIMIT_BYTES),
    )(lbl, gain, hq, hf, hi, hg)


def _layer_norm(y, g, b):
    mu = jnp.mean(y, axis=1, keepdims=True)
    yc = y - mu
    var = jnp.mean(yc * yc, axis=1, keepdims=True)
    return yc * lax.rsqrt(var + LN_EPS) * g + b


def _ffn_kernel(x_ref, att_ref, hgo_ref, wo_ref, g1_ref, b1_ref, wu_ref, wd_ref,
                g2_ref, b2_ref, o_ref, *, alpha):
    mix = (_tn_dot(att_ref[0], wo_ref[:ATT_W, :]) + _dot(hgo_ref[...], wo_ref[ATT_W:, :]))
    y1 = _layer_norm(alpha * x_ref[...] + mix, g1_ref[...], b1_ref[...])
    y1b = y1.astype(_BF16)
    h = jnp.zeros_like(y1)
    for c in range(wu_ref.shape[1] // FFN_COLS):
        cs = slice(c * FFN_COLS, (c + 1) * FFN_COLS)
        u = jnp.maximum(_dot(y1b, wu_ref[:, cs]), 0.0)
        h = h + _dot((u * u).astype(_BF16), wd_ref[cs, :])
    o_ref[...] = _layer_norm(alpha * y1 + h, g2_ref[...], b2_ref[...])


def _out_ffn(x2d, att_t, hgo2d, w_o, g1, b1, w_up, w_down, g2, b2, alpha):
    rows, d = x2d.shape
    seq = att_t.shape[2]
    tm = FFN_ROWS
    n_seq_blocks = seq // tm
    row_spec = lambda w: pl.BlockSpec((tm, w), lambda i: (i, 0))
    full_spec = lambda a: pl.BlockSpec(a.shape, lambda i: (0, 0))
    att_spec = pl.BlockSpec((1, ATT_W, tm), lambda i: (i // n_seq_blocks, 0, i % n_seq_blocks))
    vec = lambda a: a.reshape(1, d).astype(_F32)
    args = (x2d, att_t, hgo2d, w_o.astype(_BF16), vec(g1), vec(b1),
            w_up.astype(_BF16), w_down.astype(_BF16), vec(g2), vec(b2))
    in_specs = [row_spec(d), att_spec, row_spec(HG_W)] + [full_spec(a) for a in args[3:]]
    return pl.pallas_call(
        functools.partial(_ffn_kernel, alpha=alpha),
        grid=(rows // tm,),
        in_specs=in_specs,
        out_specs=row_spec(d),
        out_shape=jax.ShapeDtypeStruct((rows, d), _F32),
        compiler_params=pltpu.CompilerParams(
            dimension_semantics=("arbitrary",), vmem_limit_bytes=VMEM_LIMIT_BYTES),
    )(*args)


def kernel(x, w_in, w_o, lb_logits, hg_norm_g, ln1_g, ln1_b, w_up, w_down, ln2_g, ln2_b):
    bsz, seq, d = x.shape
    depth = w_in.shape[0]
    alpha = (2.0 * depth) ** 0.25
    x2d = x.reshape(bsz * seq, d)
    for l in range(depth):
        q, k, v, iq, ik, iw, hq, hf, hi, hg = _project(x2d, w_in[l], seq)
        r3 = lambda a: a.reshape(bsz, seq, a.shape[-1])
        att_t = _dsa_attention(r3(q), r3(k), r3(v), r3(iq), r3(ik), r3(iw))
        hgo = _hgrn2(r3(hq), r3(hf), r3(hi), r3(hg), lb_logits, hg_norm_g[l], l)
        x2d = _out_ffn(x2d, att_t, hgo.reshape(bsz * seq, HG_W),
                       w_o[l], ln1_g[l], ln1_b[l], w_up[l], w_down[l], ln2_g[l], ln2_b[l], alpha)
    return x2d.reshape(bsz, seq, d)
```

```python
import functools
import math

import numpy as np
import jax
import jax.numpy as jnp
from jax import lax
from jax.experimental import pallas as pl
from jax.experimental.pallas import tpu as pltpu

ATT_HEAD_DIM = 64
ATT_HEADS = 8
ATT_W = ATT_HEADS * ATT_HEAD_DIM
IDX_HEADS = 4
IDX_DIM = 64
IDX_W = IDX_HEADS * IDX_DIM
TOPK_MAX = 256
HG_KDIM = 128
HG_HEADS = 4
HG_W = HG_HEADS * HG_KDIM
ROPE_THETA = 10000.0
LN_EPS = 1e-5
RMS_EPS = 1e-6

LANES = 128
SUBLANES = 8
BF16_ROWS = 16
VMEM_LIMIT_BYTES = 56 * 1024 * 1024

PROJ_ROWS = 256
DSA_QB = 256
DSA_KB = 512
V_ROWS = ATT_HEAD_DIM + BF16_ROWS
COUNT_ACCS = 4
CAP_GROUPS = 2
CAP_ROWS = 2 * CAP_GROUPS * BF16_ROWS
CAP_CHUNKS_PER_BLOCK = DSA_KB // CAP_ROWS
FOLD_GROUPS = CAP_ROWS // BF16_ROWS
REFINE_BITS = 6
FOLD_MIN_CHUNKS = 14
HG_ROWS = 256
HG_CHUNK = 64
HG_SUB = 8
FFN_ROWS = 256
FFN_COLS = 1024

_F32 = jnp.float32
_BF16 = jnp.bfloat16
_I32 = jnp.int32
_I16 = jnp.int16
_INT_MIN = -(2 ** 31)
_I16_MIN = -(2 ** 15)
_I16_MAX = 2 ** 15 - 1
_NEG = -1e30


def _nt_dot(a, b):
    return lax.dot_general(a, b, (((1,), (1,)), ((), ())), preferred_element_type=_F32)


def _tn_dot(a, b):
    return lax.dot_general(a, b, (((0,), (0,)), ((), ())), preferred_element_type=_F32)


def _dot(a, b):
    return jnp.dot(a, b, preferred_element_type=_F32)


def _tree_sum(parts):
    while len(parts) > 1:
        parts = [parts[n] + parts[n + 1] for n in range(0, len(parts) - 1, 2)] + (
            [parts[-1]] if len(parts) % 2 else [])
    return parts[0]


def _rope_group(z, cos, sin_signed):
    lane = lax.broadcasted_iota(_I32, z.shape, 1)
    first_half = (lane % ATT_HEAD_DIM) < (ATT_HEAD_DIM // 2)
    upper = pltpu.roll(z, LANES - ATT_HEAD_DIM // 2, 1)
    lower = pltpu.roll(z, ATT_HEAD_DIM // 2, 1)
    return z * cos + jnp.where(first_half, upper, lower) * sin_signed


def _proj_kernel(x_ref, wa_ref, wh_ref, cos_ref, sin_ref,
                 qt_ref, k_ref, vt_ref, iqt_ref, ik_ref, iwt_ref,
                 hq_ref, hf_ref, hi_ref, hg_ref, *, q_scale, iw_scale):
    xb = x_ref[...].astype(_BF16)
    cos = cos_ref[...]
    sin = sin_ref[...]
    pa = _dot(xb, wa_ref[...])

    def roped(col0, width):
        return [_rope_group(pa[:, col0 + g * LANES: col0 + (g + 1) * LANES], cos, sin)
                for g in range(width // LANES)]

    for g, z in enumerate(roped(0, ATT_W)):
        qt_ref[0, g * LANES:(g + 1) * LANES, :] = (z * q_scale).T.astype(_BF16)
    for g, z in enumerate(roped(ATT_W, ATT_W)):
        k_ref[:, g * LANES:(g + 1) * LANES] = z.astype(_BF16)
    rows = x_ref.shape[0]
    ones_row = lax.broadcasted_iota(_I32, (V_ROWS - ATT_HEAD_DIM, rows), 0) == 0
    for g in range(ATT_W // LANES):
        vt = pa[:, 2 * ATT_W + g * LANES:2 * ATT_W + (g + 1) * LANES].T.astype(_BF16)
        for sub in range(LANES // ATT_HEAD_DIM):
            head = g * (LANES // ATT_HEAD_DIM) + sub
            vt_ref[0, head, :ATT_HEAD_DIM, :] = vt[sub * ATT_HEAD_DIM:(sub + 1) * ATT_HEAD_DIM]
            vt_ref[0, head, ATT_HEAD_DIM:, :] = jnp.where(ones_row, 1.0, 0.0).astype(_BF16)
    for g, z in enumerate(roped(3 * ATT_W, IDX_W)):
        iqt_ref[0, g * LANES:(g + 1) * LANES, :] = z.T.astype(_BF16)
    ik_ref[...] = roped(3 * ATT_W + IDX_W, LANES)[0].astype(_BF16)
    iwt_ref[0] = (pa[:, 3 * ATT_W + IDX_W + LANES:] * iw_scale).T[:SUBLANES]

    ph = _dot(xb, wh_ref[...])
    hq_ref[...] = ph[:, 0 * HG_W:1 * HG_W]
    hf_ref[...] = ph[:, 1 * HG_W:2 * HG_W]
    hi_ref[...] = ph[:, 2 * HG_W:3 * HG_W]
    hg_ref[...] = ph[:, 3 * HG_W:4 * HG_W]


def _rope_tables(seq):
    half = ATT_HEAD_DIM // 2
    inv = np.power(np.float64(ROPE_THETA), -np.arange(half, dtype=np.float64) / half)
    ang = np.arange(seq, dtype=np.float64)[:, None] * inv[None, :]
    cos = np.cos(ang)
    sin = np.sin(ang)
    cos_t = np.tile(np.concatenate([cos, cos], axis=1), (1, LANES // ATT_HEAD_DIM))
    sin_t = np.tile(np.concatenate([-sin, sin], axis=1), (1, LANES // ATT_HEAD_DIM))
    return jnp.asarray(cos_t, _F32), jnp.asarray(sin_t, _F32)


def _project(x2d, w_in, seq):
    rows, d = x2d.shape
    c = 3 * ATT_W + IDX_W
    w_ik = w_in[:, c:c + IDX_DIM]
    w_iw = w_in[:, c + IDX_DIM:c + IDX_DIM + IDX_HEADS]
    wa = jnp.concatenate([w_in[:, :c], w_ik, w_ik,
                          jnp.pad(w_iw, ((0, 0), (0, LANES - IDX_HEADS)))],
                         axis=1).astype(_BF16)
    wh = w_in[:, c + IDX_DIM + IDX_HEADS:].astype(_BF16)
    cos_t, sin_t = _rope_tables(seq)
    tm = PROJ_ROWS
    n_seq_blocks = seq // tm
    row_spec = lambda w: pl.BlockSpec((tm, w), lambda i: (i, 0))
    full_spec = lambda a: pl.BlockSpec(a.shape, lambda i: (0, 0))
    pos_spec = pl.BlockSpec((tm, LANES), lambda i: (i % n_seq_blocks, 0))
    bsz = rows // seq
    t_spec = lambda *feat: pl.BlockSpec(
        (1,) + feat + (tm,), lambda i: (i // n_seq_blocks,) + (0,) * len(feat) + (i % n_seq_blocks,))
    out_shapes = [
        jax.ShapeDtypeStruct((bsz, ATT_W, seq), _BF16),
        jax.ShapeDtypeStruct((rows, ATT_W), _BF16),
        jax.ShapeDtypeStruct((bsz, ATT_HEADS, V_ROWS, seq), _BF16),
        jax.ShapeDtypeStruct((bsz, IDX_W, seq), _BF16),
        jax.ShapeDtypeStruct((rows, LANES), _BF16),
        jax.ShapeDtypeStruct((bsz, SUBLANES, seq), _F32),
    ] + [jax.ShapeDtypeStruct((rows, HG_W), _F32)] * 4
    out_specs = [t_spec(ATT_W), row_spec(ATT_W), t_spec(ATT_HEADS, V_ROWS), t_spec(IDX_W),
                 row_spec(LANES), t_spec(SUBLANES)] + [row_spec(HG_W)] * 4
    kern = functools.partial(_proj_kernel, q_scale=ATT_HEAD_DIM ** -0.5 * math.log2(math.e),
                             iw_scale=(IDX_HEADS ** -0.5) * (IDX_DIM ** -0.5))
    return pl.pallas_call(
        kern,
        grid=(rows // tm,),
        in_specs=[row_spec(d), full_spec(wa), full_spec(wh), pos_spec, pos_spec],
        out_specs=out_specs,
        out_shape=out_shapes,
        compiler_params=pltpu.CompilerParams(
            dimension_semantics=("arbitrary",), vmem_limit_bytes=VMEM_LIMIT_BYTES),
    )(x2d, wa, wh, cos_t, sin_t)


def _dsa_kernel(qi_ref, kj_ref, ka_ref, vp_ref, vc_ref,
                qt_ref, iqt_ref, iwt_ref, ik_ref, before_ref, k_ref, vt_prev_ref, vt_ref,
                o_ref,
                hi_scr, lo_scr, fold_scr, cap_scr, bias_scr, m_scr, acc_scr,
                s_even, s_odd, bm_even, bm_odd, *, topk):
    del ka_ref, vp_ref, vc_ref
    p = pl.program_id(1)
    i = qi_ref[p]
    j = kj_ref[p]
    qb, kb = DSA_QB, DSA_KB
    n_kb = (i * qb + qb - 1) // kb + 1
    n_steps = (i * qb + qb - 1) // (2 * kb) + 1
    first_head = lax.broadcasted_iota(_I32, (LANES, qb), 0) < ATT_HEAD_DIM

    def one_head(pair_rows, h):
        keep = first_head if h % 2 == 0 else jnp.logical_not(first_head)
        return jnp.where(keep, pair_rows, jnp.zeros_like(pair_rows))

    @pl.when(j == 0)
    def _select():
        iwt = iwt_ref[0]
        qpos = i * qb + lax.broadcasted_iota(_I32, (kb, qb), 1)
        krow = lax.broadcasted_iota(_I32, (kb, qb), 0)
        iq_heads = [one_head(iqt_ref[0, (h // 2) * LANES:(h // 2 + 1) * LANES, :], h)
                    for h in range(IDX_HEADS)]

        def score_chunk(c, causal):
            row0 = pl.multiple_of(c * kb, kb)
            ikc = ik_ref[0, pl.ds(row0, kb), :]
            score = jnp.zeros((kb, qb), _F32)
            for h in range(IDX_HEADS):
                logits = _dot(ikc, iq_heads[h])
                score = score + iwt[h:h + 1, :] * jnp.maximum(logits, 0.0)
            bits = lax.bitcast_convert_type(score, _I32)
            key = bits ^ ((bits >> 31) & 0x7FFFFFFF)
            if causal:
                key = jnp.where(row0 + krow <= qpos, key, _INT_MIN)
            hi = (key >> 16).astype(_I16)
            hi_scr[c] = hi
            lo_scr[c] = ((key & 0xFFFF) + _I16_MIN).astype(_I16)
            n_tiles = kb // BF16_ROWS
            row0_fold = pl.multiple_of((c % CAP_CHUNKS_PER_BLOCK) * CAP_ROWS, CAP_ROWS)
            for g in range(FOLD_GROUPS):
                tiles = [hi[r * BF16_ROWS:(r + 1) * BF16_ROWS] for r in range(g, n_tiles, FOLD_GROUPS)]
                while len(tiles) > 1:
                    tiles = [jnp.where(tiles[n] > tiles[n + 1], tiles[n], tiles[n + 1])
                             for n in range(0, len(tiles), 2)]
                fold_scr[c // CAP_CHUNKS_PER_BLOCK, pl.ds(row0_fold + g * BF16_ROWS, BF16_ROWS), :] = tiles[0]

        def score_pair(c2, carry, causal=False):
            score_chunk(2 * c2, causal)
            score_chunk(2 * c2 + 1, causal)
            return carry

        fold_scr[...] = jnp.full_like(fold_scr, _I16_MIN)
        lax.fori_loop(0, n_steps - 1, score_pair, 0)
        score_pair(n_steps - 1, 0, causal=True)

        tile_rows = lambda r: slice(r * BF16_ROWS, (r + 1) * BF16_ROWS)
        i16_min = jnp.int16(_I16_MIN)

        def count(ref, cand, n_trips, blocks_per_trip, strict=False):
            cand_rows = jnp.broadcast_to(cand.astype(_I16), (BF16_ROWS, qb))
            one, zero = jnp.int16(1), jnp.int16(0)

            def body(t, accs):
                accs = list(accs)
                for u in range(blocks_per_trip):
                    for r in range(kb // BF16_ROWS):
                        tile = ref[blocks_per_trip * t + u, tile_rows(r), :]
                        hit = tile > cand_rows if strict else tile >= cand_rows
                        accs[r % len(accs)] = accs[r % len(accs)] + jnp.where(hit, one, zero)
                return tuple(accs)

            accs = lax.fori_loop(0, n_trips, body, (jnp.zeros((BF16_ROWS, qb), _I16),) * COUNT_ACCS)
            return jnp.sum(_tree_sum(list(accs)).astype(_I32), axis=0, keepdims=True)

        def bisect(ref, n_above, n_trips, blocks_per_trip):
            def body(step, carry):
                val, n_next = carry
                bit = jnp.left_shift(jnp.int32(1), 15 - step)
                cand = jnp.where(step == 0, 0, val | bit)
                cnt = n_above + count(ref, cand, n_trips, blocks_per_trip)
                ok = cnt >= topk
                return jnp.where(ok, cand, val), jnp.where(ok, n_next, cnt)
            return lax.fori_loop(0, 16, body, (jnp.full((1, qb), _I16_MIN, _I32), n_above))

        n_cap = (n_kb + CAP_CHUNKS_PER_BLOCK - 1) // CAP_CHUNKS_PER_BLOCK
        no_keys = jnp.zeros((1, qb), _I32)

        def high_full():
            return bisect(hi_scr, no_keys, n_steps, 2)

        def high_from_fold():
            base, _ = bisect(fold_scr, no_keys, n_cap, 1)

            def count_from(cand):
                cnt = count(hi_scr, jnp.minimum(cand, _I16_MAX), n_steps, 2)
                return jnp.where(cand > _I16_MAX, 0, cnt)

            n_beyond = count_from(base + (1 << REFINE_BITS))

            def refine():
                def body(step, carry):
                    off, n_next = carry
                    cand_off = off | jnp.left_shift(jnp.int32(1), REFINE_BITS - 1 - step)
                    cnt = count_from(base + cand_off)
                    ok = cnt >= topk
                    return jnp.where(ok, cand_off, off), jnp.where(ok, n_next, cnt)
                off, n_next = lax.fori_loop(0, REFINE_BITS, body, (no_keys, n_beyond))
                return base + off, n_next

            n_outside = jnp.sum(jnp.where(n_beyond >= topk, 1.0, 0.0))
            return lax.cond(n_outside > 0.0, high_full, refine)

        t_hi, n_gt_hi = lax.cond(n_kb >= FOLD_MIN_CHUNKS, high_from_fold, high_full)
        t_hi_tile = jnp.broadcast_to(t_hi.astype(_I16), (BF16_ROWS, qb))

        cap_scr[...] = jnp.full_like(cap_scr, _I16_MIN)

        def capture_block(c, carry):
            tops = [[jnp.full((BF16_ROWS, qb), _I16_MIN, _I16)] * 2 for _ in range(CAP_GROUPS)]
            for r in range(kb // BF16_ROWS):
                x = jnp.where(hi_scr[c, tile_rows(r), :] == t_hi_tile, lo_scr[c, tile_rows(r), :], i16_min)
                lo_scr[c, tile_rows(r), :] = x
                first, second = tops[r % CAP_GROUPS]
                above = x > first
                tops[r % CAP_GROUPS] = [jnp.where(above, x, first),
                                        jnp.where(above, first, jnp.where(x > second, x, second))]
            row0 = pl.multiple_of((c % CAP_CHUNKS_PER_BLOCK) * CAP_ROWS, CAP_ROWS)
            for g in range(CAP_GROUPS):
                for t in range(2):
                    cap_scr[c // CAP_CHUNKS_PER_BLOCK,
                            pl.ds(row0 + (2 * g + t) * BF16_ROWS, BF16_ROWS), :] = tops[g][t]
            return carry

        lax.fori_loop(0, n_kb, capture_block, 0)
        t_lo, _ = bisect(cap_scr, n_gt_hi, n_cap, 1)
        n_gt = n_gt_hi + count(lo_scr, t_lo, n_steps, 2, strict=True)
        n_wrong = jnp.sum(jnp.where(n_gt >= topk, 1.0, 0.0))
        t_lo, n_gt = lax.cond(n_wrong > 0.0,
                              lambda: bisect(lo_scr, n_gt_hi, n_steps, 2),
                              lambda: (t_lo, n_gt))
        t_hi_rows = jnp.broadcast_to(t_hi.astype(_I16), (kb, qb))
        t_lo = jnp.where((t_hi == _I16_MIN) & (t_lo == _I16_MIN), _I16_MIN + 1, t_lo)
        t_lo_rows = jnp.broadcast_to(t_lo.astype(_I16), (kb, qb))
        rem_rows = jnp.broadcast_to((topk - n_gt).astype(_I16), (kb, qb))

        def mask_block(c, seen):
            hi = hi_scr[c]
            lo = lo_scr[c]
            tie = (hi == t_hi_rows) & (lo == t_lo_rows)
            tie_count = jnp.where(tie, jnp.asarray(1, _BF16), jnp.asarray(0, _BF16))

            def tie_at(r):
                hit = (hi[r:r + 1].astype(_I32) == t_hi) & (lo[r:r + 1].astype(_I32) == t_lo)
                return jnp.where(hit, 1.0, 0.0)

            half = kb // 2
            before_top = seen + _dot(before_ref[...], tie_count[:half])
            seen_mid = before_top[half - 1:half] + tie_at(half - 1)
            before_bot = seen_mid + _dot(before_ref[...], tie_count[half:])
            ties_before = jnp.concatenate([before_top, before_bot], axis=0)
            allowed = ties_before.astype(_I32).astype(_I16) < rem_rows
            chosen = (hi > t_hi_rows) | (lo > t_lo_rows) | (tie & allowed)
            bias_scr[c] = jnp.where(chosen, jnp.asarray(0, bias_scr.dtype),
                                    jnp.asarray(_NEG, bias_scr.dtype))
            return before_bot[half - 1:half] + tie_at(kb - 1)

        lax.fori_loop(0, n_steps, lambda c2, seen: mask_block(2 * c2 + 1, mask_block(2 * c2, seen)),
                      jnp.zeros((1, qb), _F32))

        m_scr[...] = jnp.full_like(m_scr, _NEG)
        acc_scr[...] = jnp.zeros_like(acc_scr)
        s_odd[...] = jnp.full_like(s_odd, _NEG)
        bm_odd[...] = jnp.full_like(bm_odd, _NEG)

    def phase(k_rows, chunk, s_write, bm_write, s_read, bm_read, vt_read):
        bias = bias_scr[chunk].astype(_F32)
        def stage_a(h):
            pair = slice((h // 2) * LANES, (h // 2 + 1) * LANES)
            s = _dot(k_ref[0, k_rows, pair], one_head(qt_ref[0, pair, :], h)) + bias
            s_write[h] = s
            bm_write[h] = jnp.max(s, axis=0, keepdims=True)

        def stage_b(h):
            m_old = m_scr[h]
            m_new = jnp.maximum(m_old, bm_read[h])
            pr = jnp.exp2(s_read[h] - m_new).astype(_BF16)
            acc_scr[h] = jnp.exp2(m_old - m_new) * acc_scr[h] + _dot(vt_read[0, h], pr)
            m_scr[h] = m_new

        stage_a(0)
        for h in range(ATT_HEADS):
            if h + 1 < ATT_HEADS:
                stage_a(h + 1)
            stage_b(h)

    last_chunk = 2 * n_steps - 1
    phase(slice(0, kb), jnp.minimum(2 * j, last_chunk), s_even, bm_even, s_odd, bm_odd, vt_prev_ref)

    @pl.when(j < n_steps)
    def _second():
        phase(slice(kb, 2 * kb), 2 * j + 1, s_odd, bm_odd, s_even, bm_even, vt_ref)

    @pl.when(j == n_steps)
    def _finish():
        for h in range(ATT_HEADS):
            a = acc_scr[h]
            o_ref[0, h * ATT_HEAD_DIM:(h + 1) * ATT_HEAD_DIM, :] = (
                a[:ATT_HEAD_DIM] / a[ATT_HEAD_DIM:ATT_HEAD_DIM + 1]).astype(o_ref.dtype)


def _dsa_attention(qt, k, vt, iqt, ik, iwt):
    bsz, seq, _ = k.shape
    qb, kb = DSA_QB, DSA_KB
    topk = min(TOPK_MAX, seq // 4)
    n_qb = seq // qb
    n_steps = lambda i: (i * qb + qb - 1) // (2 * kb) + 1
    pairs = [(i, j) for i in range(n_qb) for j in range(n_steps(i) + 1)]
    as_i32 = lambda vals: jnp.asarray(np.array(vals, np.int32))
    qi = as_i32([i for i, j in pairs])
    kj = as_i32([j for i, j in pairs])
    k_pair = as_i32([min(j, n_steps(i) - 1) for i, j in pairs])
    v_prev = as_i32([max(2 * j - 1, 0) for i, j in pairs])
    v_this = as_i32([min(2 * j, 2 * n_steps(i) - 1) for i, j in pairs])
    before =jnp.asarray(np.tril(np.ones((kb // 2, kb // 2), np.float32), -1), _BF16)
    q_map = lambda b, p, qi_r, *_: (b, 0, qi_r[p])
    vt_spec = lambda which: pl.BlockSpec(
        (1, ATT_HEADS, V_ROWS, kb), lambda b, p, *refs: (b, 0, 0, refs[which][p]))
    grid_spec = pltpu.PrefetchScalarGridSpec(
        num_scalar_prefetch=5,
        grid=(bsz, len(pairs)),
        in_specs=[
            pl.BlockSpec((1, ATT_W, qb), q_map),
            pl.BlockSpec((1, IDX_W, qb), q_map),
            pl.BlockSpec((1, SUBLANES, qb), q_map),
            pl.BlockSpec((1, seq, LANES), lambda b, p, *_: (b, 0, 0)),
            pl.BlockSpec(before.shape, lambda b, p, *_: (0, 0)),
            pl.BlockSpec((1, 2 * kb, ATT_W), lambda b, p, *refs: (b, refs[2][p], 0)),
            vt_spec(3),
            vt_spec(4),
        ],
        out_specs=pl.BlockSpec((1, ATT_W, qb), q_map),
        scratch_shapes=[
            pltpu.VMEM((seq // kb, kb, qb), _I16),
            pltpu.VMEM((seq // kb, kb, qb), _I16),
            pltpu.VMEM((-(-(seq // kb) // CAP_CHUNKS_PER_BLOCK), kb, qb), _I16),
            pltpu.VMEM((-(-(seq // kb) // CAP_CHUNKS_PER_BLOCK), kb, qb), _I16),
            pltpu.VMEM((seq // kb, kb, qb), _BF16),
            pltpu.VMEM((ATT_HEADS, 1, qb), _F32),
            pltpu.VMEM((ATT_HEADS, V_ROWS, qb), _F32),
            pltpu.VMEM((ATT_HEADS, kb, qb), _F32),
            pltpu.VMEM((ATT_HEADS, kb, qb), _F32),
            pltpu.VMEM((ATT_HEADS, 1, qb), _F32),
            pltpu.VMEM((ATT_HEADS, 1, qb), _F32),
        ],
    )
    return pl.pallas_call(
        functools.partial(_dsa_kernel, topk=topk),
        grid_spec=grid_spec,
        out_shape=jax.ShapeDtypeStruct((bsz, ATT_W, seq), _BF16),
        compiler_params=pltpu.CompilerParams(
            dimension_semantics=("arbitrary", "arbitrary"), vmem_limit_bytes=VMEM_LIMIT_BYTES),
    )(qi, kj, k_pair, v_prev, v_this, qt, iqt, iwt, ik, before, k, vt, vt)


def _split3(a):
    hi = a.astype(_BF16)
    r1 = a - hi.astype(_F32)
    mid = r1.astype(_BF16)
    lo = (r1 - mid.astype(_F32)).astype(_BF16)
    return hi, mid, lo


def _hgrn_kernel(lbl_ref, g_ref, hq_ref, hf_ref, hi_ref, hg_ref, o_ref,
                 state_scr, kk_scr, b_scr, o_scr, *, layer):
    rows, ch, sb = HG_ROWS, HG_CHUNK, HG_SUB
    n_sub = ch // sb

    @pl.when(pl.program_id(1) == 0)
    def _reset():
        state_scr[...] = jnp.zeros_like(state_scr)

    lbl = lbl_ref[...]
    e = jnp.exp(lbl - jnp.max(lbl, axis=0, keepdims=True))
    lb = jnp.sum(e[:layer + 1], axis=0, keepdims=True) / jnp.sum(e, axis=0, keepdims=True)

    f = lb + (1.0 - lb) * jax.nn.sigmoid(hf_ref[0])
    kk_scr[...] = 1.0 - f
    logf = jnp.log(f)
    r_i = lax.broadcasted_iota(_I32, (ch, ch), 0)
    c_i = lax.broadcasted_iota(_I32, (ch, ch), 1)
    lower = jnp.where(c_i <= r_i, 1.0, 0.0).astype(_BF16)
    for c in range(rows // ch):
        parts = _split3(logf[c * ch:(c + 1) * ch])
        b_scr[c * ch:(c + 1) * ch, :] = sum(_dot(lower, part) for part in parts)

    t_idx = lax.broadcasted_iota(_I32, (sb, 1), 0)
    row_idx = lax.broadcasted_iota(_I32, (ch, 1), 0)

    def chunk(c, carry):
        r0 = pl.multiple_of(c * ch, ch)
        cs = pl.ds(r0, ch)
        b = b_scr[cs, :]
        kk = kk_scr[cs, :]
        qv = hq_ref[0, cs, :]
        vv = hi_ref[0, cs, :]
        b_last = b[ch - 1:ch]
        q_in = (qv * jnp.exp(b)).astype(_BF16)
        k_out = kk * jnp.exp(b_last - b)
        vb = vv.astype(_BF16)

        a_off = [[jnp.zeros((sb, ch), _F32)] for _ in range(HG_HEADS)]
        for s_i in range(1, n_sub):
            ref_b = b[s_i * sb - 1:s_i * sb]
            q_s = (qv[s_i * sb:(s_i + 1) * sb] * jnp.exp(b[s_i * sb:(s_i + 1) * sb] - ref_b))
            k_s = jnp.where(row_idx < s_i * sb, kk * jnp.exp(jnp.minimum(ref_b - b, 0.0)), 0.0)
            q_s = q_s.astype(_BF16)
            k_s = k_s.astype(_BF16)
            for h in range(HG_HEADS):
                hs = slice(h * HG_KDIM, (h + 1) * HG_KDIM)
                a_off[h].append(_nt_dot(q_s[:, hs], k_s[:, hs]))

        for h in range(HG_HEADS):
            hs = slice(h * HG_KDIM, (h + 1) * HG_KDIM)
            st = state_scr[h]
            o_h = _nt_dot(q_in[:, hs], st.astype(_BF16))
            a_h = jnp.concatenate(a_off[h], axis=0).astype(_BF16)
            o_scr[cs, hs] = o_h + _dot(a_h, vb[:, hs])
            state_scr[h] = (st * jnp.exp(b_last[:, hs])
                            + _tn_dot(vb[:, hs], k_out[:, hs].astype(_BF16)))

        for s_i in range(n_sub):
            rs = pl.ds(r0 + s_i * sb, sb)
            q_s = hq_ref[0, rs, :]
            b_s = b_scr[rs, :]
            acc = o_scr[rs, :]
            for t in range(sb):
                one = pl.ds(r0 + s_i * sb + t, 1)
                w = q_s * jnp.exp(jnp.minimum(b_s - b_scr[one, :], 0.0)) * kk_scr[one, :]
                v_row = hi_ref[0, one, :]
                parts = []
                for h in range(HG_HEADS):
                    hs = slice(h * HG_KDIM, (h + 1) * HG_KDIM)
                    a = jnp.sum(w[:, hs], axis=1, keepdims=True)
                    parts.append(jnp.where(t_idx >= t, a, 0.0) * v_row[:, hs])
                acc = acc + jnp.concatenate(parts, axis=1)
            o_scr[rs, :] = acc
        return carry

    lax.fori_loop(0, rows // ch, chunk, 0)

    o = o_scr[...]
    gate = hg_ref[0]
    gain = g_ref[...]
    for h in range(HG_HEADS):
        hs = slice(h * HG_KDIM, (h + 1) * HG_KDIM)
        oh = o[:, hs]
        oh = oh * lax.rsqrt(jnp.mean(oh * oh, axis=1, keepdims=True) + RMS_EPS) * gain[:, hs]
        gh = gate[:, hs]
        o_ref[0, :, hs] = (oh * (gh * jax.nn.sigmoid(gh))).astype(o_ref.dtype)


def _hgrn2(hq, hf, hi, hg, lb_logits, norm_g, layer):
    bsz, seq, _ = hq.shape
    rows = HG_ROWS
    blk = pl.BlockSpec((1, rows, HG_W), lambda b, t: (b, t, 0))
    lbl = lb_logits.reshape(lb_logits.shape[0], HG_W)
    gain = norm_g.reshape(1, HG_W)
    return pl.pallas_call(
        functools.partial(_hgrn_kernel, layer=layer),
        grid=(bsz, seq // rows),
        in_specs=[pl.BlockSpec(lbl.shape, lambda b, t: (0, 0)),
                  pl.BlockSpec(gain.shape, lambda b, t: (0, 0)),
                  blk, blk, blk, blk],
        out_specs=blk,
        out_shape=jax.ShapeDtypeStruct((bsz, seq, HG_W), _BF16),
        scratch_shapes=[
            pltpu.VMEM((HG_HEADS, HG_KDIM, HG_KDIM), _F32),
            pltpu.VMEM((rows, HG_W), _F32),
            pltpu.VMEM((rows, HG_W), _F32),
            pltpu.VMEM((rows, HG_W), _F32),
        ],
        compiler_params=pltpu.CompilerParams(
            dimension_semantics=("arbitrary", "arbitrary"), vmem_limit_bytes=VMEM_LIMIT_BYTES),
    )(lbl, gain, hq, hf, hi, hg)


def _layer_norm(y, g, b):
    mu = jnp.mean(y, axis=1, keepdims=True)
    yc = y - mu
    var = jnp.mean(yc * yc, axis=1, keepdims=True)
    return yc * lax.rsqrt(var + LN_EPS) * g + b


def _ffn_kernel(x_ref, att_ref, hgo_ref, wo_ref, g1_ref, b1_ref, wu_ref, wd_ref,
                g2_ref, b2_ref, o_ref, *, alpha):
    mix = (_tn_dot(att_ref[0], wo_ref[:ATT_W, :]) + _dot(hgo_ref[...], wo_ref[ATT_W:, :]))
    y1 = _layer_norm(alpha * x_ref[...] + mix, g1_ref[...], b1_ref[...])
    y1b = y1.astype(_BF16)
    h = jnp.zeros_like(y1)
    for c in range(wu_ref.shape[1] // FFN_COLS):
        cs = slice(c * FFN_COLS, (c + 1) * FFN_COLS)
        u = jnp.maximum(_dot(y1b, wu_ref[:, cs]), 0.0)
        h = h + _dot((u * u).astype(_BF16), wd_ref[cs, :])
    o_ref[...] = _layer_norm(alpha * y1 + h, g2_ref[...], b2_ref[...])


def _out_ffn(x2d, att_t, hgo2d, w_o, g1, b1, w_up, w_down, g2, b2, alpha):
    rows, d = x2d.shape
    seq = att_t.shape[2]
    tm = FFN_ROWS
    n_seq_blocks = seq // tm
    row_spec = lambda w: pl.BlockSpec((tm, w), lambda i: (i, 0))
    full_spec = lambda a: pl.BlockSpec(a.shape, lambda i: (0, 0))
    att_spec = pl.BlockSpec((1, ATT_W, tm), lambda i: (i // n_seq_blocks, 0, i % n_seq_blocks))
    vec = lambda a: a.reshape(1, d).astype(_F32)
    args = (x2d, att_t, hgo2d, w_o.astype(_BF16), vec(g1), vec(b1),
            w_up.astype(_BF16), w_down.astype(_BF16), vec(g2), vec(b2))
    in_specs = [row_spec(d), att_spec, row_spec(HG_W)] + [full_spec(a) for a in args[3:]]
    return pl.pallas_call(
        functools.partial(_ffn_kernel, alpha=alpha),
        grid=(rows // tm,),
        in_specs=in_specs,
        out_specs=row_spec(d),
        out_shape=jax.ShapeDtypeStruct((rows, d), _F32),
        compiler_params=pltpu.CompilerParams(
            dimension_semantics=("arbitrary",), vmem_limit_bytes=VMEM_LIMIT_BYTES),
    )(*args)


def kernel(x, w_in, w_o, lb_logits, hg_norm_g, ln1_g, ln1_b, w_up, w_down, ln2_g, ln2_b):
    bsz, seq, d = x.shape
    depth = w_in.shape[0]
    alpha = (2.0 * depth) ** 0.25
    x2d = x.reshape(bsz * seq, d)
    for l in range(depth):
        qt, k, vt, iqt, ik, iwt, hq, hf, hi, hg = _project(x2d, w_in[l], seq)
        r3 = lambda a: a.reshape(bsz, seq, a.shape[-1])
        att_t = _dsa_attention(qt, r3(k), vt, iqt, r3(ik), iwt)
        hgo = _hgrn2(r3(hq), r3(hf), r3(hi), r3(hg), lb_logits, hg_norm_g[l], l)
        x2d = _out_ffn(x2d, att_t, hgo.reshape(bsz * seq, HG_W),
                       w_o[l], ln1_g[l], ln1_b[l], w_up[l], w_down[l], ln2_g[l], ln2_b[l], alpha)
    return x2d.reshape(bsz, seq, d)
```

```python
import functools
import math

import numpy as np
import jax
import jax.numpy as jnp
from jax import lax
from jax.experimental import pallas as pl
from jax.experimental.pallas import tpu as pltpu

ATT_HEAD_DIM = 64
ATT_HEADS = 8
ATT_W = ATT_HEADS * ATT_HEAD_DIM
IDX_HEADS = 4
IDX_DIM = 64
IDX_W = IDX_HEADS * IDX_DIM
TOPK_MAX = 256
HG_KDIM = 128
HG_HEADS = 4
HG_W = HG_HEADS * HG_KDIM
ROPE_THETA = 10000.0
LN_EPS = 1e-5
RMS_EPS = 1e-6

LANES = 128
SUBLANES = 8
BF16_ROWS = 16
VMEM_LIMIT_BYTES = 56 * 1024 * 1024

PROJ_ROWS = 256
DSA_QB = 256
DSA_KB = 512
V_ROWS = ATT_HEAD_DIM + BF16_ROWS
COUNT_ACCS = 4
CAP_GROUPS = 2
CAP_ROWS = 2 * CAP_GROUPS * BF16_ROWS
CAP_CHUNKS_PER_BLOCK = DSA_KB // CAP_ROWS
FOLD_GROUPS = CAP_ROWS // BF16_ROWS
REFINE_BITS = 6
FOLD_MIN_CHUNKS = 14
HG_ROWS = 256
HG_CHUNK = 64
HG_SUB = 8
FFN_ROWS = 256
FFN_COLS = 1024

_F32 = jnp.float32
_BF16 = jnp.bfloat16
_I32 = jnp.int32
_I16 = jnp.int16
_INT_MIN = -(2 ** 31)
_I16_MIN = -(2 ** 15)
_I16_MAX = 2 ** 15 - 1
_NEG = -1e30


def _nt_dot(a, b):
    return lax.dot_general(a, b, (((1,), (1,)), ((), ())), preferred_element_type=_F32)


def _tn_dot(a, b):
    return lax.dot_general(a, b, (((0,), (0,)), ((), ())), preferred_element_type=_F32)


def _dot(a, b):
    return jnp.dot(a, b, preferred_element_type=_F32)


def _tree_sum(parts):
    while len(parts) > 1:
        parts = [parts[n] + parts[n + 1] for n in range(0, len(parts) - 1, 2)] + (
            [parts[-1]] if len(parts) % 2 else [])
    return parts[0]


def _rope_group(z, cos, sin_signed):
    lane = lax.broadcasted_iota(_I32, z.shape, 1)
    first_half = (lane % ATT_HEAD_DIM) < (ATT_HEAD_DIM // 2)
    upper = pltpu.roll(z, LANES - ATT_HEAD_DIM // 2, 1)
    lower = pltpu.roll(z, ATT_HEAD_DIM // 2, 1)
    return z * cos + jnp.where(first_half, upper, lower) * sin_signed


def _proj_kernel(x_ref, wa_ref, wh_ref, cos_ref, sin_ref,
                 qt_ref, k_ref, vt_ref, iqt_ref, ik_ref, iwt_ref,
                 hq_ref, hf_ref, hi_ref, hg_ref, *, q_scale, iw_scale):
    xb = x_ref[...].astype(_BF16)
    cos = cos_ref[...]
    sin = sin_ref[...]
    pa = _dot(xb, wa_ref[...])

    def roped(col0, width):
        return [_rope_group(pa[:, col0 + g * LANES: col0 + (g + 1) * LANES], cos, sin)
                for g in range(width // LANES)]

    for g, z in enumerate(roped(0, ATT_W)):
        qt_ref[0, g * LANES:(g + 1) * LANES, :] = (z * q_scale).T.astype(_BF16)
    for g, z in enumerate(roped(ATT_W, ATT_W)):
        k_ref[:, g * LANES:(g + 1) * LANES] = z.astype(_BF16)
    rows = x_ref.shape[0]
    ones_row = lax.broadcasted_iota(_I32, (V_ROWS - ATT_HEAD_DIM, rows), 0) == 0
    for g in range(ATT_W // LANES):
        vt = pa[:, 2 * ATT_W + g * LANES:2 * ATT_W + (g + 1) * LANES].T.astype(_BF16)
        for sub in range(LANES // ATT_HEAD_DIM):
            head = g * (LANES // ATT_HEAD_DIM) + sub
            vt_ref[0, head, :ATT_HEAD_DIM, :] = vt[sub * ATT_HEAD_DIM:(sub + 1) * ATT_HEAD_DIM]
            vt_ref[0, head, ATT_HEAD_DIM:, :] = jnp.where(ones_row, 1.0, 0.0).astype(_BF16)
    for g, z in enumerate(roped(3 * ATT_W, IDX_W)):
        iqt_ref[0, g * LANES:(g + 1) * LANES, :] = z.T.astype(_BF16)
    ik_ref[...] = roped(3 * ATT_W + IDX_W, LANES)[0].astype(_BF16)
    iwt_ref[0] = (pa[:, 3 * ATT_W + IDX_W + LANES:] * iw_scale).T[:SUBLANES]

    ph = _dot(xb, wh_ref[...])
    hq_ref[...] = ph[:, 0 * HG_W:1 * HG_W]
    hf_ref[...] = ph[:, 1 * HG_W:2 * HG_W]
    hi_ref[...] = ph[:, 2 * HG_W:3 * HG_W]
    hg_ref[...] = ph[:, 3 * HG_W:4 * HG_W]


def _rope_tables(seq):
    half = ATT_HEAD_DIM // 2
    inv = np.power(np.float64(ROPE_THETA), -np.arange(half, dtype=np.float64) / half)
    ang = np.arange(seq, dtype=np.float64)[:, None] * inv[None, :]
    cos = np.cos(ang)
    sin = np.sin(ang)
    cos_t = np.tile(np.concatenate([cos, cos], axis=1), (1, LANES // ATT_HEAD_DIM))
    sin_t = np.tile(np.concatenate([-sin, sin], axis=1), (1, LANES // ATT_HEAD_DIM))
    return jnp.asarray(cos_t, _F32), jnp.asarray(sin_t, _F32)


def _project(x2d, w_in, seq):
    rows, d = x2d.shape
    c = 3 * ATT_W + IDX_W
    w_ik = w_in[:, c:c + IDX_DIM]
    w_iw = w_in[:, c + IDX_DIM:c + IDX_DIM + IDX_HEADS]
    wa = jnp.concatenate([w_in[:, :c], w_ik, w_ik,
                          jnp.pad(w_iw, ((0, 0), (0, LANES - IDX_HEADS)))],
                         axis=1).astype(_BF16)
    wh = w_in[:, c + IDX_DIM + IDX_HEADS:].astype(_BF16)
    cos_t, sin_t = _rope_tables(seq)
    tm = PROJ_ROWS
    n_seq_blocks = seq // tm
    row_spec = lambda w: pl.BlockSpec((tm, w), lambda i: (i, 0))
    full_spec = lambda a: pl.BlockSpec(a.shape, lambda i: (0, 0))
    pos_spec = pl.BlockSpec((tm, LANES), lambda i: (i % n_seq_blocks, 0))
    bsz = rows // seq
    t_spec = lambda *feat: pl.BlockSpec(
        (1,) + feat + (tm,), lambda i: (i // n_seq_blocks,) + (0,) * len(feat) + (i % n_seq_blocks,))
    out_shapes = [
        jax.ShapeDtypeStruct((bsz, ATT_W, seq), _BF16),
        jax.ShapeDtypeStruct((rows, ATT_W), _BF16),
        jax.ShapeDtypeStruct((bsz, ATT_HEADS, V_ROWS, seq), _BF16),
        jax.ShapeDtypeStruct((bsz, IDX_W, seq), _BF16),
        jax.ShapeDtypeStruct((rows, LANES), _BF16),
        jax.ShapeDtypeStruct((bsz, SUBLANES, seq), _F32),
    ] + [jax.ShapeDtypeStruct((rows, HG_W), _F32)] * 4
    out_specs = [t_spec(ATT_W), row_spec(ATT_W), t_spec(ATT_HEADS, V_ROWS), t_spec(IDX_W),
                 row_spec(LANES), t_spec(SUBLANES)] + [row_spec(HG_W)] * 4
    kern = functools.partial(_proj_kernel, q_scale=ATT_HEAD_DIM ** -0.5 * math.log2(math.e),
                             iw_scale=(IDX_HEADS ** -0.5) * (IDX_DIM ** -0.5))
    return pl.pallas_call(
        kern,
        grid=(rows // tm,),
        in_specs=[row_spec(d), full_spec(wa), full_spec(wh), pos_spec, pos_spec],
        out_specs=out_specs,
        out_shape=out_shapes,
        compiler_params=pltpu.CompilerParams(
            dimension_semantics=("arbitrary",), vmem_limit_bytes=VMEM_LIMIT_BYTES),
    )(x2d, wa, wh, cos_t, sin_t)


def _dsa_kernel(qi_ref, kj_ref, ka_ref, vp_ref, vc_ref,
                qt_ref, iqt_ref, iwt_ref, ik_ref, before_ref, k_ref, vt_prev_ref, vt_ref,
                o_ref,
                hi_scr, lo_scr, fold_scr, cap_scr, bias_scr, m_scr, acc_scr,
                s_even, s_odd, bm_even, bm_odd, *, topk):
    del ka_ref, vp_ref, vc_ref
    p = pl.program_id(1)
    i = qi_ref[p]
    j = kj_ref[p]
    qb, kb = DSA_QB, DSA_KB
    n_kb = (i * qb + qb - 1) // kb + 1
    n_steps = (i * qb + qb - 1) // (2 * kb) + 1
    first_head = lax.broadcasted_iota(_I32, (LANES, qb), 0) < ATT_HEAD_DIM

    def one_head(pair_rows, h):
        keep = first_head if h % 2 == 0 else jnp.logical_not(first_head)
        return jnp.where(keep, pair_rows, jnp.zeros_like(pair_rows))

    @pl.when(j == 0)
    def _select():
        iwt = iwt_ref[0]
        qpos = i * qb + lax.broadcasted_iota(_I32, (kb, qb), 1)
        krow = lax.broadcasted_iota(_I32, (kb, qb), 0)
        iq_heads = [one_head(iqt_ref[0, (h // 2) * LANES:(h // 2 + 1) * LANES, :], h)
                    for h in range(IDX_HEADS)]

        def score_chunk(c, causal):
            row0 = pl.multiple_of(c * kb, kb)
            ikc = ik_ref[0, pl.ds(row0, kb), :]
            score = jnp.zeros((kb, qb), _F32)
            for h in range(IDX_HEADS):
                logits = _dot(ikc, iq_heads[h])
                score = score + iwt[h:h + 1, :] * jnp.maximum(logits, 0.0)
            bits = lax.bitcast_convert_type(score, _I32)
            key = bits ^ ((bits >> 31) & 0x7FFFFFFF)
            if causal:
                key = jnp.where(row0 + krow <= qpos, key, _INT_MIN)
            hi = (key >> 16).astype(_I16)
            hi_scr[c] = hi
            lo_scr[c] = ((key & 0xFFFF) + _I16_MIN).astype(_I16)
            n_tiles = kb // BF16_ROWS
            row0_fold = pl.multiple_of((c % CAP_CHUNKS_PER_BLOCK) * CAP_ROWS, CAP_ROWS)
            for g in range(FOLD_GROUPS):
                tiles = [hi[r * BF16_ROWS:(r + 1) * BF16_ROWS] for r in range(g, n_tiles, FOLD_GROUPS)]
                while len(tiles) > 1:
                    tiles = [jnp.where(tiles[n] > tiles[n + 1], tiles[n], tiles[n + 1])
                             for n in range(0, len(tiles), 2)]
                fold_scr[c // CAP_CHUNKS_PER_BLOCK, pl.ds(row0_fold + g * BF16_ROWS, BF16_ROWS), :] = tiles[0]

        def score_pair(c2, carry, causal=False):
            score_chunk(2 * c2, causal)
            score_chunk(2 * c2 + 1, causal)
            return carry

        fold_scr[...] = jnp.full_like(fold_scr, _I16_MIN)
        lax.fori_loop(0, n_steps - 1, score_pair, 0)
        score_pair(n_steps - 1, 0, causal=True)

        tile_rows = lambda r: slice(r * BF16_ROWS, (r + 1) * BF16_ROWS)
        i16_min = jnp.int16(_I16_MIN)

        def count(ref, cand, n_trips, blocks_per_trip, strict=False):
            cand_rows = jnp.broadcast_to(cand.astype(_I16), (BF16_ROWS, qb))
            one, zero = jnp.int16(1), jnp.int16(0)

            def body(t, accs):
                accs = list(accs)
                for u in range(blocks_per_trip):
                    for r in range(kb // BF16_ROWS):
                        tile = ref[blocks_per_trip * t + u, tile_rows(r), :]
                        hit = tile > cand_rows if strict else tile >= cand_rows
                        accs[r % len(accs)] = accs[r % len(accs)] + jnp.where(hit, one, zero)
                return tuple(accs)

            accs = lax.fori_loop(0, n_trips, body, (jnp.zeros((BF16_ROWS, qb), _I16),) * COUNT_ACCS)
            return jnp.sum(_tree_sum(list(accs)).astype(_I32), axis=0, keepdims=True)

        def bisect(ref, n_above, n_trips, blocks_per_trip):
            def body(step, carry):
                val, n_next = carry
                bit = jnp.left_shift(jnp.int32(1), 15 - step)
                cand = jnp.where(step == 0, 0, val | bit)
                cnt = n_above + count(ref, cand, n_trips, blocks_per_trip)
                ok = cnt >= topk
                return jnp.where(ok, cand, val), jnp.where(ok, n_next, cnt)
            return lax.fori_loop(0, 16, body, (jnp.full((1, qb), _I16_MIN, _I32), n_above))

        n_cap = (n_kb + CAP_CHUNKS_PER_BLOCK - 1) // CAP_CHUNKS_PER_BLOCK
        no_keys = jnp.zeros((1, qb), _I32)

        def high_full():
            return bisect(hi_scr, no_keys, n_steps, 2)

        def high_from_fold():
            base, _ = bisect(fold_scr, no_keys, n_cap, 1)

            def count_from(cand):
                cnt = count(hi_scr, jnp.minimum(cand, _I16_MAX), n_steps, 2)
                return jnp.where(cand > _I16_MAX, 0, cnt)

            n_beyond = count_from(base + (1 << REFINE_BITS))

            def refine():
                def body(step, carry):
                    off, n_next = carry
                    cand_off = off | jnp.left_shift(jnp.int32(1), REFINE_BITS - 1 - step)
                    cnt = count_from(base + cand_off)
                    ok = cnt >= topk
                    return jnp.where(ok, cand_off, off), jnp.where(ok, n_next, cnt)
                off, n_next = lax.fori_loop(0, REFINE_BITS, body, (no_keys, n_beyond))
                return base + off, n_next

            n_outside = jnp.sum(jnp.where(n_beyond >= topk, 1.0, 0.0))
            return lax.cond(n_outside > 0.0, high_full, refine)

        t_hi, n_gt_hi = lax.cond(n_kb >= FOLD_MIN_CHUNKS, high_from_fold, high_full)
        t_hi_tile = jnp.broadcast_to(t_hi.astype(_I16), (BF16_ROWS, qb))

        cap_scr[...] = jnp.full_like(cap_scr, _I16_MIN)

        def capture_block(c, carry):
            tops = [[jnp.full((BF16_ROWS, qb), _I16_MIN, _I16)] * 2 for _ in range(CAP_GROUPS)]
            for r in range(kb // BF16_ROWS):
                x = jnp.where(hi_scr[c, tile_rows(r), :] == t_hi_tile, lo_scr[c, tile_rows(r), :], i16_min)
                lo_scr[c, tile_rows(r), :] = x
                first, second = tops[r % CAP_GROUPS]
                above = x > first
                tops[r % CAP_GROUPS] = [jnp.where(above, x, first),
                                        jnp.where(above, first, jnp.where(x > second, x, second))]
            row0 = pl.multiple_of((c % CAP_CHUNKS_PER_BLOCK) * CAP_ROWS, CAP_ROWS)
            for g in range(CAP_GROUPS):
                for t in range(2):
                    cap_scr[c // CAP_CHUNKS_PER_BLOCK,
                            pl.ds(row0 + (2 * g + t) * BF16_ROWS, BF16_ROWS), :] = tops[g][t]
            return carry

        lax.fori_loop(0, n_kb, capture_block, 0)
        t_lo, _ = bisect(cap_scr, n_gt_hi, n_cap, 1)
        n_gt = n_gt_hi + count(lo_scr, t_lo, n_steps, 2, strict=True)
        n_wrong = jnp.sum(jnp.where(n_gt >= topk, 1.0, 0.0))
        t_lo, n_gt = lax.cond(n_wrong > 0.0,
                              lambda: bisect(lo_scr, n_gt_hi, n_steps, 2),
                              lambda: (t_lo, n_gt))
        t_hi_rows = jnp.broadcast_to(t_hi.astype(_I16), (kb, qb))
        t_lo = jnp.where((t_hi == _I16_MIN) & (t_lo == _I16_MIN), _I16_MIN + 1, t_lo)
        t_lo_rows = jnp.broadcast_to(t_lo.astype(_I16), (kb, qb))
        rem_rows = jnp.broadcast_to((topk - n_gt).astype(_I16), (kb, qb))

        def mask_block(c, seen):
            hi = hi_scr[c]
            lo = lo_scr[c]
            tie = (hi == t_hi_rows) & (lo == t_lo_rows)
            tie_count = jnp.where(tie, jnp.asarray(1, _BF16), jnp.asarray(0, _BF16))

            def tie_at(r):
                hit = (hi[r:r + 1].astype(_I32) == t_hi) & (lo[r:r + 1].astype(_I32) == t_lo)
                return jnp.where(hit, 1.0, 0.0)

            half = kb // 2
            before_top = seen + _dot(before_ref[...], tie_count[:half])
            seen_mid = before_top[half - 1:half] + tie_at(half - 1)
            before_bot = seen_mid + _dot(before_ref[...], tie_count[half:])
            ties_before = jnp.concatenate([before_top, before_bot], axis=0)
            allowed = ties_before.astype(_I32).astype(_I16) < rem_rows
            chosen = (hi > t_hi_rows) | (lo > t_lo_rows) | (tie & allowed)
            bias_scr[c] = jnp.where(chosen, jnp.asarray(0, bias_scr.dtype),
                                    jnp.asarray(_NEG, bias_scr.dtype))
            return before_bot[half - 1:half] + tie_at(kb - 1)

        lax.fori_loop(0, n_steps, lambda c2, seen: mask_block(2 * c2 + 1, mask_block(2 * c2, seen)),
                      jnp.zeros((1, qb), _F32))

        m_scr[...] = jnp.full_like(m_scr, _NEG)
        acc_scr[...] = jnp.zeros_like(acc_scr)
        s_odd[...] = jnp.full_like(s_odd, _NEG)
        bm_odd[...] = jnp.full_like(bm_odd, _NEG)

    def stage_b(h, s_read, bm_read, vt_read):
        m_old = m_scr[h]
        m_new = jnp.maximum(m_old, bm_read[h])
        pr = jnp.exp2(s_read[h] - m_new).astype(_BF16)
        acc_scr[h] = jnp.exp2(m_old - m_new) * acc_scr[h] + _dot(vt_read[0, h], pr)
        m_scr[h] = m_new

    def phase(k_rows, chunk, s_write, bm_write, s_read, bm_read, vt_read):
        bias = bias_scr[chunk].astype(_F32)

        def stage_a(h):
            pair = slice((h // 2) * LANES, (h // 2 + 1) * LANES)
            s = _dot(k_ref[0, k_rows, pair], one_head(qt_ref[0, pair, :], h)) + bias
            s_write[h] = s
            bm_write[h] = jnp.max(s, axis=0, keepdims=True)

        stage_a(0)
        for h in range(ATT_HEADS):
            if h + 1 < ATT_HEADS:
                stage_a(h + 1)
            stage_b(h, s_read, bm_read, vt_read)

    @pl.when(j < n_steps)
    def _first():
        phase(slice(0, kb), 2 * j, s_even, bm_even, s_odd, bm_odd, vt_prev_ref)

    @pl.when(2 * j + 1 < 2 * n_steps)
    def _second():
        phase(slice(kb, 2 * kb), 2 * j + 1, s_odd, bm_odd, s_even, bm_even, vt_ref)

    @pl.when(j == n_steps)
    def _drain():
        for h in range(ATT_HEADS):
            stage_b(h, s_odd, bm_odd, vt_prev_ref)
        for h in range(ATT_HEADS):
            a = acc_scr[h]
            o_ref[0, h * ATT_HEAD_DIM:(h + 1) * ATT_HEAD_DIM, :] = (
                a[:ATT_HEAD_DIM] / a[ATT_HEAD_DIM:ATT_HEAD_DIM + 1]).astype(o_ref.dtype)


def _dsa_attention(qt, k, vt, iqt, ik, iwt):
    bsz, seq, _ = k.shape
    qb, kb = DSA_QB, DSA_KB
    topk = min(TOPK_MAX, seq // 4)
    n_qb = seq // qb
    n_steps = lambda i: (i * qb + qb - 1) // (2 * kb) + 1
    pairs = [(i, j) for i in range(n_qb) for j in range(n_steps(i) + 1)]
    as_i32 = lambda vals: jnp.asarray(np.array(vals, np.int32))
    qi = as_i32([i for i, j in pairs])
    kj = as_i32([j for i, j in pairs])
    k_pair = as_i32([min(j, n_steps(i) - 1) for i, j in pairs])
    v_prev = as_i32([max(2 * j - 1, 0) for i, j in pairs])
    v_this = as_i32([min(2 * j, 2 * n_steps(i) - 1) for i, j in pairs])
    before =jnp.asarray(np.tril(np.ones((kb // 2, kb // 2), np.float32), -1), _BF16)
    q_map = lambda b, p, qi_r, *_: (b, 0, qi_r[p])
    vt_spec = lambda which: pl.BlockSpec(
        (1, ATT_HEADS, V_ROWS, kb), lambda b, p, *refs: (b, 0, 0, refs[which][p]))
    grid_spec = pltpu.PrefetchScalarGridSpec(
        num_scalar_prefetch=5,
        grid=(bsz, len(pairs)),
        in_specs=[
            pl.BlockSpec((1, ATT_W, qb), q_map),
            pl.BlockSpec((1, IDX_W, qb), q_map),
            pl.BlockSpec((1, SUBLANES, qb), q_map),
            pl.BlockSpec((1, seq, LANES), lambda b, p, *_: (b, 0, 0)),
            pl.BlockSpec(before.shape, lambda b, p, *_: (0, 0)),
            pl.BlockSpec((1, 2 * kb, ATT_W), lambda b, p, *refs: (b, refs[2][p], 0)),
            vt_spec(3),
            vt_spec(4),
        ],
        out_specs=pl.BlockSpec((1, ATT_W, qb), q_map),
        scratch_shapes=[
            pltpu.VMEM((seq // kb, kb, qb), _I16),
            pltpu.VMEM((seq // kb, kb, qb), _I16),
            pltpu.VMEM((-(-(seq // kb) // CAP_CHUNKS_PER_BLOCK), kb, qb), _I16),
            pltpu.VMEM((-(-(seq // kb) // CAP_CHUNKS_PER_BLOCK), kb, qb), _I16),
            pltpu.VMEM((seq // kb, kb, qb), _BF16),
            pltpu.VMEM((ATT_HEADS, 1, qb), _F32),
            pltpu.VMEM((ATT_HEADS, V_ROWS, qb), _F32),
            pltpu.VMEM((ATT_HEADS, kb, qb), _F32),
            pltpu.VMEM((ATT_HEADS, kb, qb), _F32),
            pltpu.VMEM((ATT_HEADS, 1, qb), _F32),
            pltpu.VMEM((ATT_HEADS, 1, qb), _F32),
        ],
    )
    return pl.pallas_call(
        functools.partial(_dsa_kernel, topk=topk),
        grid_spec=grid_spec,
        out_shape=jax.ShapeDtypeStruct((bsz, ATT_W, seq), _BF16),
        compiler_params=pltpu.CompilerParams(
            dimension_semantics=("arbitrary", "arbitrary"), vmem_limit_bytes=VMEM_LIMIT_BYTES),
    )(qi, kj, k_pair, v_prev, v_this, qt, iqt, iwt, ik, before, k, vt, vt)


def _split3(a):
    hi = a.astype(_BF16)
    r1 = a - hi.astype(_F32)
    mid = r1.astype(_BF16)
    lo = (r1 - mid.astype(_F32)).astype(_BF16)
    return hi, mid, lo


def _hgrn_kernel(lbl_ref, g_ref, hq_ref, hf_ref, hi_ref, hg_ref, o_ref,
                 state_scr, kk_scr, b_scr, o_scr, *, layer):
    rows, ch, sb = HG_ROWS, HG_CHUNK, HG_SUB
    n_sub = ch // sb

    @pl.when(pl.program_id(1) == 0)
    def _reset():
        state_scr[...] = jnp.zeros_like(state_scr)

    lbl = lbl_ref[...]
    e = jnp.exp(lbl - jnp.max(lbl, axis=0, keepdims=True))
    lb = jnp.sum(e[:layer + 1], axis=0, keepdims=True) / jnp.sum(e, axis=0, keepdims=True)

    f = lb + (1.0 - lb) * jax.nn.sigmoid(hf_ref[0])
    kk_scr[...] = 1.0 - f
    logf = jnp.log(f)
    r_i = lax.broadcasted_iota(_I32, (ch, ch), 0)
    c_i = lax.broadcasted_iota(_I32, (ch, ch), 1)
    lower = jnp.where(c_i <= r_i, 1.0, 0.0).astype(_BF16)
    for c in range(rows // ch):
        parts = _split3(logf[c * ch:(c + 1) * ch])
        b_scr[c * ch:(c + 1) * ch, :] = sum(_dot(lower, part) for part in parts)

    t_idx = lax.broadcasted_iota(_I32, (sb, 1), 0)
    row_idx = lax.broadcasted_iota(_I32, (ch, 1), 0)

    def chunk(c, carry):
        r0 = pl.multiple_of(c * ch, ch)
        cs = pl.ds(r0, ch)
        b = b_scr[cs, :]
        kk = kk_scr[cs, :]
        qv = hq_ref[0, cs, :]
        vv = hi_ref[0, cs, :]
        b_last = b[ch - 1:ch]
        q_in = (qv * jnp.exp(b)).astype(_BF16)
        k_out = kk * jnp.exp(b_last - b)
        vb = vv.astype(_BF16)

        a_off = [[jnp.zeros((sb, ch), _F32)] for _ in range(HG_HEADS)]
        for s_i in range(1, n_sub):
            ref_b = b[s_i * sb - 1:s_i * sb]
            q_s = (qv[s_i * sb:(s_i + 1) * sb] * jnp.exp(b[s_i * sb:(s_i + 1) * sb] - ref_b))
            k_s = jnp.where(row_idx < s_i * sb, kk * jnp.exp(jnp.minimum(ref_b - b, 0.0)), 0.0)
            q_s = q_s.astype(_BF16)
            k_s = k_s.astype(_BF16)
            for h in range(HG_HEADS):
                hs = slice(h * HG_KDIM, (h + 1) * HG_KDIM)
                a_off[h].append(_nt_dot(q_s[:, hs], k_s[:, hs]))

        for h in range(HG_HEADS):
            hs = slice(h * HG_KDIM, (h + 1) * HG_KDIM)
            st = state_scr[h]
            o_h = _nt_dot(q_in[:, hs], st.astype(_BF16))
            a_h = jnp.concatenate(a_off[h], axis=0).astype(_BF16)
            o_scr[cs, hs] = o_h + _dot(a_h, vb[:, hs])
            state_scr[h] = (st * jnp.exp(b_last[:, hs])
                            + _tn_dot(vb[:, hs], k_out[:, hs].astype(_BF16)))

        for s_i in range(n_sub):
            rs = pl.ds(r0 + s_i * sb, sb)
            q_s = hq_ref[0, rs, :]
            b_s = b_scr[rs, :]
            acc = o_scr[rs, :]
            for t in range(sb):
                one = pl.ds(r0 + s_i * sb + t, 1)
                w = q_s * jnp.exp(jnp.minimum(b_s - b_scr[one, :], 0.0)) * kk_scr[one, :]
                v_row = hi_ref[0, one, :]
                parts = []
                for h in range(HG_HEADS):
                    hs = slice(h * HG_KDIM, (h + 1) * HG_KDIM)
                    a = jnp.sum(w[:, hs], axis=1, keepdims=True)
                    parts.append(jnp.where(t_idx >= t, a, 0.0) * v_row[:, hs])
                acc = acc + jnp.concatenate(parts, axis=1)
            o_scr[rs, :] = acc
        return carry

    lax.fori_loop(0, rows // ch, chunk, 0)

    o = o_scr[...]
    gate = hg_ref[0]
    gain = g_ref[...]
    for h in range(HG_HEADS):
        hs = slice(h * HG_KDIM, (h + 1) * HG_KDIM)
        oh = o[:, hs]
        oh = oh * lax.rsqrt(jnp.mean(oh * oh, axis=1, keepdims=True) + RMS_EPS) * gain[:, hs]
        gh = gate[:, hs]
        o_ref[0, :, hs] = (oh * (gh * jax.nn.sigmoid(gh))).astype(o_ref.dtype)


def _hgrn2(hq, hf, hi, hg, lb_logits, norm_g, layer):
    bsz, seq, _ = hq.shape
    rows = HG_ROWS
    blk = pl.BlockSpec((1, rows, HG_W), lambda b, t: (b, t, 0))
    lbl = lb_logits.reshape(lb_logits.shape[0], HG_W)
    gain = norm_g.reshape(1, HG_W)
    return pl.pallas_call(
        functools.partial(_hgrn_kernel, layer=layer),
        grid=(bsz, seq // rows),
        in_specs=[pl.BlockSpec(lbl.shape, lambda b, t: (0, 0)),
                  pl.BlockSpec(gain.shape, lambda b, t: (0, 0)),
                  blk, blk, blk, blk],
        out_specs=blk,
        out_shape=jax.ShapeDtypeStruct((bsz, seq, HG_W), _BF16),
        scratch_shapes=[
            pltpu.VMEM((HG_HEADS, HG_KDIM, HG_KDIM), _F32),
            pltpu.VMEM((rows, HG_W), _F32),
            pltpu.VMEM((rows, HG_W), _F32),
            pltpu.VMEM((rows, HG_W), _F32),
        ],
        compiler_params=pltpu.CompilerParams(
            dimension_semantics=("arbitrary", "arbitrary"), vmem_limit_bytes=VMEM_LIMIT_BYTES),
    )(lbl, gain, hq, hf, hi, hg)


def _layer_norm(y, g, b):
    mu = jnp.mean(y, axis=1, keepdims=True)
    yc = y - mu
    var = jnp.mean(yc * yc, axis=1, keepdims=True)
    return yc * lax.rsqrt(var + LN_EPS) * g + b


def _ffn_kernel(x_ref, att_ref, hgo_ref, wo_ref, g1_ref, b1_ref, wu_ref, wd_ref,
                g2_ref, b2_ref, o_ref, *, alpha):
    mix = (_tn_dot(att_ref[0], wo_ref[:ATT_W, :]) + _dot(hgo_ref[...], wo_ref[ATT_W:, :]))
    y1 = _layer_norm(alpha * x_ref[...] + mix, g1_ref[...], b1_ref[...])
    y1b = y1.astype(_BF16)
    h = jnp.zeros_like(y1)
    for c in range(wu_ref.shape[1] // FFN_COLS):
        cs = slice(c * FFN_COLS, (c + 1) * FFN_COLS)
        u = jnp.maximum(_dot(y1b, wu_ref[:, cs]), 0.0)
        h = h + _dot((u * u).astype(_BF16), wd_ref[cs, :])
    o_ref[...] = _layer_norm(alpha * y1 + h, g2_ref[...], b2_ref[...])


def _out_ffn(x2d, att_t, hgo2d, w_o, g1, b1, w_up, w_down, g2, b2, alpha):
    rows, d = x2d.shape
    seq = att_t.shape[2]
    tm = FFN_ROWS
    n_seq_blocks = seq // tm
    row_spec = lambda w: pl.BlockSpec((tm, w), lambda i: (i, 0))
    full_spec = lambda a: pl.BlockSpec(a.shape, lambda i: (0, 0))
    att_spec = pl.BlockSpec((1, ATT_W, tm), lambda i: (i // n_seq_blocks, 0, i % n_seq_blocks))
    vec = lambda a: a.reshape(1, d).astype(_F32)
    args = (x2d, att_t, hgo2d, w_o.astype(_BF16), vec(g1), vec(b1),
            w_up.astype(_BF16), w_down.astype(_BF16), vec(g2), vec(b2))
    in_specs = [row_spec(d), att_spec, row_spec(HG_W)] + [full_spec(a) for a in args[3:]]
    return pl.pallas_call(
        functools.partial(_ffn_kernel, alpha=alpha),
        grid=(rows // tm,),
        in_specs=in_specs,
        out_specs=row_spec(d),
        out_shape=jax.ShapeDtypeStruct((rows, d), _F32),
        compiler_params=pltpu.CompilerParams(
            dimension_semantics=("arbitrary",), vmem_limit_bytes=VMEM_LIMIT_BYTES),
    )(*args)


def kernel(x, w_in, w_o, lb_logits, hg_norm_g, ln1_g, ln1_b, w_up, w_down, ln2_g, ln2_b):
    bsz, seq, d = x.shape
    depth = w_in.shape[0]
    alpha = (2.0 * depth) ** 0.25
    x2d = x.reshape(bsz * seq, d)
    for l in range(depth):
        qt, k, vt, iqt, ik, iwt, hq, hf, hi, hg = _project(x2d, w_in[l], seq)
        r3 = lambda a: a.reshape(bsz, seq, a.shape[-1])
        att_t = _dsa_attention(qt, r3(k), vt, iqt, r3(ik), iwt)
        hgo = _hgrn2(r3(hq), r3(hf), r3(hi), r3(hg), lb_logits, hg_norm_g[l], l)
        x2d = _out_ffn(x2d, att_t, hgo.reshape(bsz * seq, HG_W),
                       w_o[l], ln1_g[l], ln1_b[l], w_up[l], w_down[l], ln2_g[l], ln2_b[l], alpha)
    return x2d.reshape(bsz, seq, d)
```

```python
import functools
import math

import numpy as np
import jax
import jax.numpy as jnp
from jax import lax
from jax.experimental import pallas as pl
from jax.experimental.pallas import tpu as pltpu

ATT_HEAD_DIM = 64
ATT_HEADS = 8
ATT_W = ATT_HEADS * ATT_HEAD_DIM
IDX_HEADS = 4
IDX_DIM = 64
IDX_W = IDX_HEADS * IDX_DIM
TOPK_MAX = 256
HG_KDIM = 128
HG_HEADS = 4
HG_W = HG_HEADS * HG_KDIM
ROPE_THETA = 10000.0
LN_EPS = 1e-5
RMS_EPS = 1e-6

LANES = 128
SUBLANES = 8
BF16_ROWS = 16
VMEM_LIMIT_BYTES = 56 * 1024 * 1024

PROJ_ROWS = 256
DSA_QB = 256
DSA_KB = 512
V_ROWS = ATT_HEAD_DIM + BF16_ROWS
COUNT_ACCS = 4
CAP_GROUPS = 2
CAP_ROWS = 2 * CAP_GROUPS * BF16_ROWS
CAP_CHUNKS_PER_BLOCK = DSA_KB // CAP_ROWS
FOLD_GROUPS = CAP_ROWS // BF16_ROWS
REFINE_BITS = 6
FOLD_MIN_CHUNKS = 14
HG_ROWS = 256
HG_CHUNK = 64
HG_SUB = 8
FFN_ROWS = 256
FFN_COLS = 1024

_F32 = jnp.float32
_BF16 = jnp.bfloat16
_I32 = jnp.int32
_I16 = jnp.int16
_INT_MIN = -(2 ** 31)
_I16_MIN = -(2 ** 15)
_I16_MAX = 2 ** 15 - 1
_LOW_SIGN = 1 << 15
_NEG = -1e30


def _nt_dot(a, b):
    return lax.dot_general(a, b, (((1,), (1,)), ((), ())), preferred_element_type=_F32)


def _tn_dot(a, b):
    return lax.dot_general(a, b, (((0,), (0,)), ((), ())), preferred_element_type=_F32)


def _dot(a, b):
    return jnp.dot(a, b, preferred_element_type=_F32)


def _tree_sum(parts):
    while len(parts) > 1:
        parts = [parts[n] + parts[n + 1] for n in range(0, len(parts) - 1, 2)] + (
            [parts[-1]] if len(parts) % 2 else [])
    return parts[0]


def _rope_group(z, cos, sin_signed):
    lane = lax.broadcasted_iota(_I32, z.shape, 1)
    first_half = (lane % ATT_HEAD_DIM) < (ATT_HEAD_DIM // 2)
    upper = pltpu.roll(z, LANES - ATT_HEAD_DIM // 2, 1)
    lower = pltpu.roll(z, ATT_HEAD_DIM // 2, 1)
    return z * cos + jnp.where(first_half, upper, lower) * sin_signed


def _proj_kernel(x_ref, wa_ref, wh_ref, cos_ref, sin_ref,
                 qt_ref, k_ref, vt_ref, iqt_ref, ik_ref, iwt_ref,
                 hq_ref, hf_ref, hi_ref, hg_ref, *, q_scale, iw_scale):
    xb = x_ref[...].astype(_BF16)
    cos = cos_ref[...]
    sin = sin_ref[...]
    pa = _dot(xb, wa_ref[...])

    def roped(col0, width):
        return [_rope_group(pa[:, col0 + g * LANES: col0 + (g + 1) * LANES], cos, sin)
                for g in range(width // LANES)]

    for g, z in enumerate(roped(0, ATT_W)):
        qt_ref[0, g * LANES:(g + 1) * LANES, :] = (z * q_scale).T.astype(_BF16)
    for g, z in enumerate(roped(ATT_W, ATT_W)):
        k_ref[:, g * LANES:(g + 1) * LANES] = z.astype(_BF16)
    rows = x_ref.shape[0]
    ones_row = lax.broadcasted_iota(_I32, (V_ROWS - ATT_HEAD_DIM, rows), 0) == 0
    for g in range(ATT_W // LANES):
        vt = pa[:, 2 * ATT_W + g * LANES:2 * ATT_W + (g + 1) * LANES].T.astype(_BF16)
        for sub in range(LANES // ATT_HEAD_DIM):
            head = g * (LANES // ATT_HEAD_DIM) + sub
            vt_ref[0, head, :ATT_HEAD_DIM, :] = vt[sub * ATT_HEAD_DIM:(sub + 1) * ATT_HEAD_DIM]
            vt_ref[0, head, ATT_HEAD_DIM:, :] = jnp.where(ones_row, 1.0, 0.0).astype(_BF16)
    for g, z in enumerate(roped(3 * ATT_W, IDX_W)):
        iqt_ref[0, g * LANES:(g + 1) * LANES, :] = z.T.astype(_BF16)
    ik_ref[...] = roped(3 * ATT_W + IDX_W, LANES)[0].astype(_BF16)
    iwt_ref[0] = (pa[:, 3 * ATT_W + IDX_W + LANES:] * iw_scale).T[:SUBLANES]

    ph = _dot(xb, wh_ref[...])
    hq_ref[...] = ph[:, 0 * HG_W:1 * HG_W]
    hf_ref[...] = ph[:, 1 * HG_W:2 * HG_W]
    hi_ref[...] = ph[:, 2 * HG_W:3 * HG_W]
    hg_ref[...] = ph[:, 3 * HG_W:4 * HG_W]


def _rope_tables(seq):
    half = ATT_HEAD_DIM // 2
    inv = np.power(np.float64(ROPE_THETA), -np.arange(half, dtype=np.float64) / half)
    ang = np.arange(seq, dtype=np.float64)[:, None] * inv[None, :]
    cos = np.cos(ang)
    sin = np.sin(ang)
    cos_t = np.tile(np.concatenate([cos, cos], axis=1), (1, LANES // ATT_HEAD_DIM))
    sin_t = np.tile(np.concatenate([-sin, sin], axis=1), (1, LANES // ATT_HEAD_DIM))
    return jnp.asarray(cos_t, _F32), jnp.asarray(sin_t, _F32)


def _project(x2d, w_in, seq):
    rows, d = x2d.shape
    c = 3 * ATT_W + IDX_W
    w_ik = w_in[:, c:c + IDX_DIM]
    w_iw = w_in[:, c + IDX_DIM:c + IDX_DIM + IDX_HEADS]
    wa = jnp.concatenate([w_in[:, :c], w_ik, w_ik,
                          jnp.pad(w_iw, ((0, 0), (0, LANES - IDX_HEADS)))],
                         axis=1).astype(_BF16)
    wh = w_in[:, c + IDX_DIM + IDX_HEADS:].astype(_BF16)
    cos_t, sin_t = _rope_tables(seq)
    tm = PROJ_ROWS
    n_seq_blocks = seq // tm
    row_spec = lambda w: pl.BlockSpec((tm, w), lambda i: (i, 0))
    full_spec = lambda a: pl.BlockSpec(a.shape, lambda i: (0, 0))
    pos_spec = pl.BlockSpec((tm, LANES), lambda i: (i % n_seq_blocks, 0))
    bsz = rows // seq
    t_spec = lambda *feat: pl.BlockSpec(
        (1,) + feat + (tm,), lambda i: (i // n_seq_blocks,) + (0,) * len(feat) + (i % n_seq_blocks,))
    out_shapes = [
        jax.ShapeDtypeStruct((bsz, ATT_W, seq), _BF16),
        jax.ShapeDtypeStruct((rows, ATT_W), _BF16),
        jax.ShapeDtypeStruct((bsz, ATT_HEADS, V_ROWS, seq), _BF16),
        jax.ShapeDtypeStruct((bsz, IDX_W, seq), _BF16),
        jax.ShapeDtypeStruct((rows, LANES), _BF16),
        jax.ShapeDtypeStruct((bsz, SUBLANES, seq), _F32),
    ] + [jax.ShapeDtypeStruct((rows, HG_W), _F32)] * 4
    out_specs = [t_spec(ATT_W), row_spec(ATT_W), t_spec(ATT_HEADS, V_ROWS), t_spec(IDX_W),
                 row_spec(LANES), t_spec(SUBLANES)] + [row_spec(HG_W)] * 4
    kern = functools.partial(_proj_kernel, q_scale=ATT_HEAD_DIM ** -0.5 * math.log2(math.e),
                             iw_scale=(IDX_HEADS ** -0.5) * (IDX_DIM ** -0.5))
    return pl.pallas_call(
        kern,
        grid=(rows // tm,),
        in_specs=[row_spec(d), full_spec(wa), full_spec(wh), pos_spec, pos_spec],
        out_specs=out_specs,
        out_shape=out_shapes,
        compiler_params=pltpu.CompilerParams(
            dimension_semantics=("arbitrary",), vmem_limit_bytes=VMEM_LIMIT_BYTES),
    )(x2d, wa, wh, cos_t, sin_t)


def _dsa_kernel(qi_ref, kj_ref, ka_ref, vp_ref, vc_ref,
                qt_ref, iqt_ref, iwt_ref, ik_ref, before_ref, k_ref, vt_prev_ref, vt_ref,
                o_ref,
                hi_scr, lo_scr, fold_scr, cap_scr, bias_scr, m_scr, acc_scr,
                s_even, s_odd, bm_even, bm_odd, *, topk):
    del ka_ref, vp_ref, vc_ref
    p = pl.program_id(1)
    i = qi_ref[p]
    j = kj_ref[p]
    qb, kb = DSA_QB, DSA_KB
    n_kb = (i * qb + qb - 1) // kb + 1
    n_steps = (i * qb + qb - 1) // (2 * kb) + 1
    last_pair = 2 * (n_steps - 1)
    has_second = n_kb == 2 * n_steps
    first_head = lax.broadcasted_iota(_I32, (LANES, qb), 0) < ATT_HEAD_DIM

    def one_head(pair_rows, h):
        keep = first_head if h % 2 == 0 else jnp.logical_not(first_head)
        return jnp.where(keep, pair_rows, jnp.zeros_like(pair_rows))

    @pl.when(j == 0)
    def _select():
        iwt = iwt_ref[0]
        qpos = i * qb + lax.broadcasted_iota(_I32, (kb, qb), 1)
        krow = lax.broadcasted_iota(_I32, (kb, qb), 0)
        iq_heads = [one_head(iqt_ref[0, (h // 2) * LANES:(h // 2 + 1) * LANES, :], h)
                    for h in range(IDX_HEADS)]

        def score_chunk(c, causal):
            row0 = pl.multiple_of(c * kb, kb)
            ikc = ik_ref[0, pl.ds(row0, kb), :]
            score = jnp.zeros((kb, qb), _F32)
            for h in range(IDX_HEADS):
                logits = _dot(ikc, iq_heads[h])
                score = score + iwt[h:h + 1, :] * jnp.maximum(logits, 0.0)
            bits = lax.bitcast_convert_type(score, _I32)
            key = bits ^ (((bits >> 31) & 0x7FFFFFFF) ^ _LOW_SIGN)
            if causal:
                key = jnp.where(row0 + krow <= qpos, key, _INT_MIN ^ _LOW_SIGN)
            hi = (key >> 16).astype(_I16)
            hi_scr[c] = hi
            lo_scr[c] = key.astype(_I16)
            n_tiles = kb // BF16_ROWS
            row0_fold = pl.multiple_of((c % CAP_CHUNKS_PER_BLOCK) * CAP_ROWS, CAP_ROWS)
            for g in range(FOLD_GROUPS):
                tiles = [hi[r * BF16_ROWS:(r + 1) * BF16_ROWS] for r in range(g, n_tiles, FOLD_GROUPS)]
                while len(tiles) > 1:
                    tiles = [jnp.where(tiles[n] > tiles[n + 1], tiles[n], tiles[n + 1])
                             for n in range(0, len(tiles), 2)]
                fold_scr[c // CAP_CHUNKS_PER_BLOCK, pl.ds(row0_fold + g * BF16_ROWS, BF16_ROWS), :] = tiles[0]

        def score_pair(c2, carry):
            score_chunk(2 * c2, False)
            score_chunk(2 * c2 + 1, False)
            return carry

        fold_scr[...] = jnp.full_like(fold_scr, _I16_MIN)
        lax.fori_loop(0, n_steps - 1, score_pair, 0)
        score_chunk(last_pair, True)

        @pl.when(has_second)
        def _score_second():
            score_chunk(last_pair + 1, True)

        @pl.when(jnp.logical_not(has_second))
        def _blank_second():
            hi_scr[last_pair + 1] = jnp.full((kb, qb), _I16_MIN, _I16)
            lo_scr[last_pair + 1] = jnp.full((kb, qb), _I16_MIN, _I16)

        tile_rows = lambda r: slice(r * BF16_ROWS, (r + 1) * BF16_ROWS)
        i16_min = jnp.int16(_I16_MIN)

        def count(ref, cand, n_trips, blocks_per_trip, strict=False):
            cand_rows = jnp.broadcast_to(cand.astype(_I16), (BF16_ROWS, qb))
            one, zero = jnp.int16(1), jnp.int16(0)

            def body(t, accs):
                accs = list(accs)
                for u in range(blocks_per_trip):
                    for r in range(kb // BF16_ROWS):
                        tile = ref[blocks_per_trip * t + u, tile_rows(r), :]
                        hit = tile > cand_rows if strict else tile >= cand_rows
                        accs[r % len(accs)] = accs[r % len(accs)] + jnp.where(hit, one, zero)
                return tuple(accs)

            accs = lax.fori_loop(0, n_trips, body, (jnp.zeros((BF16_ROWS, qb), _I16),) * COUNT_ACCS)
            return jnp.sum(_tree_sum(list(accs)).astype(_I32), axis=0, keepdims=True)

        def bisect(ref, n_above, n_trips, blocks_per_trip):
            def body(step, carry):
                val, n_next = carry
                bit = jnp.left_shift(jnp.int32(1), 15 - step)
                cand = jnp.where(step == 0, 0, val | bit)
                cnt = n_above + count(ref, cand, n_trips, blocks_per_trip)
                ok = cnt >= topk
                return jnp.where(ok, cand, val), jnp.where(ok, n_next, cnt)
            return lax.fori_loop(0, 16, body, (jnp.full((1, qb), _I16_MIN, _I32), n_above))

        n_cap = (n_kb + CAP_CHUNKS_PER_BLOCK - 1) // CAP_CHUNKS_PER_BLOCK
        no_keys = jnp.zeros((1, qb), _I32)

        def high_full():
            return bisect(hi_scr, no_keys, n_steps, 2)

        def high_from_fold():
            base, _ = bisect(fold_scr, no_keys, n_cap, 1)

            def count_from(cand):
                cnt = count(hi_scr, jnp.minimum(cand, _I16_MAX), n_steps, 2)
                return jnp.where(cand > _I16_MAX, 0, cnt)

            n_beyond = count_from(base + (1 << REFINE_BITS))

            def refine():
                def body(step, carry):
                    off, n_next = carry
                    cand_off = off | jnp.left_shift(jnp.int32(1), REFINE_BITS - 1 - step)
                    cnt = count_from(base + cand_off)
                    ok = cnt >= topk
                    return jnp.where(ok, cand_off, off), jnp.where(ok, n_next, cnt)
                off, n_next = lax.fori_loop(0, REFINE_BITS, body, (no_keys, n_beyond))
                return base + off, n_next

            n_outside = jnp.sum(jnp.where(n_beyond >= topk, 1.0, 0.0))
            return lax.cond(n_outside > 0.0, high_full, refine)

        t_hi, n_gt_hi = lax.cond(n_kb >= FOLD_MIN_CHUNKS, high_from_fold, high_full)
        t_hi_tile = jnp.broadcast_to(t_hi.astype(_I16), (BF16_ROWS, qb))

        cap_scr[...] = jnp.full_like(cap_scr, _I16_MIN)

        def capture_block(c, carry):
            tops = [[jnp.full((BF16_ROWS, qb), _I16_MIN, _I16)] * 2 for _ in range(CAP_GROUPS)]
            for r in range(kb // BF16_ROWS):
                x = jnp.where(hi_scr[c, tile_rows(r), :] == t_hi_tile, lo_scr[c, tile_rows(r), :], i16_min)
                lo_scr[c, tile_rows(r), :] = x
                first, second = tops[r % CAP_GROUPS]
                above = x > first
                tops[r % CAP_GROUPS] = [jnp.where(above, x, first),
                                        jnp.where(above, first, jnp.where(x > second, x, second))]
            row0 = pl.multiple_of((c % CAP_CHUNKS_PER_BLOCK) * CAP_ROWS, CAP_ROWS)
            for g in range(CAP_GROUPS):
                for t in range(2):
                    cap_scr[c // CAP_CHUNKS_PER_BLOCK,
                            pl.ds(row0 + (2 * g + t) * BF16_ROWS, BF16_ROWS), :] = tops[g][t]
            return carry

        lax.fori_loop(0, n_kb, capture_block, 0)
        t_lo, _ = bisect(cap_scr, n_gt_hi, n_cap, 1)
        n_gt = n_gt_hi + count(lo_scr, t_lo, n_steps, 2, strict=True)
        n_wrong = jnp.sum(jnp.where(n_gt >= topk, 1.0, 0.0))
        t_lo, n_gt = lax.cond(n_wrong > 0.0,
                              lambda: bisect(lo_scr, n_gt_hi, n_steps, 2),
                              lambda: (t_lo, n_gt))
        t_hi_rows = jnp.broadcast_to(t_hi.astype(_I16), (kb, qb))
        t_lo = jnp.where((t_hi == _I16_MIN) & (t_lo == _I16_MIN), _I16_MIN + 1, t_lo)
        t_lo_rows = jnp.broadcast_to(t_lo.astype(_I16), (kb, qb))
        rem_rows = jnp.broadcast_to((topk - n_gt).astype(_I16), (kb, qb))

        def mask_block(c, seen):
            hi = hi_scr[c]
            lo = lo_scr[c]
            tie = (hi == t_hi_rows) & (lo == t_lo_rows)
            tie_count = jnp.where(tie, jnp.asarray(1, _BF16), jnp.asarray(0, _BF16))

            def tie_at(r):
                hit = (hi[r:r + 1].astype(_I32) == t_hi) & (lo[r:r + 1].astype(_I32) == t_lo)
                return jnp.where(hit, 1.0, 0.0)

            half = kb // 2
            before_top = seen + _dot(before_ref[...], tie_count[:half])
            seen_mid = before_top[half - 1:half] + tie_at(half - 1)
            before_bot = seen_mid + _dot(before_ref[...], tie_count[half:])
            ties_before = jnp.concatenate([before_top, before_bot], axis=0)
            allowed = ties_before.astype(_I32).astype(_I16) < rem_rows
            chosen = (hi > t_hi_rows) | (lo > t_lo_rows) | (tie & allowed)
            bias_scr[c] = jnp.where(chosen, jnp.asarray(0, bias_scr.dtype),
                                    jnp.asarray(_NEG, bias_scr.dtype))
            return before_bot[half - 1:half] + tie_at(kb - 1)

        seen = lax.fori_loop(0, n_steps - 1,
                             lambda c2, seen: mask_block(2 * c2 + 1, mask_block(2 * c2, seen)),
                             jnp.zeros((1, qb), _F32))
        seen = mask_block(last_pair, seen)

        @pl.when(has_second)
        def _mask_second():
            mask_block(last_pair + 1, seen)

        m_scr[...] = jnp.full_like(m_scr, _NEG)
        acc_scr[...] = jnp.zeros_like(acc_scr)
        s_odd[...] = jnp.full_like(s_odd, _NEG)
        bm_odd[...] = jnp.full_like(bm_odd, _NEG)

    def stage_b(h, s_read, bm_read, vt_read):
        m_old = m_scr[h]
        m_new = jnp.maximum(m_old, bm_read[h])
        pr = jnp.exp2(s_read[h] - m_new).astype(_BF16)
        acc_scr[h] = jnp.exp2(m_old - m_new) * acc_scr[h] + _dot(vt_read[0, h], pr)
        m_scr[h] = m_new

    def phase(k_rows, chunk, s_write, bm_write, s_read, bm_read, vt_read):
        bias = bias_scr[chunk].astype(_F32)

        def stage_a(h):
            pair = slice((h // 2) * LANES, (h // 2 + 1) * LANES)
            s = _dot(k_ref[0, k_rows, pair], one_head(qt_ref[0, pair, :], h)) + bias
            s_write[h] = s
            bm_write[h] = jnp.max(s, axis=0, keepdims=True)

        stage_a(0)
        for h in range(ATT_HEADS):
            if h + 1 < ATT_HEADS:
                stage_a(h + 1)
            stage_b(h, s_read, bm_read, vt_read)

    @pl.when(j < n_steps)
    def _first():
        phase(slice(0, kb), 2 * j, s_even, bm_even, s_odd, bm_odd, vt_prev_ref)

    @pl.when(2 * j + 1 < n_kb)
    def _second():
        phase(slice(kb, 2 * kb), 2 * j + 1, s_odd, bm_odd, s_even, bm_even, vt_ref)

    @pl.when((j == n_steps) & has_second)
    def _drain_odd():
        for h in range(ATT_HEADS):
            stage_b(h, s_odd, bm_odd, vt_prev_ref)

    @pl.when((j == n_steps) & jnp.logical_not(has_second))
    def _drain_even():
        for h in range(ATT_HEADS):
            stage_b(h, s_even, bm_even, vt_prev_ref)

    @pl.when(j == n_steps)
    def _finish():
        for h in range(ATT_HEADS):
            a = acc_scr[h]
            o_ref[0, h * ATT_HEAD_DIM:(h + 1) * ATT_HEAD_DIM, :] = (
                a[:ATT_HEAD_DIM] / a[ATT_HEAD_DIM:ATT_HEAD_DIM + 1]).astype(o_ref.dtype)


def _dsa_attention(qt, k, vt, iqt, ik, iwt):
    bsz, seq, _ = k.shape
    qb, kb = DSA_QB, DSA_KB
    topk = min(TOPK_MAX, seq // 4)
    n_qb = seq // qb
    n_steps = lambda i: (i * qb + qb - 1) // (2 * kb) + 1
    pairs = [(i, j) for i in range(n_qb) for j in range(n_steps(i) + 1)]
    as_i32 = lambda vals: jnp.asarray(np.array(vals, np.int32))
    qi = as_i32([i for i, j in pairs])
    kj = as_i32([j for i, j in pairs])
    k_pair = as_i32([min(j, n_steps(i) - 1) for i, j in pairs])
    n_kb = lambda i: (i * qb + qb - 1) // kb + 1
    v_prev = as_i32([n_kb(i) - 1 if j == n_steps(i) else max(2 * j - 1, 0) for i, j in pairs])
    v_this = as_i32([min(2 * j, 2 * n_steps(i) - 1) for i, j in pairs])
    before =jnp.asarray(np.tril(np.ones((kb // 2, kb // 2), np.float32), -1), _BF16)
    q_map = lambda b, p, qi_r, *_: (b, 0, qi_r[p])
    vt_spec = lambda which: pl.BlockSpec(
        (1, ATT_HEADS, V_ROWS, kb), lambda b, p, *refs: (b, 0, 0, refs[which][p]))
    grid_spec = pltpu.PrefetchScalarGridSpec(
        num_scalar_prefetch=5,
        grid=(bsz, len(pairs)),
        in_specs=[
            pl.BlockSpec((1, ATT_W, qb), q_map),
            pl.BlockSpec((1, IDX_W, qb), q_map),
            pl.BlockSpec((1, SUBLANES, qb), q_map),
            pl.BlockSpec((1, seq, LANES), lambda b, p, *_: (b, 0, 0)),
            pl.BlockSpec(before.shape, lambda b, p, *_: (0, 0)),
            pl.BlockSpec((1, 2 * kb, ATT_W), lambda b, p, *refs: (b, refs[2][p], 0)),
            vt_spec(3),
            vt_spec(4),
        ],
        out_specs=pl.BlockSpec((1, ATT_W, qb), q_map),
        scratch_shapes=[
            pltpu.VMEM((seq // kb, kb, qb), _I16),
            pltpu.VMEM((seq // kb, kb, qb), _I16),
            pltpu.VMEM((-(-(seq // kb) // CAP_CHUNKS_PER_BLOCK), kb, qb), _I16),
            pltpu.VMEM((-(-(seq // kb) // CAP_CHUNKS_PER_BLOCK), kb, qb), _I16),
            pltpu.VMEM((seq // kb, kb, qb), _BF16),
            pltpu.VMEM((ATT_HEADS, 1, qb), _F32),
            pltpu.VMEM((ATT_HEADS, V_ROWS, qb), _F32),
            pltpu.VMEM((ATT_HEADS, kb, qb), _F32),
            pltpu.VMEM((ATT_HEADS, kb, qb), _F32),
            pltpu.VMEM((ATT_HEADS, 1, qb), _F32),
            pltpu.VMEM((ATT_HEADS, 1, qb), _F32),
        ],
    )
    return pl.pallas_call(
        functools.partial(_dsa_kernel, topk=topk),
        grid_spec=grid_spec,
        out_shape=jax.ShapeDtypeStruct((bsz, ATT_W, seq), _BF16),
        compiler_params=pltpu.CompilerParams(
            dimension_semantics=("arbitrary", "arbitrary"), vmem_limit_bytes=VMEM_LIMIT_BYTES),
    )(qi, kj, k_pair, v_prev, v_this, qt, iqt, iwt, ik, before, k, vt, vt)


def _split3(a):
    hi = a.astype(_BF16)
    r1 = a - hi.astype(_F32)
    mid = r1.astype(_BF16)
    lo = (r1 - mid.astype(_F32)).astype(_BF16)
    return hi, mid, lo


def _hgrn_kernel(lbl_ref, g_ref, hq_ref, hf_ref, hi_ref, hg_ref, o_ref,
                 state_scr, kk_scr, b_scr, o_scr, *, layer):
    rows, ch, sb = HG_ROWS, HG_CHUNK, HG_SUB
    n_sub = ch // sb

    @pl.when(pl.program_id(1) == 0)
    def _reset():
        state_scr[...] = jnp.zeros_like(state_scr)

    lbl = lbl_ref[...]
    e = jnp.exp(lbl - jnp.max(lbl, axis=0, keepdims=True))
    lb = jnp.sum(e[:layer + 1], axis=0, keepdims=True) / jnp.sum(e, axis=0, keepdims=True)

    f = lb + (1.0 - lb) * jax.nn.sigmoid(hf_ref[0])
    kk_scr[...] = 1.0 - f
    logf = jnp.log(f)
    r_i = lax.broadcasted_iota(_I32, (ch, ch), 0)
    c_i = lax.broadcasted_iota(_I32, (ch, ch), 1)
    lower = jnp.where(c_i <= r_i, 1.0, 0.0).astype(_BF16)
    for c in range(rows // ch):
        parts = _split3(logf[c * ch:(c + 1) * ch])
        b_scr[c * ch:(c + 1) * ch, :] = sum(_dot(lower, part) for part in parts)

    t_idx = lax.broadcasted_iota(_I32, (sb, 1), 0)
    row_idx = lax.broadcasted_iota(_I32, (ch, 1), 0)

    def chunk(c, carry):
        r0 = pl.multiple_of(c * ch, ch)
        cs = pl.ds(r0, ch)
        b = b_scr[cs, :]
        kk = kk_scr[cs, :]
        qv = hq_ref[0, cs, :]
        vv = hi_ref[0, cs, :]
        b_last = b[ch - 1:ch]
        q_in = (qv * jnp.exp(b)).astype(_BF16)
        k_out = kk * jnp.exp(b_last - b)
        vb = vv.astype(_BF16)

        a_off = [[jnp.zeros((sb, ch), _F32)] for _ in range(HG_HEADS)]
        for s_i in range(1, n_sub):
            ref_b = b[s_i * sb - 1:s_i * sb]
            q_s = (qv[s_i * sb:(s_i + 1) * sb] * jnp.exp(b[s_i * sb:(s_i + 1) * sb] - ref_b))
            k_s = jnp.where(row_idx < s_i * sb, kk * jnp.exp(jnp.minimum(ref_b - b, 0.0)), 0.0)
            q_s = q_s.astype(_BF16)
            k_s = k_s.astype(_BF16)
            for h in range(HG_HEADS):
                hs = slice(h * HG_KDIM, (h + 1) * HG_KDIM)
                a_off[h].append(_nt_dot(q_s[:, hs], k_s[:, hs]))

        for h in range(HG_HEADS):
            hs = slice(h * HG_KDIM, (h + 1) * HG_KDIM)
            st = state_scr[h]
            o_h = _nt_dot(q_in[:, hs], st.astype(_BF16))
            a_h = jnp.concatenate(a_off[h], axis=0).astype(_BF16)
            o_scr[cs, hs] = o_h + _dot(a_h, vb[:, hs])
            state_scr[h] = (st * jnp.exp(b_last[:, hs])
                            + _tn_dot(vb[:, hs], k_out[:, hs].astype(_BF16)))

        for s_i in range(n_sub):
            rs = pl.ds(r0 + s_i * sb, sb)
            q_s = hq_ref[0, rs, :]
            b_s = b_scr[rs, :]
            acc = o_scr[rs, :]
            for t in range(sb):
                one = pl.ds(r0 + s_i * sb + t, 1)
                w = q_s * jnp.exp(jnp.minimum(b_s - b_scr[one, :], 0.0)) * kk_scr[one, :]
                v_row = hi_ref[0, one, :]
                parts = []
                for h in range(HG_HEADS):
                    hs = slice(h * HG_KDIM, (h + 1) * HG_KDIM)
                    a = jnp.sum(w[:, hs], axis=1, keepdims=True)
                    parts.append(jnp.where(t_idx >= t, a, 0.0) * v_row[:, hs])
                acc = acc + jnp.concatenate(parts, axis=1)
            o_scr[rs, :] = acc
        return carry

    lax.fori_loop(0, rows // ch, chunk, 0)

    o = o_scr[...]
    gate = hg_ref[0]
    gain = g_ref[...]
    for h in range(HG_HEADS):
        hs = slice(h * HG_KDIM, (h + 1) * HG_KDIM)
        oh = o[:, hs]
        oh = oh * lax.rsqrt(jnp.mean(oh * oh, axis=1, keepdims=True) + RMS_EPS) * gain[:, hs]
        gh = gate[:, hs]
        o_ref[0, :, hs] = (oh * (gh * jax.nn.sigmoid(gh))).astype(o_ref.dtype)


def _hgrn2(hq, hf, hi, hg, lb_logits, norm_g, layer):
    bsz, seq, _ = hq.shape
    rows = HG_ROWS
    blk = pl.BlockSpec((1, rows, HG_W), lambda b, t: (b, t, 0))
    lbl = lb_logits.reshape(lb_logits.shape[0], HG_W)
    gain = norm_g.reshape(1, HG_W)
    return pl.pallas_call(
        functools.partial(_hgrn_kernel, layer=layer),
        grid=(bsz, seq // rows),
        in_specs=[pl.BlockSpec(lbl.shape, lambda b, t: (0, 0)),
                  pl.BlockSpec(gain.shape, lambda b, t: (0, 0)),
                  blk, blk, blk, blk],
        out_specs=blk,
        out_shape=jax.ShapeDtypeStruct((bsz, seq, HG_W), _BF16),
        scratch_shapes=[
            pltpu.VMEM((HG_HEADS, HG_KDIM, HG_KDIM), _F32),
            pltpu.VMEM((rows, HG_W), _F32),
            pltpu.VMEM((rows, HG_W), _F32),
            pltpu.VMEM((rows, HG_W), _F32),
        ],
        compiler_params=pltpu.CompilerParams(
            dimension_semantics=("arbitrary", "arbitrary"), vmem_limit_bytes=VMEM_LIMIT_BYTES),
    )(lbl, gain, hq, hf, hi, hg)


def _layer_norm(y, g, b):
    mu = jnp.mean(y, axis=1, keepdims=True)
    yc = y - mu
    var = jnp.mean(yc * yc, axis=1, keepdims=True)
    return yc * lax.rsqrt(var + LN_EPS) * g + b


def _ffn_kernel(x_ref, att_ref, hgo_ref, wo_ref, g1_ref, b1_ref, wu_ref, wd_ref,
                g2_ref, b2_ref, o_ref, *, alpha):
    mix = (_tn_dot(att_ref[0], wo_ref[:ATT_W, :]) + _dot(hgo_ref[...], wo_ref[ATT_W:, :]))
    y1 = _layer_norm(alpha * x_ref[...] + mix, g1_ref[...], b1_ref[...])
    y1b = y1.astype(_BF16)
    h = jnp.zeros_like(y1)
    for c in range(wu_ref.shape[1] // FFN_COLS):
        cs = slice(c * FFN_COLS, (c + 1) * FFN_COLS)
        u = jnp.maximum(_dot(y1b, wu_ref[:, cs]), 0.0)
        h = h + _dot((u * u).astype(_BF16), wd_ref[cs, :])
    o_ref[...] = _layer_norm(alpha * y1 + h, g2_ref[...], b2_ref[...])


def _out_ffn(x2d, att_t, hgo2d, w_o, g1, b1, w_up, w_down, g2, b2, alpha):
    rows, d = x2d.shape
    seq = att_t.shape[2]
    tm = FFN_ROWS
    n_seq_blocks = seq // tm
    row_spec = lambda w: pl.BlockSpec((tm, w), lambda i: (i, 0))
    full_spec = lambda a: pl.BlockSpec(a.shape, lambda i: (0, 0))
    att_spec = pl.BlockSpec((1, ATT_W, tm), lambda i: (i // n_seq_blocks, 0, i % n_seq_blocks))
    vec = lambda a: a.reshape(1, d).astype(_F32)
    args = (x2d, att_t, hgo2d, w_o.astype(_BF16), vec(g1), vec(b1),
            w_up.astype(_BF16), w_down.astype(_BF16), vec(g2), vec(b2))
    in_specs = [row_spec(d), att_spec, row_spec(HG_W)] + [full_spec(a) for a in args[3:]]
    return pl.pallas_call(
        functools.partial(_ffn_kernel, alpha=alpha),
        grid=(rows // tm,),
        in_specs=in_specs,
        out_specs=row_spec(d),
        out_shape=jax.ShapeDtypeStruct((rows, d), _F32),
        compiler_params=pltpu.CompilerParams(
            dimension_semantics=("arbitrary",), vmem_limit_bytes=VMEM_LIMIT_BYTES),
    )(*args)


def kernel(x, w_in, w_o, lb_logits, hg_norm_g, ln1_g, ln1_b, w_up, w_down, ln2_g, ln2_b):
    bsz, seq, d = x.shape
    depth = w_in.shape[0]
    alpha = (2.0 * depth) ** 0.25
    x2d = x.reshape(bsz * seq, d)
    for l in range(depth):
        qt, k, vt, iqt, ik, iwt, hq, hf, hi, hg = _project(x2d, w_in[l], seq)
        r3 = lambda a: a.reshape(bsz, seq, a.shape[-1])
        att_t = _dsa_attention(qt, r3(k), vt, iqt, r3(ik), iwt)
        hgo = _hgrn2(r3(hq), r3(hf), r3(hi), r3(hg), lb_logits, hg_norm_g[l], l)
        x2d = _out_ffn(x2d, att_t, hgo.reshape(bsz * seq, HG_W),
                       w_o[l], ln1_g[l], ln1_b[l], w_up[l], w_down[l], ln2_g[l], ln2_b[l], alpha)
    return x2d.reshape(bsz, seq, d)
```

```python
import functools
import math

import numpy as np
import jax
import jax.numpy as jnp
from jax import lax
from jax.experimental import pallas as pl
from jax.experimental.pallas import tpu as pltpu

ATT_HEAD_DIM = 64
ATT_HEADS = 8
ATT_W = ATT_HEADS * ATT_HEAD_DIM
IDX_HEADS = 4
IDX_DIM = 64
IDX_W = IDX_HEADS * IDX_DIM
TOPK_MAX = 256
HG_KDIM = 128
HG_HEADS = 4
HG_W = HG_HEADS * HG_KDIM
ROPE_THETA = 10000.0
LN_EPS = 1e-5
RMS_EPS = 1e-6

LANES = 128
SUBLANES = 8
BF16_ROWS = 16
VMEM_LIMIT_BYTES = 56 * 1024 * 1024

PROJ_ROWS = 256
DSA_QB = 256
DSA_KB = 512
V_ROWS = ATT_HEAD_DIM + BF16_ROWS
COUNT_ACCS = 4
CAP_GROUPS = 2
CAP_ROWS = 2 * CAP_GROUPS * BF16_ROWS
CAP_CHUNKS_PER_BLOCK = DSA_KB // CAP_ROWS
FOLD_GROUPS = CAP_ROWS // BF16_ROWS
REFINE_BITS = 6
FOLD_MIN_CHUNKS = 14
HG_ROWS = 256
HG_CHUNK = 64
HG_SUB = 8
FFN_ROWS = 256
FFN_COLS = 1024

_F32 = jnp.float32
_BF16 = jnp.bfloat16
_I32 = jnp.int32
_I16 = jnp.int16
_INT_MIN = -(2 ** 31)
_I16_MIN = -(2 ** 15)
_I16_MAX = 2 ** 15 - 1
_LOW_SIGN = 1 << 15
_NEG = -1e30


def _nt_dot(a, b):
    return lax.dot_general(a, b, (((1,), (1,)), ((), ())), preferred_element_type=_F32)


def _tn_dot(a, b):
    return lax.dot_general(a, b, (((0,), (0,)), ((), ())), preferred_element_type=_F32)


def _dot(a, b):
    return jnp.dot(a, b, preferred_element_type=_F32)


def _tree_sum(parts):
    while len(parts) > 1:
        parts = [parts[n] + parts[n + 1] for n in range(0, len(parts) - 1, 2)] + (
            [parts[-1]] if len(parts) % 2 else [])
    return parts[0]


def _rope_group(z, cos, sin_signed):
    lane = lax.broadcasted_iota(_I32, z.shape, 1)
    first_half = (lane % ATT_HEAD_DIM) < (ATT_HEAD_DIM // 2)
    upper = pltpu.roll(z, LANES - ATT_HEAD_DIM // 2, 1)
    lower = pltpu.roll(z, ATT_HEAD_DIM // 2, 1)
    return z * cos + jnp.where(first_half, upper, lower) * sin_signed


def _proj_kernel(x_ref, wa_ref, wh_ref, cos_ref, sin_ref,
                 qt_ref, k_ref, vt_ref, iqt_ref, ik_ref, iwt_ref,
                 hq_ref, hf_ref, hi_ref, hg_ref, *, q_scale, iw_scale):
    xb = x_ref[...].astype(_BF16)
    cos = cos_ref[...]
    sin = sin_ref[...]
    pa = _dot(xb, wa_ref[...])

    def roped(col0, width):
        return [_rope_group(pa[:, col0 + g * LANES: col0 + (g + 1) * LANES], cos, sin)
                for g in range(width // LANES)]

    for g, z in enumerate(roped(0, ATT_W)):
        qt_ref[0, g * LANES:(g + 1) * LANES, :] = (z * q_scale).T.astype(_BF16)
    for g, z in enumerate(roped(ATT_W, ATT_W)):
        k_ref[:, g * LANES:(g + 1) * LANES] = z.astype(_BF16)
    rows = x_ref.shape[0]
    ones_row = lax.broadcasted_iota(_I32, (V_ROWS - ATT_HEAD_DIM, rows), 0) == 0
    for g in range(ATT_W // LANES):
        vt = pa[:, 2 * ATT_W + g * LANES:2 * ATT_W + (g + 1) * LANES].T.astype(_BF16)
        for sub in range(LANES // ATT_HEAD_DIM):
            head = g * (LANES // ATT_HEAD_DIM) + sub
            vt_ref[0, 0, head, :ATT_HEAD_DIM, :] = vt[sub * ATT_HEAD_DIM:(sub + 1) * ATT_HEAD_DIM]
            vt_ref[0, 0, head, ATT_HEAD_DIM:, :] = jnp.where(ones_row, 1.0, 0.0).astype(_BF16)
    for g, z in enumerate(roped(3 * ATT_W, IDX_W)):
        iqt_ref[0, g * LANES:(g + 1) * LANES, :] = z.T.astype(_BF16)
    ik_ref[...] = roped(3 * ATT_W + IDX_W, LANES)[0].astype(_BF16)
    iwt_ref[0] = (pa[:, 3 * ATT_W + IDX_W + LANES:] * iw_scale).T[:SUBLANES]

    ph = _dot(xb, wh_ref[...])
    hq_ref[...] = ph[:, 0 * HG_W:1 * HG_W]
    hf_ref[...] = ph[:, 1 * HG_W:2 * HG_W]
    hi_ref[...] = ph[:, 2 * HG_W:3 * HG_W]
    hg_ref[...] = ph[:, 3 * HG_W:4 * HG_W]


def _rope_tables(seq):
    half = ATT_HEAD_DIM // 2
    inv = np.power(np.float64(ROPE_THETA), -np.arange(half, dtype=np.float64) / half)
    ang = np.arange(seq, dtype=np.float64)[:, None] * inv[None, :]
    cos = np.cos(ang)
    sin = np.sin(ang)
    cos_t = np.tile(np.concatenate([cos, cos], axis=1), (1, LANES // ATT_HEAD_DIM))
    sin_t = np.tile(np.concatenate([-sin, sin], axis=1), (1, LANES // ATT_HEAD_DIM))
    return jnp.asarray(cos_t, _F32), jnp.asarray(sin_t, _F32)


def _project(x2d, w_in, seq):
    rows, d = x2d.shape
    c = 3 * ATT_W + IDX_W
    w_ik = w_in[:, c:c + IDX_DIM]
    w_iw = w_in[:, c + IDX_DIM:c + IDX_DIM + IDX_HEADS]
    wa = jnp.concatenate([w_in[:, :c], w_ik, w_ik,
                          jnp.pad(w_iw, ((0, 0), (0, LANES - IDX_HEADS)))],
                         axis=1).astype(_BF16)
    wh = w_in[:, c + IDX_DIM + IDX_HEADS:].astype(_BF16)
    cos_t, sin_t = _rope_tables(seq)
    tm = PROJ_ROWS
    n_seq_blocks = seq // tm
    row_spec = lambda w: pl.BlockSpec((tm, w), lambda i: (i, 0))
    full_spec = lambda a: pl.BlockSpec(a.shape, lambda i: (0, 0))
    pos_spec = pl.BlockSpec((tm, LANES), lambda i: (i % n_seq_blocks, 0))
    bsz = rows // seq
    t_spec = lambda *feat: pl.BlockSpec(
        (1,) + feat + (tm,), lambda i: (i // n_seq_blocks,) + (0,) * len(feat) + (i % n_seq_blocks,))
    out_shapes = [
        jax.ShapeDtypeStruct((bsz, ATT_W, seq), _BF16),
        jax.ShapeDtypeStruct((rows, ATT_W), _BF16),
        jax.ShapeDtypeStruct((bsz, seq // DSA_KB, ATT_HEADS, V_ROWS, DSA_KB), _BF16),
        jax.ShapeDtypeStruct((bsz, IDX_W, seq), _BF16),
        jax.ShapeDtypeStruct((rows, LANES), _BF16),
        jax.ShapeDtypeStruct((bsz, SUBLANES, seq), _F32),
    ] + [jax.ShapeDtypeStruct((rows, HG_W), _F32)] * 4
    tiles_per_chunk = DSA_KB // tm
    vt_spec = pl.BlockSpec(
        (1, 1, ATT_HEADS, V_ROWS, tm),
        lambda i: (i // n_seq_blocks, (i % n_seq_blocks) // tiles_per_chunk, 0, 0, i % tiles_per_chunk))
    out_specs = [t_spec(ATT_W), row_spec(ATT_W), vt_spec, t_spec(IDX_W),
                 row_spec(LANES), t_spec(SUBLANES)] + [row_spec(HG_W)] * 4
    kern = functools.partial(_proj_kernel, q_scale=ATT_HEAD_DIM ** -0.5 * math.log2(math.e),
                             iw_scale=(IDX_HEADS ** -0.5) * (IDX_DIM ** -0.5))
    return pl.pallas_call(
        kern,
        grid=(rows // tm,),
        in_specs=[row_spec(d), full_spec(wa), full_spec(wh), pos_spec, pos_spec],
        out_specs=out_specs,
        out_shape=out_shapes,
        compiler_params=pltpu.CompilerParams(
            dimension_semantics=("arbitrary",), vmem_limit_bytes=VMEM_LIMIT_BYTES),
    )(x2d, wa, wh, cos_t, sin_t)


def _dsa_kernel(qi_ref, kj_ref, ka_ref, vp_ref, vc_ref,
                qt_ref, iqt_ref, iwt_ref, ik_ref, before_ref, k_ref, vt_prev_ref, vt_ref,
                o_ref,
                hi_scr, lo_scr, fold_scr, cap_scr, bias_scr, m_scr, acc_scr,
                s_even, s_odd, bm_even, bm_odd, *, topk):
    del ka_ref, vp_ref, vc_ref
    p = pl.program_id(1)
    i = qi_ref[p]
    j = kj_ref[p]
    qb, kb = DSA_QB, DSA_KB
    n_kb = (i * qb + qb - 1) // kb + 1
    n_steps = (i * qb + qb - 1) // (2 * kb) + 1
    last_pair = 2 * (n_steps - 1)
    has_second = n_kb == 2 * n_steps
    first_head = lax.broadcasted_iota(_I32, (LANES, qb), 0) < ATT_HEAD_DIM

    def one_head(pair_rows, h):
        keep = first_head if h % 2 == 0 else jnp.logical_not(first_head)
        return jnp.where(keep, pair_rows, jnp.zeros_like(pair_rows))

    @pl.when(j == 0)
    def _select():
        iwt = iwt_ref[0]
        qpos = i * qb + lax.broadcasted_iota(_I32, (kb, qb), 1)
        krow = lax.broadcasted_iota(_I32, (kb, qb), 0)
        iq_heads = [one_head(iqt_ref[0, (h // 2) * LANES:(h // 2 + 1) * LANES, :], h)
                    for h in range(IDX_HEADS)]

        def score_chunk(c, causal):
            row0 = pl.multiple_of(c * kb, kb)
            ikc = ik_ref[0, pl.ds(row0, kb), :]
            score = jnp.zeros((kb, qb), _F32)
            for h in range(IDX_HEADS):
                logits = _dot(ikc, iq_heads[h])
                score = score + iwt[h:h + 1, :] * jnp.maximum(logits, 0.0)
            bits = lax.bitcast_convert_type(score, _I32)
            key = bits ^ (((bits >> 31) & 0x7FFFFFFF) ^ _LOW_SIGN)
            if causal:
                key = jnp.where(row0 + krow <= qpos, key, _INT_MIN ^ _LOW_SIGN)
            hi = (key >> 16).astype(_I16)
            hi_scr[c] = hi
            lo_scr[c] = key.astype(_I16)
            n_tiles = kb // BF16_ROWS
            row0_fold = pl.multiple_of((c % CAP_CHUNKS_PER_BLOCK) * CAP_ROWS, CAP_ROWS)
            for g in range(FOLD_GROUPS):
                tiles = [hi[r * BF16_ROWS:(r + 1) * BF16_ROWS] for r in range(g, n_tiles, FOLD_GROUPS)]
                while len(tiles) > 1:
                    tiles = [jnp.where(tiles[n] > tiles[n + 1], tiles[n], tiles[n + 1])
                             for n in range(0, len(tiles), 2)]
                fold_scr[c // CAP_CHUNKS_PER_BLOCK, pl.ds(row0_fold + g * BF16_ROWS, BF16_ROWS), :] = tiles[0]

        def score_pair(c2, carry):
            score_chunk(2 * c2, False)
            score_chunk(2 * c2 + 1, False)
            return carry

        fold_scr[...] = jnp.full_like(fold_scr, _I16_MIN)
        lax.fori_loop(0, n_steps - 1, score_pair, 0)
        score_chunk(last_pair, True)

        @pl.when(has_second)
        def _score_second():
            score_chunk(last_pair + 1, True)

        @pl.when(jnp.logical_not(has_second))
        def _blank_second():
            hi_scr[last_pair + 1] = jnp.full((kb, qb), _I16_MIN, _I16)
            lo_scr[last_pair + 1] = jnp.full((kb, qb), _I16_MIN, _I16)

        tile_rows = lambda r: slice(r * BF16_ROWS, (r + 1) * BF16_ROWS)
        i16_min = jnp.int16(_I16_MIN)

        def count(ref, cand, n_trips, blocks_per_trip, strict=False):
            cand_rows = jnp.broadcast_to(cand.astype(_I16), (BF16_ROWS, qb))
            one, zero = jnp.int16(1), jnp.int16(0)

            def body(t, accs):
                accs = list(accs)
                for u in range(blocks_per_trip):
                    for r in range(kb // BF16_ROWS):
                        tile = ref[blocks_per_trip * t + u, tile_rows(r), :]
                        hit = tile > cand_rows if strict else tile >= cand_rows
                        accs[r % len(accs)] = accs[r % len(accs)] + jnp.where(hit, one, zero)
                return tuple(accs)

            accs = lax.fori_loop(0, n_trips, body, (jnp.zeros((BF16_ROWS, qb), _I16),) * COUNT_ACCS)
            return jnp.sum(_tree_sum(list(accs)).astype(_I32), axis=0, keepdims=True)

        def bisect(ref, n_above, n_trips, blocks_per_trip):
            def body(step, carry):
                val, n_next = carry
                bit = jnp.left_shift(jnp.int32(1), 15 - step)
                cand = jnp.where(step == 0, 0, val | bit)
                cnt = n_above + count(ref, cand, n_trips, blocks_per_trip)
                ok = cnt >= topk
                return jnp.where(ok, cand, val), jnp.where(ok, n_next, cnt)
            return lax.fori_loop(0, 16, body, (jnp.full((1, qb), _I16_MIN, _I32), n_above))

        n_cap = (n_kb + CAP_CHUNKS_PER_BLOCK - 1) // CAP_CHUNKS_PER_BLOCK
        no_keys = jnp.zeros((1, qb), _I32)

        def high_full():
            return bisect(hi_scr, no_keys, n_steps, 2)

        def high_from_fold():
            base, _ = bisect(fold_scr, no_keys, n_cap, 1)

            def count_from(cand):
                cnt = count(hi_scr, jnp.minimum(cand, _I16_MAX), n_steps, 2)
                return jnp.where(cand > _I16_MAX, 0, cnt)

            n_beyond = count_from(base + (1 << REFINE_BITS))

            def refine():
                def body(step, carry):
                    off, n_next = carry
                    cand_off = off | jnp.left_shift(jnp.int32(1), REFINE_BITS - 1 - step)
                    cnt = count_from(base + cand_off)
                    ok = cnt >= topk
                    return jnp.where(ok, cand_off, off), jnp.where(ok, n_next, cnt)
                off, n_next = lax.fori_loop(0, REFINE_BITS, body, (no_keys, n_beyond))
                return base + off, n_next

            n_outside = jnp.sum(jnp.where(n_beyond >= topk, 1.0, 0.0))
            return lax.cond(n_outside > 0.0, high_full, refine)

        t_hi, n_gt_hi = lax.cond(n_kb >= FOLD_MIN_CHUNKS, high_from_fold, high_full)
        t_hi_tile = jnp.broadcast_to(t_hi.astype(_I16), (BF16_ROWS, qb))

        cap_scr[...] = jnp.full_like(cap_scr, _I16_MIN)

        def capture_block(c, carry):
            tops = [[jnp.full((BF16_ROWS, qb), _I16_MIN, _I16)] * 2 for _ in range(CAP_GROUPS)]
            for r in range(kb // BF16_ROWS):
                x = jnp.where(hi_scr[c, tile_rows(r), :] == t_hi_tile, lo_scr[c, tile_rows(r), :], i16_min)
                lo_scr[c, tile_rows(r), :] = x
                first, second = tops[r % CAP_GROUPS]
                above = x > first
                tops[r % CAP_GROUPS] = [jnp.where(above, x, first),
                                        jnp.where(above, first, jnp.where(x > second, x, second))]
            row0 = pl.multiple_of((c % CAP_CHUNKS_PER_BLOCK) * CAP_ROWS, CAP_ROWS)
            for g in range(CAP_GROUPS):
                for t in range(2):
                    cap_scr[c // CAP_CHUNKS_PER_BLOCK,
                            pl.ds(row0 + (2 * g + t) * BF16_ROWS, BF16_ROWS), :] = tops[g][t]
            return carry

        lax.fori_loop(0, n_kb, capture_block, 0)
        t_lo, _ = bisect(cap_scr, n_gt_hi, n_cap, 1)
        n_gt = n_gt_hi + count(lo_scr, t_lo, n_steps, 2, strict=True)
        n_wrong = jnp.sum(jnp.where(n_gt >= topk, 1.0, 0.0))
        t_lo, n_gt = lax.cond(n_wrong > 0.0,
                              lambda: bisect(lo_scr, n_gt_hi, n_steps, 2),
                              lambda: (t_lo, n_gt))
        t_hi_rows = jnp.broadcast_to(t_hi.astype(_I16), (kb, qb))
        t_lo = jnp.where((t_hi == _I16_MIN) & (t_lo == _I16_MIN), _I16_MIN + 1, t_lo)
        t_lo_rows = jnp.broadcast_to(t_lo.astype(_I16), (kb, qb))
        rem_rows = jnp.broadcast_to((topk - n_gt).astype(_I16), (kb, qb))

        def mask_block(c, seen):
            hi = hi_scr[c]
            lo = lo_scr[c]
            tie = (hi == t_hi_rows) & (lo == t_lo_rows)
            tie_count = jnp.where(tie, jnp.asarray(1, _BF16), jnp.asarray(0, _BF16))

            def tie_at(r):
                hit = (hi[r:r + 1].astype(_I32) == t_hi) & (lo[r:r + 1].astype(_I32) == t_lo)
                return jnp.where(hit, 1.0, 0.0)

            half = kb // 2
            before_top = seen + _dot(before_ref[...], tie_count[:half])
            seen_mid = before_top[half - 1:half] + tie_at(half - 1)
            before_bot = seen_mid + _dot(before_ref[...], tie_count[half:])
            ties_before = jnp.concatenate([before_top, before_bot], axis=0)
            allowed = ties_before.astype(_I32).astype(_I16) < rem_rows
            chosen = (hi > t_hi_rows) | (lo > t_lo_rows) | (tie & allowed)
            bias_scr[c] = jnp.where(chosen, jnp.asarray(0, bias_scr.dtype),
                                    jnp.asarray(_NEG, bias_scr.dtype))
            return before_bot[half - 1:half] + tie_at(kb - 1)

        seen = lax.fori_loop(0, n_steps - 1,
                             lambda c2, seen: mask_block(2 * c2 + 1, mask_block(2 * c2, seen)),
                             jnp.zeros((1, qb), _F32))
        seen = mask_block(last_pair, seen)

        @pl.when(has_second)
        def _mask_second():
            mask_block(last_pair + 1, seen)

        m_scr[...] = jnp.full_like(m_scr, _NEG)
        acc_scr[...] = jnp.zeros_like(acc_scr)
        s_odd[...] = jnp.full_like(s_odd, _NEG)
        bm_odd[...] = jnp.full_like(bm_odd, _NEG)

    def stage_b(h, s_read, bm_read, vt_read):
        m_old = m_scr[h]
        m_new = jnp.maximum(m_old, bm_read[h])
        pr = jnp.exp2(s_read[h] - m_new).astype(_BF16)
        acc_scr[h] = jnp.exp2(m_old - m_new) * acc_scr[h] + _dot(vt_read[0, 0, h], pr)
        m_scr[h] = m_new

    def phase(k_rows, chunk, s_write, bm_write, s_read, bm_read, vt_read):
        bias = bias_scr[chunk].astype(_F32)

        def stage_a(h):
            pair = slice((h // 2) * LANES, (h // 2 + 1) * LANES)
            s = _dot(k_ref[0, k_rows, pair], one_head(qt_ref[0, pair, :], h)) + bias
            s_write[h] = s
            bm_write[h] = jnp.max(s, axis=0, keepdims=True)

        stage_a(0)
        for h in range(ATT_HEADS):
            if h + 1 < ATT_HEADS:
                stage_a(h + 1)
            stage_b(h, s_read, bm_read, vt_read)

    @pl.when(j < n_steps)
    def _first():
        phase(slice(0, kb), 2 * j, s_even, bm_even, s_odd, bm_odd, vt_prev_ref)

    @pl.when(2 * j + 1 < n_kb)
    def _second():
        phase(slice(kb, 2 * kb), 2 * j + 1, s_odd, bm_odd, s_even, bm_even, vt_ref)

    @pl.when((j == n_steps) & has_second)
    def _drain_odd():
        for h in range(ATT_HEADS):
            stage_b(h, s_odd, bm_odd, vt_prev_ref)

    @pl.when((j == n_steps) & jnp.logical_not(has_second))
    def _drain_even():
        for h in range(ATT_HEADS):
            stage_b(h, s_even, bm_even, vt_prev_ref)

    @pl.when(j == n_steps)
    def _finish():
        for h in range(ATT_HEADS):
            a = acc_scr[h]
            o_ref[0, h * ATT_HEAD_DIM:(h + 1) * ATT_HEAD_DIM, :] = (
                a[:ATT_HEAD_DIM] / a[ATT_HEAD_DIM:ATT_HEAD_DIM + 1]).astype(o_ref.dtype)


def _dsa_attention(qt, k, vt, iqt, ik, iwt):
    bsz, seq, _ = k.shape
    qb, kb = DSA_QB, DSA_KB
    topk = min(TOPK_MAX, seq // 4)
    n_qb = seq // qb
    n_steps = lambda i: (i * qb + qb - 1) // (2 * kb) + 1
    pairs = [(i, j) for i in range(n_qb) for j in range(n_steps(i) + 1)]
    as_i32 = lambda vals: jnp.asarray(np.array(vals, np.int32))
    qi = as_i32([i for i, j in pairs])
    kj = as_i32([j for i, j in pairs])
    k_pair = as_i32([min(j, n_steps(i) - 1) for i, j in pairs])
    n_kb = lambda i: (i * qb + qb - 1) // kb + 1
    v_prev = as_i32([n_kb(i) - 1 if j == n_steps(i) else max(2 * j - 1, 0) for i, j in pairs])
    v_this = as_i32([min(2 * j, 2 * n_steps(i) - 1) for i, j in pairs])
    before =jnp.asarray(np.tril(np.ones((kb // 2, kb // 2), np.float32), -1), _BF16)
    q_map = lambda b, p, qi_r, *_: (b, 0, qi_r[p])
    vt_spec = lambda which: pl.BlockSpec(
        (1, 1, ATT_HEADS, V_ROWS, kb), lambda b, p, *refs: (b, refs[which][p], 0, 0, 0))
    grid_spec = pltpu.PrefetchScalarGridSpec(
        num_scalar_prefetch=5,
        grid=(bsz, len(pairs)),
        in_specs=[
            pl.BlockSpec((1, ATT_W, qb), q_map),
            pl.BlockSpec((1, IDX_W, qb), q_map),
            pl.BlockSpec((1, SUBLANES, qb), q_map),
            pl.BlockSpec((1, seq, LANES), lambda b, p, *_: (b, 0, 0)),
            pl.BlockSpec(before.shape, lambda b, p, *_: (0, 0)),
            pl.BlockSpec((1, 2 * kb, ATT_W), lambda b, p, *refs: (b, refs[2][p], 0)),
            vt_spec(3),
            vt_spec(4),
        ],
        out_specs=pl.BlockSpec((1, ATT_W, qb), q_map),
        scratch_shapes=[
            pltpu.VMEM((seq // kb, kb, qb), _I16),
            pltpu.VMEM((seq // kb, kb, qb), _I16),
            pltpu.VMEM((-(-(seq // kb) // CAP_CHUNKS_PER_BLOCK), kb, qb), _I16),
            pltpu.VMEM((-(-(seq // kb) // CAP_CHUNKS_PER_BLOCK), kb, qb), _I16),
            pltpu.VMEM((seq // kb, kb, qb), _BF16),
            pltpu.VMEM((ATT_HEADS, 1, qb), _F32),
            pltpu.VMEM((ATT_HEADS, V_ROWS, qb), _F32),
            pltpu.VMEM((ATT_HEADS, kb, qb), _F32),
            pltpu.VMEM((ATT_HEADS, kb, qb), _F32),
            pltpu.VMEM((ATT_HEADS, 1, qb), _F32),
            pltpu.VMEM((ATT_HEADS, 1, qb), _F32),
        ],
    )
    return pl.pallas_call(
        functools.partial(_dsa_kernel, topk=topk),
        grid_spec=grid_spec,
        out_shape=jax.ShapeDtypeStruct((bsz, ATT_W, seq), _BF16),
        compiler_params=pltpu.CompilerParams(
            dimension_semantics=("arbitrary", "arbitrary"), vmem_limit_bytes=VMEM_LIMIT_BYTES),
    )(qi, kj, k_pair, v_prev, v_this, qt, iqt, iwt, ik, before, k, vt, vt)


def _split3(a):
    hi = a.astype(_BF16)
    r1 = a - hi.astype(_F32)
    mid = r1.astype(_BF16)
    lo = (r1 - mid.astype(_F32)).astype(_BF16)
    return hi, mid, lo


def _hgrn_kernel(lbl_ref, g_ref, hq_ref, hf_ref, hi_ref, hg_ref, o_ref,
                 state_scr, kk_scr, b_scr, o_scr, *, layer):
    rows, ch, sb = HG_ROWS, HG_CHUNK, HG_SUB
    n_sub = ch // sb

    @pl.when(pl.program_id(1) == 0)
    def _reset():
        state_scr[...] = jnp.zeros_like(state_scr)

    lbl = lbl_ref[...]
    e = jnp.exp(lbl - jnp.max(lbl, axis=0, keepdims=True))
    lb = jnp.sum(e[:layer + 1], axis=0, keepdims=True) / jnp.sum(e, axis=0, keepdims=True)

    f = lb + (1.0 - lb) * jax.nn.sigmoid(hf_ref[0])
    kk_scr[...] = 1.0 - f
    logf = jnp.log(f)
    r_i = lax.broadcasted_iota(_I32, (ch, ch), 0)
    c_i = lax.broadcasted_iota(_I32, (ch, ch), 1)
    lower = jnp.where(c_i <= r_i, 1.0, 0.0).astype(_BF16)
    for c in range(rows // ch):
        parts = _split3(logf[c * ch:(c + 1) * ch])
        b_scr[c * ch:(c + 1) * ch, :] = sum(_dot(lower, part) for part in parts)

    t_idx = lax.broadcasted_iota(_I32, (sb, 1), 0)
    row_idx = lax.broadcasted_iota(_I32, (ch, 1), 0)

    def chunk(c, carry):
        r0 = pl.multiple_of(c * ch, ch)
        cs = pl.ds(r0, ch)
        b = b_scr[cs, :]
        kk = kk_scr[cs, :]
        qv = hq_ref[0, cs, :]
        vv = hi_ref[0, cs, :]
        b_last = b[ch - 1:ch]
        q_in = (qv * jnp.exp(b)).astype(_BF16)
        k_out = kk * jnp.exp(b_last - b)
        vb = vv.astype(_BF16)

        a_off = [[jnp.zeros((sb, ch), _F32)] for _ in range(HG_HEADS)]
        for s_i in range(1, n_sub):
            ref_b = b[s_i * sb - 1:s_i * sb]
            q_s = (qv[s_i * sb:(s_i + 1) * sb] * jnp.exp(b[s_i * sb:(s_i + 1) * sb] - ref_b))
            k_s = jnp.where(row_idx < s_i * sb, kk * jnp.exp(jnp.minimum(ref_b - b, 0.0)), 0.0)
            q_s = q_s.astype(_BF16)
            k_s = k_s.astype(_BF16)
            for h in range(HG_HEADS):
                hs = slice(h * HG_KDIM, (h + 1) * HG_KDIM)
                a_off[h].append(_nt_dot(q_s[:, hs], k_s[:, hs]))

        for h in range(HG_HEADS):
            hs = slice(h * HG_KDIM, (h + 1) * HG_KDIM)
            st = state_scr[h]
            o_h = _nt_dot(q_in[:, hs], st.astype(_BF16))
            a_h = jnp.concatenate(a_off[h], axis=0).astype(_BF16)
            o_scr[cs, hs] = o_h + _dot(a_h, vb[:, hs])
            state_scr[h] = (st * jnp.exp(b_last[:, hs])
                            + _tn_dot(vb[:, hs], k_out[:, hs].astype(_BF16)))

        for s_i in range(n_sub):
            rs = pl.ds(r0 + s_i * sb, sb)
            q_s = hq_ref[0, rs, :]
            b_s = b_scr[rs, :]
            acc = o_scr[rs, :]
            for t in range(sb):
                one = pl.ds(r0 + s_i * sb + t, 1)
                w = q_s * jnp.exp(jnp.minimum(b_s - b_scr[one, :], 0.0)) * kk_scr[one, :]
                v_row = hi_ref[0, one, :]
                parts = []
                for h in range(HG_HEADS):
                    hs = slice(h * HG_KDIM, (h + 1) * HG_KDIM)
                    a = jnp.sum(w[:, hs], axis=1, keepdims=True)
                    parts.append(jnp.where(t_idx >= t, a, 0.0) * v_row[:, hs])
                acc = acc + jnp.concatenate(parts, axis=1)
            o_scr[rs, :] = acc
        return carry

    lax.fori_loop(0, rows // ch, chunk, 0)

    o = o_scr[...]
    gate = hg_ref[0]
    gain = g_ref[...]
    for h in range(HG_HEADS):
        hs = slice(h * HG_KDIM, (h + 1) * HG_KDIM)
        oh = o[:, hs]
        oh = oh * lax.rsqrt(jnp.mean(oh * oh, axis=1, keepdims=True) + RMS_EPS) * gain[:, hs]
        gh = gate[:, hs]
        o_ref[0, :, hs] = (oh * (gh * jax.nn.sigmoid(gh))).astype(o_ref.dtype)


def _hgrn2(hq, hf, hi, hg, lb_logits, norm_g, layer):
    bsz, seq, _ = hq.shape
    rows = HG_ROWS
    blk = pl.BlockSpec((1, rows, HG_W), lambda b, t: (b, t, 0))
    lbl = lb_logits.reshape(lb_logits.shape[0], HG_W)
    gain = norm_g.reshape(1, HG_W)
    return pl.pallas_call(
        functools.partial(_hgrn_kernel, layer=layer),
        grid=(bsz, seq // rows),
        in_specs=[pl.BlockSpec(lbl.shape, lambda b, t: (0, 0)),
                  pl.BlockSpec(gain.shape, lambda b, t: (0, 0)),
                  blk, blk, blk, blk],
        out_specs=blk,
        out_shape=jax.ShapeDtypeStruct((bsz, seq, HG_W), _BF16),
        scratch_shapes=[
            pltpu.VMEM((HG_HEADS, HG_KDIM, HG_KDIM), _F32),
            pltpu.VMEM((rows, HG_W), _F32),
            pltpu.VMEM((rows, HG_W), _F32),
            pltpu.VMEM((rows, HG_W), _F32),
        ],
        compiler_params=pltpu.CompilerParams(
            dimension_semantics=("arbitrary", "arbitrary"), vmem_limit_bytes=VMEM_LIMIT_BYTES),
    )(lbl, gain, hq, hf, hi, hg)


def _layer_norm(y, g, b):
    mu = jnp.mean(y, axis=1, keepdims=True)
    yc = y - mu
    var = jnp.mean(yc * yc, axis=1, keepdims=True)
    return yc * lax.rsqrt(var + LN_EPS) * g + b


def _ffn_kernel(x_ref, att_ref, hgo_ref, wo_ref, g1_ref, b1_ref, wu_ref, wd_ref,
                g2_ref, b2_ref, o_ref, *, alpha):
    mix = (_tn_dot(att_ref[0], wo_ref[:ATT_W, :]) + _dot(hgo_ref[...], wo_ref[ATT_W:, :]))
    y1 = _layer_norm(alpha * x_ref[...] + mix, g1_ref[...], b1_ref[...])
    y1b = y1.astype(_BF16)
    h = jnp.zeros_like(y1)
    for c in range(wu_ref.shape[1] // FFN_COLS):
        cs = slice(c * FFN_COLS, (c + 1) * FFN_COLS)
        u = jnp.maximum(_dot(y1b, wu_ref[:, cs]), 0.0)
        h = h + _dot((u * u).astype(_BF16), wd_ref[cs, :])
    o_ref[...] = _layer_norm(alpha * y1 + h, g2_ref[...], b2_ref[...])


def _out_ffn(x2d, att_t, hgo2d, w_o, g1, b1, w_up, w_down, g2, b2, alpha):
    rows, d = x2d.shape
    seq = att_t.shape[2]
    tm = FFN_ROWS
    n_seq_blocks = seq // tm
    row_spec = lambda w: pl.BlockSpec((tm, w), lambda i: (i, 0))
    full_spec = lambda a: pl.BlockSpec(a.shape, lambda i: (0, 0))
    att_spec = pl.BlockSpec((1, ATT_W, tm), lambda i: (i // n_seq_blocks, 0, i % n_seq_blocks))
    vec = lambda a: a.reshape(1, d).astype(_F32)
    args = (x2d, att_t, hgo2d, w_o.astype(_BF16), vec(g1), vec(b1),
            w_up.astype(_BF16), w_down.astype(_BF16), vec(g2), vec(b2))
    in_specs = [row_spec(d), att_spec, row_spec(HG_W)] + [full_spec(a) for a in args[3:]]
    return pl.pallas_call(
        functools.partial(_ffn_kernel, alpha=alpha),
        grid=(rows // tm,),
        in_specs=in_specs,
        out_specs=row_spec(d),
        out_shape=jax.ShapeDtypeStruct((rows, d), _F32),
        compiler_params=pltpu.CompilerParams(
            dimension_semantics=("arbitrary",), vmem_limit_bytes=VMEM_LIMIT_BYTES),
    )(*args)


def kernel(x, w_in, w_o, lb_logits, hg_norm_g, ln1_g, ln1_b, w_up, w_down, ln2_g, ln2_b):
    bsz, seq, d = x.shape
    depth = w_in.shape[0]
    alpha = (2.0 * depth) ** 0.25
    x2d = x.reshape(bsz * seq, d)
    for l in range(depth):
        qt, k, vt, iqt, ik, iwt, hq, hf, hi, hg = _project(x2d, w_in[l], seq)
        r3 = lambda a: a.reshape(bsz, seq, a.shape[-1])
        att_t = _dsa_attention(qt, r3(k), vt, iqt, r3(ik), iwt)
        hgo = _hgrn2(r3(hq), r3(hf), r3(hi), r3(hg), lb_logits, hg_norm_g[l], l)
        x2d = _out_ffn(x2d, att_t, hgo.reshape(bsz * seq, HG_W),
                       w_o[l], ln1_g[l], ln1_b[l], w_up[l], w_down[l], ln2_g[l], ln2_b[l], alpha)
    return x2d.reshape(bsz, seq, d)
```

```python
import functools
import math

import numpy as np
import jax
import jax.numpy as jnp
from jax import lax
from jax.experimental import pallas as pl
from jax.experimental.pallas import tpu as pltpu

ATT_HEAD_DIM = 64
ATT_HEADS = 8
ATT_W = ATT_HEADS * ATT_HEAD_DIM
IDX_HEADS = 4
IDX_DIM = 64
IDX_W = IDX_HEADS * IDX_DIM
TOPK_MAX = 256
HG_KDIM = 128
HG_HEADS = 4
HG_W = HG_HEADS * HG_KDIM
ROPE_THETA = 10000.0
LN_EPS = 1e-5
RMS_EPS = 1e-6

LANES = 128
SUBLANES = 8
BF16_ROWS = 16
VMEM_LIMIT_BYTES = 56 * 1024 * 1024

PROJ_ROWS = 256
DSA_QB = 256
DSA_KB = 512
V_ROWS = ATT_HEAD_DIM + BF16_ROWS
COUNT_ACCS = 4
CAP_GROUPS = 2
CAP_ROWS = 2 * CAP_GROUPS * BF16_ROWS
CAP_CHUNKS_PER_BLOCK = DSA_KB // CAP_ROWS
FOLD_GROUPS = CAP_ROWS // BF16_ROWS
REFINE_BITS = 6
FOLD_MIN_CHUNKS = 14
HG_ROWS = 256
HG_CHUNK = 64
HG_SUB = 8
FFN_ROWS = 256
FFN_COLS = 1024

_F32 = jnp.float32
_BF16 = jnp.bfloat16
_I32 = jnp.int32
_I16 = jnp.int16
_INT_MIN = -(2 ** 31)
_I16_MIN = -(2 ** 15)
_I16_MAX = 2 ** 15 - 1
_LOW_SIGN = 1 << 15
_NEG = -1e30


def _nt_dot(a, b):
    return lax.dot_general(a, b, (((1,), (1,)), ((), ())), preferred_element_type=_F32)


def _tn_dot(a, b):
    return lax.dot_general(a, b, (((0,), (0,)), ((), ())), preferred_element_type=_F32)


def _dot(a, b):
    return jnp.dot(a, b, preferred_element_type=_F32)


def _tree_sum(parts):
    while len(parts) > 1:
        parts = [parts[n] + parts[n + 1] for n in range(0, len(parts) - 1, 2)] + (
            [parts[-1]] if len(parts) % 2 else [])
    return parts[0]


def _rope_group(z, cos, sin_signed):
    lane = lax.broadcasted_iota(_I32, z.shape, 1)
    first_half = (lane % ATT_HEAD_DIM) < (ATT_HEAD_DIM // 2)
    upper = pltpu.roll(z, LANES - ATT_HEAD_DIM // 2, 1)
    lower = pltpu.roll(z, ATT_HEAD_DIM // 2, 1)
    return z * cos + jnp.where(first_half, upper, lower) * sin_signed


def _proj_kernel(x_ref, wa_ref, wh_ref, cos_ref, sin_ref,
                 qt_ref, k_ref, vt_ref, iqt_ref, ik_ref, iwt_ref,
                 hq_ref, hf_ref, hi_ref, hg_ref, *, q_scale, iw_scale):
    xb = x_ref[...].astype(_BF16)
    cos = cos_ref[...]
    sin = sin_ref[...]
    pa = _dot(xb, wa_ref[...])

    def roped(col0, width):
        return [_rope_group(pa[:, col0 + g * LANES: col0 + (g + 1) * LANES], cos, sin)
                for g in range(width // LANES)]

    for g, z in enumerate(roped(0, ATT_W)):
        qt_ref[0, g * LANES:(g + 1) * LANES, :] = (z * q_scale).T.astype(_BF16)
    for g, z in enumerate(roped(ATT_W, ATT_W)):
        k_ref[:, g * LANES:(g + 1) * LANES] = z.astype(_BF16)
    rows = x_ref.shape[0]
    ones_row = lax.broadcasted_iota(_I32, (V_ROWS - ATT_HEAD_DIM, rows), 0) == 0
    for g in range(ATT_W // LANES):
        vt = pa[:, 2 * ATT_W + g * LANES:2 * ATT_W + (g + 1) * LANES].T.astype(_BF16)
        for sub in range(LANES // ATT_HEAD_DIM):
            head = g * (LANES // ATT_HEAD_DIM) + sub
            vt_ref[0, 0, head, :ATT_HEAD_DIM, :] = vt[sub * ATT_HEAD_DIM:(sub + 1) * ATT_HEAD_DIM]
            vt_ref[0, 0, head, ATT_HEAD_DIM:, :] = jnp.where(ones_row, 1.0, 0.0).astype(_BF16)
    for g, z in enumerate(roped(3 * ATT_W, IDX_W)):
        iqt_ref[0, g * LANES:(g + 1) * LANES, :] = z.T.astype(_BF16)
    ik_ref[...] = roped(3 * ATT_W + IDX_W, LANES)[0].astype(_BF16)
    iwt_ref[0] = (pa[:, 3 * ATT_W + IDX_W + LANES:] * iw_scale).T[:SUBLANES]

    ph = _dot(xb, wh_ref[...])
    hq_ref[...] = ph[:, 0 * HG_W:1 * HG_W]
    hf_ref[...] = ph[:, 1 * HG_W:2 * HG_W]
    hi_ref[...] = ph[:, 2 * HG_W:3 * HG_W]
    hg_ref[...] = ph[:, 3 * HG_W:4 * HG_W]


def _rope_tables(seq):
    half = ATT_HEAD_DIM // 2
    inv = np.power(np.float64(ROPE_THETA), -np.arange(half, dtype=np.float64) / half)
    ang = np.arange(seq, dtype=np.float64)[:, None] * inv[None, :]
    cos = np.cos(ang)
    sin = np.sin(ang)
    cos_t = np.tile(np.concatenate([cos, cos], axis=1), (1, LANES // ATT_HEAD_DIM))
    sin_t = np.tile(np.concatenate([-sin, sin], axis=1), (1, LANES // ATT_HEAD_DIM))
    return jnp.asarray(cos_t, _F32), jnp.asarray(sin_t, _F32)


def _project(x2d, w_in, seq):
    rows, d = x2d.shape
    c = 3 * ATT_W + IDX_W
    w_ik = w_in[:, c:c + IDX_DIM]
    w_iw = w_in[:, c + IDX_DIM:c + IDX_DIM + IDX_HEADS]
    wa = jnp.concatenate([w_in[:, :c], w_ik, w_ik,
                          jnp.pad(w_iw, ((0, 0), (0, LANES - IDX_HEADS)))],
                         axis=1).astype(_BF16)
    wh = w_in[:, c + IDX_DIM + IDX_HEADS:].astype(_BF16)
    cos_t, sin_t = _rope_tables(seq)
    tm = PROJ_ROWS
    n_seq_blocks = seq // tm
    row_spec = lambda w: pl.BlockSpec((tm, w), lambda i: (i, 0))
    full_spec = lambda a: pl.BlockSpec(a.shape, lambda i: (0, 0))
    pos_spec = pl.BlockSpec((tm, LANES), lambda i: (i % n_seq_blocks, 0))
    bsz = rows // seq
    t_spec = lambda *feat: pl.BlockSpec(
        (1,) + feat + (tm,), lambda i: (i // n_seq_blocks,) + (0,) * len(feat) + (i % n_seq_blocks,))
    out_shapes = [
        jax.ShapeDtypeStruct((bsz, ATT_W, seq), _BF16),
        jax.ShapeDtypeStruct((rows, ATT_W), _BF16),
        jax.ShapeDtypeStruct((bsz, seq // DSA_KB, ATT_HEADS, V_ROWS, DSA_KB), _BF16),
        jax.ShapeDtypeStruct((bsz, IDX_W, seq), _BF16),
        jax.ShapeDtypeStruct((rows, LANES), _BF16),
        jax.ShapeDtypeStruct((bsz, SUBLANES, seq), _F32),
    ] + [jax.ShapeDtypeStruct((rows, HG_W), _F32)] * 4
    tiles_per_chunk = DSA_KB // tm
    vt_spec = pl.BlockSpec(
        (1, 1, ATT_HEADS, V_ROWS, tm),
        lambda i: (i // n_seq_blocks, (i % n_seq_blocks) // tiles_per_chunk, 0, 0, i % tiles_per_chunk))
    out_specs = [t_spec(ATT_W), row_spec(ATT_W), vt_spec, t_spec(IDX_W),
                 row_spec(LANES), t_spec(SUBLANES)] + [row_spec(HG_W)] * 4
    kern = functools.partial(_proj_kernel, q_scale=ATT_HEAD_DIM ** -0.5 * math.log2(math.e),
                             iw_scale=(IDX_HEADS ** -0.5) * (IDX_DIM ** -0.5))
    return pl.pallas_call(
        kern,
        grid=(rows // tm,),
        in_specs=[row_spec(d), full_spec(wa), full_spec(wh), pos_spec, pos_spec],
        out_specs=out_specs,
        out_shape=out_shapes,
        compiler_params=pltpu.CompilerParams(
            dimension_semantics=("arbitrary",), vmem_limit_bytes=VMEM_LIMIT_BYTES),
    )(x2d, wa, wh, cos_t, sin_t)


def _dsa_kernel(qi_ref, kj_ref, ka_ref, vp_ref, vc_ref,
                qt_ref, iqt_ref, iwt_ref, ik_ref, before_ref, k_ref, vt_prev_ref, vt_ref,
                o_ref,
                hi_scr, lo_scr, fold_scr, cap_scr, bias_scr, m_scr, acc_scr,
                s_even, s_odd, bm_even, bm_odd, *, topk):
    del ka_ref, vp_ref, vc_ref
    p = pl.program_id(1)
    i = qi_ref[p]
    j = kj_ref[p]
    qb, kb = DSA_QB, DSA_KB
    n_kb = (i * qb + qb - 1) // kb + 1
    n_steps = (i * qb + qb - 1) // (2 * kb) + 1
    last_pair = 2 * (n_steps - 1)
    has_second = n_kb == 2 * n_steps
    first_head = lax.broadcasted_iota(_I32, (LANES, qb), 0) < ATT_HEAD_DIM

    def one_head(pair_rows, h):
        keep = first_head if h % 2 == 0 else jnp.logical_not(first_head)
        return jnp.where(keep, pair_rows, jnp.zeros_like(pair_rows))

    @pl.when(j == 0)
    def _select():
        iwt = iwt_ref[0]
        qpos = i * qb + lax.broadcasted_iota(_I32, (kb, qb), 1)
        krow = lax.broadcasted_iota(_I32, (kb, qb), 0)
        iq_heads = [one_head(iqt_ref[0, (h // 2) * LANES:(h // 2 + 1) * LANES, :], h)
                    for h in range(IDX_HEADS)]

        def score_chunk(c, causal):
            row0 = pl.multiple_of(c * kb, kb)
            ikc = ik_ref[0, pl.ds(row0, kb), :]
            score = jnp.zeros((kb, qb), _F32)
            for h in range(IDX_HEADS):
                logits = _dot(ikc, iq_heads[h])
                score = score + iwt[h:h + 1, :] * jnp.maximum(logits, 0.0)
            bits = lax.bitcast_convert_type(score, _I32)
            key = bits ^ (((bits >> 31) & 0x7FFFFFFF) ^ _LOW_SIGN)
            if causal:
                key = jnp.where(row0 + krow <= qpos, key, _INT_MIN ^ _LOW_SIGN)
            hi = (key >> 16).astype(_I16)
            hi_scr[c] = hi
            lo_scr[c] = key.astype(_I16)
            n_tiles = kb // BF16_ROWS
            row0_fold = pl.multiple_of((c % CAP_CHUNKS_PER_BLOCK) * CAP_ROWS, CAP_ROWS)
            for g in range(FOLD_GROUPS):
                tiles = [hi[r * BF16_ROWS:(r + 1) * BF16_ROWS] for r in range(g, n_tiles, FOLD_GROUPS)]
                while len(tiles) > 1:
                    tiles = [jnp.where(tiles[n] > tiles[n + 1], tiles[n], tiles[n + 1])
                             for n in range(0, len(tiles), 2)]
                fold_scr[c // CAP_CHUNKS_PER_BLOCK, pl.ds(row0_fold + g * BF16_ROWS, BF16_ROWS), :] = tiles[0]

        def score_pair(c2, carry):
            score_chunk(2 * c2, False)
            score_chunk(2 * c2 + 1, False)
            return carry

        fold_scr[...] = jnp.full_like(fold_scr, _I16_MIN)
        lax.fori_loop(0, n_steps - 1, score_pair, 0)
        score_chunk(last_pair, True)

        @pl.when(has_second)
        def _score_second():
            score_chunk(last_pair + 1, True)

        @pl.when(jnp.logical_not(has_second))
        def _blank_second():
            hi_scr[last_pair + 1] = jnp.full((kb, qb), _I16_MIN, _I16)
            lo_scr[last_pair + 1] = jnp.full((kb, qb), _I16_MIN, _I16)

        tile_rows = lambda r: slice(r * BF16_ROWS, (r + 1) * BF16_ROWS)
        i16_min = jnp.int16(_I16_MIN)

        def count(ref, cand, n_trips, blocks_per_trip, strict=False):
            cand_rows = jnp.broadcast_to(cand.astype(_I16), (BF16_ROWS, qb))
            one, zero = jnp.int16(1), jnp.int16(0)

            def body(t, accs):
                accs = list(accs)
                for u in range(blocks_per_trip):
                    for r in range(kb // BF16_ROWS):
                        tile = ref[blocks_per_trip * t + u, tile_rows(r), :]
                        hit = tile > cand_rows if strict else tile >= cand_rows
                        accs[r % len(accs)] = accs[r % len(accs)] + jnp.where(hit, one, zero)
                return tuple(accs)

            accs = lax.fori_loop(0, n_trips, body, (jnp.zeros((BF16_ROWS, qb), _I16),) * COUNT_ACCS)
            return jnp.sum(_tree_sum(list(accs)).astype(_I32), axis=0, keepdims=True)

        def bisect(ref, n_above, n_trips, blocks_per_trip):
            def body(step, carry):
                val, n_next = carry
                bit = jnp.left_shift(jnp.int32(1), 15 - step)
                cand = jnp.where(step == 0, 0, val | bit)
                cnt = n_above + count(ref, cand, n_trips, blocks_per_trip)
                ok = cnt >= topk
                return jnp.where(ok, cand, val), jnp.where(ok, n_next, cnt)
            return lax.fori_loop(0, 16, body, (jnp.full((1, qb), _I16_MIN, _I32), n_above))

        n_cap = (n_kb + CAP_CHUNKS_PER_BLOCK - 1) // CAP_CHUNKS_PER_BLOCK
        no_keys = jnp.zeros((1, qb), _I32)

        def high_full():
            return bisect(hi_scr, no_keys, n_steps, 2)

        def high_from_fold():
            base, _ = bisect(fold_scr, no_keys, n_cap, 1)

            def count_from(cand):
                cnt = count(hi_scr, jnp.minimum(cand, _I16_MAX), n_steps, 2)
                return jnp.where(cand > _I16_MAX, 0, cnt)

            n_beyond = count_from(base + (1 << REFINE_BITS))

            def refine():
                def body(step, carry):
                    off, n_next = carry
                    cand_off = off | jnp.left_shift(jnp.int32(1), REFINE_BITS - 1 - step)
                    cnt = count_from(base + cand_off)
                    ok = cnt >= topk
                    return jnp.where(ok, cand_off, off), jnp.where(ok, n_next, cnt)
                off, n_next = lax.fori_loop(0, REFINE_BITS, body, (no_keys, n_beyond))
                return base + off, n_next

            n_outside = jnp.sum(jnp.where(n_beyond >= topk, 1.0, 0.0))
            return lax.cond(n_outside > 0.0, high_full, refine)

        t_hi, n_gt_hi = lax.cond(n_kb >= FOLD_MIN_CHUNKS, high_from_fold, high_full)
        t_hi_tile = jnp.broadcast_to(t_hi.astype(_I16), (BF16_ROWS, qb))

        cap_scr[...] = jnp.full_like(cap_scr, _I16_MIN)

        def capture_block(c, carry):
            tops = [[jnp.full((BF16_ROWS, qb), _I16_MIN, _I16)] * 2 for _ in range(CAP_GROUPS)]
            for r in range(kb // BF16_ROWS):
                x = jnp.where(hi_scr[c, tile_rows(r), :] == t_hi_tile, lo_scr[c, tile_rows(r), :], i16_min)
                lo_scr[c, tile_rows(r), :] = x
                first, second = tops[r % CAP_GROUPS]
                above = x > first
                tops[r % CAP_GROUPS] = [jnp.where(above, x, first),
                                        jnp.where(above, first, jnp.where(x > second, x, second))]
            row0 = pl.multiple_of((c % CAP_CHUNKS_PER_BLOCK) * CAP_ROWS, CAP_ROWS)
            for g in range(CAP_GROUPS):
                for t in range(2):
                    cap_scr[c // CAP_CHUNKS_PER_BLOCK,
                            pl.ds(row0 + (2 * g + t) * BF16_ROWS, BF16_ROWS), :] = tops[g][t]
            return carry

        lax.fori_loop(0, n_kb, capture_block, 0)
        t_lo, _ = bisect(cap_scr, n_gt_hi, n_cap, 1)
        n_gt = n_gt_hi + count(lo_scr, t_lo, n_steps, 2, strict=True)
        n_wrong = jnp.sum(jnp.where(n_gt >= topk, 1.0, 0.0))
        t_lo, n_gt = lax.cond(n_wrong > 0.0,
                              lambda: bisect(lo_scr, n_gt_hi, n_steps, 2),
                              lambda: (t_lo, n_gt))
        t_hi_rows = jnp.broadcast_to(t_hi.astype(_I16), (kb, qb))
        t_lo = jnp.where((t_hi == _I16_MIN) & (t_lo == _I16_MIN), _I16_MIN + 1, t_lo)
        t_lo_rows = jnp.broadcast_to(t_lo.astype(_I16), (kb, qb))
        rem_rows = jnp.broadcast_to((topk - n_gt).astype(_I16), (kb, qb))

        def mask_block(c, seen):
            hi = hi_scr[c]
            lo = lo_scr[c]
            tie = (hi == t_hi_rows) & (lo == t_lo_rows)
            tie_count = jnp.where(tie, jnp.asarray(1, _BF16), jnp.asarray(0, _BF16))

            def tie_at(r):
                hit = (hi[r:r + 1].astype(_I32) == t_hi) & (lo[r:r + 1].astype(_I32) == t_lo)
                return jnp.where(hit, 1.0, 0.0)

            half = kb // 2
            before_top = seen + _dot(before_ref[...], tie_count[:half])
            seen_mid = before_top[half - 1:half] + tie_at(half - 1)
            before_bot = seen_mid + _dot(before_ref[...], tie_count[half:])
            ties_before = jnp.concatenate([before_top, before_bot], axis=0)
            allowed = ties_before.astype(_I32).astype(_I16) < rem_rows
            chosen = (hi > t_hi_rows) | (lo > t_lo_rows) | (tie & allowed)
            bias_scr[c] = jnp.where(chosen, jnp.asarray(0, bias_scr.dtype),
                                    jnp.asarray(_NEG, bias_scr.dtype))
            return before_bot[half - 1:half] + tie_at(kb - 1)

        seen = lax.fori_loop(0, n_steps - 1,
                             lambda c2, seen: mask_block(2 * c2 + 1, mask_block(2 * c2, seen)),
                             jnp.zeros((1, qb), _F32))
        seen = mask_block(last_pair, seen)

        @pl.when(has_second)
        def _mask_second():
            mask_block(last_pair + 1, seen)

        m_scr[...] = jnp.full_like(m_scr, _NEG)
        acc_scr[...] = jnp.zeros_like(acc_scr)

    def stage_b(h, s_read, bm_read, vt_read):
        m_old = m_scr[h]
        m_new = jnp.maximum(m_old, bm_read[h])
        pr = jnp.exp2(s_read[h] - m_new).astype(_BF16)
        acc_scr[h] = jnp.exp2(m_old - m_new) * acc_scr[h] + _dot(vt_read[0, 0, h], pr)
        m_scr[h] = m_new

    def phase(k_rows, chunk, s_write, bm_write, s_read, bm_read, vt_read):
        bias = bias_scr[chunk].astype(_F32)

        def stage_a(h):
            pair = slice((h // 2) * LANES, (h // 2 + 1) * LANES)
            s = _dot(k_ref[0, k_rows, pair], one_head(qt_ref[0, pair, :], h)) + bias
            s_write[h] = s
            bm_write[h] = jnp.max(s, axis=0, keepdims=True)

        stage_a(0)
        for h in range(ATT_HEADS):
            if h + 1 < ATT_HEADS:
                stage_a(h + 1)
            if s_read is not None:
                stage_b(h, s_read, bm_read, vt_read)

    @pl.when(j == 0)
    def _fill():
        phase(slice(0, kb), 0, s_even, bm_even, None, None, None)

    @pl.when((j > 0) & (j < n_steps))
    def _first():
        phase(slice(0, kb), 2 * j, s_even, bm_even, s_odd, bm_odd, vt_prev_ref)

    @pl.when(2 * j + 1 < n_kb)
    def _second():
        phase(slice(kb, 2 * kb), 2 * j + 1, s_odd, bm_odd, s_even, bm_even, vt_ref)

    @pl.when((j == n_steps) & has_second)
    def _drain_odd():
        for h in range(ATT_HEADS):
            stage_b(h, s_odd, bm_odd, vt_prev_ref)

    @pl.when((j == n_steps) & jnp.logical_not(has_second))
    def _drain_even():
        for h in range(ATT_HEADS):
            stage_b(h, s_even, bm_even, vt_prev_ref)

    @pl.when(j == n_steps)
    def _finish():
        for h in range(ATT_HEADS):
            a = acc_scr[h]
            o_ref[0, h * ATT_HEAD_DIM:(h + 1) * ATT_HEAD_DIM, :] = (
                a[:ATT_HEAD_DIM] / a[ATT_HEAD_DIM:ATT_HEAD_DIM + 1]).astype(o_ref.dtype)


def _dsa_attention(qt, k, vt, iqt, ik, iwt):
    bsz, seq, _ = k.shape
    qb, kb = DSA_QB, DSA_KB
    topk = min(TOPK_MAX, seq // 4)
    n_qb = seq // qb
    n_steps = lambda i: (i * qb + qb - 1) // (2 * kb) + 1
    pairs = [(i, j) for i in range(n_qb) for j in range(n_steps(i) + 1)]
    as_i32 = lambda vals: jnp.asarray(np.array(vals, np.int32))
    qi = as_i32([i for i, j in pairs])
    kj = as_i32([j for i, j in pairs])
    k_pair = as_i32([min(j, n_steps(i) - 1) for i, j in pairs])
    n_kb = lambda i: (i * qb + qb - 1) // kb + 1
    v_prev = as_i32([n_kb(i) - 1 if j == n_steps(i) else max(2 * j - 1, 0) for i, j in pairs])
    v_this = as_i32([min(2 * j, 2 * n_steps(i) - 1) for i, j in pairs])
    before =jnp.asarray(np.tril(np.ones((kb // 2, kb // 2), np.float32), -1), _BF16)
    q_map = lambda b, p, qi_r, *_: (b, 0, qi_r[p])
    vt_spec = lambda which: pl.BlockSpec(
        (1, 1, ATT_HEADS, V_ROWS, kb), lambda b, p, *refs: (b, refs[which][p], 0, 0, 0))
    grid_spec = pltpu.PrefetchScalarGridSpec(
        num_scalar_prefetch=5,
        grid=(bsz, len(pairs)),
        in_specs=[
            pl.BlockSpec((1, ATT_W, qb), q_map),
            pl.BlockSpec((1, IDX_W, qb), q_map),
            pl.BlockSpec((1, SUBLANES, qb), q_map),
            pl.BlockSpec((1, seq, LANES), lambda b, p, *_: (b, 0, 0)),
            pl.BlockSpec(before.shape, lambda b, p, *_: (0, 0)),
            pl.BlockSpec((1, 2 * kb, ATT_W), lambda b, p, *refs: (b, refs[2][p], 0)),
            vt_spec(3),
            vt_spec(4),
        ],
        out_specs=pl.BlockSpec((1, ATT_W, qb), q_map),
        scratch_shapes=[
            pltpu.VMEM((seq // kb, kb, qb), _I16),
            pltpu.VMEM((seq // kb, kb, qb), _I16),
            pltpu.VMEM((-(-(seq // kb) // CAP_CHUNKS_PER_BLOCK), kb, qb), _I16),
            pltpu.VMEM((-(-(seq // kb) // CAP_CHUNKS_PER_BLOCK), kb, qb), _I16),
            pltpu.VMEM((seq // kb, kb, qb), _BF16),
            pltpu.VMEM((ATT_HEADS, 1, qb), _F32),
            pltpu.VMEM((ATT_HEADS, V_ROWS, qb), _F32),
            pltpu.VMEM((ATT_HEADS, kb, qb), _F32),
            pltpu.VMEM((ATT_HEADS, kb, qb), _F32),
            pltpu.VMEM((ATT_HEADS, 1, qb), _F32),
            pltpu.VMEM((ATT_HEADS, 1, qb), _F32),
        ],
    )
    return pl.pallas_call(
        functools.partial(_dsa_kernel, topk=topk),
        grid_spec=grid_spec,
        out_shape=jax.ShapeDtypeStruct((bsz, ATT_W, seq), _BF16),
        compiler_params=pltpu.CompilerParams(
            dimension_semantics=("arbitrary", "arbitrary"), vmem_limit_bytes=VMEM_LIMIT_BYTES),
    )(qi, kj, k_pair, v_prev, v_this, qt, iqt, iwt, ik, before, k, vt, vt)


def _split3(a):
    hi = a.astype(_BF16)
    r1 = a - hi.astype(_F32)
    mid = r1.astype(_BF16)
    lo = (r1 - mid.astype(_F32)).astype(_BF16)
    return hi, mid, lo


def _hgrn_kernel(lbl_ref, g_ref, hq_ref, hf_ref, hi_ref, hg_ref, o_ref,
                 state_scr, kk_scr, b_scr, o_scr, *, layer):
    rows, ch, sb = HG_ROWS, HG_CHUNK, HG_SUB
    n_sub = ch // sb

    @pl.when(pl.program_id(1) == 0)
    def _reset():
        state_scr[...] = jnp.zeros_like(state_scr)

    lbl = lbl_ref[...]
    e = jnp.exp(lbl - jnp.max(lbl, axis=0, keepdims=True))
    lb = jnp.sum(e[:layer + 1], axis=0, keepdims=True) / jnp.sum(e, axis=0, keepdims=True)

    f = lb + (1.0 - lb) * jax.nn.sigmoid(hf_ref[0])
    kk_scr[...] = 1.0 - f
    logf = jnp.log(f)
    r_i = lax.broadcasted_iota(_I32, (ch, ch), 0)
    c_i = lax.broadcasted_iota(_I32, (ch, ch), 1)
    lower = jnp.where(c_i <= r_i, 1.0, 0.0).astype(_BF16)
    for c in range(rows // ch):
        parts = _split3(logf[c * ch:(c + 1) * ch])
        b_scr[c * ch:(c + 1) * ch, :] = sum(_dot(lower, part) for part in parts)

    t_idx = lax.broadcasted_iota(_I32, (sb, 1), 0)
    row_idx = lax.broadcasted_iota(_I32, (ch, 1), 0)

    def chunk(c, carry):
        r0 = pl.multiple_of(c * ch, ch)
        cs = pl.ds(r0, ch)
        b = b_scr[cs, :]
        kk = kk_scr[cs, :]
        qv = hq_ref[0, cs, :]
        vv = hi_ref[0, cs, :]
        b_last = b[ch - 1:ch]
        q_in = (qv * jnp.exp(b)).astype(_BF16)
        k_out = kk * jnp.exp(b_last - b)
        vb = vv.astype(_BF16)

        a_off = [[jnp.zeros((sb, ch), _F32)] for _ in range(HG_HEADS)]
        for s_i in range(1, n_sub):
            ref_b = b[s_i * sb - 1:s_i * sb]
            q_s = (qv[s_i * sb:(s_i + 1) * sb] * jnp.exp(b[s_i * sb:(s_i + 1) * sb] - ref_b))
            k_s = jnp.where(row_idx < s_i * sb, kk * jnp.exp(jnp.minimum(ref_b - b, 0.0)), 0.0)
            q_s = q_s.astype(_BF16)
            k_s = k_s.astype(_BF16)
            for h in range(HG_HEADS):
                hs = slice(h * HG_KDIM, (h + 1) * HG_KDIM)
                a_off[h].append(_nt_dot(q_s[:, hs], k_s[:, hs]))

        for h in range(HG_HEADS):
            hs = slice(h * HG_KDIM, (h + 1) * HG_KDIM)
            st = state_scr[h]
            o_h = _nt_dot(q_in[:, hs], st.astype(_BF16))
            a_h = jnp.concatenate(a_off[h], axis=0).astype(_BF16)
            o_scr[cs, hs] = o_h + _dot(a_h, vb[:, hs])
            state_scr[h] = (st * jnp.exp(b_last[:, hs])
                            + _tn_dot(vb[:, hs], k_out[:, hs].astype(_BF16)))

        for s_i in range(n_sub):
            rs = pl.ds(r0 + s_i * sb, sb)
            q_s = hq_ref[0, rs, :]
            b_s = b_scr[rs, :]
            acc = o_scr[rs, :]
            for t in range(sb):
                one = pl.ds(r0 + s_i * sb + t, 1)
                w = q_s * jnp.exp(jnp.minimum(b_s - b_scr[one, :], 0.0)) * kk_scr[one, :]
                v_row = hi_ref[0, one, :]
                parts = []
                for h in range(HG_HEADS):
                    hs = slice(h * HG_KDIM, (h + 1) * HG_KDIM)
                    a = jnp.sum(w[:, hs], axis=1, keepdims=True)
                    parts.append(jnp.where(t_idx >= t, a, 0.0) * v_row[:, hs])
                acc = acc + jnp.concatenate(parts, axis=1)
            o_scr[rs, :] = acc
        return carry

    lax.fori_loop(0, rows // ch, chunk, 0)

    o = o_scr[...]
    gate = hg_ref[0]
    gain = g_ref[...]
    for h in range(HG_HEADS):
        hs = slice(h * HG_KDIM, (h + 1) * HG_KDIM)
        oh = o[:, hs]
        oh = oh * lax.rsqrt(jnp.mean(oh * oh, axis=1, keepdims=True) + RMS_EPS) * gain[:, hs]
        gh = gate[:, hs]
        o_ref[0, :, hs] = (oh * (gh * jax.nn.sigmoid(gh))).astype(o_ref.dtype)


def _hgrn2(hq, hf, hi, hg, lb_logits, norm_g, layer):
    bsz, seq, _ = hq.shape
    rows = HG_ROWS
    blk = pl.BlockSpec((1, rows, HG_W), lambda b, t: (b, t, 0))
    lbl = lb_logits.reshape(lb_logits.shape[0], HG_W)
    gain = norm_g.reshape(1, HG_W)
    return pl.pallas_call(
        functools.partial(_hgrn_kernel, layer=layer),
        grid=(bsz, seq // rows),
        in_specs=[pl.BlockSpec(lbl.shape, lambda b, t: (0, 0)),
                  pl.BlockSpec(gain.shape, lambda b, t: (0, 0)),
                  blk, blk, blk, blk],
        out_specs=blk,
        out_shape=jax.ShapeDtypeStruct((bsz, seq, HG_W), _BF16),
        scratch_shapes=[
            pltpu.VMEM((HG_HEADS, HG_KDIM, HG_KDIM), _F32),
            pltpu.VMEM((rows, HG_W), _F32),
            pltpu.VMEM((rows, HG_W), _F32),
            pltpu.VMEM((rows, HG_W), _F32),
        ],
        compiler_params=pltpu.CompilerParams(
            dimension_semantics=("arbitrary", "arbitrary"), vmem_limit_bytes=VMEM_LIMIT_BYTES),
    )(lbl, gain, hq, hf, hi, hg)


def _layer_norm(y, g, b):
    mu = jnp.mean(y, axis=1, keepdims=True)
    yc = y - mu
    var = jnp.mean(yc * yc, axis=1, keepdims=True)
    return yc * lax.rsqrt(var + LN_EPS) * g + b


def _ffn_kernel(x_ref, att_ref, hgo_ref, wo_ref, g1_ref, b1_ref, wu_ref, wd_ref,
                g2_ref, b2_ref, o_ref, *, alpha):
    mix = (_tn_dot(att_ref[0], wo_ref[:ATT_W, :]) + _dot(hgo_ref[...], wo_ref[ATT_W:, :]))
    y1 = _layer_norm(alpha * x_ref[...] + mix, g1_ref[...], b1_ref[...])
    y1b = y1.astype(_BF16)
    h = jnp.zeros_like(y1)
    for c in range(wu_ref.shape[1] // FFN_COLS):
        cs = slice(c * FFN_COLS, (c + 1) * FFN_COLS)
        u = jnp.maximum(_dot(y1b, wu_ref[:, cs]), 0.0)
        h = h + _dot((u * u).astype(_BF16), wd_ref[cs, :])
    o_ref[...] = _layer_norm(alpha * y1 + h, g2_ref[...], b2_ref[...])


def _out_ffn(x2d, att_t, hgo2d, w_o, g1, b1, w_up, w_down, g2, b2, alpha):
    rows, d = x2d.shape
    seq = att_t.shape[2]
    tm = FFN_ROWS
    n_seq_blocks = seq // tm
    row_spec = lambda w: pl.BlockSpec((tm, w), lambda i: (i, 0))
    full_spec = lambda a: pl.BlockSpec(a.shape, lambda i: (0, 0))
    att_spec = pl.BlockSpec((1, ATT_W, tm), lambda i: (i // n_seq_blocks, 0, i % n_seq_blocks))
    vec = lambda a: a.reshape(1, d).astype(_F32)
    args = (x2d, att_t, hgo2d, w_o.astype(_BF16), vec(g1), vec(b1),
            w_up.astype(_BF16), w_down.astype(_BF16), vec(g2), vec(b2))
    in_specs = [row_spec(d), att_spec, row_spec(HG_W)] + [full_spec(a) for a in args[3:]]
    return pl.pallas_call(
        functools.partial(_ffn_kernel, alpha=alpha),
        grid=(rows // tm,),
        in_specs=in_specs,
        out_specs=row_spec(d),
        out_shape=jax.ShapeDtypeStruct((rows, d), _F32),
        compiler_params=pltpu.CompilerParams(
            dimension_semantics=("arbitrary",), vmem_limit_bytes=VMEM_LIMIT_BYTES),
    )(*args)


def kernel(x, w_in, w_o, lb_logits, hg_norm_g, ln1_g, ln1_b, w_up, w_down, ln2_g, ln2_b):
    bsz, seq, d = x.shape
    depth = w_in.shape[0]
    alpha = (2.0 * depth) ** 0.25
    x2d = x.reshape(bsz * seq, d)
    for l in range(depth):
        qt, k, vt, iqt, ik, iwt, hq, hf, hi, hg = _project(x2d, w_in[l], seq)
        r3 = lambda a: a.reshape(bsz, seq, a.shape[-1])
        att_t = _dsa_attention(qt, r3(k), vt, iqt, r3(ik), iwt)
        hgo = _hgrn2(r3(hq), r3(hf), r3(hi), r3(hg), lb_logits, hg_norm_g[l], l)
        x2d = _out_ffn(x2d, att_t, hgo.reshape(bsz * seq, HG_W),
                       w_o[l], ln1_g[l], ln1_b[l], w_up[l], w_down[l], ln2_g[l], ln2_b[l], alpha)
    return x2d.reshape(bsz, seq, d)
```

```python
import functools
import math

import numpy as np
import jax
import jax.numpy as jnp
from jax import lax
from jax.experimental import pallas as pl
from jax.experimental.pallas import tpu as pltpu

ATT_HEAD_DIM = 64
ATT_HEADS = 8
ATT_W = ATT_HEADS * ATT_HEAD_DIM
IDX_HEADS = 4
IDX_DIM = 64
IDX_W = IDX_HEADS * IDX_DIM
TOPK_MAX = 256
HG_KDIM = 128
HG_HEADS = 4
HG_W = HG_HEADS * HG_KDIM
ROPE_THETA = 10000.0
LN_EPS = 1e-5
RMS_EPS = 1e-6

LANES = 128
SUBLANES = 8
BF16_ROWS = 16
VMEM_LIMIT_BYTES = 56 * 1024 * 1024

PROJ_ROWS = 256
DSA_QB = 256
DSA_KB = 512
V_ROWS = ATT_HEAD_DIM + BF16_ROWS
COUNT_ACCS = 4
CAP_GROUPS = 2
CAP_ROWS = 2 * CAP_GROUPS * BF16_ROWS
CAP_CHUNKS_PER_BLOCK = DSA_KB // CAP_ROWS
FOLD_GROUPS = CAP_ROWS // BF16_ROWS
REFINE_BITS = 6
FOLD_MIN_CHUNKS = 14
HG_ROWS = 256
HG_CHUNK = 64
HG_SUB = 8
HG_MAX_DECAY = 80.0
FFN_ROWS = 256
FFN_COLS = 1024

_F32 = jnp.float32
_BF16 = jnp.bfloat16
_I32 = jnp.int32
_I16 = jnp.int16
_INT_MIN = -(2 ** 31)
_I16_MIN = -(2 ** 15)
_I16_MAX = 2 ** 15 - 1
_LOW_SIGN = 1 << 15
_NEG = -1e30


def _nt_dot(a, b):
    return lax.dot_general(a, b, (((1,), (1,)), ((), ())), preferred_element_type=_F32)


def _tn_dot(a, b):
    return lax.dot_general(a, b, (((0,), (0,)), ((), ())), preferred_element_type=_F32)


def _dot(a, b):
    return jnp.dot(a, b, preferred_element_type=_F32)


def _tree_sum(parts):
    while len(parts) > 1:
        parts = [parts[n] + parts[n + 1] for n in range(0, len(parts) - 1, 2)] + (
            [parts[-1]] if len(parts) % 2 else [])
    return parts[0]


def _rope_group(z, cos, sin_signed):
    lane = lax.broadcasted_iota(_I32, z.shape, 1)
    first_half = (lane % ATT_HEAD_DIM) < (ATT_HEAD_DIM // 2)
    upper = pltpu.roll(z, LANES - ATT_HEAD_DIM // 2, 1)
    lower = pltpu.roll(z, ATT_HEAD_DIM // 2, 1)
    return z * cos + jnp.where(first_half, upper, lower) * sin_signed


def _proj_kernel(x_ref, wa_ref, wh_ref, cos_ref, sin_ref,
                 qt_ref, k_ref, vt_ref, iqt_ref, ik_ref, iwt_ref,
                 hq_ref, hf_ref, hi_ref, hg_ref, *, q_scale, iw_scale):
    xb = x_ref[...].astype(_BF16)
    cos = cos_ref[...]
    sin = sin_ref[...]
    pa = _dot(xb, wa_ref[...])

    def roped(col0, width):
        return [_rope_group(pa[:, col0 + g * LANES: col0 + (g + 1) * LANES], cos, sin)
                for g in range(width // LANES)]

    for g, z in enumerate(roped(0, ATT_W)):
        qt_ref[0, g * LANES:(g + 1) * LANES, :] = (z * q_scale).T.astype(_BF16)
    for g, z in enumerate(roped(ATT_W, ATT_W)):
        k_ref[:, g * LANES:(g + 1) * LANES] = z.astype(_BF16)
    rows = x_ref.shape[0]
    ones_row = lax.broadcasted_iota(_I32, (V_ROWS - ATT_HEAD_DIM, rows), 0) == 0
    for g in range(ATT_W // LANES):
        vt = pa[:, 2 * ATT_W + g * LANES:2 * ATT_W + (g + 1) * LANES].T.astype(_BF16)
        for sub in range(LANES // ATT_HEAD_DIM):
            head = g * (LANES // ATT_HEAD_DIM) + sub
            vt_ref[0, 0, head, :ATT_HEAD_DIM, :] = vt[sub * ATT_HEAD_DIM:(sub + 1) * ATT_HEAD_DIM]
            vt_ref[0, 0, head, ATT_HEAD_DIM:, :] = jnp.where(ones_row, 1.0, 0.0).astype(_BF16)
    for g, z in enumerate(roped(3 * ATT_W, IDX_W)):
        iqt_ref[0, g * LANES:(g + 1) * LANES, :] = z.T.astype(_BF16)
    ik_ref[...] = roped(3 * ATT_W + IDX_W, LANES)[0].astype(_BF16)
    iwt_ref[0] = (pa[:, 3 * ATT_W + IDX_W + LANES:] * iw_scale).T[:SUBLANES]

    ph = _dot(xb, wh_ref[...])
    hq_ref[...] = ph[:, 0 * HG_W:1 * HG_W]
    hf_ref[...] = ph[:, 1 * HG_W:2 * HG_W]
    hi_ref[...] = ph[:, 2 * HG_W:3 * HG_W]
    hg_ref[...] = ph[:, 3 * HG_W:4 * HG_W]


def _rope_tables(seq):
    half = ATT_HEAD_DIM // 2
    inv = np.power(np.float64(ROPE_THETA), -np.arange(half, dtype=np.float64) / half)
    ang = np.arange(seq, dtype=np.float64)[:, None] * inv[None, :]
    cos = np.cos(ang)
    sin = np.sin(ang)
    cos_t = np.tile(np.concatenate([cos, cos], axis=1), (1, LANES // ATT_HEAD_DIM))
    sin_t = np.tile(np.concatenate([-sin, sin], axis=1), (1, LANES // ATT_HEAD_DIM))
    return jnp.asarray(cos_t, _F32), jnp.asarray(sin_t, _F32)


def _project(x2d, w_in, seq):
    rows, d = x2d.shape
    c = 3 * ATT_W + IDX_W
    w_ik = w_in[:, c:c + IDX_DIM]
    w_iw = w_in[:, c + IDX_DIM:c + IDX_DIM + IDX_HEADS]
    wa = jnp.concatenate([w_in[:, :c], w_ik, w_ik,
                          jnp.pad(w_iw, ((0, 0), (0, LANES - IDX_HEADS)))],
                         axis=1).astype(_BF16)
    wh = w_in[:, c + IDX_DIM + IDX_HEADS:].astype(_BF16)
    cos_t, sin_t = _rope_tables(seq)
    tm = PROJ_ROWS
    n_seq_blocks = seq // tm
    row_spec = lambda w: pl.BlockSpec((tm, w), lambda i: (i, 0))
    full_spec = lambda a: pl.BlockSpec(a.shape, lambda i: (0, 0))
    pos_spec = pl.BlockSpec((tm, LANES), lambda i: (i % n_seq_blocks, 0))
    bsz = rows // seq
    t_spec = lambda *feat: pl.BlockSpec(
        (1,) + feat + (tm,), lambda i: (i // n_seq_blocks,) + (0,) * len(feat) + (i % n_seq_blocks,))
    out_shapes = [
        jax.ShapeDtypeStruct((bsz, ATT_W, seq), _BF16),
        jax.ShapeDtypeStruct((rows, ATT_W), _BF16),
        jax.ShapeDtypeStruct((bsz, seq // DSA_KB, ATT_HEADS, V_ROWS, DSA_KB), _BF16),
        jax.ShapeDtypeStruct((bsz, IDX_W, seq), _BF16),
        jax.ShapeDtypeStruct((rows, LANES), _BF16),
        jax.ShapeDtypeStruct((bsz, SUBLANES, seq), _F32),
    ] + [jax.ShapeDtypeStruct((rows, HG_W), _F32)] * 4
    tiles_per_chunk = DSA_KB // tm
    vt_spec = pl.BlockSpec(
        (1, 1, ATT_HEADS, V_ROWS, tm),
        lambda i: (i // n_seq_blocks, (i % n_seq_blocks) // tiles_per_chunk, 0, 0, i % tiles_per_chunk))
    out_specs = [t_spec(ATT_W), row_spec(ATT_W), vt_spec, t_spec(IDX_W),
                 row_spec(LANES), t_spec(SUBLANES)] + [row_spec(HG_W)] * 4
    kern = functools.partial(_proj_kernel, q_scale=ATT_HEAD_DIM ** -0.5 * math.log2(math.e),
                             iw_scale=(IDX_HEADS ** -0.5) * (IDX_DIM ** -0.5))
    return pl.pallas_call(
        kern,
        grid=(rows // tm,),
        in_specs=[row_spec(d), full_spec(wa), full_spec(wh), pos_spec, pos_spec],
        out_specs=out_specs,
        out_shape=out_shapes,
        compiler_params=pltpu.CompilerParams(
            dimension_semantics=("arbitrary",), vmem_limit_bytes=VMEM_LIMIT_BYTES),
    )(x2d, wa, wh, cos_t, sin_t)


def _dsa_kernel(qi_ref, kj_ref, ka_ref, vp_ref, vc_ref,
                qt_ref, iqt_ref, iwt_ref, ik_ref, before_ref, k_ref, vt_prev_ref, vt_ref,
                o_ref,
                hi_scr, lo_scr, fold_scr, cap_scr, bias_scr, m_scr, acc_scr,
                s_even, s_odd, bm_even, bm_odd, *, topk):
    del ka_ref, vp_ref, vc_ref
    p = pl.program_id(1)
    i = qi_ref[p]
    j = kj_ref[p]
    qb, kb = DSA_QB, DSA_KB
    n_kb = (i * qb + qb - 1) // kb + 1
    n_steps = (i * qb + qb - 1) // (2 * kb) + 1
    last_pair = 2 * (n_steps - 1)
    has_second = n_kb == 2 * n_steps
    first_head = lax.broadcasted_iota(_I32, (LANES, qb), 0) < ATT_HEAD_DIM

    def one_head(pair_rows, h):
        keep = first_head if h % 2 == 0 else jnp.logical_not(first_head)
        return jnp.where(keep, pair_rows, jnp.zeros_like(pair_rows))

    @pl.when(j == 0)
    def _select():
        iwt = iwt_ref[0]
        qpos = i * qb + lax.broadcasted_iota(_I32, (kb, qb), 1)
        krow = lax.broadcasted_iota(_I32, (kb, qb), 0)
        iq_heads = [one_head(iqt_ref[0, (h // 2) * LANES:(h // 2 + 1) * LANES, :], h)
                    for h in range(IDX_HEADS)]

        def score_chunk(c, causal):
            row0 = pl.multiple_of(c * kb, kb)
            ikc = ik_ref[0, pl.ds(row0, kb), :]
            score = jnp.zeros((kb, qb), _F32)
            for h in range(IDX_HEADS):
                logits = _dot(ikc, iq_heads[h])
                score = score + iwt[h:h + 1, :] * jnp.maximum(logits, 0.0)
            bits = lax.bitcast_convert_type(score, _I32)
            key = bits ^ (((bits >> 31) & 0x7FFFFFFF) ^ _LOW_SIGN)
            if causal:
                key = jnp.where(row0 + krow <= qpos, key, _INT_MIN ^ _LOW_SIGN)
            hi = (key >> 16).astype(_I16)
            hi_scr[c] = hi
            lo_scr[c] = key.astype(_I16)
            n_tiles = kb // BF16_ROWS
            row0_fold = pl.multiple_of((c % CAP_CHUNKS_PER_BLOCK) * CAP_ROWS, CAP_ROWS)
            for g in range(FOLD_GROUPS):
                tiles = [hi[r * BF16_ROWS:(r + 1) * BF16_ROWS] for r in range(g, n_tiles, FOLD_GROUPS)]
                while len(tiles) > 1:
                    tiles = [jnp.where(tiles[n] > tiles[n + 1], tiles[n], tiles[n + 1])
                             for n in range(0, len(tiles), 2)]
                fold_scr[c // CAP_CHUNKS_PER_BLOCK, pl.ds(row0_fold + g * BF16_ROWS, BF16_ROWS), :] = tiles[0]

        def score_pair(c2, carry):
            score_chunk(2 * c2, False)
            score_chunk(2 * c2 + 1, False)
            return carry

        fold_scr[...] = jnp.full_like(fold_scr, _I16_MIN)
        lax.fori_loop(0, n_steps - 1, score_pair, 0)
        score_chunk(last_pair, True)

        @pl.when(has_second)
        def _score_second():
            score_chunk(last_pair + 1, True)

        @pl.when(jnp.logical_not(has_second))
        def _blank_second():
            hi_scr[last_pair + 1] = jnp.full((kb, qb), _I16_MIN, _I16)
            lo_scr[last_pair + 1] = jnp.full((kb, qb), _I16_MIN, _I16)

        tile_rows = lambda r: slice(r * BF16_ROWS, (r + 1) * BF16_ROWS)
        i16_min = jnp.int16(_I16_MIN)

        def count(ref, cand, n_trips, blocks_per_trip, strict=False):
            cand_rows = jnp.broadcast_to(cand.astype(_I16), (BF16_ROWS, qb))
            one, zero = jnp.int16(1), jnp.int16(0)

            def body(t, accs):
                accs = list(accs)
                for u in range(blocks_per_trip):
                    for r in range(kb // BF16_ROWS):
                        tile = ref[blocks_per_trip * t + u, tile_rows(r), :]
                        hit = tile > cand_rows if strict else tile >= cand_rows
                        accs[r % len(accs)] = accs[r % len(accs)] + jnp.where(hit, one, zero)
                return tuple(accs)

            accs = lax.fori_loop(0, n_trips, body, (jnp.zeros((BF16_ROWS, qb), _I16),) * COUNT_ACCS)
            return jnp.sum(_tree_sum(list(accs)).astype(_I32), axis=0, keepdims=True)

        def bisect(ref, n_above, n_trips, blocks_per_trip):
            def body(step, carry):
                val, n_next = carry
                bit = jnp.left_shift(jnp.int32(1), 15 - step)
                cand = jnp.where(step == 0, 0, val | bit)
                cnt = n_above + count(ref, cand, n_trips, blocks_per_trip)
                ok = cnt >= topk
                return jnp.where(ok, cand, val), jnp.where(ok, n_next, cnt)
            return lax.fori_loop(0, 16, body, (jnp.full((1, qb), _I16_MIN, _I32), n_above))

        n_cap = (n_kb + CAP_CHUNKS_PER_BLOCK - 1) // CAP_CHUNKS_PER_BLOCK
        no_keys = jnp.zeros((1, qb), _I32)

        def high_full():
            return bisect(hi_scr, no_keys, n_steps, 2)

        def high_from_fold():
            base, _ = bisect(fold_scr, no_keys, n_cap, 1)

            def count_from(cand):
                cnt = count(hi_scr, jnp.minimum(cand, _I16_MAX), n_steps, 2)
                return jnp.where(cand > _I16_MAX, 0, cnt)

            n_beyond = count_from(base + (1 << REFINE_BITS))

            def refine():
                def body(step, carry):
                    off, n_next = carry
                    cand_off = off | jnp.left_shift(jnp.int32(1), REFINE_BITS - 1 - step)
                    cnt = count_from(base + cand_off)
                    ok = cnt >= topk
                    return jnp.where(ok, cand_off, off), jnp.where(ok, n_next, cnt)
                off, n_next = lax.fori_loop(0, REFINE_BITS, body, (no_keys, n_beyond))
                return base + off, n_next

            n_outside = jnp.sum(jnp.where(n_beyond >= topk, 1.0, 0.0))
            return lax.cond(n_outside > 0.0, high_full, refine)

        t_hi, n_gt_hi = lax.cond(n_kb >= FOLD_MIN_CHUNKS, high_from_fold, high_full)
        t_hi_tile = jnp.broadcast_to(t_hi.astype(_I16), (BF16_ROWS, qb))

        cap_scr[...] = jnp.full_like(cap_scr, _I16_MIN)

        def capture_block(c, carry):
            tops = [[jnp.full((BF16_ROWS, qb), _I16_MIN, _I16)] * 2 for _ in range(CAP_GROUPS)]
            for r in range(kb // BF16_ROWS):
                x = jnp.where(hi_scr[c, tile_rows(r), :] == t_hi_tile, lo_scr[c, tile_rows(r), :], i16_min)
                lo_scr[c, tile_rows(r), :] = x
                first, second = tops[r % CAP_GROUPS]
                above = x > first
                tops[r % CAP_GROUPS] = [jnp.where(above, x, first),
                                        jnp.where(above, first, jnp.where(x > second, x, second))]
            row0 = pl.multiple_of((c % CAP_CHUNKS_PER_BLOCK) * CAP_ROWS, CAP_ROWS)
            for g in range(CAP_GROUPS):
                for t in range(2):
                    cap_scr[c // CAP_CHUNKS_PER_BLOCK,
                            pl.ds(row0 + (2 * g + t) * BF16_ROWS, BF16_ROWS), :] = tops[g][t]
            return carry

        lax.fori_loop(0, n_kb, capture_block, 0)
        t_lo, _ = bisect(cap_scr, n_gt_hi, n_cap, 1)
        n_gt = n_gt_hi + count(lo_scr, t_lo, n_steps, 2, strict=True)
        n_wrong = jnp.sum(jnp.where(n_gt >= topk, 1.0, 0.0))
        t_lo, n_gt = lax.cond(n_wrong > 0.0,
                              lambda: bisect(lo_scr, n_gt_hi, n_steps, 2),
                              lambda: (t_lo, n_gt))
        t_hi_rows = jnp.broadcast_to(t_hi.astype(_I16), (kb, qb))
        t_lo = jnp.where((t_hi == _I16_MIN) & (t_lo == _I16_MIN), _I16_MIN + 1, t_lo)
        t_lo_rows = jnp.broadcast_to(t_lo.astype(_I16), (kb, qb))
        rem_rows = jnp.broadcast_to((topk - n_gt).astype(_I16), (kb, qb))

        def mask_block(c, seen):
            hi = hi_scr[c]
            lo = lo_scr[c]
            tie = (hi == t_hi_rows) & (lo == t_lo_rows)
            tie_count = jnp.where(tie, jnp.asarray(1, _BF16), jnp.asarray(0, _BF16))

            def tie_at(r):
                hit = (hi[r:r + 1].astype(_I32) == t_hi) & (lo[r:r + 1].astype(_I32) == t_lo)
                return jnp.where(hit, 1.0, 0.0)

            half = kb // 2
            before_top = seen + _dot(before_ref[...], tie_count[:half])
            seen_mid = before_top[half - 1:half] + tie_at(half - 1)
            before_bot = seen_mid + _dot(before_ref[...], tie_count[half:])
            ties_before = jnp.concatenate([before_top, before_bot], axis=0)
            allowed = ties_before.astype(_I32).astype(_I16) < rem_rows
            chosen = (hi > t_hi_rows) | (lo > t_lo_rows) | (tie & allowed)
            bias_scr[c] = jnp.where(chosen, jnp.asarray(0, bias_scr.dtype),
                                    jnp.asarray(_NEG, bias_scr.dtype))
            return before_bot[half - 1:half] + tie_at(kb - 1)

        seen = lax.fori_loop(0, n_steps - 1,
                             lambda c2, seen: mask_block(2 * c2 + 1, mask_block(2 * c2, seen)),
                             jnp.zeros((1, qb), _F32))
        seen = mask_block(last_pair, seen)

        @pl.when(has_second)
        def _mask_second():
            mask_block(last_pair + 1, seen)

        m_scr[...] = jnp.full_like(m_scr, _NEG)
        acc_scr[...] = jnp.zeros_like(acc_scr)

    def stage_b(h, s_read, bm_read, vt_read):
        m_old = m_scr[h]
        m_new = jnp.maximum(m_old, bm_read[h])
        pr = jnp.exp2(s_read[h] - m_new).astype(_BF16)
        acc_scr[h] = jnp.exp2(m_old - m_new) * acc_scr[h] + _dot(vt_read[0, 0, h], pr)
        m_scr[h] = m_new

    def phase(k_rows, chunk, s_write, bm_write, s_read, bm_read, vt_read):
        bias = bias_scr[chunk].astype(_F32)

        def stage_a(h):
            pair = slice((h // 2) * LANES, (h // 2 + 1) * LANES)
            s = _dot(k_ref[0, k_rows, pair], one_head(qt_ref[0, pair, :], h)) + bias
            s_write[h] = s
            bm_write[h] = jnp.max(s, axis=0, keepdims=True)

        stage_a(0)
        for h in range(ATT_HEADS):
            if h + 1 < ATT_HEADS:
                stage_a(h + 1)
            if s_read is not None:
                stage_b(h, s_read, bm_read, vt_read)

    @pl.when(j == 0)
    def _fill():
        phase(slice(0, kb), 0, s_even, bm_even, None, None, None)

    @pl.when((j > 0) & (j < n_steps))
    def _first():
        phase(slice(0, kb), 2 * j, s_even, bm_even, s_odd, bm_odd, vt_prev_ref)

    @pl.when(2 * j + 1 < n_kb)
    def _second():
        phase(slice(kb, 2 * kb), 2 * j + 1, s_odd, bm_odd, s_even, bm_even, vt_ref)

    @pl.when((j == n_steps) & has_second)
    def _drain_odd():
        for h in range(ATT_HEADS):
            stage_b(h, s_odd, bm_odd, vt_prev_ref)

    @pl.when((j == n_steps) & jnp.logical_not(has_second))
    def _drain_even():
        for h in range(ATT_HEADS):
            stage_b(h, s_even, bm_even, vt_prev_ref)

    @pl.when(j == n_steps)
    def _finish():
        for h in range(ATT_HEADS):
            a = acc_scr[h]
            o_ref[0, h * ATT_HEAD_DIM:(h + 1) * ATT_HEAD_DIM, :] = (
                a[:ATT_HEAD_DIM] / a[ATT_HEAD_DIM:ATT_HEAD_DIM + 1]).astype(o_ref.dtype)


def _dsa_attention(qt, k, vt, iqt, ik, iwt):
    bsz, seq, _ = k.shape
    qb, kb = DSA_QB, DSA_KB
    topk = min(TOPK_MAX, seq // 4)
    n_qb = seq // qb
    n_steps = lambda i: (i * qb + qb - 1) // (2 * kb) + 1
    pairs = [(i, j) for i in range(n_qb) for j in range(n_steps(i) + 1)]
    as_i32 = lambda vals: jnp.asarray(np.array(vals, np.int32))
    qi = as_i32([i for i, j in pairs])
    kj = as_i32([j for i, j in pairs])
    k_pair = as_i32([min(j, n_steps(i) - 1) for i, j in pairs])
    n_kb = lambda i: (i * qb + qb - 1) // kb + 1
    v_prev = as_i32([n_kb(i) - 1 if j == n_steps(i) else max(2 * j - 1, 0) for i, j in pairs])
    v_this = as_i32([min(2 * j, 2 * n_steps(i) - 1) for i, j in pairs])
    before =jnp.asarray(np.tril(np.ones((kb // 2, kb // 2), np.float32), -1), _BF16)
    q_map = lambda b, p, qi_r, *_: (b, 0, qi_r[p])
    vt_spec = lambda which: pl.BlockSpec(
        (1, 1, ATT_HEADS, V_ROWS, kb), lambda b, p, *refs: (b, refs[which][p], 0, 0, 0))
    grid_spec = pltpu.PrefetchScalarGridSpec(
        num_scalar_prefetch=5,
        grid=(bsz, len(pairs)),
        in_specs=[
            pl.BlockSpec((1, ATT_W, qb), q_map),
            pl.BlockSpec((1, IDX_W, qb), q_map),
            pl.BlockSpec((1, SUBLANES, qb), q_map),
            pl.BlockSpec((1, seq, LANES), lambda b, p, *_: (b, 0, 0)),
            pl.BlockSpec(before.shape, lambda b, p, *_: (0, 0)),
            pl.BlockSpec((1, 2 * kb, ATT_W), lambda b, p, *refs: (b, refs[2][p], 0)),
            vt_spec(3),
            vt_spec(4),
        ],
        out_specs=pl.BlockSpec((1, ATT_W, qb), q_map),
        scratch_shapes=[
            pltpu.VMEM((seq // kb, kb, qb), _I16),
            pltpu.VMEM((seq // kb, kb, qb), _I16),
            pltpu.VMEM((-(-(seq // kb) // CAP_CHUNKS_PER_BLOCK), kb, qb), _I16),
            pltpu.VMEM((-(-(seq // kb) // CAP_CHUNKS_PER_BLOCK), kb, qb), _I16),
            pltpu.VMEM((seq // kb, kb, qb), _BF16),
            pltpu.VMEM((ATT_HEADS, 1, qb), _F32),
            pltpu.VMEM((ATT_HEADS, V_ROWS, qb), _F32),
            pltpu.VMEM((ATT_HEADS, kb, qb), _F32),
            pltpu.VMEM((ATT_HEADS, kb, qb), _F32),
            pltpu.VMEM((ATT_HEADS, 1, qb), _F32),
            pltpu.VMEM((ATT_HEADS, 1, qb), _F32),
        ],
    )
    return pl.pallas_call(
        functools.partial(_dsa_kernel, topk=topk),
        grid_spec=grid_spec,
        out_shape=jax.ShapeDtypeStruct((bsz, ATT_W, seq), _BF16),
        compiler_params=pltpu.CompilerParams(
            dimension_semantics=("arbitrary", "arbitrary"), vmem_limit_bytes=VMEM_LIMIT_BYTES),
    )(qi, kj, k_pair, v_prev, v_this, qt, iqt, iwt, ik, before, k, vt, vt)


def _split3(a):
    hi = a.astype(_BF16)
    r1 = a - hi.astype(_F32)
    mid = r1.astype(_BF16)
    lo = (r1 - mid.astype(_F32)).astype(_BF16)
    return hi, mid, lo


def _hgrn_kernel(lbl_ref, g_ref, hq_ref, hf_ref, hi_ref, hg_ref, o_ref,
                 state_scr, kk_scr, b_scr, o_scr, *, layer):
    rows, ch, sb = HG_ROWS, HG_CHUNK, HG_SUB
    n_sub = ch // sb

    @pl.when(pl.program_id(1) == 0)
    def _reset():
        state_scr[...] = jnp.zeros_like(state_scr)

    lbl = lbl_ref[...]
    e = jnp.exp(lbl - jnp.max(lbl, axis=0, keepdims=True))
    lb = jnp.sum(e[:layer + 1], axis=0, keepdims=True) / jnp.sum(e, axis=0, keepdims=True)

    f = lb + (1.0 - lb) * jax.nn.sigmoid(hf_ref[0])
    kk_scr[...] = 1.0 - f
    logf = jnp.log(f)
    r_i = lax.broadcasted_iota(_I32, (ch, ch), 0)
    c_i = lax.broadcasted_iota(_I32, (ch, ch), 1)
    lower = jnp.where(c_i <= r_i, 1.0, 0.0).astype(_BF16)
    for c in range(rows // ch):
        parts = _split3(logf[c * ch:(c + 1) * ch])
        b_scr[c * ch:(c + 1) * ch, :] = sum(_dot(lower, part) for part in parts)

    t_idx = lax.broadcasted_iota(_I32, (sb, 1), 0)
    row_idx = lax.broadcasted_iota(_I32, (ch, 1), 0)

    def chunk(c, carry, bounded):
        r0 = pl.multiple_of(c * ch, ch)
        cs = pl.ds(r0, ch)
        b = b_scr[cs, :]
        kk = kk_scr[cs, :]
        qv = hq_ref[0, cs, :]
        vv = hi_ref[0, cs, :]
        b_last = b[ch - 1:ch]
        q_in = (qv * jnp.exp(b)).astype(_BF16)
        k_out = kk * jnp.exp(b_last - b)
        vb = vv.astype(_BF16)

        a_off = [[] for _ in range(HG_HEADS)]
        for s_i in range(n_sub):
            if s_i == 0 and not bounded:
                for h in range(HG_HEADS):
                    a_off[h].append(jnp.zeros((sb, ch), _F32))
                continue
            ref_b = b[s_i * sb - 1:s_i * sb] if s_i else jnp.zeros_like(b_last)
            q_s = (qv[s_i * sb:(s_i + 1) * sb] * jnp.exp(b[s_i * sb:(s_i + 1) * sb] - ref_b))
            n_rows = (s_i + 1) * sb if bounded else s_i * sb
            k_s = jnp.where(row_idx < n_rows,
                            kk * jnp.exp(jnp.where(row_idx < n_rows, ref_b - b, 0.0)), 0.0)
            q_s = q_s.astype(_BF16)
            k_s = k_s.astype(_BF16)
            for h in range(HG_HEADS):
                hs = slice(h * HG_KDIM, (h + 1) * HG_KDIM)
                a_off[h].append(_nt_dot(q_s[:, hs], k_s[:, hs]))

        for h in range(HG_HEADS):
            hs = slice(h * HG_KDIM, (h + 1) * HG_KDIM)
            st = state_scr[h]
            o_h = _nt_dot(q_in[:, hs], st.astype(_BF16))
            a_h = jnp.concatenate(a_off[h], axis=0)
            if bounded:
                a_h = jnp.where(c_i <= r_i, a_h, 0.0)
            o_scr[cs, hs] = o_h + _dot(a_h.astype(_BF16), vb[:, hs])
            state_scr[h] = (st * jnp.exp(b_last[:, hs])
                            + _tn_dot(vb[:, hs], k_out[:, hs].astype(_BF16)))
        if bounded:
            return carry

        for s_i in range(n_sub):
            rs = pl.ds(r0 + s_i * sb, sb)
            q_s = hq_ref[0, rs, :]
            b_s = b_scr[rs, :]
            acc = o_scr[rs, :]
            for t in range(sb):
                one = pl.ds(r0 + s_i * sb + t, 1)
                w = q_s * jnp.exp(jnp.minimum(b_s - b_scr[one, :], 0.0)) * kk_scr[one, :]
                v_row = hi_ref[0, one, :]
                parts = []
                for h in range(HG_HEADS):
                    hs = slice(h * HG_KDIM, (h + 1) * HG_KDIM)
                    a = jnp.sum(w[:, hs], axis=1, keepdims=True)
                    parts.append(jnp.where(t_idx >= t, a, 0.0) * v_row[:, hs])
                acc = acc + jnp.concatenate(parts, axis=1)
            o_scr[rs, :] = acc
        return carry

    run = lambda bounded: lambda: lax.fori_loop(
        0, rows // ch, functools.partial(chunk, bounded=bounded), 0)
    lax.cond(jnp.min(logf) >= -HG_MAX_DECAY / sb, run(True), run(False))

    o = o_scr[...]
    gate = hg_ref[0]
    gain = g_ref[...]
    for h in range(HG_HEADS):
        hs = slice(h * HG_KDIM, (h + 1) * HG_KDIM)
        oh = o[:, hs]
        oh = oh * lax.rsqrt(jnp.mean(oh * oh, axis=1, keepdims=True) + RMS_EPS) * gain[:, hs]
        gh = gate[:, hs]
        o_ref[0, :, hs] = (oh * (gh * jax.nn.sigmoid(gh))).astype(o_ref.dtype)


def _hgrn2(hq, hf, hi, hg, lb_logits, norm_g, layer):
    bsz, seq, _ = hq.shape
    rows = HG_ROWS
    blk = pl.BlockSpec((1, rows, HG_W), lambda b, t: (b, t, 0))
    lbl = lb_logits.reshape(lb_logits.shape[0], HG_W)
    gain = norm_g.reshape(1, HG_W)
    return pl.pallas_call(
        functools.partial(_hgrn_kernel, layer=layer),
        grid=(bsz, seq // rows),
        in_specs=[pl.BlockSpec(lbl.shape, lambda b, t: (0, 0)),
                  pl.BlockSpec(gain.shape, lambda b, t: (0, 0)),
                  blk, blk, blk, blk],
        out_specs=blk,
        out_shape=jax.ShapeDtypeStruct((bsz, seq, HG_W), _BF16),
        scratch_shapes=[
            pltpu.VMEM((HG_HEADS, HG_KDIM, HG_KDIM), _F32),
            pltpu.VMEM((rows, HG_W), _F32),
            pltpu.VMEM((rows, HG_W), _F32),
            pltpu.VMEM((rows, HG_W), _F32),
        ],
        compiler_params=pltpu.CompilerParams(
            dimension_semantics=("arbitrary", "arbitrary"), vmem_limit_bytes=VMEM_LIMIT_BYTES),
    )(lbl, gain, hq, hf, hi, hg)


def _layer_norm(y, g, b):
    mu = jnp.mean(y, axis=1, keepdims=True)
    yc = y - mu
    var = jnp.mean(yc * yc, axis=1, keepdims=True)
    return yc * lax.rsqrt(var + LN_EPS) * g + b


def _ffn_kernel(x_ref, att_ref, hgo_ref, wo_ref, g1_ref, b1_ref, wu_ref, wd_ref,
                g2_ref, b2_ref, o_ref, *, alpha):
    mix = (_tn_dot(att_ref[0], wo_ref[:ATT_W, :]) + _dot(hgo_ref[...], wo_ref[ATT_W:, :]))
    y1 = _layer_norm(alpha * x_ref[...] + mix, g1_ref[...], b1_ref[...])
    y1b = y1.astype(_BF16)
    h = jnp.zeros_like(y1)
    for c in range(wu_ref.shape[1] // FFN_COLS):
        cs = slice(c * FFN_COLS, (c + 1) * FFN_COLS)
        u = jnp.maximum(_dot(y1b, wu_ref[:, cs]), 0.0)
        h = h + _dot((u * u).astype(_BF16), wd_ref[cs, :])
    o_ref[...] = _layer_norm(alpha * y1 + h, g2_ref[...], b2_ref[...])


def _out_ffn(x2d, att_t, hgo2d, w_o, g1, b1, w_up, w_down, g2, b2, alpha):
    rows, d = x2d.shape
    seq = att_t.shape[2]
    tm = FFN_ROWS
    n_seq_blocks = seq // tm
    row_spec = lambda w: pl.BlockSpec((tm, w), lambda i: (i, 0))
    full_spec = lambda a: pl.BlockSpec(a.shape, lambda i: (0, 0))
    att_spec = pl.BlockSpec((1, ATT_W, tm), lambda i: (i // n_seq_blocks, 0, i % n_seq_blocks))
    vec = lambda a: a.reshape(1, d).astype(_F32)
    args = (x2d, att_t, hgo2d, w_o.astype(_BF16), vec(g1), vec(b1),
            w_up.astype(_BF16), w_down.astype(_BF16), vec(g2), vec(b2))
    in_specs = [row_spec(d), att_spec, row_spec(HG_W)] + [full_spec(a) for a in args[3:]]
    return pl.pallas_call(
        functools.partial(_ffn_kernel, alpha=alpha),
        grid=(rows // tm,),
        in_specs=in_specs,
        out_specs=row_spec(d),
        out_shape=jax.ShapeDtypeStruct((rows, d), _F32),
        compiler_params=pltpu.CompilerParams(
            dimension_semantics=("arbitrary",), vmem_limit_bytes=VMEM_LIMIT_BYTES),
    )(*args)


def kernel(x, w_in, w_o, lb_logits, hg_norm_g, ln1_g, ln1_b, w_up, w_down, ln2_g, ln2_b):
    bsz, seq, d = x.shape
    depth = w_in.shape[0]
    alpha = (2.0 * depth) ** 0.25
    x2d = x.reshape(bsz * seq, d)
    for l in range(depth):
        qt, k, vt, iqt, ik, iwt, hq, hf, hi, hg = _project(x2d, w_in[l], seq)
        r3 = lambda a: a.reshape(bsz, seq, a.shape[-1])
        att_t = _dsa_attention(qt, r3(k), vt, iqt, r3(ik), iwt)
        hgo = _hgrn2(r3(hq), r3(hf), r3(hi), r3(hg), lb_logits, hg_norm_g[l], l)
        x2d = _out_ffn(x2d, att_t, hgo.reshape(bsz * seq, HG_W),
                       w_o[l], ln1_g[l], ln1_b[l], w_up[l], w_down[l], ln2_g[l], ln2_b[l], alpha)
    return x2d.reshape(bsz, seq, d)
```

```python
import functools
import math

import numpy as np
import jax
import jax.numpy as jnp
from jax import lax
from jax.experimental import pallas as pl
from jax.experimental.pallas import tpu as pltpu

ATT_HEAD_DIM = 64
ATT_HEADS = 8
ATT_W = ATT_HEADS * ATT_HEAD_DIM
IDX_HEADS = 4
IDX_DIM = 64
IDX_W = IDX_HEADS * IDX_DIM
TOPK_MAX = 256
HG_KDIM = 128
HG_HEADS = 4
HG_W = HG_HEADS * HG_KDIM
ROPE_THETA = 10000.0
LN_EPS = 1e-5
RMS_EPS = 1e-6

LANES = 128
SUBLANES = 8
BF16_ROWS = 16
VMEM_LIMIT_BYTES = 56 * 1024 * 1024

PROJ_ROWS = 256
DSA_QB = 256
DSA_KB = 512
V_ROWS = ATT_HEAD_DIM + BF16_ROWS
COUNT_ACCS = 4
CAP_GROUPS = 2
CAP_ROWS = 2 * CAP_GROUPS * BF16_ROWS
CAP_CHUNKS_PER_BLOCK = DSA_KB // CAP_ROWS
FOLD_GROUPS = CAP_ROWS // BF16_ROWS
REFINE_BITS = 6
FOLD_MIN_CHUNKS = 14
HG_ROWS = 256
HG_CHUNK = 64
HG_SUB = 8
HG_SUB_BOUNDED = 16
HG_MAX_DECAY = 80.0
FFN_ROWS = 256
FFN_COLS = 1024

_F32 = jnp.float32
_BF16 = jnp.bfloat16
_I32 = jnp.int32
_I16 = jnp.int16
_INT_MIN = -(2 ** 31)
_I16_MIN = -(2 ** 15)
_I16_MAX = 2 ** 15 - 1
_LOW_SIGN = 1 << 15
_NEG = -1e30


def _nt_dot(a, b):
    return lax.dot_general(a, b, (((1,), (1,)), ((), ())), preferred_element_type=_F32)


def _tn_dot(a, b):
    return lax.dot_general(a, b, (((0,), (0,)), ((), ())), preferred_element_type=_F32)


def _dot(a, b):
    return jnp.dot(a, b, preferred_element_type=_F32)


def _tree_sum(parts):
    while len(parts) > 1:
        parts = [parts[n] + parts[n + 1] for n in range(0, len(parts) - 1, 2)] + (
            [parts[-1]] if len(parts) % 2 else [])
    return parts[0]


def _rope_group(z, cos, sin_signed):
    lane = lax.broadcasted_iota(_I32, z.shape, 1)
    first_half = (lane % ATT_HEAD_DIM) < (ATT_HEAD_DIM // 2)
    upper = pltpu.roll(z, LANES - ATT_HEAD_DIM // 2, 1)
    lower = pltpu.roll(z, ATT_HEAD_DIM // 2, 1)
    return z * cos + jnp.where(first_half, upper, lower) * sin_signed


def _proj_kernel(x_ref, wa_ref, wh_ref, cos_ref, sin_ref,
                 qt_ref, k_ref, vt_ref, iqt_ref, ik_ref, iwt_ref,
                 hq_ref, hf_ref, hi_ref, hg_ref, *, q_scale, iw_scale):
    xb = x_ref[...].astype(_BF16)
    cos = cos_ref[...]
    sin = sin_ref[...]
    pa = _dot(xb, wa_ref[...])

    def roped(col0, width):
        return [_rope_group(pa[:, col0 + g * LANES: col0 + (g + 1) * LANES], cos, sin)
                for g in range(width // LANES)]

    for g, z in enumerate(roped(0, ATT_W)):
        qt_ref[0, g * LANES:(g + 1) * LANES, :] = (z * q_scale).T.astype(_BF16)
    for g, z in enumerate(roped(ATT_W, ATT_W)):
        k_ref[:, g * LANES:(g + 1) * LANES] = z.astype(_BF16)
    rows = x_ref.shape[0]
    ones_row = lax.broadcasted_iota(_I32, (V_ROWS - ATT_HEAD_DIM, rows), 0) == 0
    for g in range(ATT_W // LANES):
        vt = pa[:, 2 * ATT_W + g * LANES:2 * ATT_W + (g + 1) * LANES].T.astype(_BF16)
        for sub in range(LANES // ATT_HEAD_DIM):
            head = g * (LANES // ATT_HEAD_DIM) + sub
            vt_ref[0, 0, head, :ATT_HEAD_DIM, :] = vt[sub * ATT_HEAD_DIM:(sub + 1) * ATT_HEAD_DIM]
            vt_ref[0, 0, head, ATT_HEAD_DIM:, :] = jnp.where(ones_row, 1.0, 0.0).astype(_BF16)
    for g, z in enumerate(roped(3 * ATT_W, IDX_W)):
        iqt_ref[0, g * LANES:(g + 1) * LANES, :] = z.T.astype(_BF16)
    ik_ref[...] = roped(3 * ATT_W + IDX_W, LANES)[0].astype(_BF16)
    iwt_ref[0] = (pa[:, 3 * ATT_W + IDX_W + LANES:] * iw_scale).T[:SUBLANES]

    ph = _dot(xb, wh_ref[...])
    hq_ref[...] = ph[:, 0 * HG_W:1 * HG_W]
    hf_ref[...] = ph[:, 1 * HG_W:2 * HG_W]
    hi_ref[...] = ph[:, 2 * HG_W:3 * HG_W]
    hg_ref[...] = ph[:, 3 * HG_W:4 * HG_W]


def _rope_tables(seq):
    half = ATT_HEAD_DIM // 2
    inv = np.power(np.float64(ROPE_THETA), -np.arange(half, dtype=np.float64) / half)
    ang = np.arange(seq, dtype=np.float64)[:, None] * inv[None, :]
    cos = np.cos(ang)
    sin = np.sin(ang)
    cos_t = np.tile(np.concatenate([cos, cos], axis=1), (1, LANES // ATT_HEAD_DIM))
    sin_t = np.tile(np.concatenate([-sin, sin], axis=1), (1, LANES // ATT_HEAD_DIM))
    return jnp.asarray(cos_t, _F32), jnp.asarray(sin_t, _F32)


def _project(x2d, w_in, seq):
    rows, d = x2d.shape
    c = 3 * ATT_W + IDX_W
    w_ik = w_in[:, c:c + IDX_DIM]
    w_iw = w_in[:, c + IDX_DIM:c + IDX_DIM + IDX_HEADS]
    wa = jnp.concatenate([w_in[:, :c], w_ik, w_ik,
                          jnp.pad(w_iw, ((0, 0), (0, LANES - IDX_HEADS)))],
                         axis=1).astype(_BF16)
    wh = w_in[:, c + IDX_DIM + IDX_HEADS:].astype(_BF16)
    cos_t, sin_t = _rope_tables(seq)
    tm = PROJ_ROWS
    n_seq_blocks = seq // tm
    row_spec = lambda w: pl.BlockSpec((tm, w), lambda i: (i, 0))
    full_spec = lambda a: pl.BlockSpec(a.shape, lambda i: (0, 0))
    pos_spec = pl.BlockSpec((tm, LANES), lambda i: (i % n_seq_blocks, 0))
    bsz = rows // seq
    t_spec = lambda *feat: pl.BlockSpec(
        (1,) + feat + (tm,), lambda i: (i // n_seq_blocks,) + (0,) * len(feat) + (i % n_seq_blocks,))
    out_shapes = [
        jax.ShapeDtypeStruct((bsz, ATT_W, seq), _BF16),
        jax.ShapeDtypeStruct((rows, ATT_W), _BF16),
        jax.ShapeDtypeStruct((bsz, seq // DSA_KB, ATT_HEADS, V_ROWS, DSA_KB), _BF16),
        jax.ShapeDtypeStruct((bsz, IDX_W, seq), _BF16),
        jax.ShapeDtypeStruct((rows, LANES), _BF16),
        jax.ShapeDtypeStruct((bsz, SUBLANES, seq), _F32),
    ] + [jax.ShapeDtypeStruct((rows, HG_W), _F32)] * 4
    tiles_per_chunk = DSA_KB // tm
    vt_spec = pl.BlockSpec(
        (1, 1, ATT_HEADS, V_ROWS, tm),
        lambda i: (i // n_seq_blocks, (i % n_seq_blocks) // tiles_per_chunk, 0, 0, i % tiles_per_chunk))
    out_specs = [t_spec(ATT_W), row_spec(ATT_W), vt_spec, t_spec(IDX_W),
                 row_spec(LANES), t_spec(SUBLANES)] + [row_spec(HG_W)] * 4
    kern = functools.partial(_proj_kernel, q_scale=ATT_HEAD_DIM ** -0.5 * math.log2(math.e),
                             iw_scale=(IDX_HEADS ** -0.5) * (IDX_DIM ** -0.5))
    return pl.pallas_call(
        kern,
        grid=(rows // tm,),
        in_specs=[row_spec(d), full_spec(wa), full_spec(wh), pos_spec, pos_spec],
        out_specs=out_specs,
        out_shape=out_shapes,
        compiler_params=pltpu.CompilerParams(
            dimension_semantics=("arbitrary",), vmem_limit_bytes=VMEM_LIMIT_BYTES),
    )(x2d, wa, wh, cos_t, sin_t)


def _dsa_kernel(qi_ref, kj_ref, ka_ref, vp_ref, vc_ref,
                qt_ref, iqt_ref, iwt_ref, ik_ref, before_ref, k_ref, vt_prev_ref, vt_ref,
                o_ref,
                hi_scr, lo_scr, fold_scr, cap_scr, bias_scr, m_scr, acc_scr,
                s_even, s_odd, bm_even, bm_odd, *, topk):
    del ka_ref, vp_ref, vc_ref
    p = pl.program_id(1)
    i = qi_ref[p]
    j = kj_ref[p]
    qb, kb = DSA_QB, DSA_KB
    n_kb = (i * qb + qb - 1) // kb + 1
    n_steps = (i * qb + qb - 1) // (2 * kb) + 1
    last_pair = 2 * (n_steps - 1)
    has_second = n_kb == 2 * n_steps
    first_head = lax.broadcasted_iota(_I32, (LANES, qb), 0) < ATT_HEAD_DIM

    def one_head(pair_rows, h):
        keep = first_head if h % 2 == 0 else jnp.logical_not(first_head)
        return jnp.where(keep, pair_rows, jnp.zeros_like(pair_rows))

    @pl.when(j == 0)
    def _select():
        iwt = iwt_ref[0]
        qpos = i * qb + lax.broadcasted_iota(_I32, (kb, qb), 1)
        krow = lax.broadcasted_iota(_I32, (kb, qb), 0)
        iq_heads = [one_head(iqt_ref[0, (h // 2) * LANES:(h // 2 + 1) * LANES, :], h)
                    for h in range(IDX_HEADS)]

        def score_chunk(c, causal):
            row0 = pl.multiple_of(c * kb, kb)
            ikc = ik_ref[0, pl.ds(row0, kb), :]
            score = jnp.zeros((kb, qb), _F32)
            for h in range(IDX_HEADS):
                logits = _dot(ikc, iq_heads[h])
                score = score + iwt[h:h + 1, :] * jnp.maximum(logits, 0.0)
            bits = lax.bitcast_convert_type(score, _I32)
            key = bits ^ (((bits >> 31) & 0x7FFFFFFF) ^ _LOW_SIGN)
            if causal:
                key = jnp.where(row0 + krow <= qpos, key, _INT_MIN ^ _LOW_SIGN)
            hi = (key >> 16).astype(_I16)
            hi_scr[c] = hi
            lo_scr[c] = key.astype(_I16)
            n_tiles = kb // BF16_ROWS
            row0_fold = pl.multiple_of((c % CAP_CHUNKS_PER_BLOCK) * CAP_ROWS, CAP_ROWS)
            for g in range(FOLD_GROUPS):
                tiles = [hi[r * BF16_ROWS:(r + 1) * BF16_ROWS] for r in range(g, n_tiles, FOLD_GROUPS)]
                while len(tiles) > 1:
                    tiles = [jnp.where(tiles[n] > tiles[n + 1], tiles[n], tiles[n + 1])
                             for n in range(0, len(tiles), 2)]
                fold_scr[c // CAP_CHUNKS_PER_BLOCK, pl.ds(row0_fold + g * BF16_ROWS, BF16_ROWS), :] = tiles[0]

        def score_pair(c2, carry):
            score_chunk(2 * c2, False)
            score_chunk(2 * c2 + 1, False)
            return carry

        fold_scr[...] = jnp.full_like(fold_scr, _I16_MIN)
        lax.fori_loop(0, n_steps - 1, score_pair, 0)
        score_chunk(last_pair, True)

        @pl.when(has_second)
        def _score_second():
            score_chunk(last_pair + 1, True)

        @pl.when(jnp.logical_not(has_second))
        def _blank_second():
            hi_scr[last_pair + 1] = jnp.full((kb, qb), _I16_MIN, _I16)
            lo_scr[last_pair + 1] = jnp.full((kb, qb), _I16_MIN, _I16)

        tile_rows = lambda r: slice(r * BF16_ROWS, (r + 1) * BF16_ROWS)
        i16_min = jnp.int16(_I16_MIN)

        def count(ref, cand, n_trips, blocks_per_trip, strict=False):
            cand_rows = jnp.broadcast_to(cand.astype(_I16), (BF16_ROWS, qb))
            one, zero = jnp.int16(1), jnp.int16(0)

            def body(t, accs):
                accs = list(accs)
                for u in range(blocks_per_trip):
                    for r in range(kb // BF16_ROWS):
                        tile = ref[blocks_per_trip * t + u, tile_rows(r), :]
                        hit = tile > cand_rows if strict else tile >= cand_rows
                        accs[r % len(accs)] = accs[r % len(accs)] + jnp.where(hit, one, zero)
                return tuple(accs)

            accs = lax.fori_loop(0, n_trips, body, (jnp.zeros((BF16_ROWS, qb), _I16),) * COUNT_ACCS)
            return jnp.sum(_tree_sum(list(accs)).astype(_I32), axis=0, keepdims=True)

        def bisect(ref, n_above, n_trips, blocks_per_trip):
            def body(step, carry):
                val, n_next = carry
                bit = jnp.left_shift(jnp.int32(1), 15 - step)
                cand = jnp.where(step == 0, 0, val | bit)
                cnt = n_above + count(ref, cand, n_trips, blocks_per_trip)
                ok = cnt >= topk
                return jnp.where(ok, cand, val), jnp.where(ok, n_next, cnt)
            return lax.fori_loop(0, 16, body, (jnp.full((1, qb), _I16_MIN, _I32), n_above))

        n_cap = (n_kb + CAP_CHUNKS_PER_BLOCK - 1) // CAP_CHUNKS_PER_BLOCK
        no_keys = jnp.zeros((1, qb), _I32)

        def high_full():
            return bisect(hi_scr, no_keys, n_steps, 2)

        def high_from_fold():
            base, _ = bisect(fold_scr, no_keys, n_cap, 1)

            def count_from(cand):
                cnt = count(hi_scr, jnp.minimum(cand, _I16_MAX), n_steps, 2)
                return jnp.where(cand > _I16_MAX, 0, cnt)

            n_beyond = count_from(base + (1 << REFINE_BITS))

            def refine():
                def body(step, carry):
                    off, n_next = carry
                    cand_off = off | jnp.left_shift(jnp.int32(1), REFINE_BITS - 1 - step)
                    cnt = count_from(base + cand_off)
                    ok = cnt >= topk
                    return jnp.where(ok, cand_off, off), jnp.where(ok, n_next, cnt)
                off, n_next = lax.fori_loop(0, REFINE_BITS, body, (no_keys, n_beyond))
                return base + off, n_next

            n_outside = jnp.sum(jnp.where(n_beyond >= topk, 1.0, 0.0))
            return lax.cond(n_outside > 0.0, high_full, refine)

        t_hi, n_gt_hi = lax.cond(n_kb >= FOLD_MIN_CHUNKS, high_from_fold, high_full)
        t_hi_tile = jnp.broadcast_to(t_hi.astype(_I16), (BF16_ROWS, qb))

        cap_scr[...] = jnp.full_like(cap_scr, _I16_MIN)

        def capture_block(c, carry):
            tops = [[jnp.full((BF16_ROWS, qb), _I16_MIN, _I16)] * 2 for _ in range(CAP_GROUPS)]
            for r in range(kb // BF16_ROWS):
                x = jnp.where(hi_scr[c, tile_rows(r), :] == t_hi_tile, lo_scr[c, tile_rows(r), :], i16_min)
                lo_scr[c, tile_rows(r), :] = x
                first, second = tops[r % CAP_GROUPS]
                above = x > first
                tops[r % CAP_GROUPS] = [jnp.where(above, x, first),
                                        jnp.where(above, first, jnp.where(x > second, x, second))]
            row0 = pl.multiple_of((c % CAP_CHUNKS_PER_BLOCK) * CAP_ROWS, CAP_ROWS)
            for g in range(CAP_GROUPS):
                for t in range(2):
                    cap_scr[c // CAP_CHUNKS_PER_BLOCK,
                            pl.ds(row0 + (2 * g + t) * BF16_ROWS, BF16_ROWS), :] = tops[g][t]
            return carry

        lax.fori_loop(0, n_kb, capture_block, 0)
        t_lo, _ = bisect(cap_scr, n_gt_hi, n_cap, 1)
        n_gt = n_gt_hi + count(lo_scr, t_lo, n_steps, 2, strict=True)
        n_wrong = jnp.sum(jnp.where(n_gt >= topk, 1.0, 0.0))
        t_lo, n_gt = lax.cond(n_wrong > 0.0,
                              lambda: bisect(lo_scr, n_gt_hi, n_steps, 2),
                              lambda: (t_lo, n_gt))
        t_hi_rows = jnp.broadcast_to(t_hi.astype(_I16), (kb, qb))
        t_lo = jnp.where((t_hi == _I16_MIN) & (t_lo == _I16_MIN), _I16_MIN + 1, t_lo)
        t_lo_rows = jnp.broadcast_to(t_lo.astype(_I16), (kb, qb))
        rem_rows = jnp.broadcast_to((topk - n_gt).astype(_I16), (kb, qb))

        def mask_block(c, seen):
            hi = hi_scr[c]
            lo = lo_scr[c]
            tie = (hi == t_hi_rows) & (lo == t_lo_rows)
            tie_count = jnp.where(tie, jnp.asarray(1, _BF16), jnp.asarray(0, _BF16))

            def tie_at(r):
                hit = (hi[r:r + 1].astype(_I32) == t_hi) & (lo[r:r + 1].astype(_I32) == t_lo)
                return jnp.where(hit, 1.0, 0.0)

            half = kb // 2
            before_top = seen + _dot(before_ref[...], tie_count[:half])
            seen_mid = before_top[half - 1:half] + tie_at(half - 1)
            before_bot = seen_mid + _dot(before_ref[...], tie_count[half:])
            ties_before = jnp.concatenate([before_top, before_bot], axis=0)
            allowed = ties_before.astype(_I32).astype(_I16) < rem_rows
            chosen = (hi > t_hi_rows) | (lo > t_lo_rows) | (tie & allowed)
            bias_scr[c] = jnp.where(chosen, jnp.asarray(0, bias_scr.dtype),
                                    jnp.asarray(_NEG, bias_scr.dtype))
            return before_bot[half - 1:half] + tie_at(kb - 1)

        seen = lax.fori_loop(0, n_steps - 1,
                             lambda c2, seen: mask_block(2 * c2 + 1, mask_block(2 * c2, seen)),
                             jnp.zeros((1, qb), _F32))
        seen = mask_block(last_pair, seen)

        @pl.when(has_second)
        def _mask_second():
            mask_block(last_pair + 1, seen)

        m_scr[...] = jnp.full_like(m_scr, _NEG)
        acc_scr[...] = jnp.zeros_like(acc_scr)

    def stage_b(h, s_read, bm_read, vt_read):
        m_old = m_scr[h]
        m_new = jnp.maximum(m_old, bm_read[h])
        pr = jnp.exp2(s_read[h] - m_new).astype(_BF16)
        acc_scr[h] = jnp.exp2(m_old - m_new) * acc_scr[h] + _dot(vt_read[0, 0, h], pr)
        m_scr[h] = m_new

    def phase(k_rows, chunk, s_write, bm_write, s_read, bm_read, vt_read):
        bias = bias_scr[chunk].astype(_F32)

        def stage_a(h):
            pair = slice((h // 2) * LANES, (h // 2 + 1) * LANES)
            s = _dot(k_ref[0, k_rows, pair], one_head(qt_ref[0, pair, :], h)) + bias
            s_write[h] = s
            bm_write[h] = jnp.max(s, axis=0, keepdims=True)

        stage_a(0)
        for h in range(ATT_HEADS):
            if h + 1 < ATT_HEADS:
                stage_a(h + 1)
            if s_read is not None:
                stage_b(h, s_read, bm_read, vt_read)

    @pl.when(j == 0)
    def _fill():
        phase(slice(0, kb), 0, s_even, bm_even, None, None, None)

    @pl.when((j > 0) & (j < n_steps))
    def _first():
        phase(slice(0, kb), 2 * j, s_even, bm_even, s_odd, bm_odd, vt_prev_ref)

    @pl.when(2 * j + 1 < n_kb)
    def _second():
        phase(slice(kb, 2 * kb), 2 * j + 1, s_odd, bm_odd, s_even, bm_even, vt_ref)

    @pl.when((j == n_steps) & has_second)
    def _drain_odd():
        for h in range(ATT_HEADS):
            stage_b(h, s_odd, bm_odd, vt_prev_ref)

    @pl.when((j == n_steps) & jnp.logical_not(has_second))
    def _drain_even():
        for h in range(ATT_HEADS):
            stage_b(h, s_even, bm_even, vt_prev_ref)

    @pl.when(j == n_steps)
    def _finish():
        for h in range(ATT_HEADS):
            a = acc_scr[h]
            o_ref[0, h * ATT_HEAD_DIM:(h + 1) * ATT_HEAD_DIM, :] = (
                a[:ATT_HEAD_DIM] / a[ATT_HEAD_DIM:ATT_HEAD_DIM + 1]).astype(o_ref.dtype)


def _dsa_attention(qt, k, vt, iqt, ik, iwt):
    bsz, seq, _ = k.shape
    qb, kb = DSA_QB, DSA_KB
    topk = min(TOPK_MAX, seq // 4)
    n_qb = seq // qb
    n_steps = lambda i: (i * qb + qb - 1) // (2 * kb) + 1
    pairs = [(i, j) for i in range(n_qb) for j in range(n_steps(i) + 1)]
    as_i32 = lambda vals: jnp.asarray(np.array(vals, np.int32))
    qi = as_i32([i for i, j in pairs])
    kj = as_i32([j for i, j in pairs])
    k_pair = as_i32([min(j, n_steps(i) - 1) for i, j in pairs])
    n_kb = lambda i: (i * qb + qb - 1) // kb + 1
    v_prev = as_i32([n_kb(i) - 1 if j == n_steps(i) else max(2 * j - 1, 0) for i, j in pairs])
    v_this = as_i32([min(2 * j, 2 * n_steps(i) - 1) for i, j in pairs])
    before =jnp.asarray(np.tril(np.ones((kb // 2, kb // 2), np.float32), -1), _BF16)
    q_map = lambda b, p, qi_r, *_: (b, 0, qi_r[p])
    vt_spec = lambda which: pl.BlockSpec(
        (1, 1, ATT_HEADS, V_ROWS, kb), lambda b, p, *refs: (b, refs[which][p], 0, 0, 0))
    grid_spec = pltpu.PrefetchScalarGridSpec(
        num_scalar_prefetch=5,
        grid=(bsz, len(pairs)),
        in_specs=[
            pl.BlockSpec((1, ATT_W, qb), q_map),
            pl.BlockSpec((1, IDX_W, qb), q_map),
            pl.BlockSpec((1, SUBLANES, qb), q_map),
            pl.BlockSpec((1, seq, LANES), lambda b, p, *_: (b, 0, 0)),
            pl.BlockSpec(before.shape, lambda b, p, *_: (0, 0)),
            pl.BlockSpec((1, 2 * kb, ATT_W), lambda b, p, *refs: (b, refs[2][p], 0)),
            vt_spec(3),
            vt_spec(4),
        ],
        out_specs=pl.BlockSpec((1, ATT_W, qb), q_map),
        scratch_shapes=[
            pltpu.VMEM((seq // kb, kb, qb), _I16),
            pltpu.VMEM((seq // kb, kb, qb), _I16),
            pltpu.VMEM((-(-(seq // kb) // CAP_CHUNKS_PER_BLOCK), kb, qb), _I16),
            pltpu.VMEM((-(-(seq // kb) // CAP_CHUNKS_PER_BLOCK), kb, qb), _I16),
            pltpu.VMEM((seq // kb, kb, qb), _BF16),
            pltpu.VMEM((ATT_HEADS, 1, qb), _F32),
            pltpu.VMEM((ATT_HEADS, V_ROWS, qb), _F32),
            pltpu.VMEM((ATT_HEADS, kb, qb), _F32),
            pltpu.VMEM((ATT_HEADS, kb, qb), _F32),
            pltpu.VMEM((ATT_HEADS, 1, qb), _F32),
            pltpu.VMEM((ATT_HEADS, 1, qb), _F32),
        ],
    )
    return pl.pallas_call(
        functools.partial(_dsa_kernel, topk=topk),
        grid_spec=grid_spec,
        out_shape=jax.ShapeDtypeStruct((bsz, ATT_W, seq), _BF16),
        compiler_params=pltpu.CompilerParams(
            dimension_semantics=("arbitrary", "arbitrary"), vmem_limit_bytes=VMEM_LIMIT_BYTES),
    )(qi, kj, k_pair, v_prev, v_this, qt, iqt, iwt, ik, before, k, vt, vt)


def _split3(a):
    hi = a.astype(_BF16)
    r1 = a - hi.astype(_F32)
    mid = r1.astype(_BF16)
    lo = (r1 - mid.astype(_F32)).astype(_BF16)
    return hi, mid, lo


def _hgrn_kernel(lbl_ref, g_ref, hq_ref, hf_ref, hi_ref, hg_ref, o_ref,
                 state_scr, kk_scr, b_scr, o_scr, *, layer):
    rows, ch, sb = HG_ROWS, HG_CHUNK, HG_SUB
    n_sub = ch // sb

    @pl.when(pl.program_id(1) == 0)
    def _reset():
        state_scr[...] = jnp.zeros_like(state_scr)

    lbl = lbl_ref[...]
    e = jnp.exp(lbl - jnp.max(lbl, axis=0, keepdims=True))
    lb = jnp.sum(e[:layer + 1], axis=0, keepdims=True) / jnp.sum(e, axis=0, keepdims=True)

    f = lb + (1.0 - lb) * jax.nn.sigmoid(hf_ref[0])
    kk_scr[...] = 1.0 - f
    logf = jnp.log(f)
    r_i = lax.broadcasted_iota(_I32, (ch, ch), 0)
    c_i = lax.broadcasted_iota(_I32, (ch, ch), 1)
    lower = jnp.where(c_i <= r_i, 1.0, 0.0).astype(_BF16)
    for c in range(rows // ch):
        parts = _split3(logf[c * ch:(c + 1) * ch])
        b_scr[c * ch:(c + 1) * ch, :] = sum(_dot(lower, part) for part in parts)

    t_idx = lax.broadcasted_iota(_I32, (sb, 1), 0)
    row_idx = lax.broadcasted_iota(_I32, (ch, 1), 0)

    def chunk(c, carry, bounded):
        r0 = pl.multiple_of(c * ch, ch)
        cs = pl.ds(r0, ch)
        b = b_scr[cs, :]
        kk = kk_scr[cs, :]
        qv = hq_ref[0, cs, :]
        vv = hi_ref[0, cs, :]
        b_last = b[ch - 1:ch]
        q_in = (qv * jnp.exp(b)).astype(_BF16)
        k_out = kk * jnp.exp(b_last - b)
        vb = vv.astype(_BF16)

        a_off = [[] for _ in range(HG_HEADS)]
        sbs = HG_SUB_BOUNDED if bounded else sb
        for s_i in range(ch // sbs):
            if s_i == 0 and not bounded:
                for h in range(HG_HEADS):
                    a_off[h].append(jnp.zeros((sbs, ch), _F32))
                continue
            ref_b = b[s_i * sbs - 1:s_i * sbs] if s_i else jnp.zeros_like(b_last)
            q_s = (qv[s_i * sbs:(s_i + 1) * sbs] * jnp.exp(b[s_i * sbs:(s_i + 1) * sbs] - ref_b))
            n_rows = (s_i + 1) * sbs if bounded else s_i * sbs
            k_s = jnp.where(row_idx < n_rows,
                            kk * jnp.exp(jnp.where(row_idx < n_rows, ref_b - b, 0.0)), 0.0)
            q_s = q_s.astype(_BF16)
            k_s = k_s.astype(_BF16)
            for h in range(HG_HEADS):
                hs = slice(h * HG_KDIM, (h + 1) * HG_KDIM)
                a_off[h].append(_nt_dot(q_s[:, hs], k_s[:, hs]))

        for h in range(HG_HEADS):
            hs = slice(h * HG_KDIM, (h + 1) * HG_KDIM)
            st = state_scr[h]
            o_h = _nt_dot(q_in[:, hs], st.astype(_BF16))
            a_h = jnp.concatenate(a_off[h], axis=0)
            if bounded:
                a_h = jnp.where(c_i <= r_i, a_h, 0.0)
            o_scr[cs, hs] = o_h + _dot(a_h.astype(_BF16), vb[:, hs])
            state_scr[h] = (st * jnp.exp(b_last[:, hs])
                            + _tn_dot(vb[:, hs], k_out[:, hs].astype(_BF16)))
        if bounded:
            return carry

        for s_i in range(n_sub):
            rs = pl.ds(r0 + s_i * sb, sb)
            q_s = hq_ref[0, rs, :]
            b_s = b_scr[rs, :]
            acc = o_scr[rs, :]
            for t in range(sb):
                one = pl.ds(r0 + s_i * sb + t, 1)
                w = q_s * jnp.exp(jnp.minimum(b_s - b_scr[one, :], 0.0)) * kk_scr[one, :]
                v_row = hi_ref[0, one, :]
                parts = []
                for h in range(HG_HEADS):
                    hs = slice(h * HG_KDIM, (h + 1) * HG_KDIM)
                    a = jnp.sum(w[:, hs], axis=1, keepdims=True)
                    parts.append(jnp.where(t_idx >= t, a, 0.0) * v_row[:, hs])
                acc = acc + jnp.concatenate(parts, axis=1)
            o_scr[rs, :] = acc
        return carry

    run = lambda bounded: lambda: lax.fori_loop(
        0, rows // ch, functools.partial(chunk, bounded=bounded), 0)
    lax.cond(jnp.min(logf) >= -HG_MAX_DECAY / HG_SUB_BOUNDED, run(True), run(False))

    o = o_scr[...]
    gate = hg_ref[0]
    gain = g_ref[...]
    for h in range(HG_HEADS):
        hs = slice(h * HG_KDIM, (h + 1) * HG_KDIM)
        oh = o[:, hs]
        oh = oh * lax.rsqrt(jnp.mean(oh * oh, axis=1, keepdims=True) + RMS_EPS) * gain[:, hs]
        gh = gate[:, hs]
        o_ref[0, :, hs] = (oh * (gh * jax.nn.sigmoid(gh))).astype(o_ref.dtype)


def _hgrn2(hq, hf, hi, hg, lb_logits, norm_g, layer):
    bsz, seq, _ = hq.shape
    rows = HG_ROWS
    blk = pl.BlockSpec((1, rows, HG_W), lambda b, t: (b, t, 0))
    lbl = lb_logits.reshape(lb_logits.shape[0], HG_W)
    gain = norm_g.reshape(1, HG_W)
    return pl.pallas_call(
        functools.partial(_hgrn_kernel, layer=layer),
        grid=(bsz, seq // rows),
        in_specs=[pl.BlockSpec(lbl.shape, lambda b, t: (0, 0)),
                  pl.BlockSpec(gain.shape, lambda b, t: (0, 0)),
                  blk, blk, blk, blk],
        out_specs=blk,
        out_shape=jax.ShapeDtypeStruct((bsz, seq, HG_W), _BF16),
        scratch_shapes=[
            pltpu.VMEM((HG_HEADS, HG_KDIM, HG_KDIM), _F32),
            pltpu.VMEM((rows, HG_W), _F32),
            pltpu.VMEM((rows, HG_W), _F32),
            pltpu.VMEM((rows, HG_W), _F32),
        ],
        compiler_params=pltpu.CompilerParams(
            dimension_semantics=("arbitrary", "arbitrary"), vmem_limit_bytes=VMEM_LIMIT_BYTES),
    )(lbl, gain, hq, hf, hi, hg)


def _layer_norm(y, g, b):
    mu = jnp.mean(y, axis=1, keepdims=True)
    yc = y - mu
    var = jnp.mean(yc * yc, axis=1, keepdims=True)
    return yc * lax.rsqrt(var + LN_EPS) * g + b


def _ffn_kernel(x_ref, att_ref, hgo_ref, wo_ref, g1_ref, b1_ref, wu_ref, wd_ref,
                g2_ref, b2_ref, o_ref, *, alpha):
    mix = (_tn_dot(att_ref[0], wo_ref[:ATT_W, :]) + _dot(hgo_ref[...], wo_ref[ATT_W:, :]))
    y1 = _layer_norm(alpha * x_ref[...] + mix, g1_ref[...], b1_ref[...])
    y1b = y1.astype(_BF16)
    h = jnp.zeros_like(y1)
    for c in range(wu_ref.shape[1] // FFN_COLS):
        cs = slice(c * FFN_COLS, (c + 1) * FFN_COLS)
        u = jnp.maximum(_dot(y1b, wu_ref[:, cs]), 0.0)
        h = h + _dot((u * u).astype(_BF16), wd_ref[cs, :])
    o_ref[...] = _layer_norm(alpha * y1 + h, g2_ref[...], b2_ref[...])


def _out_ffn(x2d, att_t, hgo2d, w_o, g1, b1, w_up, w_down, g2, b2, alpha):
    rows, d = x2d.shape
    seq = att_t.shape[2]
    tm = FFN_ROWS
    n_seq_blocks = seq // tm
    row_spec = lambda w: pl.BlockSpec((tm, w), lambda i: (i, 0))
    full_spec = lambda a: pl.BlockSpec(a.shape, lambda i: (0, 0))
    att_spec = pl.BlockSpec((1, ATT_W, tm), lambda i: (i // n_seq_blocks, 0, i % n_seq_blocks))
    vec = lambda a: a.reshape(1, d).astype(_F32)
    args = (x2d, att_t, hgo2d, w_o.astype(_BF16), vec(g1), vec(b1),
            w_up.astype(_BF16), w_down.astype(_BF16), vec(g2), vec(b2))
    in_specs = [row_spec(d), att_spec, row_spec(HG_W)] + [full_spec(a) for a in args[3:]]
    return pl.pallas_call(
        functools.partial(_ffn_kernel, alpha=alpha),
        grid=(rows // tm,),
        in_specs=in_specs,
        out_specs=row_spec(d),
        out_shape=jax.ShapeDtypeStruct((rows, d), _F32),
        compiler_params=pltpu.CompilerParams(
            dimension_semantics=("arbitrary",), vmem_limit_bytes=VMEM_LIMIT_BYTES),
    )(*args)


def kernel(x, w_in, w_o, lb_logits, hg_norm_g, ln1_g, ln1_b, w_up, w_down, ln2_g, ln2_b):
    bsz, seq, d = x.shape
    depth = w_in.shape[0]
    alpha = (2.0 * depth) ** 0.25
    x2d = x.reshape(bsz * seq, d)
    for l in range(depth):
        qt, k, vt, iqt, ik, iwt, hq, hf, hi, hg = _project(x2d, w_in[l], seq)
        r3 = lambda a: a.reshape(bsz, seq, a.shape[-1])
        att_t = _dsa_attention(qt, r3(k), vt, iqt, r3(ik), iwt)
        hgo = _hgrn2(r3(hq), r3(hf), r3(hi), r3(hg), lb_logits, hg_norm_g[l], l)
        x2d = _out_ffn(x2d, att_t, hgo.reshape(bsz * seq, HG_W),
                       w_o[l], ln1_g[l], ln1_b[l], w_up[l], w_down[l], ln2_g[l], ln2_b[l], alpha)
    return x2d.reshape(bsz, seq, d)
```

```python
import functools
import math

import numpy as np
import jax
import jax.numpy as jnp
from jax import lax
from jax.experimental import pallas as pl
from jax.experimental.pallas import tpu as pltpu

ATT_HEAD_DIM = 64
ATT_HEADS = 8
ATT_W = ATT_HEADS * ATT_HEAD_DIM
IDX_HEADS = 4
IDX_DIM = 64
IDX_W = IDX_HEADS * IDX_DIM
TOPK_MAX = 256
HG_KDIM = 128
HG_HEADS = 4
HG_W = HG_HEADS * HG_KDIM
ROPE_THETA = 10000.0
LN_EPS = 1e-5
RMS_EPS = 1e-6

LANES = 128
SUBLANES = 8
BF16_ROWS = 16
VMEM_LIMIT_BYTES = 56 * 1024 * 1024

PROJ_ROWS = 256
DSA_QB = 256
DSA_KB = 512
V_ROWS = ATT_HEAD_DIM + BF16_ROWS
COUNT_ACCS = 4
CAP_GROUPS = 2
CAP_ROWS = 2 * CAP_GROUPS * BF16_ROWS
CAP_CHUNKS_PER_BLOCK = DSA_KB // CAP_ROWS
FOLD_GROUPS = CAP_ROWS // BF16_ROWS
REFINE_BITS = 6
FOLD_MIN_CHUNKS = 14
FOLD_LONG_CHUNKS = 22
HG_ROWS = 256
HG_CHUNK = 64
HG_SUB = 8
HG_SUB_BOUNDED = 16
HG_MAX_DECAY = 80.0
FFN_ROWS = 512
FFN_COLS = 1024

_F32 = jnp.float32
_BF16 = jnp.bfloat16
_I32 = jnp.int32
_I16 = jnp.int16
_INT_MIN = -(2 ** 31)
_I16_MIN = -(2 ** 15)
_I16_MAX = 2 ** 15 - 1
_LOW_SIGN = 1 << 15
_NEG = -1e30


def _nt_dot(a, b):
    return lax.dot_general(a, b, (((1,), (1,)), ((), ())), preferred_element_type=_F32)


def _tn_dot(a, b):
    return lax.dot_general(a, b, (((0,), (0,)), ((), ())), preferred_element_type=_F32)


def _dot(a, b):
    return jnp.dot(a, b, preferred_element_type=_F32)


def _tree_sum(parts):
    while len(parts) > 1:
        parts = [parts[n] + parts[n + 1] for n in range(0, len(parts) - 1, 2)] + (
            [parts[-1]] if len(parts) % 2 else [])
    return parts[0]


def _rope_group(z, cos, sin_signed):
    lane = lax.broadcasted_iota(_I32, z.shape, 1)
    first_half = (lane % ATT_HEAD_DIM) < (ATT_HEAD_DIM // 2)
    upper = pltpu.roll(z, LANES - ATT_HEAD_DIM // 2, 1)
    lower = pltpu.roll(z, ATT_HEAD_DIM // 2, 1)
    return z * cos + jnp.where(first_half, upper, lower) * sin_signed


def _proj_kernel(x_ref, wa_ref, wh_ref, cos_ref, sin_ref,
                 qt_ref, k_ref, vt_ref, iqt_ref, ik_ref, iwt_ref,
                 hq_ref, hf_ref, hi_ref, hg_ref, *, q_scale, iw_scale):
    xb = x_ref[...].astype(_BF16)
    cos = cos_ref[...]
    sin = sin_ref[...]
    pa = _dot(xb, wa_ref[...])

    def roped(col0, width):
        return [_rope_group(pa[:, col0 + g * LANES: col0 + (g + 1) * LANES], cos, sin)
                for g in range(width // LANES)]

    for g, z in enumerate(roped(0, ATT_W)):
        qt_ref[0, g * LANES:(g + 1) * LANES, :] = (z * q_scale).T.astype(_BF16)
    for g, z in enumerate(roped(ATT_W, ATT_W)):
        k_ref[:, g * LANES:(g + 1) * LANES] = z.astype(_BF16)
    rows = x_ref.shape[0]
    ones_row = lax.broadcasted_iota(_I32, (V_ROWS - ATT_HEAD_DIM, rows), 0) == 0
    for g in range(ATT_W // LANES):
        vt = pa[:, 2 * ATT_W + g * LANES:2 * ATT_W + (g + 1) * LANES].T.astype(_BF16)
        for sub in range(LANES // ATT_HEAD_DIM):
            head = g * (LANES // ATT_HEAD_DIM) + sub
            vt_ref[0, 0, head, :ATT_HEAD_DIM, :] = vt[sub * ATT_HEAD_DIM:(sub + 1) * ATT_HEAD_DIM]
            vt_ref[0, 0, head, ATT_HEAD_DIM:, :] = jnp.where(ones_row, 1.0, 0.0).astype(_BF16)
    for g, z in enumerate(roped(3 * ATT_W, IDX_W)):
        iqt_ref[0, g * LANES:(g + 1) * LANES, :] = z.T.astype(_BF16)
    ik_ref[...] = roped(3 * ATT_W + IDX_W, LANES)[0].astype(_BF16)
    iwt_ref[0] = (pa[:, 3 * ATT_W + IDX_W + LANES:] * iw_scale).T[:SUBLANES]

    ph = _dot(xb, wh_ref[...])
    hq_ref[...] = ph[:, 0 * HG_W:1 * HG_W]
    hf_ref[...] = ph[:, 1 * HG_W:2 * HG_W]
    hi_ref[...] = ph[:, 2 * HG_W:3 * HG_W]
    hg_ref[...] = ph[:, 3 * HG_W:4 * HG_W]


def _rope_tables(seq):
    half = ATT_HEAD_DIM // 2
    inv = np.power(np.float64(ROPE_THETA), -np.arange(half, dtype=np.float64) / half)
    ang = np.arange(seq, dtype=np.float64)[:, None] * inv[None, :]
    cos = np.cos(ang)
    sin = np.sin(ang)
    cos_t = np.tile(np.concatenate([cos, cos], axis=1), (1, LANES // ATT_HEAD_DIM))
    sin_t = np.tile(np.concatenate([-sin, sin], axis=1), (1, LANES // ATT_HEAD_DIM))
    return jnp.asarray(cos_t, _F32), jnp.asarray(sin_t, _F32)


def _project(x2d, w_in, seq):
    rows, d = x2d.shape
    c = 3 * ATT_W + IDX_W
    w_ik = w_in[:, c:c + IDX_DIM]
    w_iw = w_in[:, c + IDX_DIM:c + IDX_DIM + IDX_HEADS]
    wa = jnp.concatenate([w_in[:, :c], w_ik, w_ik,
                          jnp.pad(w_iw, ((0, 0), (0, LANES - IDX_HEADS)))],
                         axis=1).astype(_BF16)
    wh = w_in[:, c + IDX_DIM + IDX_HEADS:].astype(_BF16)
    cos_t, sin_t = _rope_tables(seq)
    tm = PROJ_ROWS
    n_seq_blocks = seq // tm
    row_spec = lambda w: pl.BlockSpec((tm, w), lambda i: (i, 0))
    full_spec = lambda a: pl.BlockSpec(a.shape, lambda i: (0, 0))
    pos_spec = pl.BlockSpec((tm, LANES), lambda i: (i % n_seq_blocks, 0))
    bsz = rows // seq
    t_spec = lambda *feat: pl.BlockSpec(
        (1,) + feat + (tm,), lambda i: (i // n_seq_blocks,) + (0,) * len(feat) + (i % n_seq_blocks,))
    out_shapes = [
        jax.ShapeDtypeStruct((bsz, ATT_W, seq), _BF16),
        jax.ShapeDtypeStruct((rows, ATT_W), _BF16),
        jax.ShapeDtypeStruct((bsz, seq // DSA_KB, ATT_HEADS, V_ROWS, DSA_KB), _BF16),
        jax.ShapeDtypeStruct((bsz, IDX_W, seq), _BF16),
        jax.ShapeDtypeStruct((rows, LANES), _BF16),
        jax.ShapeDtypeStruct((bsz, SUBLANES, seq), _F32),
    ] + [jax.ShapeDtypeStruct((rows, HG_W), _F32)] * 4
    tiles_per_chunk = DSA_KB // tm
    vt_spec = pl.BlockSpec(
        (1, 1, ATT_HEADS, V_ROWS, tm),
        lambda i: (i // n_seq_blocks, (i % n_seq_blocks) // tiles_per_chunk, 0, 0, i % tiles_per_chunk))
    out_specs = [t_spec(ATT_W), row_spec(ATT_W), vt_spec, t_spec(IDX_W),
                 row_spec(LANES), t_spec(SUBLANES)] + [row_spec(HG_W)] * 4
    kern = functools.partial(_proj_kernel, q_scale=ATT_HEAD_DIM ** -0.5 * math.log2(math.e),
                             iw_scale=(IDX_HEADS ** -0.5) * (IDX_DIM ** -0.5))
    return pl.pallas_call(
        kern,
        grid=(rows // tm,),
        in_specs=[row_spec(d), full_spec(wa), full_spec(wh), pos_spec, pos_spec],
        out_specs=out_specs,
        out_shape=out_shapes,
        compiler_params=pltpu.CompilerParams(
            dimension_semantics=("arbitrary",), vmem_limit_bytes=VMEM_LIMIT_BYTES),
    )(x2d, wa, wh, cos_t, sin_t)


def _dsa_kernel(qi_ref, kj_ref, ka_ref, vp_ref, vc_ref,
                qt_ref, iqt_ref, iwt_ref, ik_ref, before_ref, k_ref, vt_prev_ref, vt_ref,
                o_ref,
                hi_scr, lo_scr, fold_scr, cap_scr, bias_scr, m_scr, acc_scr,
                s_even, s_odd, bm_even, bm_odd, *, topk):
    del ka_ref, vp_ref, vc_ref
    p = pl.program_id(1)
    i = qi_ref[p]
    j = kj_ref[p]
    qb, kb = DSA_QB, DSA_KB
    n_kb = (i * qb + qb - 1) // kb + 1
    n_steps = (i * qb + qb - 1) // (2 * kb) + 1
    last_pair = 2 * (n_steps - 1)
    has_second = n_kb == 2 * n_steps
    first_head = lax.broadcasted_iota(_I32, (LANES, qb), 0) < ATT_HEAD_DIM

    def one_head(pair_rows, h):
        keep = first_head if h % 2 == 0 else jnp.logical_not(first_head)
        return jnp.where(keep, pair_rows, jnp.zeros_like(pair_rows))

    @pl.when(j == 0)
    def _select():
        iwt = iwt_ref[0]
        qpos = i * qb + lax.broadcasted_iota(_I32, (kb, qb), 1)
        krow = lax.broadcasted_iota(_I32, (kb, qb), 0)
        iq_heads = [one_head(iqt_ref[0, (h // 2) * LANES:(h // 2 + 1) * LANES, :], h)
                    for h in range(IDX_HEADS)]

        def score_chunk(c, causal):
            row0 = pl.multiple_of(c * kb, kb)
            ikc = ik_ref[0, pl.ds(row0, kb), :]
            score = jnp.zeros((kb, qb), _F32)
            for h in range(IDX_HEADS):
                logits = _dot(ikc, iq_heads[h])
                score = score + iwt[h:h + 1, :] * jnp.maximum(logits, 0.0)
            bits = lax.bitcast_convert_type(score, _I32)
            key = bits ^ (((bits >> 31) & 0x7FFFFFFF) ^ _LOW_SIGN)
            if causal:
                key = jnp.where(row0 + krow <= qpos, key, _INT_MIN ^ _LOW_SIGN)
            hi = (key >> 16).astype(_I16)
            hi_scr[c] = hi
            lo_scr[c] = key.astype(_I16)
            n_tiles = kb // BF16_ROWS
            row0_fold = pl.multiple_of((c % CAP_CHUNKS_PER_BLOCK) * CAP_ROWS, CAP_ROWS)
            for g in range(FOLD_GROUPS):
                tiles = [hi[r * BF16_ROWS:(r + 1) * BF16_ROWS] for r in range(g, n_tiles, FOLD_GROUPS)]
                while len(tiles) > 1:
                    tiles = [jnp.where(tiles[n] > tiles[n + 1], tiles[n], tiles[n + 1])
                             for n in range(0, len(tiles), 2)]
                fold_scr[c // CAP_CHUNKS_PER_BLOCK, pl.ds(row0_fold + g * BF16_ROWS, BF16_ROWS), :] = tiles[0]

        def score_pair(c2, carry):
            score_chunk(2 * c2, False)
            score_chunk(2 * c2 + 1, False)
            return carry

        fold_scr[...] = jnp.full_like(fold_scr, _I16_MIN)
        lax.fori_loop(0, n_steps - 1, score_pair, 0)
        score_chunk(last_pair, True)

        @pl.when(has_second)
        def _score_second():
            score_chunk(last_pair + 1, True)

        @pl.when(jnp.logical_not(has_second))
        def _blank_second():
            hi_scr[last_pair + 1] = jnp.full((kb, qb), _I16_MIN, _I16)
            lo_scr[last_pair + 1] = jnp.full((kb, qb), _I16_MIN, _I16)

        tile_rows = lambda r: slice(r * BF16_ROWS, (r + 1) * BF16_ROWS)
        i16_min = jnp.int16(_I16_MIN)

        def count(ref, cand, n_trips, blocks_per_trip, strict=False):
            cand_rows = jnp.broadcast_to(cand.astype(_I16), (BF16_ROWS, qb))
            one, zero = jnp.int16(1), jnp.int16(0)

            def body(t, accs):
                accs = list(accs)
                for u in range(blocks_per_trip):
                    for r in range(kb // BF16_ROWS):
                        tile = ref[blocks_per_trip * t + u, tile_rows(r), :]
                        hit = tile > cand_rows if strict else tile >= cand_rows
                        accs[r % len(accs)] = accs[r % len(accs)] + jnp.where(hit, one, zero)
                return tuple(accs)

            accs = lax.fori_loop(0, n_trips, body, (jnp.zeros((BF16_ROWS, qb), _I16),) * COUNT_ACCS)
            return jnp.sum(_tree_sum(list(accs)).astype(_I32), axis=0, keepdims=True)

        def bisect(ref, n_above, n_trips, blocks_per_trip):
            def body(step, carry):
                val, n_next = carry
                bit = jnp.left_shift(jnp.int32(1), 15 - step)
                cand = jnp.where(step == 0, 0, val | bit)
                cnt = n_above + count(ref, cand, n_trips, blocks_per_trip)
                ok = cnt >= topk
                return jnp.where(ok, cand, val), jnp.where(ok, n_next, cnt)
            return lax.fori_loop(0, 16, body, (jnp.full((1, qb), _I16_MIN, _I32), n_above))

        n_cap = (n_kb + CAP_CHUNKS_PER_BLOCK - 1) // CAP_CHUNKS_PER_BLOCK
        no_keys = jnp.zeros((1, qb), _I32)

        def high_full():
            return bisect(hi_scr, no_keys, n_steps, 2)

        def high_from_fold():
            base, _ = bisect(fold_scr, no_keys, n_cap, 1)
            n_bits = jnp.where(n_kb >= FOLD_LONG_CHUNKS, REFINE_BITS - 1, REFINE_BITS)

            def count_from(cand):
                cnt = count(hi_scr, jnp.minimum(cand, _I16_MAX), n_steps, 2)
                return jnp.where(cand > _I16_MAX, 0, cnt)

            n_beyond = count_from(base + jnp.left_shift(jnp.int32(1), n_bits))

            def refine():
                def body(step, carry):
                    off, n_next = carry
                    cand_off = off | jnp.left_shift(jnp.int32(1), n_bits - 1 - step)
                    cnt = count_from(base + cand_off)
                    ok = cnt >= topk
                    return jnp.where(ok, cand_off, off), jnp.where(ok, n_next, cnt)
                off, n_next = lax.fori_loop(0, n_bits, body, (no_keys, n_beyond))
                return base + off, n_next

            n_outside = jnp.sum(jnp.where(n_beyond >= topk, 1.0, 0.0))
            return lax.cond(n_outside > 0.0, high_full, refine)

        t_hi, n_gt_hi = lax.cond(n_kb >= FOLD_MIN_CHUNKS, high_from_fold, high_full)
        t_hi_tile = jnp.broadcast_to(t_hi.astype(_I16), (BF16_ROWS, qb))

        cap_scr[...] = jnp.full_like(cap_scr, _I16_MIN)

        def capture_block(c, carry):
            tops = [[jnp.full((BF16_ROWS, qb), _I16_MIN, _I16)] * 2 for _ in range(CAP_GROUPS)]
            for r in range(kb // BF16_ROWS):
                x = jnp.where(hi_scr[c, tile_rows(r), :] == t_hi_tile, lo_scr[c, tile_rows(r), :], i16_min)
                lo_scr[c, tile_rows(r), :] = x
                first, second = tops[r % CAP_GROUPS]
                above = x > first
                tops[r % CAP_GROUPS] = [jnp.where(above, x, first),
                                        jnp.where(above, first, jnp.where(x > second, x, second))]
            row0 = pl.multiple_of((c % CAP_CHUNKS_PER_BLOCK) * CAP_ROWS, CAP_ROWS)
            for g in range(CAP_GROUPS):
                for t in range(2):
                    cap_scr[c // CAP_CHUNKS_PER_BLOCK,
                            pl.ds(row0 + (2 * g + t) * BF16_ROWS, BF16_ROWS), :] = tops[g][t]
            return carry

        lax.fori_loop(0, n_kb, capture_block, 0)
        t_lo, _ = bisect(cap_scr, n_gt_hi, n_cap, 1)
        n_gt = n_gt_hi + count(lo_scr, t_lo, n_steps, 2, strict=True)
        n_wrong = jnp.sum(jnp.where(n_gt >= topk, 1.0, 0.0))
        t_lo, n_gt = lax.cond(n_wrong > 0.0,
                              lambda: bisect(lo_scr, n_gt_hi, n_steps, 2),
                              lambda: (t_lo, n_gt))
        t_hi_rows = jnp.broadcast_to(t_hi.astype(_I16), (kb, qb))
        t_lo = jnp.where((t_hi == _I16_MIN) & (t_lo == _I16_MIN), _I16_MIN + 1, t_lo)
        t_lo_rows = jnp.broadcast_to(t_lo.astype(_I16), (kb, qb))
        rem_rows = jnp.broadcast_to((topk - n_gt).astype(_I16), (kb, qb))

        def mask_block(c, seen):
            hi = hi_scr[c]
            lo = lo_scr[c]
            tie = (hi == t_hi_rows) & (lo == t_lo_rows)
            tie_count = jnp.where(tie, jnp.asarray(1, _BF16), jnp.asarray(0, _BF16))

            def tie_at(r):
                hit = (hi[r:r + 1].astype(_I32) == t_hi) & (lo[r:r + 1].astype(_I32) == t_lo)
                return jnp.where(hit, 1.0, 0.0)

            half = kb // 2
            before_top = seen + _dot(before_ref[...], tie_count[:half])
            seen_mid = before_top[half - 1:half] + tie_at(half - 1)
            before_bot = seen_mid + _dot(before_ref[...], tie_count[half:])
            ties_before = jnp.concatenate([before_top, before_bot], axis=0)
            allowed = ties_before.astype(_I32).astype(_I16) < rem_rows
            chosen = (hi > t_hi_rows) | (lo > t_lo_rows) | (tie & allowed)
            bias_scr[c] = jnp.where(chosen, jnp.asarray(0, bias_scr.dtype),
                                    jnp.asarray(_NEG, bias_scr.dtype))
            return before_bot[half - 1:half] + tie_at(kb - 1)

        seen = lax.fori_loop(0, n_steps - 1,
                             lambda c2, seen: mask_block(2 * c2 + 1, mask_block(2 * c2, seen)),
                             jnp.zeros((1, qb), _F32))
        seen = mask_block(last_pair, seen)

        @pl.when(has_second)
        def _mask_second():
            mask_block(last_pair + 1, seen)

        m_scr[...] = jnp.full_like(m_scr, _NEG)
        acc_scr[...] = jnp.zeros_like(acc_scr)

    def stage_b(h, s_read, bm_read, vt_read):
        m_old = m_scr[h]
        m_new = jnp.maximum(m_old, bm_read[h])
        pr = jnp.exp2(s_read[h] - m_new).astype(_BF16)
        acc_scr[h] = jnp.exp2(m_old - m_new) * acc_scr[h] + _dot(vt_read[0, 0, h], pr)
        m_scr[h] = m_new

    def phase(k_rows, chunk, s_write, bm_write, s_read, bm_read, vt_read):
        bias = bias_scr[chunk].astype(_F32)

        def stage_a(h):
            pair = slice((h // 2) * LANES, (h // 2 + 1) * LANES)
            s = _dot(k_ref[0, k_rows, pair], one_head(qt_ref[0, pair, :], h)) + bias
            s_write[h] = s
            bm_write[h] = jnp.max(s, axis=0, keepdims=True)

        stage_a(0)
        for h in range(ATT_HEADS):
            if h + 1 < ATT_HEADS:
                stage_a(h + 1)
            if s_read is not None:
                stage_b(h, s_read, bm_read, vt_read)

    @pl.when(j == 0)
    def _fill():
        phase(slice(0, kb), 0, s_even, bm_even, None, None, None)

    @pl.when((j > 0) & (j < n_steps))
    def _first():
        phase(slice(0, kb), 2 * j, s_even, bm_even, s_odd, bm_odd, vt_prev_ref)

    @pl.when(2 * j + 1 < n_kb)
    def _second():
        phase(slice(kb, 2 * kb), 2 * j + 1, s_odd, bm_odd, s_even, bm_even, vt_ref)

    @pl.when((j == n_steps) & has_second)
    def _drain_odd():
        for h in range(ATT_HEADS):
            stage_b(h, s_odd, bm_odd, vt_prev_ref)

    @pl.when((j == n_steps) & jnp.logical_not(has_second))
    def _drain_even():
        for h in range(ATT_HEADS):
            stage_b(h, s_even, bm_even, vt_prev_ref)

    @pl.when(j == n_steps)
    def _finish():
        for h in range(ATT_HEADS):
            a = acc_scr[h]
            o_ref[0, h * ATT_HEAD_DIM:(h + 1) * ATT_HEAD_DIM, :] = (
                a[:ATT_HEAD_DIM] / a[ATT_HEAD_DIM:ATT_HEAD_DIM + 1]).astype(o_ref.dtype)


def _dsa_attention(qt, k, vt, iqt, ik, iwt):
    bsz, seq, _ = k.shape
    qb, kb = DSA_QB, DSA_KB
    topk = min(TOPK_MAX, seq // 4)
    n_qb = seq // qb
    n_steps = lambda i: (i * qb + qb - 1) // (2 * kb) + 1
    pairs = [(i, j) for i in range(n_qb) for j in range(n_steps(i) + 1)]
    as_i32 = lambda vals: jnp.asarray(np.array(vals, np.int32))
    qi = as_i32([i for i, j in pairs])
    kj = as_i32([j for i, j in pairs])
    k_pair = as_i32([min(j, n_steps(i) - 1) for i, j in pairs])
    n_kb = lambda i: (i * qb + qb - 1) // kb + 1
    v_prev = as_i32([n_kb(i) - 1 if j == n_steps(i) else max(2 * j - 1, 0) for i, j in pairs])
    v_this = as_i32([min(2 * j, 2 * n_steps(i) - 1) for i, j in pairs])
    before =jnp.asarray(np.tril(np.ones((kb // 2, kb // 2), np.float32), -1), _BF16)
    q_map = lambda b, p, qi_r, *_: (b, 0, qi_r[p])
    vt_spec = lambda which: pl.BlockSpec(
        (1, 1, ATT_HEADS, V_ROWS, kb), lambda b, p, *refs: (b, refs[which][p], 0, 0, 0))
    grid_spec = pltpu.PrefetchScalarGridSpec(
        num_scalar_prefetch=5,
        grid=(bsz, len(pairs)),
        in_specs=[
            pl.BlockSpec((1, ATT_W, qb), q_map),
            pl.BlockSpec((1, IDX_W, qb), q_map),
            pl.BlockSpec((1, SUBLANES, qb), q_map),
            pl.BlockSpec((1, seq, LANES), lambda b, p, *_: (b, 0, 0)),
            pl.BlockSpec(before.shape, lambda b, p, *_: (0, 0)),
            pl.BlockSpec((1, 2 * kb, ATT_W), lambda b, p, *refs: (b, refs[2][p], 0)),
            vt_spec(3),
            vt_spec(4),
        ],
        out_specs=pl.BlockSpec((1, ATT_W, qb), q_map),
        scratch_shapes=[
            pltpu.VMEM((seq // kb, kb, qb), _I16),
            pltpu.VMEM((seq // kb, kb, qb), _I16),
            pltpu.VMEM((-(-(seq // kb) // CAP_CHUNKS_PER_BLOCK), kb, qb), _I16),
            pltpu.VMEM((-(-(seq // kb) // CAP_CHUNKS_PER_BLOCK), kb, qb), _I16),
            pltpu.VMEM((seq // kb, kb, qb), _BF16),
            pltpu.VMEM((ATT_HEADS, 1, qb), _F32),
            pltpu.VMEM((ATT_HEADS, V_ROWS, qb), _F32),
            pltpu.VMEM((ATT_HEADS, kb, qb), _F32),
            pltpu.VMEM((ATT_HEADS, kb, qb), _F32),
            pltpu.VMEM((ATT_HEADS, 1, qb), _F32),
            pltpu.VMEM((ATT_HEADS, 1, qb), _F32),
        ],
    )
    return pl.pallas_call(
        functools.partial(_dsa_kernel, topk=topk),
        grid_spec=grid_spec,
        out_shape=jax.ShapeDtypeStruct((bsz, ATT_W, seq), _BF16),
        compiler_params=pltpu.CompilerParams(
            dimension_semantics=("arbitrary", "arbitrary"), vmem_limit_bytes=VMEM_LIMIT_BYTES),
    )(qi, kj, k_pair, v_prev, v_this, qt, iqt, iwt, ik, before, k, vt, vt)


def _split3(a):
    hi = a.astype(_BF16)
    r1 = a - hi.astype(_F32)
    mid = r1.astype(_BF16)
    lo = (r1 - mid.astype(_F32)).astype(_BF16)
    return hi, mid, lo


def _hgrn_kernel(lbl_ref, g_ref, hq_ref, hf_ref, hi_ref, hg_ref, o_ref,
                 state_scr, kk_scr, b_scr, o_scr, *, layer):
    rows, ch, sb = HG_ROWS, HG_CHUNK, HG_SUB
    n_sub = ch // sb

    @pl.when(pl.program_id(1) == 0)
    def _reset():
        state_scr[...] = jnp.zeros_like(state_scr)

    lbl = lbl_ref[...]
    e = jnp.exp(lbl - jnp.max(lbl, axis=0, keepdims=True))
    lb = jnp.sum(e[:layer + 1], axis=0, keepdims=True) / jnp.sum(e, axis=0, keepdims=True)

    f = lb + (1.0 - lb) * jax.nn.sigmoid(hf_ref[0])
    kk_scr[...] = 1.0 - f
    logf = jnp.log(f)
    r_i = lax.broadcasted_iota(_I32, (ch, ch), 0)
    c_i = lax.broadcasted_iota(_I32, (ch, ch), 1)
    lower = jnp.where(c_i <= r_i, 1.0, 0.0).astype(_BF16)
    for c in range(rows // ch):
        parts = _split3(logf[c * ch:(c + 1) * ch])
        b_scr[c * ch:(c + 1) * ch, :] = sum(_dot(lower, part) for part in parts)

    t_idx = lax.broadcasted_iota(_I32, (sb, 1), 0)
    row_idx = lax.broadcasted_iota(_I32, (ch, 1), 0)

    def chunk(c, carry, bounded):
        r0 = pl.multiple_of(c * ch, ch)
        cs = pl.ds(r0, ch)
        b = b_scr[cs, :]
        kk = kk_scr[cs, :]
        qv = hq_ref[0, cs, :]
        vv = hi_ref[0, cs, :]
        b_last = b[ch - 1:ch]
        q_in = (qv * jnp.exp(b)).astype(_BF16)
        k_out = kk * jnp.exp(b_last - b)
        vb = vv.astype(_BF16)

        a_off = [[] for _ in range(HG_HEADS)]
        sbs = HG_SUB_BOUNDED if bounded else sb
        for s_i in range(ch // sbs):
            if s_i == 0 and not bounded:
                for h in range(HG_HEADS):
                    a_off[h].append(jnp.zeros((sbs, ch), _F32))
                continue
            ref_b = b[s_i * sbs - 1:s_i * sbs] if s_i else jnp.zeros_like(b_last)
            q_s = (qv[s_i * sbs:(s_i + 1) * sbs] * jnp.exp(b[s_i * sbs:(s_i + 1) * sbs] - ref_b))
            n_rows = (s_i + 1) * sbs if bounded else s_i * sbs
            k_s = jnp.where(row_idx < n_rows,
                            kk * jnp.exp(jnp.where(row_idx < n_rows, ref_b - b, 0.0)), 0.0)
            q_s = q_s.astype(_BF16)
            k_s = k_s.astype(_BF16)
            for h in range(HG_HEADS):
                hs = slice(h * HG_KDIM, (h + 1) * HG_KDIM)
                a_off[h].append(_nt_dot(q_s[:, hs], k_s[:, hs]))

        for h in range(HG_HEADS):
            hs = slice(h * HG_KDIM, (h + 1) * HG_KDIM)
            st = state_scr[h]
            o_h = _nt_dot(q_in[:, hs], st.astype(_BF16))
            a_h = jnp.concatenate(a_off[h], axis=0)
            if bounded:
                a_h = jnp.where(c_i <= r_i, a_h, 0.0)
            o_scr[cs, hs] = o_h + _dot(a_h.astype(_BF16), vb[:, hs])
            state_scr[h] = (st * jnp.exp(b_last[:, hs])
                            + _tn_dot(vb[:, hs], k_out[:, hs].astype(_BF16)))
        if bounded:
            return carry

        for s_i in range(n_sub):
            rs = pl.ds(r0 + s_i * sb, sb)
            q_s = hq_ref[0, rs, :]
            b_s = b_scr[rs, :]
            acc = o_scr[rs, :]
            for t in range(sb):
                one = pl.ds(r0 + s_i * sb + t, 1)
                w = q_s * jnp.exp(jnp.minimum(b_s - b_scr[one, :], 0.0)) * kk_scr[one, :]
                v_row = hi_ref[0, one, :]
                parts = []
                for h in range(HG_HEADS):
                    hs = slice(h * HG_KDIM, (h + 1) * HG_KDIM)
                    a = jnp.sum(w[:, hs], axis=1, keepdims=True)
                    parts.append(jnp.where(t_idx >= t, a, 0.0) * v_row[:, hs])
                acc = acc + jnp.concatenate(parts, axis=1)
            o_scr[rs, :] = acc
        return carry

    run = lambda bounded: lambda: lax.fori_loop(
        0, rows // ch, functools.partial(chunk, bounded=bounded), 0)
    lax.cond(jnp.min(logf) >= -HG_MAX_DECAY / HG_SUB_BOUNDED, run(True), run(False))

    o = o_scr[...]
    gate = hg_ref[0]
    gain = g_ref[...]
    for h in range(HG_HEADS):
        hs = slice(h * HG_KDIM, (h + 1) * HG_KDIM)
        oh = o[:, hs]
        oh = oh * lax.rsqrt(jnp.mean(oh * oh, axis=1, keepdims=True) + RMS_EPS) * gain[:, hs]
        gh = gate[:, hs]
        o_ref[0, :, hs] = (oh * (gh * jax.nn.sigmoid(gh))).astype(o_ref.dtype)


def _hgrn2(hq, hf, hi, hg, lb_logits, norm_g, layer):
    bsz, seq, _ = hq.shape
    rows = HG_ROWS
    blk = pl.BlockSpec((1, rows, HG_W), lambda b, t: (b, t, 0))
    lbl = lb_logits.reshape(lb_logits.shape[0], HG_W)
    gain = norm_g.reshape(1, HG_W)
    return pl.pallas_call(
        functools.partial(_hgrn_kernel, layer=layer),
        grid=(bsz, seq // rows),
        in_specs=[pl.BlockSpec(lbl.shape, lambda b, t: (0, 0)),
                  pl.BlockSpec(gain.shape, lambda b, t: (0, 0)),
                  blk, blk, blk, blk],
        out_specs=blk,
        out_shape=jax.ShapeDtypeStruct((bsz, seq, HG_W), _BF16),
        scratch_shapes=[
            pltpu.VMEM((HG_HEADS, HG_KDIM, HG_KDIM), _F32),
            pltpu.VMEM((rows, HG_W), _F32),
            pltpu.VMEM((rows, HG_W), _F32),
            pltpu.VMEM((rows, HG_W), _F32),
        ],
        compiler_params=pltpu.CompilerParams(
            dimension_semantics=("arbitrary", "arbitrary"), vmem_limit_bytes=VMEM_LIMIT_BYTES),
    )(lbl, gain, hq, hf, hi, hg)


def _layer_norm(y, g, b):
    mu = jnp.mean(y, axis=1, keepdims=True)
    yc = y - mu
    var = jnp.mean(yc * yc, axis=1, keepdims=True)
    return yc * lax.rsqrt(var + LN_EPS) * g + b


def _ffn_kernel(x_ref, att_ref, hgo_ref, wo_ref, g1_ref, b1_ref, wu_ref, wd_ref,
                g2_ref, b2_ref, o_ref, *, alpha):
    mix = (_tn_dot(att_ref[0], wo_ref[:ATT_W, :]) + _dot(hgo_ref[...], wo_ref[ATT_W:, :]))
    y1 = _layer_norm(alpha * x_ref[...] + mix, g1_ref[...], b1_ref[...])
    y1b = y1.astype(_BF16)
    h = jnp.zeros_like(y1)
    for c in range(wu_ref.shape[1] // FFN_COLS):
        cs = slice(c * FFN_COLS, (c + 1) * FFN_COLS)
        u = jnp.maximum(_dot(y1b, wu_ref[:, cs]), 0.0)
        h = h + _dot((u * u).astype(_BF16), wd_ref[cs, :])
    o_ref[...] = _layer_norm(alpha * y1 + h, g2_ref[...], b2_ref[...])


def _out_ffn(x2d, att_t, hgo2d, w_o, g1, b1, w_up, w_down, g2, b2, alpha):
    rows, d = x2d.shape
    seq = att_t.shape[2]
    tm = FFN_ROWS
    n_seq_blocks = seq // tm
    row_spec = lambda w: pl.BlockSpec((tm, w), lambda i: (i, 0))
    full_spec = lambda a: pl.BlockSpec(a.shape, lambda i: (0, 0), pipeline_mode=pl.Buffered(1))
    att_spec = pl.BlockSpec((1, ATT_W, tm), lambda i: (i // n_seq_blocks, 0, i % n_seq_blocks))
    vec = lambda a: a.reshape(1, d).astype(_F32)
    args = (x2d, att_t, hgo2d, w_o.astype(_BF16), vec(g1), vec(b1),
            w_up.astype(_BF16), w_down.astype(_BF16), vec(g2), vec(b2))
    in_specs = [row_spec(d), att_spec, row_spec(HG_W)] + [full_spec(a) for a in args[3:]]
    return pl.pallas_call(
        functools.partial(_ffn_kernel, alpha=alpha),
        grid=(rows // tm,),
        in_specs=in_specs,
        out_specs=row_spec(d),
        out_shape=jax.ShapeDtypeStruct((rows, d), _F32),
        compiler_params=pltpu.CompilerParams(
            dimension_semantics=("arbitrary",), vmem_limit_bytes=VMEM_LIMIT_BYTES),
    )(*args)


def kernel(x, w_in, w_o, lb_logits, hg_norm_g, ln1_g, ln1_b, w_up, w_down, ln2_g, ln2_b):
    bsz, seq, d = x.shape
    depth = w_in.shape[0]
    alpha = (2.0 * depth) ** 0.25
    x2d = x.reshape(bsz * seq, d)
    for l in range(depth):
        qt, k, vt, iqt, ik, iwt, hq, hf, hi, hg = _project(x2d, w_in[l], seq)
        r3 = lambda a: a.reshape(bsz, seq, a.shape[-1])
        att_t = _dsa_attention(qt, r3(k), vt, iqt, r3(ik), iwt)
        hgo = _hgrn2(r3(hq), r3(hf), r3(hi), r3(hg), lb_logits, hg_norm_g[l], l)
        x2d = _out_ffn(x2d, att_t, hgo.reshape(bsz * seq, HG_W),
                       w_o[l], ln1_g[l], ln1_b[l], w_up[l], w_down[l], ln2_g[l], ln2_b[l], alpha)
    return x2d.reshape(bsz, seq, d)
```

```python
import functools
import math

import numpy as np
import jax
import jax.numpy as jnp
from jax import lax
from jax.experimental import pallas as pl
from jax.experimental.pallas import tpu as pltpu

ATT_HEAD_DIM = 64
ATT_HEADS = 8
ATT_W = ATT_HEADS * ATT_HEAD_DIM
IDX_HEADS = 4
IDX_DIM = 64
IDX_W = IDX_HEADS * IDX_DIM
TOPK_MAX = 256
HG_KDIM = 128
HG_HEADS = 4
HG_W = HG_HEADS * HG_KDIM
ROPE_THETA = 10000.0
LN_EPS = 1e-5
RMS_EPS = 1e-6

LANES = 128
SUBLANES = 8
BF16_ROWS = 16
VMEM_LIMIT_BYTES = 56 * 1024 * 1024

PROJ_ROWS = 512
DSA_QB = 256
DSA_KB = 512
V_ROWS = ATT_HEAD_DIM + BF16_ROWS
COUNT_ACCS = 4
CAP_GROUPS = 2
CAP_ROWS = 2 * CAP_GROUPS * BF16_ROWS
CAP_CHUNKS_PER_BLOCK = DSA_KB // CAP_ROWS
FOLD_GROUPS = CAP_ROWS // BF16_ROWS
REFINE_BITS = 6
FOLD_MIN_CHUNKS = 14
FOLD_LONG_CHUNKS = 22
HG_ROWS = 256
HG_CHUNK = 64
HG_SUB = 8
HG_SUB_BOUNDED = 16
HG_MAX_DECAY = 80.0
FFN_ROWS = 512
FFN_COLS = 1024

_F32 = jnp.float32
_BF16 = jnp.bfloat16
_I32 = jnp.int32
_I16 = jnp.int16
_INT_MIN = -(2 ** 31)
_I16_MIN = -(2 ** 15)
_I16_MAX = 2 ** 15 - 1
_LOW_SIGN = 1 << 15
_NEG = -1e30


def _nt_dot(a, b):
    return lax.dot_general(a, b, (((1,), (1,)), ((), ())), preferred_element_type=_F32)


def _tn_dot(a, b):
    return lax.dot_general(a, b, (((0,), (0,)), ((), ())), preferred_element_type=_F32)


def _dot(a, b):
    return jnp.dot(a, b, preferred_element_type=_F32)


def _tree_sum(parts):
    while len(parts) > 1:
        parts = [parts[n] + parts[n + 1] for n in range(0, len(parts) - 1, 2)] + (
            [parts[-1]] if len(parts) % 2 else [])
    return parts[0]


def _rope_group(z, cos, sin_signed):
    lane = lax.broadcasted_iota(_I32, z.shape, 1)
    first_half = (lane % ATT_HEAD_DIM) < (ATT_HEAD_DIM // 2)
    upper = pltpu.roll(z, LANES - ATT_HEAD_DIM // 2, 1)
    lower = pltpu.roll(z, ATT_HEAD_DIM // 2, 1)
    return z * cos + jnp.where(first_half, upper, lower) * sin_signed


def _proj_kernel(x_ref, wa_ref, wh_ref, cos_ref, sin_ref,
                 qt_ref, k_ref, vt_ref, iqt_ref, ik_ref, iwt_ref,
                 hq_ref, hf_ref, hi_ref, hg_ref, *, q_scale, iw_scale):
    xb = x_ref[...].astype(_BF16)
    cos = cos_ref[...]
    sin = sin_ref[...]
    pa = _dot(xb, wa_ref[...])

    def roped(col0, width):
        return [_rope_group(pa[:, col0 + g * LANES: col0 + (g + 1) * LANES], cos, sin)
                for g in range(width // LANES)]

    for g, z in enumerate(roped(0, ATT_W)):
        qt_ref[0, g * LANES:(g + 1) * LANES, :] = (z * q_scale).T.astype(_BF16)
    for g, z in enumerate(roped(ATT_W, ATT_W)):
        k_ref[:, g * LANES:(g + 1) * LANES] = z.astype(_BF16)
    rows = x_ref.shape[0]
    ones_row = lax.broadcasted_iota(_I32, (V_ROWS - ATT_HEAD_DIM, rows), 0) == 0
    for g in range(ATT_W // LANES):
        vt = pa[:, 2 * ATT_W + g * LANES:2 * ATT_W + (g + 1) * LANES].T.astype(_BF16)
        for sub in range(LANES // ATT_HEAD_DIM):
            head = g * (LANES // ATT_HEAD_DIM) + sub
            vt_ref[0, 0, head, :ATT_HEAD_DIM, :] = vt[sub * ATT_HEAD_DIM:(sub + 1) * ATT_HEAD_DIM]
            vt_ref[0, 0, head, ATT_HEAD_DIM:, :] = jnp.where(ones_row, 1.0, 0.0).astype(_BF16)
    for g, z in enumerate(roped(3 * ATT_W, IDX_W)):
        iqt_ref[0, g * LANES:(g + 1) * LANES, :] = z.T.astype(_BF16)
    ik_ref[...] = roped(3 * ATT_W + IDX_W, LANES)[0].astype(_BF16)
    iwt_ref[0] = (pa[:, 3 * ATT_W + IDX_W + LANES:] * iw_scale).T[:SUBLANES]

    ph = _dot(xb, wh_ref[...])
    hq_ref[...] = ph[:, 0 * HG_W:1 * HG_W]
    hf_ref[...] = ph[:, 1 * HG_W:2 * HG_W]
    hi_ref[...] = ph[:, 2 * HG_W:3 * HG_W]
    hg_ref[...] = ph[:, 3 * HG_W:4 * HG_W]


def _rope_tables(seq):
    half = ATT_HEAD_DIM // 2
    inv = np.power(np.float64(ROPE_THETA), -np.arange(half, dtype=np.float64) / half)
    ang = np.arange(seq, dtype=np.float64)[:, None] * inv[None, :]
    cos = np.cos(ang)
    sin = np.sin(ang)
    cos_t = np.tile(np.concatenate([cos, cos], axis=1), (1, LANES // ATT_HEAD_DIM))
    sin_t = np.tile(np.concatenate([-sin, sin], axis=1), (1, LANES // ATT_HEAD_DIM))
    return jnp.asarray(cos_t, _F32), jnp.asarray(sin_t, _F32)


def _project(x2d, w_in, seq):
    rows, d = x2d.shape
    c = 3 * ATT_W + IDX_W
    w_ik = w_in[:, c:c + IDX_DIM]
    w_iw = w_in[:, c + IDX_DIM:c + IDX_DIM + IDX_HEADS]
    wa = jnp.concatenate([w_in[:, :c], w_ik, w_ik,
                          jnp.pad(w_iw, ((0, 0), (0, LANES - IDX_HEADS)))],
                         axis=1).astype(_BF16)
    wh = w_in[:, c + IDX_DIM + IDX_HEADS:].astype(_BF16)
    cos_t, sin_t = _rope_tables(seq)
    tm = PROJ_ROWS
    n_seq_blocks = seq // tm
    row_spec = lambda w: pl.BlockSpec((tm, w), lambda i: (i, 0))
    full_spec = lambda a: pl.BlockSpec(a.shape, lambda i: (0, 0), pipeline_mode=pl.Buffered(1))
    pos_spec = pl.BlockSpec((tm, LANES), lambda i: (i % n_seq_blocks, 0))
    bsz = rows // seq
    t_spec = lambda *feat: pl.BlockSpec(
        (1,) + feat + (tm,), lambda i: (i // n_seq_blocks,) + (0,) * len(feat) + (i % n_seq_blocks,))
    out_shapes = [
        jax.ShapeDtypeStruct((bsz, ATT_W, seq), _BF16),
        jax.ShapeDtypeStruct((rows, ATT_W), _BF16),
        jax.ShapeDtypeStruct((bsz, seq // DSA_KB, ATT_HEADS, V_ROWS, DSA_KB), _BF16),
        jax.ShapeDtypeStruct((bsz, IDX_W, seq), _BF16),
        jax.ShapeDtypeStruct((rows, LANES), _BF16),
        jax.ShapeDtypeStruct((bsz, SUBLANES, seq), _F32),
    ] + [jax.ShapeDtypeStruct((rows, HG_W), _F32)] * 4
    tiles_per_chunk = DSA_KB // tm
    vt_spec = pl.BlockSpec(
        (1, 1, ATT_HEADS, V_ROWS, tm),
        lambda i: (i // n_seq_blocks, (i % n_seq_blocks) // tiles_per_chunk, 0, 0, i % tiles_per_chunk))
    out_specs = [t_spec(ATT_W), row_spec(ATT_W), vt_spec, t_spec(IDX_W),
                 row_spec(LANES), t_spec(SUBLANES)] + [row_spec(HG_W)] * 4
    kern = functools.partial(_proj_kernel, q_scale=ATT_HEAD_DIM ** -0.5 * math.log2(math.e),
                             iw_scale=(IDX_HEADS ** -0.5) * (IDX_DIM ** -0.5))
    return pl.pallas_call(
        kern,
        grid=(rows // tm,),
        in_specs=[row_spec(d), full_spec(wa), full_spec(wh), pos_spec, pos_spec],
        out_specs=out_specs,
        out_shape=out_shapes,
        compiler_params=pltpu.CompilerParams(
            dimension_semantics=("arbitrary",), vmem_limit_bytes=VMEM_LIMIT_BYTES),
    )(x2d, wa, wh, cos_t, sin_t)


def _dsa_kernel(qi_ref, kj_ref, ka_ref, vp_ref, vc_ref,
                qt_ref, iqt_ref, iwt_ref, ik_ref, before_ref, k_ref, vt_prev_ref, vt_ref,
                o_ref,
                hi_scr, lo_scr, fold_scr, cap_scr, bias_scr, m_scr, acc_scr,
                s_even, s_odd, bm_even, bm_odd, *, topk):
    del ka_ref, vp_ref, vc_ref
    p = pl.program_id(1)
    i = qi_ref[p]
    j = kj_ref[p]
    qb, kb = DSA_QB, DSA_KB
    n_kb = (i * qb + qb - 1) // kb + 1
    n_steps = (i * qb + qb - 1) // (2 * kb) + 1
    last_pair = 2 * (n_steps - 1)
    has_second = n_kb == 2 * n_steps
    first_head = lax.broadcasted_iota(_I32, (LANES, qb), 0) < ATT_HEAD_DIM

    def one_head(pair_rows, h):
        keep = first_head if h % 2 == 0 else jnp.logical_not(first_head)
        return jnp.where(keep, pair_rows, jnp.zeros_like(pair_rows))

    @pl.when(j == 0)
    def _select():
        iwt = iwt_ref[0]
        qpos = i * qb + lax.broadcasted_iota(_I32, (kb, qb), 1)
        krow = lax.broadcasted_iota(_I32, (kb, qb), 0)
        iq_heads = [one_head(iqt_ref[0, (h // 2) * LANES:(h // 2 + 1) * LANES, :], h)
                    for h in range(IDX_HEADS)]

        def score_chunk(c, causal):
            row0 = pl.multiple_of(c * kb, kb)
            ikc = ik_ref[0, pl.ds(row0, kb), :]
            score = jnp.zeros((kb, qb), _F32)
            for h in range(IDX_HEADS):
                logits = _dot(ikc, iq_heads[h])
                score = score + iwt[h:h + 1, :] * jnp.maximum(logits, 0.0)
            bits = lax.bitcast_convert_type(score, _I32)
            key = bits ^ (((bits >> 31) & 0x7FFFFFFF) ^ _LOW_SIGN)
            if causal:
                key = jnp.where(row0 + krow <= qpos, key, _INT_MIN ^ _LOW_SIGN)
            hi = (key >> 16).astype(_I16)
            hi_scr[c] = hi
            lo_scr[c] = key.astype(_I16)
            n_tiles = kb // BF16_ROWS
            row0_fold = pl.multiple_of((c % CAP_CHUNKS_PER_BLOCK) * CAP_ROWS, CAP_ROWS)
            for g in range(FOLD_GROUPS):
                tiles = [hi[r * BF16_ROWS:(r + 1) * BF16_ROWS] for r in range(g, n_tiles, FOLD_GROUPS)]
                while len(tiles) > 1:
                    tiles = [jnp.where(tiles[n] > tiles[n + 1], tiles[n], tiles[n + 1])
                             for n in range(0, len(tiles), 2)]
                fold_scr[c // CAP_CHUNKS_PER_BLOCK, pl.ds(row0_fold + g * BF16_ROWS, BF16_ROWS), :] = tiles[0]

        def score_pair(c2, carry):
            score_chunk(2 * c2, False)
            score_chunk(2 * c2 + 1, False)
            return carry

        fold_scr[...] = jnp.full_like(fold_scr, _I16_MIN)
        lax.fori_loop(0, n_steps - 1, score_pair, 0)
        score_chunk(last_pair, True)

        @pl.when(has_second)
        def _score_second():
            score_chunk(last_pair + 1, True)

        @pl.when(jnp.logical_not(has_second))
        def _blank_second():
            hi_scr[last_pair + 1] = jnp.full((kb, qb), _I16_MIN, _I16)
            lo_scr[last_pair + 1] = jnp.full((kb, qb), _I16_MIN, _I16)

        tile_rows = lambda r: slice(r * BF16_ROWS, (r + 1) * BF16_ROWS)
        i16_min = jnp.int16(_I16_MIN)

        def count(ref, cand, n_trips, blocks_per_trip, strict=False):
            cand_rows = jnp.broadcast_to(cand.astype(_I16), (BF16_ROWS, qb))
            one, zero = jnp.int16(1), jnp.int16(0)

            def body(t, accs):
                accs = list(accs)
                for u in range(blocks_per_trip):
                    for r in range(kb // BF16_ROWS):
                        tile = ref[blocks_per_trip * t + u, tile_rows(r), :]
                        hit = tile > cand_rows if strict else tile >= cand_rows
                        accs[r % len(accs)] = accs[r % len(accs)] + jnp.where(hit, one, zero)
                return tuple(accs)

            accs = lax.fori_loop(0, n_trips, body, (jnp.zeros((BF16_ROWS, qb), _I16),) * COUNT_ACCS)
            return jnp.sum(_tree_sum(list(accs)).astype(_I32), axis=0, keepdims=True)

        def bisect(ref, n_above, n_trips, blocks_per_trip):
            def body(step, carry):
                val, n_next = carry
                bit = jnp.left_shift(jnp.int32(1), 15 - step)
                cand = jnp.where(step == 0, 0, val | bit)
                cnt = n_above + count(ref, cand, n_trips, blocks_per_trip)
                ok = cnt >= topk
                return jnp.where(ok, cand, val), jnp.where(ok, n_next, cnt)
            return lax.fori_loop(0, 16, body, (jnp.full((1, qb), _I16_MIN, _I32), n_above))

        n_cap = (n_kb + CAP_CHUNKS_PER_BLOCK - 1) // CAP_CHUNKS_PER_BLOCK
        no_keys = jnp.zeros((1, qb), _I32)

        def high_full():
            return bisect(hi_scr, no_keys, n_steps, 2)

        def high_from_fold():
            base, _ = bisect(fold_scr, no_keys, n_cap, 1)
            n_bits = jnp.where(n_kb >= FOLD_LONG_CHUNKS, REFINE_BITS - 1, REFINE_BITS)

            def count_from(cand):
                cnt = count(hi_scr, jnp.minimum(cand, _I16_MAX), n_steps, 2)
                return jnp.where(cand > _I16_MAX, 0, cnt)

            n_beyond = count_from(base + jnp.left_shift(jnp.int32(1), n_bits))

            def refine():
                def body(step, carry):
                    off, n_next = carry
                    cand_off = off | jnp.left_shift(jnp.int32(1), n_bits - 1 - step)
                    cnt = count_from(base + cand_off)
                    ok = cnt >= topk
                    return jnp.where(ok, cand_off, off), jnp.where(ok, n_next, cnt)
                off, n_next = lax.fori_loop(0, n_bits, body, (no_keys, n_beyond))
                return base + off, n_next

            n_outside = jnp.sum(jnp.where(n_beyond >= topk, 1.0, 0.0))
            return lax.cond(n_outside > 0.0, high_full, refine)

        t_hi, n_gt_hi = lax.cond(n_kb >= FOLD_MIN_CHUNKS, high_from_fold, high_full)
        t_hi_tile = jnp.broadcast_to(t_hi.astype(_I16), (BF16_ROWS, qb))

        cap_scr[...] = jnp.full_like(cap_scr, _I16_MIN)

        def capture_block(c, carry):
            tops = [[jnp.full((BF16_ROWS, qb), _I16_MIN, _I16)] * 2 for _ in range(CAP_GROUPS)]
            for r in range(kb // BF16_ROWS):
                x = jnp.where(hi_scr[c, tile_rows(r), :] == t_hi_tile, lo_scr[c, tile_rows(r), :], i16_min)
                lo_scr[c, tile_rows(r), :] = x
                first, second = tops[r % CAP_GROUPS]
                above = x > first
                tops[r % CAP_GROUPS] = [jnp.where(above, x, first),
                                        jnp.where(above, first, jnp.where(x > second, x, second))]
            row0 = pl.multiple_of((c % CAP_CHUNKS_PER_BLOCK) * CAP_ROWS, CAP_ROWS)
            for g in range(CAP_GROUPS):
                for t in range(2):
                    cap_scr[c // CAP_CHUNKS_PER_BLOCK,
                            pl.ds(row0 + (2 * g + t) * BF16_ROWS, BF16_ROWS), :] = tops[g][t]
            return carry

        lax.fori_loop(0, n_kb, capture_block, 0)
        t_lo, _ = bisect(cap_scr, n_gt_hi, n_cap, 1)
        n_gt = n_gt_hi + count(lo_scr, t_lo, n_steps, 2, strict=True)
        n_wrong = jnp.sum(jnp.where(n_gt >= topk, 1.0, 0.0))
        t_lo, n_gt = lax.cond(n_wrong > 0.0,
                              lambda: bisect(lo_scr, n_gt_hi, n_steps, 2),
                              lambda: (t_lo, n_gt))
        t_hi_rows = jnp.broadcast_to(t_hi.astype(_I16), (kb, qb))
        t_lo = jnp.where((t_hi == _I16_MIN) & (t_lo == _I16_MIN), _I16_MIN + 1, t_lo)
        t_lo_rows = jnp.broadcast_to(t_lo.astype(_I16), (kb, qb))
        rem_rows = jnp.broadcast_to((topk - n_gt).astype(_I16), (kb, qb))

        def mask_block(c, seen):
            hi = hi_scr[c]
            lo = lo_scr[c]
            tie = (hi == t_hi_rows) & (lo == t_lo_rows)
            tie_count = jnp.where(tie, jnp.asarray(1, _BF16), jnp.asarray(0, _BF16))

            def tie_at(r):
                hit = (hi[r:r + 1].astype(_I32) == t_hi) & (lo[r:r + 1].astype(_I32) == t_lo)
                return jnp.where(hit, 1.0, 0.0)

            half = kb // 2
            before_top = seen + _dot(before_ref[...], tie_count[:half])
            seen_mid = before_top[half - 1:half] + tie_at(half - 1)
            before_bot = seen_mid + _dot(before_ref[...], tie_count[half:])
            ties_before = jnp.concatenate([before_top, before_bot], axis=0)
            allowed = ties_before.astype(_I32).astype(_I16) < rem_rows
            chosen = (hi > t_hi_rows) | (lo > t_lo_rows) | (tie & allowed)
            bias_scr[c] = jnp.where(chosen, jnp.asarray(0, bias_scr.dtype),
                                    jnp.asarray(_NEG, bias_scr.dtype))
            return before_bot[half - 1:half] + tie_at(kb - 1)

        seen = lax.fori_loop(0, n_steps - 1,
                             lambda c2, seen: mask_block(2 * c2 + 1, mask_block(2 * c2, seen)),
                             jnp.zeros((1, qb), _F32))
        seen = mask_block(last_pair, seen)

        @pl.when(has_second)
        def _mask_second():
            mask_block(last_pair + 1, seen)

        m_scr[...] = jnp.full_like(m_scr, _NEG)
        acc_scr[...] = jnp.zeros_like(acc_scr)

    def stage_b(h, s_read, bm_read, vt_read):
        m_old = m_scr[h]
        m_new = jnp.maximum(m_old, bm_read[h])
        pr = jnp.exp2(s_read[h] - m_new).astype(_BF16)
        acc_scr[h] = jnp.exp2(m_old - m_new) * acc_scr[h] + _dot(vt_read[0, 0, h], pr)
        m_scr[h] = m_new

    def phase(k_rows, chunk, s_write, bm_write, s_read, bm_read, vt_read):
        bias = bias_scr[chunk].astype(_F32)

        def stage_a(h):
            pair = slice((h // 2) * LANES, (h // 2 + 1) * LANES)
            s = _dot(k_ref[0, k_rows, pair], one_head(qt_ref[0, pair, :], h)) + bias
            s_write[h] = s
            bm_write[h] = jnp.max(s, axis=0, keepdims=True)

        stage_a(0)
        for h in range(ATT_HEADS):
            if h + 1 < ATT_HEADS:
                stage_a(h + 1)
            if s_read is not None:
                stage_b(h, s_read, bm_read, vt_read)

    @pl.when(j == 0)
    def _fill():
        phase(slice(0, kb), 0, s_even, bm_even, None, None, None)

    @pl.when((j > 0) & (j < n_steps))
    def _first():
        phase(slice(0, kb), 2 * j, s_even, bm_even, s_odd, bm_odd, vt_prev_ref)

    @pl.when(2 * j + 1 < n_kb)
    def _second():
        phase(slice(kb, 2 * kb), 2 * j + 1, s_odd, bm_odd, s_even, bm_even, vt_ref)

    @pl.when((j == n_steps) & has_second)
    def _drain_odd():
        for h in range(ATT_HEADS):
            stage_b(h, s_odd, bm_odd, vt_prev_ref)

    @pl.when((j == n_steps) & jnp.logical_not(has_second))
    def _drain_even():
        for h in range(ATT_HEADS):
            stage_b(h, s_even, bm_even, vt_prev_ref)

    @pl.when(j == n_steps)
    def _finish():
        for h in range(ATT_HEADS):
            a = acc_scr[h]
            o_ref[0, h * ATT_HEAD_DIM:(h + 1) * ATT_HEAD_DIM, :] = (
                a[:ATT_HEAD_DIM] / a[ATT_HEAD_DIM:ATT_HEAD_DIM + 1]).astype(o_ref.dtype)


def _dsa_attention(qt, k, vt, iqt, ik, iwt):
    bsz, seq, _ = k.shape
    qb, kb = DSA_QB, DSA_KB
    topk = min(TOPK_MAX, seq // 4)
    n_qb = seq // qb
    n_steps = lambda i: (i * qb + qb - 1) // (2 * kb) + 1
    pairs = [(i, j) for i in range(n_qb) for j in range(n_steps(i) + 1)]
    as_i32 = lambda vals: jnp.asarray(np.array(vals, np.int32))
    qi = as_i32([i for i, j in pairs])
    kj = as_i32([j for i, j in pairs])
    k_pair = as_i32([min(j, n_steps(i) - 1) for i, j in pairs])
    n_kb = lambda i: (i * qb + qb - 1) // kb + 1
    v_prev = as_i32([n_kb(i) - 1 if j == n_steps(i) else max(2 * j - 1, 0) for i, j in pairs])
    v_this = as_i32([min(2 * j, 2 * n_steps(i) - 1) for i, j in pairs])
    before =jnp.asarray(np.tril(np.ones((kb // 2, kb // 2), np.float32), -1), _BF16)
    q_map = lambda b, p, qi_r, *_: (b, 0, qi_r[p])
    vt_spec = lambda which: pl.BlockSpec(
        (1, 1, ATT_HEADS, V_ROWS, kb), lambda b, p, *refs: (b, refs[which][p], 0, 0, 0))
    grid_spec = pltpu.PrefetchScalarGridSpec(
        num_scalar_prefetch=5,
        grid=(bsz, len(pairs)),
        in_specs=[
            pl.BlockSpec((1, ATT_W, qb), q_map),
            pl.BlockSpec((1, IDX_W, qb), q_map),
            pl.BlockSpec((1, SUBLANES, qb), q_map),
            pl.BlockSpec((1, seq, LANES), lambda b, p, *_: (b, 0, 0)),
            pl.BlockSpec(before.shape, lambda b, p, *_: (0, 0)),
            pl.BlockSpec((1, 2 * kb, ATT_W), lambda b, p, *refs: (b, refs[2][p], 0)),
            vt_spec(3),
            vt_spec(4),
        ],
        out_specs=pl.BlockSpec((1, ATT_W, qb), q_map),
        scratch_shapes=[
            pltpu.VMEM((seq // kb, kb, qb), _I16),
            pltpu.VMEM((seq // kb, kb, qb), _I16),
            pltpu.VMEM((-(-(seq // kb) // CAP_CHUNKS_PER_BLOCK), kb, qb), _I16),
            pltpu.VMEM((-(-(seq // kb) // CAP_CHUNKS_PER_BLOCK), kb, qb), _I16),
            pltpu.VMEM((seq // kb, kb, qb), _BF16),
            pltpu.VMEM((ATT_HEADS, 1, qb), _F32),
            pltpu.VMEM((ATT_HEADS, V_ROWS, qb), _F32),
            pltpu.VMEM((ATT_HEADS, kb, qb), _F32),
            pltpu.VMEM((ATT_HEADS, kb, qb), _F32),
            pltpu.VMEM((ATT_HEADS, 1, qb), _F32),
            pltpu.VMEM((ATT_HEADS, 1, qb), _F32),
        ],
    )
    return pl.pallas_call(
        functools.partial(_dsa_kernel, topk=topk),
        grid_spec=grid_spec,
        out_shape=jax.ShapeDtypeStruct((bsz, ATT_W, seq), _BF16),
        compiler_params=pltpu.CompilerParams(
            dimension_semantics=("arbitrary", "arbitrary"), vmem_limit_bytes=VMEM_LIMIT_BYTES),
    )(qi, kj, k_pair, v_prev, v_this, qt, iqt, iwt, ik, before, k, vt, vt)


def _split3(a):
    hi = a.astype(_BF16)
    r1 = a - hi.astype(_F32)
    mid = r1.astype(_BF16)
    lo = (r1 - mid.astype(_F32)).astype(_BF16)
    return hi, mid, lo


def _hgrn_kernel(lbl_ref, g_ref, hq_ref, hf_ref, hi_ref, hg_ref, o_ref,
                 state_scr, kk_scr, b_scr, o_scr, *, layer):
    rows, ch, sb = HG_ROWS, HG_CHUNK, HG_SUB
    n_sub = ch // sb

    @pl.when(pl.program_id(1) == 0)
    def _reset():
        state_scr[...] = jnp.zeros_like(state_scr)

    lbl = lbl_ref[...]
    e = jnp.exp(lbl - jnp.max(lbl, axis=0, keepdims=True))
    lb = jnp.sum(e[:layer + 1], axis=0, keepdims=True) / jnp.sum(e, axis=0, keepdims=True)

    f = lb + (1.0 - lb) * jax.nn.sigmoid(hf_ref[0])
    kk_scr[...] = 1.0 - f
    logf = jnp.log(f)
    r_i = lax.broadcasted_iota(_I32, (ch, ch), 0)
    c_i = lax.broadcasted_iota(_I32, (ch, ch), 1)
    lower = jnp.where(c_i <= r_i, 1.0, 0.0).astype(_BF16)
    for c in range(rows // ch):
        parts = _split3(logf[c * ch:(c + 1) * ch])
        b_scr[c * ch:(c + 1) * ch, :] = sum(_dot(lower, part) for part in parts)

    t_idx = lax.broadcasted_iota(_I32, (sb, 1), 0)
    row_idx = lax.broadcasted_iota(_I32, (ch, 1), 0)

    def chunk(c, carry, bounded):
        r0 = pl.multiple_of(c * ch, ch)
        cs = pl.ds(r0, ch)
        b = b_scr[cs, :]
        kk = kk_scr[cs, :]
        qv = hq_ref[0, cs, :]
        vv = hi_ref[0, cs, :]
        b_last = b[ch - 1:ch]
        q_in = (qv * jnp.exp(b)).astype(_BF16)
        k_out = kk * jnp.exp(b_last - b)
        vb = vv.astype(_BF16)

        a_off = [[] for _ in range(HG_HEADS)]
        sbs = HG_SUB_BOUNDED if bounded else sb
        for s_i in range(ch // sbs):
            if s_i == 0 and not bounded:
                for h in range(HG_HEADS):
                    a_off[h].append(jnp.zeros((sbs, ch), _F32))
                continue
            ref_b = b[s_i * sbs - 1:s_i * sbs] if s_i else jnp.zeros_like(b_last)
            q_s = (qv[s_i * sbs:(s_i + 1) * sbs] * jnp.exp(b[s_i * sbs:(s_i + 1) * sbs] - ref_b))
            n_rows = (s_i + 1) * sbs if bounded else s_i * sbs
            k_s = jnp.where(row_idx < n_rows,
                            kk * jnp.exp(jnp.where(row_idx < n_rows, ref_b - b, 0.0)), 0.0)
            q_s = q_s.astype(_BF16)
            k_s = k_s.astype(_BF16)
            for h in range(HG_HEADS):
                hs = slice(h * HG_KDIM, (h + 1) * HG_KDIM)
                a_off[h].append(_nt_dot(q_s[:, hs], k_s[:, hs]))

        for h in range(HG_HEADS):
            hs = slice(h * HG_KDIM, (h + 1) * HG_KDIM)
            st = state_scr[h]
            o_h = _nt_dot(q_in[:, hs], st.astype(_BF16))
            a_h = jnp.concatenate(a_off[h], axis=0)
            if bounded:
                a_h = jnp.where(c_i <= r_i, a_h, 0.0)
            o_scr[cs, hs] = o_h + _dot(a_h.astype(_BF16), vb[:, hs])
            state_scr[h] = (st * jnp.exp(b_last[:, hs])
                            + _tn_dot(vb[:, hs], k_out[:, hs].astype(_BF16)))
        if bounded:
            return carry

        for s_i in range(n_sub):
            rs = pl.ds(r0 + s_i * sb, sb)
            q_s = hq_ref[0, rs, :]
            b_s = b_scr[rs, :]
            acc = o_scr[rs, :]
            for t in range(sb):
                one = pl.ds(r0 + s_i * sb + t, 1)
                w = q_s * jnp.exp(jnp.minimum(b_s - b_scr[one, :], 0.0)) * kk_scr[one, :]
                v_row = hi_ref[0, one, :]
                parts = []
                for h in range(HG_HEADS):
                    hs = slice(h * HG_KDIM, (h + 1) * HG_KDIM)
                    a = jnp.sum(w[:, hs], axis=1, keepdims=True)
                    parts.append(jnp.where(t_idx >= t, a, 0.0) * v_row[:, hs])
                acc = acc + jnp.concatenate(parts, axis=1)
            o_scr[rs, :] = acc
        return carry

    run = lambda bounded: lambda: lax.fori_loop(
        0, rows // ch, functools.partial(chunk, bounded=bounded), 0)
    lax.cond(jnp.min(logf) >= -HG_MAX_DECAY / HG_SUB_BOUNDED, run(True), run(False))

    o = o_scr[...]
    gate = hg_ref[0]
    gain = g_ref[...]
    for h in range(HG_HEADS):
        hs = slice(h * HG_KDIM, (h + 1) * HG_KDIM)
        oh = o[:, hs]
        oh = oh * lax.rsqrt(jnp.mean(oh * oh, axis=1, keepdims=True) + RMS_EPS) * gain[:, hs]
        gh = gate[:, hs]
        o_ref[0, :, hs] = (oh * (gh * jax.nn.sigmoid(gh))).astype(o_ref.dtype)


def _hgrn2(hq, hf, hi, hg, lb_logits, norm_g, layer):
    bsz, seq, _ = hq.shape
    rows = HG_ROWS
    blk = pl.BlockSpec((1, rows, HG_W), lambda b, t: (b, t, 0))
    lbl = lb_logits.reshape(lb_logits.shape[0], HG_W)
    gain = norm_g.reshape(1, HG_W)
    return pl.pallas_call(
        functools.partial(_hgrn_kernel, layer=layer),
        grid=(bsz, seq // rows),
        in_specs=[pl.BlockSpec(lbl.shape, lambda b, t: (0, 0)),
                  pl.BlockSpec(gain.shape, lambda b, t: (0, 0)),
                  blk, blk, blk, blk],
        out_specs=blk,
        out_shape=jax.ShapeDtypeStruct((bsz, seq, HG_W), _BF16),
        scratch_shapes=[
            pltpu.VMEM((HG_HEADS, HG_KDIM, HG_KDIM), _F32),
            pltpu.VMEM((rows, HG_W), _F32),
            pltpu.VMEM((rows, HG_W), _F32),
            pltpu.VMEM((rows, HG_W), _F32),
        ],
        compiler_params=pltpu.CompilerParams(
            dimension_semantics=("arbitrary", "arbitrary"), vmem_limit_bytes=VMEM_LIMIT_BYTES),
    )(lbl, gain, hq, hf, hi, hg)


def _layer_norm(y, g, b):
    mu = jnp.mean(y, axis=1, keepdims=True)
    yc = y - mu
    var = jnp.mean(yc * yc, axis=1, keepdims=True)
    return yc * lax.rsqrt(var + LN_EPS) * g + b


def _ffn_kernel(x_ref, att_ref, hgo_ref, wo_ref, g1_ref, b1_ref, wu_ref, wd_ref,
                g2_ref, b2_ref, o_ref, *, alpha):
    mix = (_tn_dot(att_ref[0], wo_ref[:ATT_W, :]) + _dot(hgo_ref[...], wo_ref[ATT_W:, :]))
    y1 = _layer_norm(alpha * x_ref[...] + mix, g1_ref[...], b1_ref[...])
    y1b = y1.astype(_BF16)
    h = jnp.zeros_like(y1)
    for c in range(wu_ref.shape[1] // FFN_COLS):
        cs = slice(c * FFN_COLS, (c + 1) * FFN_COLS)
        u = jnp.maximum(_dot(y1b, wu_ref[:, cs]), 0.0)
        h = h + _dot((u * u).astype(_BF16), wd_ref[cs, :])
    o_ref[...] = _layer_norm(alpha * y1 + h, g2_ref[...], b2_ref[...])


def _out_ffn(x2d, att_t, hgo2d, w_o, g1, b1, w_up, w_down, g2, b2, alpha):
    rows, d = x2d.shape
    seq = att_t.shape[2]
    tm = FFN_ROWS
    n_seq_blocks = seq // tm
    row_spec = lambda w: pl.BlockSpec((tm, w), lambda i: (i, 0))
    full_spec = lambda a: pl.BlockSpec(a.shape, lambda i: (0, 0), pipeline_mode=pl.Buffered(1))
    att_spec = pl.BlockSpec((1, ATT_W, tm), lambda i: (i // n_seq_blocks, 0, i % n_seq_blocks))
    vec = lambda a: a.reshape(1, d).astype(_F32)
    args = (x2d, att_t, hgo2d, w_o.astype(_BF16), vec(g1), vec(b1),
            w_up.astype(_BF16), w_down.astype(_BF16), vec(g2), vec(b2))
    in_specs = [row_spec(d), att_spec, row_spec(HG_W)] + [full_spec(a) for a in args[3:]]
    return pl.pallas_call(
        functools.partial(_ffn_kernel, alpha=alpha),
        grid=(rows // tm,),
        in_specs=in_specs,
        out_specs=row_spec(d),
        out_shape=jax.ShapeDtypeStruct((rows, d), _F32),
        compiler_params=pltpu.CompilerParams(
            dimension_semantics=("arbitrary",), vmem_limit_bytes=VMEM_LIMIT_BYTES),
    )(*args)


def kernel(x, w_in, w_o, lb_logits, hg_norm_g, ln1_g, ln1_b, w_up, w_down, ln2_g, ln2_b):
    bsz, seq, d = x.shape
    depth = w_in.shape[0]
    alpha = (2.0 * depth) ** 0.25
    x2d = x.reshape(bsz * seq, d)
    for l in range(depth):
        qt, k, vt, iqt, ik, iwt, hq, hf, hi, hg = _project(x2d, w_in[l], seq)
        r3 = lambda a: a.reshape(bsz, seq, a.shape[-1])
        att_t = _dsa_attention(qt, r3(k), vt, iqt, r3(ik), iwt)
        hgo = _hgrn2(r3(hq), r3(hf), r3(hi), r3(hg), lb_logits, hg_norm_g[l], l)
        x2d = _out_ffn(x2d, att_t, hgo.reshape(bsz * seq, HG_W),
                       w_o[l], ln1_g[l], ln1_b[l], w_up[l], w_down[l], ln2_g[l], ln2_b[l], alpha)
    return x2d.reshape(bsz, seq, d)
```

```python
import functools
import math

import numpy as np
import jax
import jax.numpy as jnp
from jax import lax
from jax.experimental import pallas as pl
from jax.experimental.pallas import tpu as pltpu

ATT_HEAD_DIM = 64
ATT_HEADS = 8
ATT_W = ATT_HEADS * ATT_HEAD_DIM
IDX_HEADS = 4
IDX_DIM = 64
IDX_W = IDX_HEADS * IDX_DIM
TOPK_MAX = 256
HG_KDIM = 128
HG_HEADS = 4
HG_W = HG_HEADS * HG_KDIM
ROPE_THETA = 10000.0
LN_EPS = 1e-5
RMS_EPS = 1e-6

LANES = 128
SUBLANES = 8
BF16_ROWS = 16
VMEM_LIMIT_BYTES = 56 * 1024 * 1024

PROJ_ROWS = 512
DSA_QB = 256
DSA_KB = 512
V_ROWS = ATT_HEAD_DIM + BF16_ROWS
COUNT_ACCS = 4
CAP_GROUPS = 2
CAP_ROWS = 2 * CAP_GROUPS * BF16_ROWS
CAP_CHUNKS_PER_BLOCK = DSA_KB // CAP_ROWS
FOLD_GROUPS = CAP_ROWS // BF16_ROWS
REFINE_BITS = 6
FOLD_MIN_CHUNKS = 14
FOLD_LONG_CHUNKS = 22
HG_ROWS = 512
HG_CHUNK = 64
HG_SUB = 8
HG_SUB_BOUNDED = 16
HG_MAX_DECAY = 80.0
FFN_ROWS = 512
FFN_COLS = 1024

_F32 = jnp.float32
_BF16 = jnp.bfloat16
_I32 = jnp.int32
_I16 = jnp.int16
_INT_MIN = -(2 ** 31)
_I16_MIN = -(2 ** 15)
_I16_MAX = 2 ** 15 - 1
_LOW_SIGN = 1 << 15
_NEG = -1e30


def _nt_dot(a, b):
    return lax.dot_general(a, b, (((1,), (1,)), ((), ())), preferred_element_type=_F32)


def _tn_dot(a, b):
    return lax.dot_general(a, b, (((0,), (0,)), ((), ())), preferred_element_type=_F32)


def _dot(a, b):
    return jnp.dot(a, b, preferred_element_type=_F32)


def _tree_sum(parts):
    while len(parts) > 1:
        parts = [parts[n] + parts[n + 1] for n in range(0, len(parts) - 1, 2)] + (
            [parts[-1]] if len(parts) % 2 else [])
    return parts[0]


def _rope_group(z, cos, sin_signed):
    lane = lax.broadcasted_iota(_I32, z.shape, 1)
    first_half = (lane % ATT_HEAD_DIM) < (ATT_HEAD_DIM // 2)
    upper = pltpu.roll(z, LANES - ATT_HEAD_DIM // 2, 1)
    lower = pltpu.roll(z, ATT_HEAD_DIM // 2, 1)
    return z * cos + jnp.where(first_half, upper, lower) * sin_signed


def _proj_kernel(x_ref, wa_ref, wh_ref, cos_ref, sin_ref,
                 qt_ref, k_ref, vt_ref, iqt_ref, ik_ref, iwt_ref,
                 hq_ref, hf_ref, hi_ref, hg_ref, *, q_scale, iw_scale):
    xb = x_ref[...].astype(_BF16)
    cos = cos_ref[...]
    sin = sin_ref[...]
    pa = _dot(xb, wa_ref[...])

    def roped(col0, width):
        return [_rope_group(pa[:, col0 + g * LANES: col0 + (g + 1) * LANES], cos, sin)
                for g in range(width // LANES)]

    for g, z in enumerate(roped(0, ATT_W)):
        qt_ref[0, g * LANES:(g + 1) * LANES, :] = (z * q_scale).T.astype(_BF16)
    for g, z in enumerate(roped(ATT_W, ATT_W)):
        k_ref[:, g * LANES:(g + 1) * LANES] = z.astype(_BF16)
    rows = x_ref.shape[0]
    ones_row = lax.broadcasted_iota(_I32, (V_ROWS - ATT_HEAD_DIM, rows), 0) == 0
    for g in range(ATT_W // LANES):
        vt = pa[:, 2 * ATT_W + g * LANES:2 * ATT_W + (g + 1) * LANES].T.astype(_BF16)
        for sub in range(LANES // ATT_HEAD_DIM):
            head = g * (LANES // ATT_HEAD_DIM) + sub
            vt_ref[0, 0, head, :ATT_HEAD_DIM, :] = vt[sub * ATT_HEAD_DIM:(sub + 1) * ATT_HEAD_DIM]
            vt_ref[0, 0, head, ATT_HEAD_DIM:, :] = jnp.where(ones_row, 1.0, 0.0).astype(_BF16)
    for g, z in enumerate(roped(3 * ATT_W, IDX_W)):
        iqt_ref[0, g * LANES:(g + 1) * LANES, :] = z.T.astype(_BF16)
    ik_ref[...] = roped(3 * ATT_W + IDX_W, LANES)[0].astype(_BF16)
    iwt_ref[0] = (pa[:, 3 * ATT_W + IDX_W + LANES:] * iw_scale).T[:SUBLANES]

    ph = _dot(xb, wh_ref[...])
    hq_ref[...] = ph[:, 0 * HG_W:1 * HG_W]
    hf_ref[...] = ph[:, 1 * HG_W:2 * HG_W]
    hi_ref[...] = ph[:, 2 * HG_W:3 * HG_W]
    hg_ref[...] = ph[:, 3 * HG_W:4 * HG_W]


def _rope_tables(seq):
    half = ATT_HEAD_DIM // 2
    inv = np.power(np.float64(ROPE_THETA), -np.arange(half, dtype=np.float64) / half)
    ang = np.arange(seq, dtype=np.float64)[:, None] * inv[None, :]
    cos = np.cos(ang)
    sin = np.sin(ang)
    cos_t = np.tile(np.concatenate([cos, cos], axis=1), (1, LANES // ATT_HEAD_DIM))
    sin_t = np.tile(np.concatenate([-sin, sin], axis=1), (1, LANES // ATT_HEAD_DIM))
    return jnp.asarray(cos_t, _F32), jnp.asarray(sin_t, _F32)


def _project(x2d, w_in, seq):
    rows, d = x2d.shape
    c = 3 * ATT_W + IDX_W
    w_in = w_in.astype(_BF16)
    w_ik = w_in[:, c:c + IDX_DIM]
    w_iw = w_in[:, c + IDX_DIM:c + IDX_DIM + IDX_HEADS]
    wa = jnp.concatenate([w_in[:, :c], w_ik, w_ik,
                          jnp.pad(w_iw, ((0, 0), (0, LANES - IDX_HEADS)))], axis=1)
    wh = w_in[:, c + IDX_DIM + IDX_HEADS:]
    cos_t, sin_t = _rope_tables(seq)
    tm = PROJ_ROWS
    n_seq_blocks = seq // tm
    row_spec = lambda w: pl.BlockSpec((tm, w), lambda i: (i, 0))
    full_spec = lambda a: pl.BlockSpec(a.shape, lambda i: (0, 0), pipeline_mode=pl.Buffered(1))
    pos_spec = pl.BlockSpec((tm, LANES), lambda i: (i % n_seq_blocks, 0))
    bsz = rows // seq
    t_spec = lambda *feat: pl.BlockSpec(
        (1,) + feat + (tm,), lambda i: (i // n_seq_blocks,) + (0,) * len(feat) + (i % n_seq_blocks,))
    out_shapes = [
        jax.ShapeDtypeStruct((bsz, ATT_W, seq), _BF16),
        jax.ShapeDtypeStruct((rows, ATT_W), _BF16),
        jax.ShapeDtypeStruct((bsz, seq // DSA_KB, ATT_HEADS, V_ROWS, DSA_KB), _BF16),
        jax.ShapeDtypeStruct((bsz, IDX_W, seq), _BF16),
        jax.ShapeDtypeStruct((rows, LANES), _BF16),
        jax.ShapeDtypeStruct((bsz, SUBLANES, seq), _F32),
    ] + [jax.ShapeDtypeStruct((rows, HG_W), _F32)] * 4
    tiles_per_chunk = DSA_KB // tm
    vt_spec = pl.BlockSpec(
        (1, 1, ATT_HEADS, V_ROWS, tm),
        lambda i: (i // n_seq_blocks, (i % n_seq_blocks) // tiles_per_chunk, 0, 0, i % tiles_per_chunk))
    out_specs = [t_spec(ATT_W), row_spec(ATT_W), vt_spec, t_spec(IDX_W),
                 row_spec(LANES), t_spec(SUBLANES)] + [row_spec(HG_W)] * 4
    kern = functools.partial(_proj_kernel, q_scale=ATT_HEAD_DIM ** -0.5 * math.log2(math.e),
                             iw_scale=(IDX_HEADS ** -0.5) * (IDX_DIM ** -0.5))
    return pl.pallas_call(
        kern,
        grid=(rows // tm,),
        in_specs=[row_spec(d), full_spec(wa), full_spec(wh), pos_spec, pos_spec],
        out_specs=out_specs,
        out_shape=out_shapes,
        compiler_params=pltpu.CompilerParams(
            dimension_semantics=("arbitrary",), vmem_limit_bytes=VMEM_LIMIT_BYTES),
    )(x2d, wa, wh, cos_t, sin_t)


def _dsa_kernel(qi_ref, kj_ref, ka_ref, vp_ref, vc_ref,
                qt_ref, iqt_ref, iwt_ref, ik_ref, before_ref, k_ref, vt_prev_ref, vt_ref,
                o_ref,
                hi_scr, lo_scr, fold_scr, cap_scr, bias_scr, m_scr, acc_scr,
                s_even, s_odd, bm_even, bm_odd, *, topk):
    del ka_ref, vp_ref, vc_ref
    p = pl.program_id(1)
    i = qi_ref[p]
    j = kj_ref[p]
    qb, kb = DSA_QB, DSA_KB
    n_kb = (i * qb + qb - 1) // kb + 1
    n_steps = (i * qb + qb - 1) // (2 * kb) + 1
    last_pair = 2 * (n_steps - 1)
    has_second = n_kb == 2 * n_steps
    first_head = lax.broadcasted_iota(_I32, (LANES, qb), 0) < ATT_HEAD_DIM

    def one_head(pair_rows, h):
        keep = first_head if h % 2 == 0 else jnp.logical_not(first_head)
        return jnp.where(keep, pair_rows, jnp.zeros_like(pair_rows))

    @pl.when(j == 0)
    def _select():
        iwt = iwt_ref[0]
        qpos = i * qb + lax.broadcasted_iota(_I32, (kb, qb), 1)
        krow = lax.broadcasted_iota(_I32, (kb, qb), 0)
        iq_heads = [one_head(iqt_ref[0, (h // 2) * LANES:(h // 2 + 1) * LANES, :], h)
                    for h in range(IDX_HEADS)]

        def score_chunk(c, causal):
            row0 = pl.multiple_of(c * kb, kb)
            ikc = ik_ref[0, pl.ds(row0, kb), :]
            score = jnp.zeros((kb, qb), _F32)
            for h in range(IDX_HEADS):
                logits = _dot(ikc, iq_heads[h])
                score = score + iwt[h:h + 1, :] * jnp.maximum(logits, 0.0)
            bits = lax.bitcast_convert_type(score, _I32)
            key = bits ^ (((bits >> 31) & 0x7FFFFFFF) ^ _LOW_SIGN)
            if causal:
                key = jnp.where(row0 + krow <= qpos, key, _INT_MIN ^ _LOW_SIGN)
            hi = (key >> 16).astype(_I16)
            hi_scr[c] = hi
            lo_scr[c] = key.astype(_I16)
            n_tiles = kb // BF16_ROWS
            row0_fold = pl.multiple_of((c % CAP_CHUNKS_PER_BLOCK) * CAP_ROWS, CAP_ROWS)
            for g in range(FOLD_GROUPS):
                tiles = [hi[r * BF16_ROWS:(r + 1) * BF16_ROWS] for r in range(g, n_tiles, FOLD_GROUPS)]
                while len(tiles) > 1:
                    tiles = [jnp.where(tiles[n] > tiles[n + 1], tiles[n], tiles[n + 1])
                             for n in range(0, len(tiles), 2)]
                fold_scr[c // CAP_CHUNKS_PER_BLOCK, pl.ds(row0_fold + g * BF16_ROWS, BF16_ROWS), :] = tiles[0]

        def score_pair(c2, carry):
            score_chunk(2 * c2, False)
            score_chunk(2 * c2 + 1, False)
            return carry

        fold_scr[...] = jnp.full_like(fold_scr, _I16_MIN)
        lax.fori_loop(0, n_steps - 1, score_pair, 0)
        score_chunk(last_pair, True)

        @pl.when(has_second)
        def _score_second():
            score_chunk(last_pair + 1, True)

        @pl.when(jnp.logical_not(has_second))
        def _blank_second():
            hi_scr[last_pair + 1] = jnp.full((kb, qb), _I16_MIN, _I16)
            lo_scr[last_pair + 1] = jnp.full((kb, qb), _I16_MIN, _I16)

        tile_rows = lambda r: slice(r * BF16_ROWS, (r + 1) * BF16_ROWS)
        i16_min = jnp.int16(_I16_MIN)

        def count(ref, cand, n_trips, blocks_per_trip, strict=False):
            cand_rows = jnp.broadcast_to(cand.astype(_I16), (BF16_ROWS, qb))
            one, zero = jnp.int16(1), jnp.int16(0)

            def body(t, accs):
                accs = list(accs)
                for u in range(blocks_per_trip):
                    for r in range(kb // BF16_ROWS):
                        tile = ref[blocks_per_trip * t + u, tile_rows(r), :]
                        hit = tile > cand_rows if strict else tile >= cand_rows
                        accs[r % len(accs)] = accs[r % len(accs)] + jnp.where(hit, one, zero)
                return tuple(accs)

            accs = lax.fori_loop(0, n_trips, body, (jnp.zeros((BF16_ROWS, qb), _I16),) * COUNT_ACCS)
            return jnp.sum(_tree_sum(list(accs)).astype(_I32), axis=0, keepdims=True)

        def bisect(ref, n_above, n_trips, blocks_per_trip):
            def body(step, carry):
                val, n_next = carry
                bit = jnp.left_shift(jnp.int32(1), 15 - step)
                cand = jnp.where(step == 0, 0, val | bit)
                cnt = n_above + count(ref, cand, n_trips, blocks_per_trip)
                ok = cnt >= topk
                return jnp.where(ok, cand, val), jnp.where(ok, n_next, cnt)
            return lax.fori_loop(0, 16, body, (jnp.full((1, qb), _I16_MIN, _I32), n_above))

        n_cap = (n_kb + CAP_CHUNKS_PER_BLOCK - 1) // CAP_CHUNKS_PER_BLOCK
        no_keys = jnp.zeros((1, qb), _I32)

        def high_full():
            return bisect(hi_scr, no_keys, n_steps, 2)

        def high_from_fold():
            base, _ = bisect(fold_scr, no_keys, n_cap, 1)
            n_bits = jnp.where(n_kb >= FOLD_LONG_CHUNKS, REFINE_BITS - 1, REFINE_BITS)

            def count_from(cand):
                cnt = count(hi_scr, jnp.minimum(cand, _I16_MAX), n_steps, 2)
                return jnp.where(cand > _I16_MAX, 0, cnt)

            n_beyond = count_from(base + jnp.left_shift(jnp.int32(1), n_bits))

            def refine():
                def body(step, carry):
                    off, n_next = carry
                    cand_off = off | jnp.left_shift(jnp.int32(1), n_bits - 1 - step)
                    cnt = count_from(base + cand_off)
                    ok = cnt >= topk
                    return jnp.where(ok, cand_off, off), jnp.where(ok, n_next, cnt)
                off, n_next = lax.fori_loop(0, n_bits, body, (no_keys, n_beyond))
                return base + off, n_next

            n_outside = jnp.sum(jnp.where(n_beyond >= topk, 1.0, 0.0))
            return lax.cond(n_outside > 0.0, high_full, refine)

        t_hi, n_gt_hi = lax.cond(n_kb >= FOLD_MIN_CHUNKS, high_from_fold, high_full)
        t_hi_tile = jnp.broadcast_to(t_hi.astype(_I16), (BF16_ROWS, qb))

        cap_scr[...] = jnp.full_like(cap_scr, _I16_MIN)

        def capture_block(c, carry):
            tops = [[jnp.full((BF16_ROWS, qb), _I16_MIN, _I16)] * 2 for _ in range(CAP_GROUPS)]
            for r in range(kb // BF16_ROWS):
                x = jnp.where(hi_scr[c, tile_rows(r), :] == t_hi_tile, lo_scr[c, tile_rows(r), :], i16_min)
                lo_scr[c, tile_rows(r), :] = x
                first, second = tops[r % CAP_GROUPS]
                above = x > first
                tops[r % CAP_GROUPS] = [jnp.where(above, x, first),
                                        jnp.where(above, first, jnp.where(x > second, x, second))]
            row0 = pl.multiple_of((c % CAP_CHUNKS_PER_BLOCK) * CAP_ROWS, CAP_ROWS)
            for g in range(CAP_GROUPS):
                for t in range(2):
                    cap_scr[c // CAP_CHUNKS_PER_BLOCK,
                            pl.ds(row0 + (2 * g + t) * BF16_ROWS, BF16_ROWS), :] = tops[g][t]
            return carry

        lax.fori_loop(0, n_kb, capture_block, 0)
        t_lo, _ = bisect(cap_scr, n_gt_hi, n_cap, 1)
        n_gt = n_gt_hi + count(lo_scr, t_lo, n_steps, 2, strict=True)
        n_wrong = jnp.sum(jnp.where(n_gt >= topk, 1.0, 0.0))
        t_lo, n_gt = lax.cond(n_wrong > 0.0,
                              lambda: bisect(lo_scr, n_gt_hi, n_steps, 2),
                              lambda: (t_lo, n_gt))
        t_hi_rows = jnp.broadcast_to(t_hi.astype(_I16), (kb, qb))
        t_lo = jnp.where((t_hi == _I16_MIN) & (t_lo == _I16_MIN), _I16_MIN + 1, t_lo)
        t_lo_rows = jnp.broadcast_to(t_lo.astype(_I16), (kb, qb))
        rem_rows = jnp.broadcast_to((topk - n_gt).astype(_I16), (kb, qb))

        def mask_block(c, seen):
            hi = hi_scr[c]
            lo = lo_scr[c]
            tie = (hi == t_hi_rows) & (lo == t_lo_rows)
            tie_count = jnp.where(tie, jnp.asarray(1, _BF16), jnp.asarray(0, _BF16))

            def tie_at(r):
                hit = (hi[r:r + 1].astype(_I32) == t_hi) & (lo[r:r + 1].astype(_I32) == t_lo)
                return jnp.where(hit, 1.0, 0.0)

            half = kb // 2
            before_top = seen + _dot(before_ref[...], tie_count[:half])
            seen_mid = before_top[half - 1:half] + tie_at(half - 1)
            before_bot = seen_mid + _dot(before_ref[...], tie_count[half:])
            ties_before = jnp.concatenate([before_top, before_bot], axis=0)
            allowed = ties_before.astype(_I32).astype(_I16) < rem_rows
            chosen = (hi > t_hi_rows) | (lo > t_lo_rows) | (tie & allowed)
            bias_scr[c] = jnp.where(chosen, jnp.asarray(0, bias_scr.dtype),
                                    jnp.asarray(_NEG, bias_scr.dtype))
            return before_bot[half - 1:half] + tie_at(kb - 1)

        seen = lax.fori_loop(0, n_steps - 1,
                             lambda c2, seen: mask_block(2 * c2 + 1, mask_block(2 * c2, seen)),
                             jnp.zeros((1, qb), _F32))
        seen = mask_block(last_pair, seen)

        @pl.when(has_second)
        def _mask_second():
            mask_block(last_pair + 1, seen)

        m_scr[...] = jnp.full_like(m_scr, _NEG)
        acc_scr[...] = jnp.zeros_like(acc_scr)

    def stage_b(h, s_read, bm_read, vt_read):
        m_old = m_scr[h]
        m_new = jnp.maximum(m_old, bm_read[h])
        pr = jnp.exp2(s_read[h] - m_new).astype(_BF16)
        acc_scr[h] = jnp.exp2(m_old - m_new) * acc_scr[h] + _dot(vt_read[0, 0, h], pr)
        m_scr[h] = m_new

    def phase(k_rows, chunk, s_write, bm_write, s_read, bm_read, vt_read):
        bias = bias_scr[chunk].astype(_F32)

        def stage_a(h):
            pair = slice((h // 2) * LANES, (h // 2 + 1) * LANES)
            s = _dot(k_ref[0, k_rows, pair], one_head(qt_ref[0, pair, :], h)) + bias
            s_write[h] = s
            bm_write[h] = jnp.max(s, axis=0, keepdims=True)

        stage_a(0)
        for h in range(ATT_HEADS):
            if h + 1 < ATT_HEADS:
                stage_a(h + 1)
            if s_read is not None:
                stage_b(h, s_read, bm_read, vt_read)

    @pl.when(j == 0)
    def _fill():
        phase(slice(0, kb), 0, s_even, bm_even, None, None, None)

    @pl.when((j > 0) & (j < n_steps))
    def _first():
        phase(slice(0, kb), 2 * j, s_even, bm_even, s_odd, bm_odd, vt_prev_ref)

    @pl.when(2 * j + 1 < n_kb)
    def _second():
        phase(slice(kb, 2 * kb), 2 * j + 1, s_odd, bm_odd, s_even, bm_even, vt_ref)

    @pl.when((j == n_steps) & has_second)
    def _drain_odd():
        for h in range(ATT_HEADS):
            stage_b(h, s_odd, bm_odd, vt_prev_ref)

    @pl.when((j == n_steps) & jnp.logical_not(has_second))
    def _drain_even():
        for h in range(ATT_HEADS):
            stage_b(h, s_even, bm_even, vt_prev_ref)

    @pl.when(j == n_steps)
    def _finish():
        for h in range(ATT_HEADS):
            a = acc_scr[h]
            o_ref[0, h * ATT_HEAD_DIM:(h + 1) * ATT_HEAD_DIM, :] = (
                a[:ATT_HEAD_DIM] / a[ATT_HEAD_DIM:ATT_HEAD_DIM + 1]).astype(o_ref.dtype)


def _dsa_attention(qt, k, vt, iqt, ik, iwt):
    bsz, seq, _ = k.shape
    qb, kb = DSA_QB, DSA_KB
    topk = min(TOPK_MAX, seq // 4)
    n_qb = seq // qb
    n_steps = lambda i: (i * qb + qb - 1) // (2 * kb) + 1
    pairs = [(i, j) for i in range(n_qb) for j in range(n_steps(i) + 1)]
    as_i32 = lambda vals: jnp.asarray(np.array(vals, np.int32))
    qi = as_i32([i for i, j in pairs])
    kj = as_i32([j for i, j in pairs])
    k_pair = as_i32([min(j, n_steps(i) - 1) for i, j in pairs])
    n_kb = lambda i: (i * qb + qb - 1) // kb + 1
    v_prev = as_i32([n_kb(i) - 1 if j == n_steps(i) else max(2 * j - 1, 0) for i, j in pairs])
    v_this = as_i32([min(2 * j, 2 * n_steps(i) - 1) for i, j in pairs])
    before =jnp.asarray(np.tril(np.ones((kb // 2, kb // 2), np.float32), -1), _BF16)
    q_map = lambda b, p, qi_r, *_: (b, 0, qi_r[p])
    vt_spec = lambda which: pl.BlockSpec(
        (1, 1, ATT_HEADS, V_ROWS, kb), lambda b, p, *refs: (b, refs[which][p], 0, 0, 0))
    grid_spec = pltpu.PrefetchScalarGridSpec(
        num_scalar_prefetch=5,
        grid=(bsz, len(pairs)),
        in_specs=[
            pl.BlockSpec((1, ATT_W, qb), q_map),
            pl.BlockSpec((1, IDX_W, qb), q_map),
            pl.BlockSpec((1, SUBLANES, qb), q_map),
            pl.BlockSpec((1, seq, LANES), lambda b, p, *_: (b, 0, 0)),
            pl.BlockSpec(before.shape, lambda b, p, *_: (0, 0)),
            pl.BlockSpec((1, 2 * kb, ATT_W), lambda b, p, *refs: (b, refs[2][p], 0)),
            vt_spec(3),
            vt_spec(4),
        ],
        out_specs=pl.BlockSpec((1, ATT_W, qb), q_map),
        scratch_shapes=[
            pltpu.VMEM((seq // kb, kb, qb), _I16),
            pltpu.VMEM((seq // kb, kb, qb), _I16),
            pltpu.VMEM((-(-(seq // kb) // CAP_CHUNKS_PER_BLOCK), kb, qb), _I16),
            pltpu.VMEM((-(-(seq // kb) // CAP_CHUNKS_PER_BLOCK), kb, qb), _I16),
            pltpu.VMEM((seq // kb, kb, qb), _BF16),
            pltpu.VMEM((ATT_HEADS, 1, qb), _F32),
            pltpu.VMEM((ATT_HEADS, V_ROWS, qb), _F32),
            pltpu.VMEM((ATT_HEADS, kb, qb), _F32),
            pltpu.VMEM((ATT_HEADS, kb, qb), _F32),
            pltpu.VMEM((ATT_HEADS, 1, qb), _F32),
            pltpu.VMEM((ATT_HEADS, 1, qb), _F32),
        ],
    )
    return pl.pallas_call(
        functools.partial(_dsa_kernel, topk=topk),
        grid_spec=grid_spec,
        out_shape=jax.ShapeDtypeStruct((bsz, ATT_W, seq), _BF16),
        compiler_params=pltpu.CompilerParams(
            dimension_semantics=("arbitrary", "arbitrary"), vmem_limit_bytes=VMEM_LIMIT_BYTES),
    )(qi, kj, k_pair, v_prev, v_this, qt, iqt, iwt, ik, before, k, vt, vt)


def _split3(a):
    hi = a.astype(_BF16)
    r1 = a - hi.astype(_F32)
    mid = r1.astype(_BF16)
    lo = (r1 - mid.astype(_F32)).astype(_BF16)
    return hi, mid, lo


def _hgrn_kernel(lbl_ref, g_ref, hq_ref, hf_ref, hi_ref, hg_ref, o_ref,
                 state_scr, kk_scr, b_scr, o_scr, *, layer):
    rows, ch, sb = HG_ROWS, HG_CHUNK, HG_SUB
    n_sub = ch // sb

    @pl.when(pl.program_id(1) == 0)
    def _reset():
        state_scr[...] = jnp.zeros_like(state_scr)

    lbl = lbl_ref[...]
    e = jnp.exp(lbl - jnp.max(lbl, axis=0, keepdims=True))
    lb = jnp.sum(e[:layer + 1], axis=0, keepdims=True) / jnp.sum(e, axis=0, keepdims=True)

    f = lb + (1.0 - lb) * jax.nn.sigmoid(hf_ref[0])
    kk_scr[...] = 1.0 - f
    logf = jnp.log(f)
    r_i = lax.broadcasted_iota(_I32, (ch, ch), 0)
    c_i = lax.broadcasted_iota(_I32, (ch, ch), 1)
    lower = jnp.where(c_i <= r_i, 1.0, 0.0).astype(_BF16)
    for c in range(rows // ch):
        parts = _split3(logf[c * ch:(c + 1) * ch])
        b_scr[c * ch:(c + 1) * ch, :] = sum(_dot(lower, part) for part in parts)

    t_idx = lax.broadcasted_iota(_I32, (sb, 1), 0)
    row_idx = lax.broadcasted_iota(_I32, (ch, 1), 0)

    def chunk(c, carry, bounded):
        r0 = pl.multiple_of(c * ch, ch)
        cs = pl.ds(r0, ch)
        b = b_scr[cs, :]
        kk = kk_scr[cs, :]
        qv = hq_ref[0, cs, :]
        vv = hi_ref[0, cs, :]
        b_last = b[ch - 1:ch]
        q_in = (qv * jnp.exp(b)).astype(_BF16)
        k_out = kk * jnp.exp(b_last - b)
        vb = vv.astype(_BF16)

        a_off = [[] for _ in range(HG_HEADS)]
        sbs = HG_SUB_BOUNDED if bounded else sb
        for s_i in range(ch // sbs):
            if s_i == 0 and not bounded:
                for h in range(HG_HEADS):
                    a_off[h].append(jnp.zeros((sbs, ch), _F32))
                continue
            ref_b = b[s_i * sbs - 1:s_i * sbs] if s_i else jnp.zeros_like(b_last)
            q_s = (qv[s_i * sbs:(s_i + 1) * sbs] * jnp.exp(b[s_i * sbs:(s_i + 1) * sbs] - ref_b))
            n_rows = (s_i + 1) * sbs if bounded else s_i * sbs
            k_s = jnp.where(row_idx < n_rows,
                            kk * jnp.exp(jnp.where(row_idx < n_rows, ref_b - b, 0.0)), 0.0)
            q_s = q_s.astype(_BF16)
            k_s = k_s.astype(_BF16)
            for h in range(HG_HEADS):
                hs = slice(h * HG_KDIM, (h + 1) * HG_KDIM)
                a_off[h].append(_nt_dot(q_s[:, hs], k_s[:, hs]))

        for h in range(HG_HEADS):
            hs = slice(h * HG_KDIM, (h + 1) * HG_KDIM)
            st = state_scr[h]
            o_h = _nt_dot(q_in[:, hs], st.astype(_BF16))
            a_h = jnp.concatenate(a_off[h], axis=0)
            if bounded:
                a_h = jnp.where(c_i <= r_i, a_h, 0.0)
            o_scr[cs, hs] = o_h + _dot(a_h.astype(_BF16), vb[:, hs])
            state_scr[h] = (st * jnp.exp(b_last[:, hs])
                            + _tn_dot(vb[:, hs], k_out[:, hs].astype(_BF16)))
        if bounded:
            return carry

        for s_i in range(n_sub):
            rs = pl.ds(r0 + s_i * sb, sb)
            q_s = hq_ref[0, rs, :]
            b_s = b_scr[rs, :]
            acc = o_scr[rs, :]
            for t in range(sb):
                one = pl.ds(r0 + s_i * sb + t, 1)
                w = q_s * jnp.exp(jnp.minimum(b_s - b_scr[one, :], 0.0)) * kk_scr[one, :]
                v_row = hi_ref[0, one, :]
                parts = []
                for h in range(HG_HEADS):
                    hs = slice(h * HG_KDIM, (h + 1) * HG_KDIM)
                    a = jnp.sum(w[:, hs], axis=1, keepdims=True)
                    parts.append(jnp.where(t_idx >= t, a, 0.0) * v_row[:, hs])
                acc = acc + jnp.concatenate(parts, axis=1)
            o_scr[rs, :] = acc
        return carry

    run = lambda bounded: lambda: lax.fori_loop(
        0, rows // ch, functools.partial(chunk, bounded=bounded), 0)
    lax.cond(jnp.min(logf) >= -HG_MAX_DECAY / HG_SUB_BOUNDED, run(True), run(False))

    o = o_scr[...]
    gate = hg_ref[0]
    gain = g_ref[...]
    for h in range(HG_HEADS):
        hs = slice(h * HG_KDIM, (h + 1) * HG_KDIM)
        oh = o[:, hs]
        oh = oh * lax.rsqrt(jnp.mean(oh * oh, axis=1, keepdims=True) + RMS_EPS) * gain[:, hs]
        gh = gate[:, hs]
        o_ref[0, :, hs] = (oh * (gh * jax.nn.sigmoid(gh))).astype(o_ref.dtype)


def _hgrn2(hq, hf, hi, hg, lb_logits, norm_g, layer):
    bsz, seq, _ = hq.shape
    rows = HG_ROWS
    blk = pl.BlockSpec((1, rows, HG_W), lambda b, t: (b, t, 0))
    lbl = lb_logits.reshape(lb_logits.shape[0], HG_W)
    gain = norm_g.reshape(1, HG_W)
    return pl.pallas_call(
        functools.partial(_hgrn_kernel, layer=layer),
        grid=(bsz, seq // rows),
        in_specs=[pl.BlockSpec(lbl.shape, lambda b, t: (0, 0)),
                  pl.BlockSpec(gain.shape, lambda b, t: (0, 0)),
                  blk, blk, blk, blk],
        out_specs=blk,
        out_shape=jax.ShapeDtypeStruct((bsz, seq, HG_W), _BF16),
        scratch_shapes=[
            pltpu.VMEM((HG_HEADS, HG_KDIM, HG_KDIM), _F32),
            pltpu.VMEM((rows, HG_W), _F32),
            pltpu.VMEM((rows, HG_W), _F32),
            pltpu.VMEM((rows, HG_W), _F32),
        ],
        compiler_params=pltpu.CompilerParams(
            dimension_semantics=("arbitrary", "arbitrary"), vmem_limit_bytes=VMEM_LIMIT_BYTES),
    )(lbl, gain, hq, hf, hi, hg)


def _layer_norm(y, g, b):
    mu = jnp.mean(y, axis=1, keepdims=True)
    yc = y - mu
    var = jnp.mean(yc * yc, axis=1, keepdims=True)
    return yc * lax.rsqrt(var + LN_EPS) * g + b


def _ffn_kernel(x_ref, att_ref, hgo_ref, wo_ref, g1_ref, b1_ref, wu_ref, wd_ref,
                g2_ref, b2_ref, o_ref, *, alpha):
    mix = (_tn_dot(att_ref[0], wo_ref[:ATT_W, :]) + _dot(hgo_ref[...], wo_ref[ATT_W:, :]))
    y1 = _layer_norm(alpha * x_ref[...] + mix, g1_ref[...], b1_ref[...])
    y1b = y1.astype(_BF16)
    h = jnp.zeros_like(y1)
    for c in range(wu_ref.shape[1] // FFN_COLS):
        cs = slice(c * FFN_COLS, (c + 1) * FFN_COLS)
        u = jnp.maximum(_dot(y1b, wu_ref[:, cs]), 0.0)
        h = h + _dot((u * u).astype(_BF16), wd_ref[cs, :])
    o_ref[...] = _layer_norm(alpha * y1 + h, g2_ref[...], b2_ref[...])


def _out_ffn(x2d, att_t, hgo2d, w_o, g1, b1, w_up, w_down, g2, b2, alpha):
    rows, d = x2d.shape
    seq = att_t.shape[2]
    tm = FFN_ROWS
    n_seq_blocks = seq // tm
    row_spec = lambda w: pl.BlockSpec((tm, w), lambda i: (i, 0))
    full_spec = lambda a: pl.BlockSpec(a.shape, lambda i: (0, 0), pipeline_mode=pl.Buffered(1))
    att_spec = pl.BlockSpec((1, ATT_W, tm), lambda i: (i // n_seq_blocks, 0, i % n_seq_blocks))
    vec = lambda a: a.reshape(1, d).astype(_F32)
    args = (x2d, att_t, hgo2d, w_o.astype(_BF16), vec(g1), vec(b1),
            w_up.astype(_BF16), w_down.astype(_BF16), vec(g2), vec(b2))
    in_specs = [row_spec(d), att_spec, row_spec(HG_W)] + [full_spec(a) for a in args[3:]]
    return pl.pallas_call(
        functools.partial(_ffn_kernel, alpha=alpha),
        grid=(rows // tm,),
        in_specs=in_specs,
        out_specs=row_spec(d),
        out_shape=jax.ShapeDtypeStruct((rows, d), _F32),
        compiler_params=pltpu.CompilerParams(
            dimension_semantics=("arbitrary",), vmem_limit_bytes=VMEM_LIMIT_BYTES),
    )(*args)


def kernel(x, w_in, w_o, lb_logits, hg_norm_g, ln1_g, ln1_b, w_up, w_down, ln2_g, ln2_b):
    bsz, seq, d = x.shape
    depth = w_in.shape[0]
    alpha = (2.0 * depth) ** 0.25
    x2d = x.reshape(bsz * seq, d)
    for l in range(depth):
        qt, k, vt, iqt, ik, iwt, hq, hf, hi, hg = _project(x2d, w_in[l], seq)
        r3 = lambda a: a.reshape(bsz, seq, a.shape[-1])
        att_t = _dsa_attention(qt, r3(k), vt, iqt, r3(ik), iwt)
        hgo = _hgrn2(r3(hq), r3(hf), r3(hi), r3(hg), lb_logits, hg_norm_g[l], l)
        x2d = _out_ffn(x2d, att_t, hgo.reshape(bsz * seq, HG_W),
                       w_o[l], ln1_g[l], ln1_b[l], w_up[l], w_down[l], ln2_g[l], ln2_b[l], alpha)
    return x2d.reshape(bsz, seq, d)
```

```python
import functools
import math

import numpy as np
import jax
import jax.numpy as jnp
from jax import lax
from jax.experimental import pallas as pl
from jax.experimental.pallas import tpu as pltpu

ATT_HEAD_DIM = 64
ATT_HEADS = 8
ATT_W = ATT_HEADS * ATT_HEAD_DIM
IDX_HEADS = 4
IDX_DIM = 64
IDX_W = IDX_HEADS * IDX_DIM
TOPK_MAX = 256
HG_KDIM = 128
HG_HEADS = 4
HG_W = HG_HEADS * HG_KDIM
ROPE_THETA = 10000.0
LN_EPS = 1e-5
RMS_EPS = 1e-6

LANES = 128
SUBLANES = 8
BF16_ROWS = 16
VMEM_LIMIT_BYTES = 56 * 1024 * 1024

PROJ_ROWS = 512
DSA_QB = 256
DSA_KB = 512
V_ROWS = ATT_HEAD_DIM + BF16_ROWS
COUNT_ACCS = 4
CAP_GROUPS = 2
CAP_ROWS = 2 * CAP_GROUPS * BF16_ROWS
CAP_CHUNKS_PER_BLOCK = DSA_KB // CAP_ROWS
FOLD_GROUPS = CAP_ROWS // BF16_ROWS
REFINE_BITS = 6
FOLD_MIN_CHUNKS = 14
FOLD_LONG_CHUNKS = 22
HG_ROWS = 512
HG_CHUNK = 64
HG_SUB = 8
HG_SUB_BOUNDED = 16
HG_MAX_DECAY = 80.0
FFN_ROWS = 512
FFN_COLS = 1024

_F32 = jnp.float32
_BF16 = jnp.bfloat16
_I32 = jnp.int32
_I16 = jnp.int16
_INT_MIN = -(2 ** 31)
_I16_MIN = -(2 ** 15)
_I16_MAX = 2 ** 15 - 1
_LOW_SIGN = 1 << 15
_NEG = -1e30


def _nt_dot(a, b):
    return lax.dot_general(a, b, (((1,), (1,)), ((), ())), preferred_element_type=_F32)


def _tn_dot(a, b):
    return lax.dot_general(a, b, (((0,), (0,)), ((), ())), preferred_element_type=_F32)


def _dot(a, b):
    return jnp.dot(a, b, preferred_element_type=_F32)


def _tree_sum(parts):
    while len(parts) > 1:
        parts = [parts[n] + parts[n + 1] for n in range(0, len(parts) - 1, 2)] + (
            [parts[-1]] if len(parts) % 2 else [])
    return parts[0]


def _rope_group(z, cos, sin_signed):
    lane = lax.broadcasted_iota(_I32, z.shape, 1)
    first_half = (lane % ATT_HEAD_DIM) < (ATT_HEAD_DIM // 2)
    upper = pltpu.roll(z, LANES - ATT_HEAD_DIM // 2, 1)
    lower = pltpu.roll(z, ATT_HEAD_DIM // 2, 1)
    return z * cos + jnp.where(first_half, upper, lower) * sin_signed


def _proj_kernel(x_ref, wa_ref, wh_ref, cos_ref, sin_ref,
                 qt_ref, k_ref, vt_ref, iqt_ref, ik_ref, iwt_ref,
                 hq_ref, hf_ref, hi_ref, hg_ref, *, q_scale, iw_scale):
    xb = x_ref[...].astype(_BF16)
    cos = cos_ref[...]
    sin = sin_ref[...]
    pa = _dot(xb, wa_ref[...])

    def roped(col0, width):
        return [_rope_group(pa[:, col0 + g * LANES: col0 + (g + 1) * LANES], cos, sin)
                for g in range(width // LANES)]

    for g, z in enumerate(roped(0, ATT_W)):
        qt_ref[0, g * LANES:(g + 1) * LANES, :] = (z * q_scale).T.astype(_BF16)
    for g, z in enumerate(roped(ATT_W, ATT_W)):
        k_ref[:, g * LANES:(g + 1) * LANES] = z.astype(_BF16)
    rows = x_ref.shape[0]
    ones_row = lax.broadcasted_iota(_I32, (V_ROWS - ATT_HEAD_DIM, rows), 0) == 0
    for g in range(ATT_W // LANES):
        vt = pa[:, 2 * ATT_W + g * LANES:2 * ATT_W + (g + 1) * LANES].T.astype(_BF16)
        for sub in range(LANES // ATT_HEAD_DIM):
            head = g * (LANES // ATT_HEAD_DIM) + sub
            vt_ref[0, 0, head, :ATT_HEAD_DIM, :] = vt[sub * ATT_HEAD_DIM:(sub + 1) * ATT_HEAD_DIM]
            vt_ref[0, 0, head, ATT_HEAD_DIM:, :] = jnp.where(ones_row, 1.0, 0.0).astype(_BF16)
    for g, z in enumerate(roped(3 * ATT_W, IDX_W)):
        iqt_ref[0, g * LANES:(g + 1) * LANES, :] = z.T.astype(_BF16)
    ik_ref[...] = roped(3 * ATT_W + IDX_W, LANES)[0].astype(_BF16)
    iwt_ref[0] = (pa[:, 3 * ATT_W + IDX_W + LANES:] * iw_scale).T[:SUBLANES]

    ph = _dot(xb, wh_ref[...])
    hq_ref[...] = ph[:, 0 * HG_W:1 * HG_W]
    hf_ref[...] = ph[:, 1 * HG_W:2 * HG_W]
    hi_ref[...] = ph[:, 2 * HG_W:3 * HG_W]
    hg_ref[...] = ph[:, 3 * HG_W:4 * HG_W]


def _rope_tables(seq):
    half = ATT_HEAD_DIM // 2
    inv = np.power(np.float64(ROPE_THETA), -np.arange(half, dtype=np.float64) / half)
    ang = np.arange(seq, dtype=np.float64)[:, None] * inv[None, :]
    cos = np.cos(ang)
    sin = np.sin(ang)
    cos_t = np.tile(np.concatenate([cos, cos], axis=1), (1, LANES // ATT_HEAD_DIM))
    sin_t = np.tile(np.concatenate([-sin, sin], axis=1), (1, LANES // ATT_HEAD_DIM))
    return jnp.asarray(cos_t, _F32), jnp.asarray(sin_t, _F32)


def _project(x2d, w_in, seq):
    rows, d = x2d.shape
    c = 3 * ATT_W + IDX_W
    w_in = w_in.astype(_BF16)
    w_ik = w_in[:, c:c + IDX_DIM]
    w_iw = w_in[:, c + IDX_DIM:c + IDX_DIM + IDX_HEADS]
    wa = jnp.concatenate([w_in[:, :c], w_ik, w_ik,
                          jnp.pad(w_iw, ((0, 0), (0, LANES - IDX_HEADS)))], axis=1)
    wh = w_in[:, c + IDX_DIM + IDX_HEADS:]
    cos_t, sin_t = _rope_tables(seq)
    tm = PROJ_ROWS
    n_seq_blocks = seq // tm
    row_spec = lambda w: pl.BlockSpec((tm, w), lambda i: (i, 0))
    full_spec = lambda a: pl.BlockSpec(a.shape, lambda i: (0, 0), pipeline_mode=pl.Buffered(1))
    pos_spec = pl.BlockSpec((tm, LANES), lambda i: (i % n_seq_blocks, 0))
    bsz = rows // seq
    t_spec = lambda *feat: pl.BlockSpec(
        (1,) + feat + (tm,), lambda i: (i // n_seq_blocks,) + (0,) * len(feat) + (i % n_seq_blocks,))
    out_shapes = [
        jax.ShapeDtypeStruct((bsz, ATT_W, seq), _BF16),
        jax.ShapeDtypeStruct((rows, ATT_W), _BF16),
        jax.ShapeDtypeStruct((bsz, seq // DSA_KB, ATT_HEADS, V_ROWS, DSA_KB), _BF16),
        jax.ShapeDtypeStruct((bsz, IDX_W, seq), _BF16),
        jax.ShapeDtypeStruct((rows, LANES), _BF16),
        jax.ShapeDtypeStruct((bsz, SUBLANES, seq), _F32),
    ] + [jax.ShapeDtypeStruct((rows, HG_W), _F32)] * 4
    tiles_per_chunk = DSA_KB // tm
    vt_spec = pl.BlockSpec(
        (1, 1, ATT_HEADS, V_ROWS, tm),
        lambda i: (i // n_seq_blocks, (i % n_seq_blocks) // tiles_per_chunk, 0, 0, i % tiles_per_chunk))
    out_specs = [t_spec(ATT_W), row_spec(ATT_W), vt_spec, t_spec(IDX_W),
                 row_spec(LANES), t_spec(SUBLANES)] + [row_spec(HG_W)] * 4
    kern = functools.partial(_proj_kernel, q_scale=ATT_HEAD_DIM ** -0.5 * math.log2(math.e),
                             iw_scale=(IDX_HEADS ** -0.5) * (IDX_DIM ** -0.5))
    return pl.pallas_call(
        kern,
        grid=(rows // tm,),
        in_specs=[row_spec(d), full_spec(wa), full_spec(wh), pos_spec, pos_spec],
        out_specs=out_specs,
        out_shape=out_shapes,
        compiler_params=pltpu.CompilerParams(
            dimension_semantics=("arbitrary",), vmem_limit_bytes=VMEM_LIMIT_BYTES),
    )(x2d, wa, wh, cos_t, sin_t)


def _dsa_kernel(qi_ref, kj_ref, ka_ref, vp_ref, vc_ref,
                qt_ref, iqt_ref, iwt_ref, ik_ref, before_ref, k_ref, vt_prev_ref, vt_ref,
                o_ref,
                hi_scr, lo_scr, fold_scr, cap_scr, bias_scr, m_scr, acc_scr,
                s_even, s_odd, bm_even, bm_odd, *, topk):
    del ka_ref, vp_ref, vc_ref
    p = pl.program_id(1)
    i = qi_ref[p]
    j = kj_ref[p]
    qb, kb = DSA_QB, DSA_KB
    n_kb = (i * qb + qb - 1) // kb + 1
    n_steps = (i * qb + qb - 1) // (2 * kb) + 1
    last_pair = 2 * (n_steps - 1)
    has_second = n_kb == 2 * n_steps
    first_head = lax.broadcasted_iota(_I32, (LANES, qb), 0) < ATT_HEAD_DIM

    def one_head(pair_rows, h):
        keep = first_head if h % 2 == 0 else jnp.logical_not(first_head)
        return jnp.where(keep, pair_rows, jnp.zeros_like(pair_rows))

    @pl.when(j == 0)
    def _select():
        iwt = iwt_ref[0]
        qpos = i * qb + lax.broadcasted_iota(_I32, (kb, qb), 1)
        krow = lax.broadcasted_iota(_I32, (kb, qb), 0)
        iq_heads = [one_head(iqt_ref[0, (h // 2) * LANES:(h // 2 + 1) * LANES, :], h)
                    for h in range(IDX_HEADS)]

        def score_chunk(c, causal):
            row0 = pl.multiple_of(c * kb, kb)
            ikc = ik_ref[0, pl.ds(row0, kb), :]
            score = jnp.zeros((kb, qb), _F32)
            for h in range(IDX_HEADS):
                logits = _dot(ikc, iq_heads[h])
                score = score + iwt[h:h + 1, :] * jnp.maximum(logits, 0.0)
            bits = lax.bitcast_convert_type(score, _I32)
            key = bits ^ (((bits >> 31) & 0x7FFFFFFF) ^ _LOW_SIGN)
            if causal:
                key = jnp.where(row0 + krow <= qpos, key, _INT_MIN ^ _LOW_SIGN)
            hi = (key >> 16).astype(_I16)
            hi_scr[c] = hi
            lo_scr[c] = key.astype(_I16)
            n_tiles = kb // BF16_ROWS
            row0_fold = pl.multiple_of((c % CAP_CHUNKS_PER_BLOCK) * CAP_ROWS, CAP_ROWS)
            for g in range(FOLD_GROUPS):
                tiles = [hi[r * BF16_ROWS:(r + 1) * BF16_ROWS] for r in range(g, n_tiles, FOLD_GROUPS)]
                while len(tiles) > 1:
                    tiles = [jnp.where(tiles[n] > tiles[n + 1], tiles[n], tiles[n + 1])
                             for n in range(0, len(tiles), 2)]
                fold_scr[c // CAP_CHUNKS_PER_BLOCK, pl.ds(row0_fold + g * BF16_ROWS, BF16_ROWS), :] = tiles[0]

        def score_pair(c2, carry):
            score_chunk(2 * c2, False)
            score_chunk(2 * c2 + 1, False)
            return carry

        fold_scr[...] = jnp.full_like(fold_scr, _I16_MIN)
        lax.fori_loop(0, n_steps - 1, score_pair, 0)
        score_chunk(last_pair, True)

        @pl.when(has_second)
        def _score_second():
            score_chunk(last_pair + 1, True)

        @pl.when(jnp.logical_not(has_second))
        def _blank_second():
            hi_scr[last_pair + 1] = jnp.full((kb, qb), _I16_MIN, _I16)
            lo_scr[last_pair + 1] = jnp.full((kb, qb), _I16_MIN, _I16)

        tile_rows = lambda r: slice(r * BF16_ROWS, (r + 1) * BF16_ROWS)
        i16_min = jnp.int16(_I16_MIN)

        def count(ref, cand, n_trips, blocks_per_trip, strict=False):
            cand_rows = jnp.broadcast_to(cand.astype(_I16), (BF16_ROWS, qb))
            one, zero = jnp.int16(1), jnp.int16(0)

            def body(t, accs):
                accs = list(accs)
                for u in range(blocks_per_trip):
                    for r in range(kb // BF16_ROWS):
                        tile = ref[blocks_per_trip * t + u, tile_rows(r), :]
                        hit = tile > cand_rows if strict else tile >= cand_rows
                        accs[r % len(accs)] = accs[r % len(accs)] + jnp.where(hit, one, zero)
                return tuple(accs)

            accs = lax.fori_loop(0, n_trips, body, (jnp.zeros((BF16_ROWS, qb), _I16),) * COUNT_ACCS)
            return jnp.sum(_tree_sum(list(accs)).astype(_I32), axis=0, keepdims=True)

        def bisect(ref, n_above, n_trips, blocks_per_trip):
            def body(step, carry):
                val, n_next = carry
                bit = jnp.left_shift(jnp.int32(1), 15 - step)
                cand = jnp.where(step == 0, 0, val | bit)
                cnt = n_above + count(ref, cand, n_trips, blocks_per_trip)
                ok = cnt >= topk
                return jnp.where(ok, cand, val), jnp.where(ok, n_next, cnt)
            return lax.fori_loop(0, 16, body, (jnp.full((1, qb), _I16_MIN, _I32), n_above))

        n_cap = (n_kb + CAP_CHUNKS_PER_BLOCK - 1) // CAP_CHUNKS_PER_BLOCK
        no_keys = jnp.zeros((1, qb), _I32)

        def high_full():
            return bisect(hi_scr, no_keys, n_steps, 2)

        def high_from_fold():
            base, _ = bisect(fold_scr, no_keys, n_cap, 1)
            n_bits = jnp.where(n_kb >= FOLD_LONG_CHUNKS, REFINE_BITS - 1, REFINE_BITS)

            def count_from(cand):
                cnt = count(hi_scr, jnp.minimum(cand, _I16_MAX), n_steps, 2)
                return jnp.where(cand > _I16_MAX, 0, cnt)

            n_beyond = count_from(base + jnp.left_shift(jnp.int32(1), n_bits))

            def refine():
                def body(step, carry):
                    off, n_next = carry
                    cand_off = off | jnp.left_shift(jnp.int32(1), n_bits - 1 - step)
                    cnt = count_from(base + cand_off)
                    ok = cnt >= topk
                    return jnp.where(ok, cand_off, off), jnp.where(ok, n_next, cnt)
                off, n_next = lax.fori_loop(0, n_bits, body, (no_keys, n_beyond))
                return base + off, n_next

            n_outside = jnp.sum(jnp.where(n_beyond >= topk, 1.0, 0.0))
            return lax.cond(n_outside > 0.0, high_full, refine)

        t_hi, n_gt_hi = lax.cond(n_kb >= FOLD_MIN_CHUNKS, high_from_fold, high_full)
        t_hi_tile = jnp.broadcast_to(t_hi.astype(_I16), (BF16_ROWS, qb))

        cap_scr[...] = jnp.full_like(cap_scr, _I16_MIN)

        def capture_block(c, carry):
            tops = [[jnp.full((BF16_ROWS, qb), _I16_MIN, _I16)] * 2 for _ in range(CAP_GROUPS)]
            for r in range(kb // BF16_ROWS):
                x = jnp.where(hi_scr[c, tile_rows(r), :] == t_hi_tile, lo_scr[c, tile_rows(r), :], i16_min)
                lo_scr[c, tile_rows(r), :] = x
                first, second = tops[r % CAP_GROUPS]
                above = x > first
                tops[r % CAP_GROUPS] = [jnp.where(above, x, first),
                                        jnp.where(above, first, jnp.where(x > second, x, second))]
            row0 = pl.multiple_of((c % CAP_CHUNKS_PER_BLOCK) * CAP_ROWS, CAP_ROWS)
            for g in range(CAP_GROUPS):
                for t in range(2):
                    cap_scr[c // CAP_CHUNKS_PER_BLOCK,
                            pl.ds(row0 + (2 * g + t) * BF16_ROWS, BF16_ROWS), :] = tops[g][t]
            return carry

        lax.fori_loop(0, n_kb, capture_block, 0)
        t_lo, _ = bisect(cap_scr, n_gt_hi, n_cap, 1)
        n_gt = n_gt_hi + count(lo_scr, t_lo, n_steps, 2, strict=True)
        n_wrong = jnp.sum(jnp.where(n_gt >= topk, 1.0, 0.0))
        t_lo, n_gt = lax.cond(n_wrong > 0.0,
                              lambda: bisect(lo_scr, n_gt_hi, n_steps, 2),
                              lambda: (t_lo, n_gt))
        t_hi_rows = jnp.broadcast_to(t_hi.astype(_I16), (kb, qb))
        t_lo = jnp.where((t_hi == _I16_MIN) & (t_lo == _I16_MIN), _I16_MIN + 1, t_lo)
        t_lo_rows = jnp.broadcast_to(t_lo.astype(_I16), (kb, qb))
        rem_rows = jnp.broadcast_to((topk - n_gt).astype(_I16), (kb, qb))

        def mask_block(c, seen):
            hi = hi_scr[c]
            lo = lo_scr[c]
            tie = (hi == t_hi_rows) & (lo == t_lo_rows)
            tie_count = jnp.where(tie, jnp.asarray(1, _BF16), jnp.asarray(0, _BF16))

            def tie_at(r):
                hit = (hi[r:r + 1].astype(_I32) == t_hi) & (lo[r:r + 1].astype(_I32) == t_lo)
                return jnp.where(hit, 1.0, 0.0)

            half = kb // 2
            before_top = seen + _dot(before_ref[...], tie_count[:half])
            seen_mid = before_top[half - 1:half] + tie_at(half - 1)
            before_bot = seen_mid + _dot(before_ref[...], tie_count[half:])
            ties_before = jnp.concatenate([before_top, before_bot], axis=0)
            allowed = ties_before.astype(_I32).astype(_I16) < rem_rows
            chosen = (hi > t_hi_rows) | (lo > t_lo_rows) | (tie & allowed)
            bias_scr[c] = jnp.where(chosen, jnp.asarray(0, bias_scr.dtype),
                                    jnp.asarray(_NEG, bias_scr.dtype))
            return before_bot[half - 1:half] + tie_at(kb - 1)

        seen = lax.fori_loop(0, n_steps - 1,
                             lambda c2, seen: mask_block(2 * c2 + 1, mask_block(2 * c2, seen)),
                             jnp.zeros((1, qb), _F32))
        seen = mask_block(last_pair, seen)

        @pl.when(has_second)
        def _mask_second():
            mask_block(last_pair + 1, seen)

        m_scr[...] = jnp.full_like(m_scr, _NEG)
        acc_scr[...] = jnp.zeros_like(acc_scr)

    def stage_b(h, s_read, bm_read, vt_read):
        m_old = m_scr[h]
        m_new = jnp.maximum(m_old, bm_read[h])
        pr = jnp.exp2(s_read[h] - m_new).astype(_BF16)
        acc_scr[h] = jnp.exp2(m_old - m_new) * acc_scr[h] + _dot(vt_read[0, 0, h], pr)
        m_scr[h] = m_new

    def phase(k_rows, chunk, s_write, bm_write, s_read, bm_read, vt_read):
        bias = bias_scr[chunk].astype(_F32)

        def stage_a(h):
            pair = slice((h // 2) * LANES, (h // 2 + 1) * LANES)
            s = _dot(k_ref[0, k_rows, pair], one_head(qt_ref[0, pair, :], h)) + bias
            s_write[h] = s
            bm_write[h] = jnp.max(s, axis=0, keepdims=True)

        stage_a(0)
        for h in range(ATT_HEADS):
            if h + 1 < ATT_HEADS:
                stage_a(h + 1)
            if s_read is not None:
                stage_b(h, s_read, bm_read, vt_read)

    @pl.when(j == 0)
    def _fill():
        phase(slice(0, kb), 0, s_even, bm_even, None, None, None)

    @pl.when((j > 0) & (j < n_steps))
    def _first():
        phase(slice(0, kb), 2 * j, s_even, bm_even, s_odd, bm_odd, vt_prev_ref)

    @pl.when(2 * j + 1 < n_kb)
    def _second():
        phase(slice(kb, 2 * kb), 2 * j + 1, s_odd, bm_odd, s_even, bm_even, vt_ref)

    @pl.when((j == n_steps) & has_second)
    def _drain_odd():
        for h in range(ATT_HEADS):
            stage_b(h, s_odd, bm_odd, vt_prev_ref)

    @pl.when((j == n_steps) & jnp.logical_not(has_second))
    def _drain_even():
        for h in range(ATT_HEADS):
            stage_b(h, s_even, bm_even, vt_prev_ref)

    @pl.when(j == n_steps)
    def _finish():
        for h in range(ATT_HEADS):
            a = acc_scr[h]
            o_ref[0, h * ATT_HEAD_DIM:(h + 1) * ATT_HEAD_DIM, :] = (
                a[:ATT_HEAD_DIM] / a[ATT_HEAD_DIM:ATT_HEAD_DIM + 1]).astype(o_ref.dtype)


def _dsa_attention(qt, k, vt, iqt, ik, iwt):
    bsz, seq, _ = k.shape
    qb, kb = DSA_QB, DSA_KB
    topk = min(TOPK_MAX, seq // 4)
    n_qb = seq // qb
    n_steps = lambda i: (i * qb + qb - 1) // (2 * kb) + 1
    pairs = [(i, j) for i in range(n_qb) for j in range(n_steps(i) + 1)]
    as_i32 = lambda vals: jnp.asarray(np.array(vals, np.int32))
    qi = as_i32([i for i, j in pairs])
    kj = as_i32([j for i, j in pairs])
    k_pair = as_i32([min(j, n_steps(i) - 1) for i, j in pairs])
    n_kb = lambda i: (i * qb + qb - 1) // kb + 1
    v_prev = as_i32([n_kb(i) - 1 if j == n_steps(i) else max(2 * j - 1, 0) for i, j in pairs])
    v_this = as_i32([min(2 * j, 2 * n_steps(i) - 1) for i, j in pairs])
    before =jnp.asarray(np.tril(np.ones((kb // 2, kb // 2), np.float32), -1), _BF16)
    q_map = lambda b, p, qi_r, *_: (b, 0, qi_r[p])
    vt_spec = lambda which: pl.BlockSpec(
        (1, 1, ATT_HEADS, V_ROWS, kb), lambda b, p, *refs: (b, refs[which][p], 0, 0, 0))
    grid_spec = pltpu.PrefetchScalarGridSpec(
        num_scalar_prefetch=5,
        grid=(bsz, len(pairs)),
        in_specs=[
            pl.BlockSpec((1, ATT_W, qb), q_map),
            pl.BlockSpec((1, IDX_W, qb), q_map),
            pl.BlockSpec((1, SUBLANES, qb), q_map),
            pl.BlockSpec((1, seq, LANES), lambda b, p, *_: (b, 0, 0), pipeline_mode=pl.Buffered(1)),
            pl.BlockSpec(before.shape, lambda b, p, *_: (0, 0), pipeline_mode=pl.Buffered(1)),
            pl.BlockSpec((1, 2 * kb, ATT_W), lambda b, p, *refs: (b, refs[2][p], 0)),
            vt_spec(3),
            vt_spec(4),
        ],
        out_specs=pl.BlockSpec((1, ATT_W, qb), q_map),
        scratch_shapes=[
            pltpu.VMEM((seq // kb, kb, qb), _I16),
            pltpu.VMEM((seq // kb, kb, qb), _I16),
            pltpu.VMEM((-(-(seq // kb) // CAP_CHUNKS_PER_BLOCK), kb, qb), _I16),
            pltpu.VMEM((-(-(seq // kb) // CAP_CHUNKS_PER_BLOCK), kb, qb), _I16),
            pltpu.VMEM((seq // kb, kb, qb), _BF16),
            pltpu.VMEM((ATT_HEADS, 1, qb), _F32),
            pltpu.VMEM((ATT_HEADS, V_ROWS, qb), _F32),
            pltpu.VMEM((ATT_HEADS, kb, qb), _F32),
            pltpu.VMEM((ATT_HEADS, kb, qb), _F32),
            pltpu.VMEM((ATT_HEADS, 1, qb), _F32),
            pltpu.VMEM((ATT_HEADS, 1, qb), _F32),
        ],
    )
    return pl.pallas_call(
        functools.partial(_dsa_kernel, topk=topk),
        grid_spec=grid_spec,
        out_shape=jax.ShapeDtypeStruct((bsz, ATT_W, seq), _BF16),
        compiler_params=pltpu.CompilerParams(
            dimension_semantics=("arbitrary", "arbitrary"), vmem_limit_bytes=VMEM_LIMIT_BYTES),
    )(qi, kj, k_pair, v_prev, v_this, qt, iqt, iwt, ik, before, k, vt, vt)


def _split3(a):
    hi = a.astype(_BF16)
    r1 = a - hi.astype(_F32)
    mid = r1.astype(_BF16)
    lo = (r1 - mid.astype(_F32)).astype(_BF16)
    return hi, mid, lo


def _hgrn_kernel(lbl_ref, g_ref, hq_ref, hf_ref, hi_ref, hg_ref, o_ref,
                 state_scr, kk_scr, b_scr, o_scr, *, layer):
    rows, ch, sb = HG_ROWS, HG_CHUNK, HG_SUB
    n_sub = ch // sb

    @pl.when(pl.program_id(1) == 0)
    def _reset():
        state_scr[...] = jnp.zeros_like(state_scr)

    lbl = lbl_ref[...]
    e = jnp.exp(lbl - jnp.max(lbl, axis=0, keepdims=True))
    lb = jnp.sum(e[:layer + 1], axis=0, keepdims=True) / jnp.sum(e, axis=0, keepdims=True)

    f = lb + (1.0 - lb) * jax.nn.sigmoid(hf_ref[0])
    kk_scr[...] = 1.0 - f
    logf = jnp.log(f)
    r_i = lax.broadcasted_iota(_I32, (ch, ch), 0)
    c_i = lax.broadcasted_iota(_I32, (ch, ch), 1)
    lower = jnp.where(c_i <= r_i, 1.0, 0.0).astype(_BF16)
    for c in range(rows // ch):
        parts = _split3(logf[c * ch:(c + 1) * ch])
        b_scr[c * ch:(c + 1) * ch, :] = sum(_dot(lower, part) for part in parts)

    t_idx = lax.broadcasted_iota(_I32, (sb, 1), 0)
    row_idx = lax.broadcasted_iota(_I32, (ch, 1), 0)

    def chunk(c, carry, bounded):
        r0 = pl.multiple_of(c * ch, ch)
        cs = pl.ds(r0, ch)
        b = b_scr[cs, :]
        kk = kk_scr[cs, :]
        qv = hq_ref[0, cs, :]
        vv = hi_ref[0, cs, :]
        b_last = b[ch - 1:ch]
        q_in = (qv * jnp.exp(b)).astype(_BF16)
        k_out = kk * jnp.exp(b_last - b)
        vb = vv.astype(_BF16)

        a_off = [[] for _ in range(HG_HEADS)]
        sbs = HG_SUB_BOUNDED if bounded else sb
        for s_i in range(ch // sbs):
            if s_i == 0 and not bounded:
                for h in range(HG_HEADS):
                    a_off[h].append(jnp.zeros((sbs, ch), _F32))
                continue
            ref_b = b[s_i * sbs - 1:s_i * sbs] if s_i else jnp.zeros_like(b_last)
            q_s = (qv[s_i * sbs:(s_i + 1) * sbs] * jnp.exp(b[s_i * sbs:(s_i + 1) * sbs] - ref_b))
            n_rows = (s_i + 1) * sbs if bounded else s_i * sbs
            k_s = jnp.where(row_idx < n_rows,
                            kk * jnp.exp(jnp.where(row_idx < n_rows, ref_b - b, 0.0)), 0.0)
            q_s = q_s.astype(_BF16)
            k_s = k_s.astype(_BF16)
            for h in range(HG_HEADS):
                hs = slice(h * HG_KDIM, (h + 1) * HG_KDIM)
                a_off[h].append(_nt_dot(q_s[:, hs], k_s[:, hs]))

        for h in range(HG_HEADS):
            hs = slice(h * HG_KDIM, (h + 1) * HG_KDIM)
            st = state_scr[h]
            o_h = _nt_dot(q_in[:, hs], st.astype(_BF16))
            a_h = jnp.concatenate(a_off[h], axis=0)
            if bounded:
                a_h = jnp.where(c_i <= r_i, a_h, 0.0)
            o_scr[cs, hs] = o_h + _dot(a_h.astype(_BF16), vb[:, hs])
            state_scr[h] = (st * jnp.exp(b_last[:, hs])
                            + _tn_dot(vb[:, hs], k_out[:, hs].astype(_BF16)))
        if bounded:
            return carry

        for s_i in range(n_sub):
            rs = pl.ds(r0 + s_i * sb, sb)
            q_s = hq_ref[0, rs, :]
            b_s = b_scr[rs, :]
            acc = o_scr[rs, :]
            for t in range(sb):
                one = pl.ds(r0 + s_i * sb + t, 1)
                w = q_s * jnp.exp(jnp.minimum(b_s - b_scr[one, :], 0.0)) * kk_scr[one, :]
                v_row = hi_ref[0, one, :]
                parts = []
                for h in range(HG_HEADS):
                    hs = slice(h * HG_KDIM, (h + 1) * HG_KDIM)
                    a = jnp.sum(w[:, hs], axis=1, keepdims=True)
                    parts.append(jnp.where(t_idx >= t, a, 0.0) * v_row[:, hs])
                acc = acc + jnp.concatenate(parts, axis=1)
            o_scr[rs, :] = acc
        return carry

    run = lambda bounded: lambda: lax.fori_loop(
        0, rows // ch, functools.partial(chunk, bounded=bounded), 0)
    lax.cond(jnp.min(logf) >= -HG_MAX_DECAY / HG_SUB_BOUNDED, run(True), run(False))

    o = o_scr[...]
    gate = hg_ref[0]
    gain = g_ref[...]
    for h in range(HG_HEADS):
        hs = slice(h * HG_KDIM, (h + 1) * HG_KDIM)
        oh = o[:, hs]
        oh = oh * lax.rsqrt(jnp.mean(oh * oh, axis=1, keepdims=True) + RMS_EPS) * gain[:, hs]
        gh = gate[:, hs]
        o_ref[0, :, hs] = (oh * (gh * jax.nn.sigmoid(gh))).astype(o_ref.dtype)


def _hgrn2(hq, hf, hi, hg, lb_logits, norm_g, layer):
    bsz, seq, _ = hq.shape
    rows = HG_ROWS
    blk = pl.BlockSpec((1, rows, HG_W), lambda b, t: (b, t, 0))
    lbl = lb_logits.reshape(lb_logits.shape[0], HG_W)
    gain = norm_g.reshape(1, HG_W)
    return pl.pallas_call(
        functools.partial(_hgrn_kernel, layer=layer),
        grid=(bsz, seq // rows),
        in_specs=[pl.BlockSpec(lbl.shape, lambda b, t: (0, 0)),
                  pl.BlockSpec(gain.shape, lambda b, t: (0, 0)),
                  blk, blk, blk, blk],
        out_specs=blk,
        out_shape=jax.ShapeDtypeStruct((bsz, seq, HG_W), _BF16),
        scratch_shapes=[
            pltpu.VMEM((HG_HEADS, HG_KDIM, HG_KDIM), _F32),
            pltpu.VMEM((rows, HG_W), _F32),
            pltpu.VMEM((rows, HG_W), _F32),
            pltpu.VMEM((rows, HG_W), _F32),
        ],
        compiler_params=pltpu.CompilerParams(
            dimension_semantics=("arbitrary", "arbitrary"), vmem_limit_bytes=VMEM_LIMIT_BYTES),
    )(lbl, gain, hq, hf, hi, hg)


def _layer_norm(y, g, b):
    mu = jnp.mean(y, axis=1, keepdims=True)
    yc = y - mu
    var = jnp.mean(yc * yc, axis=1, keepdims=True)
    return yc * lax.rsqrt(var + LN_EPS) * g + b


def _ffn_kernel(x_ref, att_ref, hgo_ref, wo_ref, g1_ref, b1_ref, wu_ref, wd_ref,
                g2_ref, b2_ref, o_ref, *, alpha):
    mix = (_tn_dot(att_ref[0], wo_ref[:ATT_W, :]) + _dot(hgo_ref[...], wo_ref[ATT_W:, :]))
    y1 = _layer_norm(alpha * x_ref[...] + mix, g1_ref[...], b1_ref[...])
    y1b = y1.astype(_BF16)
    h = jnp.zeros_like(y1)
    for c in range(wu_ref.shape[1] // FFN_COLS):
        cs = slice(c * FFN_COLS, (c + 1) * FFN_COLS)
        u = jnp.maximum(_dot(y1b, wu_ref[:, cs]), 0.0)
        h = h + _dot((u * u).astype(_BF16), wd_ref[cs, :])
    o_ref[...] = _layer_norm(alpha * y1 + h, g2_ref[...], b2_ref[...])


def _out_ffn(x2d, att_t, hgo2d, w_o, g1, b1, w_up, w_down, g2, b2, alpha):
    rows, d = x2d.shape
    seq = att_t.shape[2]
    tm = FFN_ROWS
    n_seq_blocks = seq // tm
    row_spec = lambda w: pl.BlockSpec((tm, w), lambda i: (i, 0))
    full_spec = lambda a: pl.BlockSpec(a.shape, lambda i: (0, 0), pipeline_mode=pl.Buffered(1))
    att_spec = pl.BlockSpec((1, ATT_W, tm), lambda i: (i // n_seq_blocks, 0, i % n_seq_blocks))
    vec = lambda a: a.reshape(1, d).astype(_F32)
    args = (x2d, att_t, hgo2d, w_o.astype(_BF16), vec(g1), vec(b1),
            w_up.astype(_BF16), w_down.astype(_BF16), vec(g2), vec(b2))
    in_specs = [row_spec(d), att_spec, row_spec(HG_W)] + [full_spec(a) for a in args[3:]]
    return pl.pallas_call(
        functools.partial(_ffn_kernel, alpha=alpha),
        grid=(rows // tm,),
        in_specs=in_specs,
        out_specs=row_spec(d),
        out_shape=jax.ShapeDtypeStruct((rows, d), _F32),
        compiler_params=pltpu.CompilerParams(
            dimension_semantics=("arbitrary",), vmem_limit_bytes=VMEM_LIMIT_BYTES),
    )(*args)


def kernel(x, w_in, w_o, lb_logits, hg_norm_g, ln1_g, ln1_b, w_up, w_down, ln2_g, ln2_b):
    bsz, seq, d = x.shape
    depth = w_in.shape[0]
    alpha = (2.0 * depth) ** 0.25
    x2d = x.reshape(bsz * seq, d)
    for l in range(depth):
        qt, k, vt, iqt, ik, iwt, hq, hf, hi, hg = _project(x2d, w_in[l], seq)
        r3 = lambda a: a.reshape(bsz, seq, a.shape[-1])
        att_t = _dsa_attention(qt, r3(k), vt, iqt, r3(ik), iwt)
        hgo = _hgrn2(r3(hq), r3(hf), r3(hi), r3(hg), lb_logits, hg_norm_g[l], l)
        x2d = _out_ffn(x2d, att_t, hgo.reshape(bsz * seq, HG_W),
                       w_o[l], ln1_g[l], ln1_b[l], w_up[l], w_down[l], ln2_g[l], ln2_b[l], alpha)
    return x2d.reshape(bsz, seq, d)
```

```python
import functools
import math

import numpy as np
import jax
import jax.numpy as jnp
from jax import lax
from jax.experimental import pallas as pl
from jax.experimental.pallas import tpu as pltpu

ATT_HEAD_DIM = 64
ATT_HEADS = 8
ATT_W = ATT_HEADS * ATT_HEAD_DIM
IDX_HEADS = 4
IDX_DIM = 64
IDX_W = IDX_HEADS * IDX_DIM
TOPK_MAX = 256
HG_KDIM = 128
HG_HEADS = 4
HG_W = HG_HEADS * HG_KDIM
ROPE_THETA = 10000.0
LN_EPS = 1e-5
RMS_EPS = 1e-6

LANES = 128
SUBLANES = 8
BF16_ROWS = 16
VMEM_LIMIT_BYTES = 56 * 1024 * 1024

PROJ_ROWS = 512
DSA_QB = 256
DSA_KB = 512
V_ROWS = ATT_HEAD_DIM + BF16_ROWS
COUNT_ACCS = 4
CAP_GROUPS = 2
CAP_ROWS = 2 * CAP_GROUPS * BF16_ROWS
CAP_CHUNKS_PER_BLOCK = DSA_KB // CAP_ROWS
FOLD_GROUPS = CAP_ROWS // BF16_ROWS
REFINE_BITS = 6
FOLD_MIN_CHUNKS = 14
FOLD_LONG_CHUNKS = 22
HG_ROWS = 512
HG_CHUNK = 64
HG_SUB = 8
HG_SUB_BOUNDED = 16
HG_MAX_DECAY = 80.0
FFN_ROWS = 512
FFN_COLS = 1024

_F32 = jnp.float32
_BF16 = jnp.bfloat16
_I32 = jnp.int32
_I16 = jnp.int16
_INT_MIN = -(2 ** 31)
_I16_MIN = -(2 ** 15)
_I16_MAX = 2 ** 15 - 1
_LOW_SIGN = 1 << 15
_NEG = -1e30


def _nt_dot(a, b):
    return lax.dot_general(a, b, (((1,), (1,)), ((), ())), preferred_element_type=_F32)


def _tn_dot(a, b):
    return lax.dot_general(a, b, (((0,), (0,)), ((), ())), preferred_element_type=_F32)


def _dot(a, b):
    return jnp.dot(a, b, preferred_element_type=_F32)


def _tree_sum(parts):
    while len(parts) > 1:
        parts = [parts[n] + parts[n + 1] for n in range(0, len(parts) - 1, 2)] + (
            [parts[-1]] if len(parts) % 2 else [])
    return parts[0]


def _rope_group(z, cos, sin_signed):
    lane = lax.broadcasted_iota(_I32, z.shape, 1)
    first_half = (lane % ATT_HEAD_DIM) < (ATT_HEAD_DIM // 2)
    upper = pltpu.roll(z, LANES - ATT_HEAD_DIM // 2, 1)
    lower = pltpu.roll(z, ATT_HEAD_DIM // 2, 1)
    return z * cos + jnp.where(first_half, upper, lower) * sin_signed


def _proj_kernel(x_ref, wa_ref, wh_ref, cos_ref, sin_ref,
                 qt_ref, k_ref, vt_ref, iqt_ref, ik_ref, iwt_ref,
                 hq_ref, hf_ref, hi_ref, hg_ref, *, q_scale, iw_scale):
    xb = x_ref[...].astype(_BF16)
    cos = cos_ref[...]
    sin = sin_ref[...]
    pa = _dot(xb, wa_ref[...])

    def roped(col0, width):
        return [_rope_group(pa[:, col0 + g * LANES: col0 + (g + 1) * LANES], cos, sin)
                for g in range(width // LANES)]

    for g, z in enumerate(roped(0, ATT_W)):
        qt_ref[0, g * LANES:(g + 1) * LANES, :] = (z * q_scale).T.astype(_BF16)
    for g, z in enumerate(roped(ATT_W, ATT_W)):
        k_ref[:, g * LANES:(g + 1) * LANES] = z.astype(_BF16)
    rows = x_ref.shape[0]
    ones_row = lax.broadcasted_iota(_I32, (V_ROWS - ATT_HEAD_DIM, rows), 0) == 0
    for g in range(ATT_W // LANES):
        vt = pa[:, 2 * ATT_W + g * LANES:2 * ATT_W + (g + 1) * LANES].T.astype(_BF16)
        for sub in range(LANES // ATT_HEAD_DIM):
            head = g * (LANES // ATT_HEAD_DIM) + sub
            vt_ref[0, 0, head, :ATT_HEAD_DIM, :] = vt[sub * ATT_HEAD_DIM:(sub + 1) * ATT_HEAD_DIM]
            vt_ref[0, 0, head, ATT_HEAD_DIM:, :] = jnp.where(ones_row, 1.0, 0.0).astype(_BF16)
    for g, z in enumerate(roped(3 * ATT_W, IDX_W)):
        iqt_ref[0, g * LANES:(g + 1) * LANES, :] = z.T.astype(_BF16)
    ik_ref[...] = roped(3 * ATT_W + IDX_W, LANES)[0].astype(_BF16)
    iwt_ref[0] = (pa[:, 3 * ATT_W + IDX_W + LANES:] * iw_scale).T[:SUBLANES]

    ph = _dot(xb, wh_ref[...])
    hq_ref[...] = ph[:, 0 * HG_W:1 * HG_W]
    hf_ref[...] = ph[:, 1 * HG_W:2 * HG_W]
    hi_ref[...] = ph[:, 2 * HG_W:3 * HG_W]
    hg_ref[...] = ph[:, 3 * HG_W:4 * HG_W]


def _rope_tables(seq):
    half = ATT_HEAD_DIM // 2
    inv = np.power(np.float64(ROPE_THETA), -np.arange(half, dtype=np.float64) / half)
    ang = np.arange(seq, dtype=np.float64)[:, None] * inv[None, :]
    cos = np.cos(ang)
    sin = np.sin(ang)
    cos_t = np.tile(np.concatenate([cos, cos], axis=1), (1, LANES // ATT_HEAD_DIM))
    sin_t = np.tile(np.concatenate([-sin, sin], axis=1), (1, LANES // ATT_HEAD_DIM))
    return jnp.asarray(cos_t, _F32), jnp.asarray(sin_t, _F32)


def _project(x2d, w_in, seq):
    rows, d = x2d.shape
    c = 3 * ATT_W + IDX_W
    w_in = w_in.astype(_BF16)
    w_ik = w_in[:, c:c + IDX_DIM]
    w_iw = w_in[:, c + IDX_DIM:c + IDX_DIM + IDX_HEADS]
    wa = jnp.concatenate([w_in[:, :c], w_ik, w_ik,
                          jnp.pad(w_iw, ((0, 0), (0, LANES - IDX_HEADS)))], axis=1)
    wh = w_in[:, c + IDX_DIM + IDX_HEADS:]
    cos_t, sin_t = _rope_tables(seq)
    tm = PROJ_ROWS
    n_seq_blocks = seq // tm
    row_spec = lambda w: pl.BlockSpec((tm, w), lambda i: (i, 0))
    full_spec = lambda a: pl.BlockSpec(a.shape, lambda i: (0, 0), pipeline_mode=pl.Buffered(1))
    pos_spec = pl.BlockSpec((tm, LANES), lambda i: (i % n_seq_blocks, 0))
    bsz = rows // seq
    t_spec = lambda *feat: pl.BlockSpec(
        (1,) + feat + (tm,), lambda i: (i // n_seq_blocks,) + (0,) * len(feat) + (i % n_seq_blocks,))
    out_shapes = [
        jax.ShapeDtypeStruct((bsz, ATT_W, seq), _BF16),
        jax.ShapeDtypeStruct((rows, ATT_W), _BF16),
        jax.ShapeDtypeStruct((bsz, seq // DSA_KB, ATT_HEADS, V_ROWS, DSA_KB), _BF16),
        jax.ShapeDtypeStruct((bsz, IDX_W, seq), _BF16),
        jax.ShapeDtypeStruct((rows, LANES), _BF16),
        jax.ShapeDtypeStruct((bsz, SUBLANES, seq), _F32),
    ] + [jax.ShapeDtypeStruct((rows, HG_W), _F32)] * 4
    tiles_per_chunk = DSA_KB // tm
    vt_spec = pl.BlockSpec(
        (1, 1, ATT_HEADS, V_ROWS, tm),
        lambda i: (i // n_seq_blocks, (i % n_seq_blocks) // tiles_per_chunk, 0, 0, i % tiles_per_chunk))
    out_specs = [t_spec(ATT_W), row_spec(ATT_W), vt_spec, t_spec(IDX_W),
                 row_spec(LANES), t_spec(SUBLANES)] + [row_spec(HG_W)] * 4
    kern = functools.partial(_proj_kernel, q_scale=ATT_HEAD_DIM ** -0.5 * math.log2(math.e),
                             iw_scale=(IDX_HEADS ** -0.5) * (IDX_DIM ** -0.5))
    return pl.pallas_call(
        kern,
        grid=(rows // tm,),
        in_specs=[row_spec(d), full_spec(wa), full_spec(wh), pos_spec, pos_spec],
        out_specs=out_specs,
        out_shape=out_shapes,
        compiler_params=pltpu.CompilerParams(
            dimension_semantics=("arbitrary",), vmem_limit_bytes=VMEM_LIMIT_BYTES),
    )(x2d, wa, wh, cos_t, sin_t)


def _dsa_kernel(qi_ref, kj_ref, ka_ref, vp_ref, vc_ref,
                qt_ref, iqt_ref, iwt_ref, ik_ref, before_ref, k_ref, vt_prev_ref, vt_ref,
                o_ref,
                hi_scr, lo_scr, fold_scr, cap_scr, bias_scr, m_scr, acc_scr,
                s_even, s_odd, bm_even, bm_odd, *, topk):
    del ka_ref, vp_ref, vc_ref
    p = pl.program_id(1)
    i = qi_ref[p]
    j = kj_ref[p]
    qb, kb = DSA_QB, DSA_KB
    n_kb = (i * qb + qb - 1) // kb + 1
    n_steps = (i * qb + qb - 1) // (2 * kb) + 1
    last_pair = 2 * (n_steps - 1)
    has_second = n_kb == 2 * n_steps
    first_head = lax.broadcasted_iota(_I32, (LANES, qb), 0) < ATT_HEAD_DIM

    def one_head(pair_rows, h):
        keep = first_head if h % 2 == 0 else jnp.logical_not(first_head)
        return jnp.where(keep, pair_rows, jnp.zeros_like(pair_rows))

    @pl.when(j == 0)
    def _select():
        iwt = iwt_ref[0]
        qpos = i * qb + lax.broadcasted_iota(_I32, (kb, qb), 1)
        krow = lax.broadcasted_iota(_I32, (kb, qb), 0)
        iq_heads = [one_head(iqt_ref[0, (h // 2) * LANES:(h // 2 + 1) * LANES, :], h)
                    for h in range(IDX_HEADS)]

        def score_chunk(c, causal):
            row0 = pl.multiple_of(c * kb, kb)
            ikc = ik_ref[0, pl.ds(row0, kb), :]
            score = jnp.zeros((kb, qb), _F32)
            for h in range(IDX_HEADS):
                logits = _dot(ikc, iq_heads[h])
                score = score + iwt[h:h + 1, :] * jnp.maximum(logits, 0.0)
            bits = lax.bitcast_convert_type(score, _I32)
            key = bits ^ (((bits >> 31) & 0x7FFFFFFF) ^ _LOW_SIGN)
            if causal:
                key = jnp.where(row0 + krow <= qpos, key, _INT_MIN ^ _LOW_SIGN)
            hi = (key >> 16).astype(_I16)
            hi_scr[c] = hi
            lo_scr[c] = key.astype(_I16)
            n_tiles = kb // BF16_ROWS
            row0_fold = pl.multiple_of((c % CAP_CHUNKS_PER_BLOCK) * CAP_ROWS, CAP_ROWS)
            for g in range(FOLD_GROUPS):
                tiles = [hi[r * BF16_ROWS:(r + 1) * BF16_ROWS] for r in range(g, n_tiles, FOLD_GROUPS)]
                while len(tiles) > 1:
                    tiles = [jnp.where(tiles[n] > tiles[n + 1], tiles[n], tiles[n + 1])
                             for n in range(0, len(tiles), 2)]
                fold_scr[c // CAP_CHUNKS_PER_BLOCK, pl.ds(row0_fold + g * BF16_ROWS, BF16_ROWS), :] = tiles[0]

        def score_pair(c2, carry):
            score_chunk(2 * c2, False)
            score_chunk(2 * c2 + 1, False)
            return carry

        fold_scr[...] = jnp.full_like(fold_scr, _I16_MIN)
        lax.fori_loop(0, n_steps - 1, score_pair, 0)
        score_chunk(last_pair, True)

        @pl.when(has_second)
        def _score_second():
            score_chunk(last_pair + 1, True)

        @pl.when(jnp.logical_not(has_second))
        def _blank_second():
            hi_scr[last_pair + 1] = jnp.full((kb, qb), _I16_MIN, _I16)
            lo_scr[last_pair + 1] = jnp.full((kb, qb), _I16_MIN, _I16)

        tile_rows = lambda r: slice(r * BF16_ROWS, (r + 1) * BF16_ROWS)
        i16_min = jnp.int16(_I16_MIN)

        def count(ref, cand, n_trips, blocks_per_trip, strict=False):
            cand_rows = jnp.broadcast_to(cand.astype(_I16), (BF16_ROWS, qb))
            one, zero = jnp.int16(1), jnp.int16(0)

            def body(t, accs):
                accs = list(accs)
                for u in range(blocks_per_trip):
                    for r in range(kb // BF16_ROWS):
                        tile = ref[blocks_per_trip * t + u, tile_rows(r), :]
                        hit = tile > cand_rows if strict else tile >= cand_rows
                        accs[r % len(accs)] = accs[r % len(accs)] + jnp.where(hit, one, zero)
                return tuple(accs)

            accs = lax.fori_loop(0, n_trips, body, (jnp.zeros((BF16_ROWS, qb), _I16),) * COUNT_ACCS)
            return jnp.sum(_tree_sum(list(accs)).astype(_I32), axis=0, keepdims=True)

        def bisect(ref, n_above, n_trips, blocks_per_trip):
            def body(step, carry):
                val, n_next = carry
                bit = jnp.left_shift(jnp.int32(1), 15 - step)
                cand = jnp.where(step == 0, 0, val | bit)
                cnt = n_above + count(ref, cand, n_trips, blocks_per_trip)
                ok = cnt >= topk
                return jnp.where(ok, cand, val), jnp.where(ok, n_next, cnt)
            return lax.fori_loop(0, 16, body, (jnp.full((1, qb), _I16_MIN, _I32), n_above))

        n_cap = (n_kb + CAP_CHUNKS_PER_BLOCK - 1) // CAP_CHUNKS_PER_BLOCK
        no_keys = jnp.zeros((1, qb), _I32)

        def high_full():
            return bisect(hi_scr, no_keys, n_steps, 2)

        def high_from_fold():
            base, _ = bisect(fold_scr, no_keys, n_cap, 1)
            n_bits = jnp.where(n_kb >= FOLD_LONG_CHUNKS, REFINE_BITS - 1, REFINE_BITS)

            def count_from(cand):
                cnt = count(hi_scr, jnp.minimum(cand, _I16_MAX), n_steps, 2)
                return jnp.where(cand > _I16_MAX, 0, cnt)

            n_beyond = count_from(base + jnp.left_shift(jnp.int32(1), n_bits))

            def refine():
                def body(step, carry):
                    off, n_next = carry
                    cand_off = off | jnp.left_shift(jnp.int32(1), n_bits - 1 - step)
                    cnt = count_from(base + cand_off)
                    ok = cnt >= topk
                    return jnp.where(ok, cand_off, off), jnp.where(ok, n_next, cnt)
                off, n_next = lax.fori_loop(0, n_bits, body, (no_keys, n_beyond))
                return base + off, n_next

            n_outside = jnp.sum(jnp.where(n_beyond >= topk, 1.0, 0.0))
            return lax.cond(n_outside > 0.0, high_full, refine)

        t_hi, n_gt_hi = lax.cond(n_kb >= FOLD_MIN_CHUNKS, high_from_fold, high_full)
        t_hi_tile = jnp.broadcast_to(t_hi.astype(_I16), (BF16_ROWS, qb))

        cap_scr[...] = jnp.full_like(cap_scr, _I16_MIN)

        def capture_block(c, carry):
            tops = [[jnp.full((BF16_ROWS, qb), _I16_MIN, _I16)] * 2 for _ in range(CAP_GROUPS)]
            for r in range(kb // BF16_ROWS):
                x = jnp.where(hi_scr[c, tile_rows(r), :] == t_hi_tile, lo_scr[c, tile_rows(r), :], i16_min)
                lo_scr[c, tile_rows(r), :] = x
                first, second = tops[r % CAP_GROUPS]
                above = x > first
                tops[r % CAP_GROUPS] = [jnp.where(above, x, first),
                                        jnp.where(above, first, jnp.where(x > second, x, second))]
            row0 = pl.multiple_of((c % CAP_CHUNKS_PER_BLOCK) * CAP_ROWS, CAP_ROWS)
            for g in range(CAP_GROUPS):
                for t in range(2):
                    cap_scr[c // CAP_CHUNKS_PER_BLOCK,
                            pl.ds(row0 + (2 * g + t) * BF16_ROWS, BF16_ROWS), :] = tops[g][t]
            return carry

        lax.fori_loop(0, n_kb, capture_block, 0)
        t_lo, _ = bisect(cap_scr, n_gt_hi, n_cap, 1)
        n_gt = n_gt_hi + count(lo_scr, t_lo, n_steps, 2, strict=True)
        n_wrong = jnp.sum(jnp.where(n_gt >= topk, 1.0, 0.0))
        t_lo, n_gt = lax.cond(n_wrong > 0.0,
                              lambda: bisect(lo_scr, n_gt_hi, n_steps, 2),
                              lambda: (t_lo, n_gt))
        t_hi_rows = jnp.broadcast_to(t_hi.astype(_I16), (kb, qb))
        t_lo = jnp.where((t_hi == _I16_MIN) & (t_lo == _I16_MIN), _I16_MIN + 1, t_lo)
        t_lo_rows = jnp.broadcast_to(t_lo.astype(_I16), (kb, qb))
        rem_rows = jnp.broadcast_to((topk - n_gt).astype(_I16), (kb, qb))

        def mask_block(c, seen):
            hi = hi_scr[c]
            lo = lo_scr[c]
            tie = (hi == t_hi_rows) & (lo == t_lo_rows)
            tie_count = jnp.where(tie, jnp.asarray(1, _BF16), jnp.asarray(0, _BF16))

            def tie_at(r):
                hit = (hi[r:r + 1].astype(_I32) == t_hi) & (lo[r:r + 1].astype(_I32) == t_lo)
                return jnp.where(hit, 1.0, 0.0)

            half = kb // 2
            before_top = seen + _dot(before_ref[...], tie_count[:half])
            seen_mid = before_top[half - 1:half] + tie_at(half - 1)
            before_bot = seen_mid + _dot(before_ref[...], tie_count[half:])
            ties_before = jnp.concatenate([before_top, before_bot], axis=0)
            allowed = ties_before.astype(_I32).astype(_I16) < rem_rows
            chosen = (hi > t_hi_rows) | (lo > t_lo_rows) | (tie & allowed)
            bias_scr[c] = jnp.where(chosen, jnp.asarray(0, bias_scr.dtype),
                                    jnp.asarray(_NEG, bias_scr.dtype))
            return before_bot[half - 1:half] + tie_at(kb - 1)

        seen = lax.fori_loop(0, n_steps - 1,
                             lambda c2, seen: mask_block(2 * c2 + 1, mask_block(2 * c2, seen)),
                             jnp.zeros((1, qb), _F32))
        seen = mask_block(last_pair, seen)

        @pl.when(has_second)
        def _mask_second():
            mask_block(last_pair + 1, seen)

        m_scr[...] = jnp.full_like(m_scr, _NEG)
        acc_scr[...] = jnp.zeros_like(acc_scr)

    def stage_b(h, s_read, bm_read, vt_read):
        m_old = m_scr[h]
        m_new = jnp.maximum(m_old, bm_read[h])
        pr = jnp.exp2(s_read[h] - m_new).astype(_BF16)
        acc_scr[h] = jnp.exp2(m_old - m_new) * acc_scr[h] + _dot(vt_read[0, 0, h], pr)
        m_scr[h] = m_new

    def phase(k_rows, chunk, s_write, bm_write, s_read, bm_read, vt_read):
        bias = bias_scr[chunk].astype(_F32)

        def stage_a(h):
            pair = slice((h // 2) * LANES, (h // 2 + 1) * LANES)
            s = _dot(k_ref[0, k_rows, pair], one_head(qt_ref[0, pair, :], h)) + bias
            s_write[h] = s
            bm_write[h] = jnp.max(s, axis=0, keepdims=True)

        stage_a(0)
        for h in range(ATT_HEADS):
            if h + 1 < ATT_HEADS:
                stage_a(h + 1)
            if s_read is not None:
                stage_b(h, s_read, bm_read, vt_read)

    @pl.when(j == 0)
    def _fill():
        phase(slice(0, kb), 0, s_even, bm_even, None, None, None)

    @pl.when((j > 0) & (j < n_steps))
    def _first():
        phase(slice(0, kb), 2 * j, s_even, bm_even, s_odd, bm_odd, vt_prev_ref)

    @pl.when(2 * j + 1 < n_kb)
    def _second():
        phase(slice(kb, 2 * kb), 2 * j + 1, s_odd, bm_odd, s_even, bm_even, vt_ref)

    @pl.when((j == n_steps) & has_second)
    def _drain_odd():
        for h in range(ATT_HEADS):
            stage_b(h, s_odd, bm_odd, vt_prev_ref)

    @pl.when((j == n_steps) & jnp.logical_not(has_second))
    def _drain_even():
        for h in range(ATT_HEADS):
            stage_b(h, s_even, bm_even, vt_prev_ref)

    @pl.when(j == n_steps)
    def _finish():
        for h in range(ATT_HEADS):
            a = acc_scr[h]
            o_ref[0, h * ATT_HEAD_DIM:(h + 1) * ATT_HEAD_DIM, :] = (
                a[:ATT_HEAD_DIM] / a[ATT_HEAD_DIM:ATT_HEAD_DIM + 1]).astype(o_ref.dtype)


def _dsa_attention(qt, k, vt, iqt, ik, iwt):
    bsz, seq, _ = k.shape
    qb, kb = DSA_QB, DSA_KB
    topk = min(TOPK_MAX, seq // 4)
    n_qb = seq // qb
    n_steps = lambda i: (i * qb + qb - 1) // (2 * kb) + 1
    pairs = [(i, j) for i in range(n_qb) for j in range(n_steps(i) + 1)]
    as_i32 = lambda vals: jnp.asarray(np.array(vals, np.int32))
    qi = as_i32([i for i, j in pairs])
    kj = as_i32([j for i, j in pairs])
    k_pair = as_i32([min(j, n_steps(i) - 1) for i, j in pairs])
    n_kb = lambda i: (i * qb + qb - 1) // kb + 1
    v_prev = as_i32([n_kb(i) - 1 if j == n_steps(i) else max(2 * j - 1, 0) for i, j in pairs])
    v_this = as_i32([min(2 * j, 2 * n_steps(i) - 1) for i, j in pairs])
    before =jnp.asarray(np.tril(np.ones((kb // 2, kb // 2), np.float32), -1), _BF16)
    q_map = lambda b, p, qi_r, *_: (b, 0, qi_r[p])
    vt_spec = lambda which: pl.BlockSpec(
        (1, 1, ATT_HEADS, V_ROWS, kb), lambda b, p, *refs: (b, refs[which][p], 0, 0, 0))
    grid_spec = pltpu.PrefetchScalarGridSpec(
        num_scalar_prefetch=5,
        grid=(bsz, len(pairs)),
        in_specs=[
            pl.BlockSpec((1, ATT_W, qb), q_map),
            pl.BlockSpec((1, IDX_W, qb), q_map),
            pl.BlockSpec((1, SUBLANES, qb), q_map),
            pl.BlockSpec((1, seq, LANES), lambda b, p, *_: (b, 0, 0)),
            pl.BlockSpec(before.shape, lambda b, p, *_: (0, 0)),
            pl.BlockSpec((1, 2 * kb, ATT_W), lambda b, p, *refs: (b, refs[2][p], 0)),
            vt_spec(3),
            vt_spec(4),
        ],
        out_specs=pl.BlockSpec((1, ATT_W, qb), q_map),
        scratch_shapes=[
            pltpu.VMEM((seq // kb, kb, qb), _I16),
            pltpu.VMEM((seq // kb, kb, qb), _I16),
            pltpu.VMEM((-(-(seq // kb) // CAP_CHUNKS_PER_BLOCK), kb, qb), _I16),
            pltpu.VMEM((-(-(seq // kb) // CAP_CHUNKS_PER_BLOCK), kb, qb), _I16),
            pltpu.VMEM((seq // kb, kb, qb), _BF16),
            pltpu.VMEM((ATT_HEADS, 1, qb), _F32),
            pltpu.VMEM((ATT_HEADS, V_ROWS, qb), _F32),
            pltpu.VMEM((ATT_HEADS, kb, qb), _F32),
            pltpu.VMEM((ATT_HEADS, kb, qb), _F32),
            pltpu.VMEM((ATT_HEADS, 1, qb), _F32),
            pltpu.VMEM((ATT_HEADS, 1, qb), _F32),
        ],
    )
    return pl.pallas_call(
        functools.partial(_dsa_kernel, topk=topk),
        grid_spec=grid_spec,
        out_shape=jax.ShapeDtypeStruct((bsz, ATT_W, seq), _BF16),
        compiler_params=pltpu.CompilerParams(
            dimension_semantics=("arbitrary", "arbitrary"), vmem_limit_bytes=VMEM_LIMIT_BYTES),
    )(qi, kj, k_pair, v_prev, v_this, qt, iqt, iwt, ik, before, k, vt, vt)


def _split3(a):
    hi = a.astype(_BF16)
    r1 = a - hi.astype(_F32)
    mid = r1.astype(_BF16)
    lo = (r1 - mid.astype(_F32)).astype(_BF16)
    return hi, mid, lo


def _hgrn_kernel(lbl_ref, g_ref, hq_ref, hf_ref, hi_ref, hg_ref, o_ref,
                 state_scr, kk_scr, b_scr, o_scr, *, layer):
    rows, ch, sb = HG_ROWS, HG_CHUNK, HG_SUB
    n_sub = ch // sb

    @pl.when(pl.program_id(1) == 0)
    def _reset():
        state_scr[...] = jnp.zeros_like(state_scr)

    lbl = lbl_ref[...]
    e = jnp.exp(lbl - jnp.max(lbl, axis=0, keepdims=True))
    lb = jnp.sum(e[:layer + 1], axis=0, keepdims=True) / jnp.sum(e, axis=0, keepdims=True)

    f = lb + (1.0 - lb) * jax.nn.sigmoid(hf_ref[0])
    kk_scr[...] = 1.0 - f
    logf = jnp.log(f)
    r_i = lax.broadcasted_iota(_I32, (ch, ch), 0)
    c_i = lax.broadcasted_iota(_I32, (ch, ch), 1)
    lower = jnp.where(c_i <= r_i, 1.0, 0.0).astype(_BF16)
    for c in range(rows // ch):
        parts = _split3(logf[c * ch:(c + 1) * ch])
        b_scr[c * ch:(c + 1) * ch, :] = sum(_dot(lower, part) for part in parts)

    t_idx = lax.broadcasted_iota(_I32, (sb, 1), 0)
    row_idx = lax.broadcasted_iota(_I32, (ch, 1), 0)

    def chunk(c, carry, bounded):
        r0 = pl.multiple_of(c * ch, ch)
        cs = pl.ds(r0, ch)
        b = b_scr[cs, :]
        kk = kk_scr[cs, :]
        qv = hq_ref[0, cs, :]
        vv = hi_ref[0, cs, :]
        b_last = b[ch - 1:ch]
        q_in = (qv * jnp.exp(b)).astype(_BF16)
        k_out = kk * jnp.exp(b_last - b)
        vb = vv.astype(_BF16)

        a_off = [[] for _ in range(HG_HEADS)]
        sbs = HG_SUB_BOUNDED if bounded else sb
        for s_i in range(ch // sbs):
            if s_i == 0 and not bounded:
                for h in range(HG_HEADS):
                    a_off[h].append(jnp.zeros((sbs, ch), _F32))
                continue
            ref_b = b[s_i * sbs - 1:s_i * sbs] if s_i else jnp.zeros_like(b_last)
            q_s = (qv[s_i * sbs:(s_i + 1) * sbs] * jnp.exp(b[s_i * sbs:(s_i + 1) * sbs] - ref_b))
            n_rows = (s_i + 1) * sbs if bounded else s_i * sbs
            k_s = jnp.where(row_idx < n_rows,
                            kk * jnp.exp(jnp.where(row_idx < n_rows, ref_b - b, 0.0)), 0.0)
            q_s = q_s.astype(_BF16)
            k_s = k_s.astype(_BF16)
            for h in range(HG_HEADS):
                hs = slice(h * HG_KDIM, (h + 1) * HG_KDIM)
                a_off[h].append(_nt_dot(q_s[:, hs], k_s[:, hs]))

        for h in range(HG_HEADS):
            hs = slice(h * HG_KDIM, (h + 1) * HG_KDIM)
            st = state_scr[h]
            o_h = _nt_dot(q_in[:, hs], st.astype(_BF16))
            a_h = jnp.concatenate(a_off[h], axis=0)
            if bounded:
                a_h = jnp.where(c_i <= r_i, a_h, 0.0)
            o_scr[cs, hs] = o_h + _dot(a_h.astype(_BF16), vb[:, hs])
            state_scr[h] = (st * jnp.exp(b_last[:, hs])
                            + _tn_dot(vb[:, hs], k_out[:, hs].astype(_BF16)))
        if bounded:
            return carry

        for s_i in range(n_sub):
            rs = pl.ds(r0 + s_i * sb, sb)
            q_s = hq_ref[0, rs, :]
            b_s = b_scr[rs, :]
            acc = o_scr[rs, :]
            for t in range(sb):
                one = pl.ds(r0 + s_i * sb + t, 1)
                w = q_s * jnp.exp(jnp.minimum(b_s - b_scr[one, :], 0.0)) * kk_scr[one, :]
                v_row = hi_ref[0, one, :]
                parts = []
                for h in range(HG_HEADS):
                    hs = slice(h * HG_KDIM, (h + 1) * HG_KDIM)
                    a = jnp.sum(w[:, hs], axis=1, keepdims=True)
                    parts.append(jnp.where(t_idx >= t, a, 0.0) * v_row[:, hs])
                acc = acc + jnp.concatenate(parts, axis=1)
            o_scr[rs, :] = acc
        return carry

    run = lambda bounded: lambda: lax.fori_loop(
        0, rows // ch, functools.partial(chunk, bounded=bounded), 0)
    lax.cond(jnp.min(logf) >= -HG_MAX_DECAY / HG_SUB_BOUNDED, run(True), run(False))

    o = o_scr[...]
    gate = hg_ref[0]
    gain = g_ref[...]
    for h in range(HG_HEADS):
        hs = slice(h * HG_KDIM, (h + 1) * HG_KDIM)
        oh = o[:, hs]
        oh = oh * lax.rsqrt(jnp.mean(oh * oh, axis=1, keepdims=True) + RMS_EPS) * gain[:, hs]
        gh = gate[:, hs]
        o_ref[0, :, hs] = (oh * (gh * jax.nn.sigmoid(gh))).astype(o_ref.dtype)


def _hgrn2(hq, hf, hi, hg, lb_logits, norm_g, layer):
    bsz, seq, _ = hq.shape
    rows = HG_ROWS
    blk = pl.BlockSpec((1, rows, HG_W), lambda b, t: (b, t, 0))
    lbl = lb_logits.reshape(lb_logits.shape[0], HG_W)
    gain = norm_g.reshape(1, HG_W)
    return pl.pallas_call(
        functools.partial(_hgrn_kernel, layer=layer),
        grid=(bsz, seq // rows),
        in_specs=[pl.BlockSpec(lbl.shape, lambda b, t: (0, 0)),
                  pl.BlockSpec(gain.shape, lambda b, t: (0, 0)),
                  blk, blk, blk, blk],
        out_specs=blk,
        out_shape=jax.ShapeDtypeStruct((bsz, seq, HG_W), _BF16),
        scratch_shapes=[
            pltpu.VMEM((HG_HEADS, HG_KDIM, HG_KDIM), _F32),
            pltpu.VMEM((rows, HG_W), _F32),
            pltpu.VMEM((rows, HG_W), _F32),
            pltpu.VMEM((rows, HG_W), _F32),
        ],
        compiler_params=pltpu.CompilerParams(
            dimension_semantics=("arbitrary", "arbitrary"), vmem_limit_bytes=VMEM_LIMIT_BYTES),
    )(lbl, gain, hq, hf, hi, hg)


def _layer_norm(y, g, b):
    mu = jnp.mean(y, axis=1, keepdims=True)
    yc = y - mu
    var = jnp.mean(yc * yc, axis=1, keepdims=True)
    return yc * lax.rsqrt(var + LN_EPS) * g + b


def _ffn_kernel(x_ref, att_ref, hgo_ref, wo_ref, g1_ref, b1_ref, wu_ref, wd_ref,
                g2_ref, b2_ref, o_ref, *, alpha):
    mix = (_tn_dot(att_ref[0], wo_ref[:ATT_W, :]) + _dot(hgo_ref[...], wo_ref[ATT_W:, :]))
    y1 = _layer_norm(alpha * x_ref[...] + mix, g1_ref[...], b1_ref[...])
    y1b = y1.astype(_BF16)
    h = jnp.zeros_like(y1)
    for c in range(wu_ref.shape[1] // FFN_COLS):
        cs = slice(c * FFN_COLS, (c + 1) * FFN_COLS)
        u = jnp.maximum(_dot(y1b, wu_ref[:, cs]), 0.0)
        h = h + _dot((u * u).astype(_BF16), wd_ref[cs, :])
    o_ref[...] = _layer_norm(alpha * y1 + h, g2_ref[...], b2_ref[...])


def _out_ffn(x2d, att_t, hgo2d, w_o, g1, b1, w_up, w_down, g2, b2, alpha):
    rows, d = x2d.shape
    seq = att_t.shape[2]
    tm = FFN_ROWS
    n_seq_blocks = seq // tm
    row_spec = lambda w: pl.BlockSpec((tm, w), lambda i: (i, 0))
    full_spec = lambda a: pl.BlockSpec(a.shape, lambda i: (0, 0), pipeline_mode=pl.Buffered(1))
    att_spec = pl.BlockSpec((1, ATT_W, tm), lambda i: (i // n_seq_blocks, 0, i % n_seq_blocks))
    vec = lambda a: a.reshape(1, d).astype(_F32)
    args = (x2d, att_t, hgo2d, w_o.astype(_BF16), vec(g1), vec(b1),
            w_up.astype(_BF16), w_down.astype(_BF16), vec(g2), vec(b2))
    in_specs = [row_spec(d), att_spec, row_spec(HG_W)] + [full_spec(a) for a in args[3:]]
    return pl.pallas_call(
        functools.partial(_ffn_kernel, alpha=alpha),
        grid=(rows // tm,),
        in_specs=in_specs,
        out_specs=row_spec(d),
        out_shape=jax.ShapeDtypeStruct((rows, d), _F32),
        compiler_params=pltpu.CompilerParams(
            dimension_semantics=("arbitrary",), vmem_limit_bytes=VMEM_LIMIT_BYTES),
    )(*args)


def kernel(x, w_in, w_o, lb_logits, hg_norm_g, ln1_g, ln1_b, w_up, w_down, ln2_g, ln2_b):
    bsz, seq, d = x.shape
    depth = w_in.shape[0]
    alpha = (2.0 * depth) ** 0.25
    x2d = x.reshape(bsz * seq, d)
    for l in range(depth):
        qt, k, vt, iqt, ik, iwt, hq, hf, hi, hg = _project(x2d, w_in[l], seq)
        r3 = lambda a: a.reshape(bsz, seq, a.shape[-1])
        att_t = _dsa_attention(qt, r3(k), vt, iqt, r3(ik), iwt)
        hgo = _hgrn2(r3(hq), r3(hf), r3(hi), r3(hg), lb_logits, hg_norm_g[l], l)
        x2d = _out_ffn(x2d, att_t, hgo.reshape(bsz * seq, HG_W),
                       w_o[l], ln1_g[l], ln1_b[l], w_up[l], w_down[l], ln2_g[l], ln2_b[l], alpha)
    return x2d.reshape(bsz, seq, d)
```

```python
import functools
import math

import numpy as np
import jax
import jax.numpy as jnp
from jax import lax
from jax.experimental import pallas as pl
from jax.experimental.pallas import tpu as pltpu

ATT_HEAD_DIM = 64
ATT_HEADS = 8
ATT_W = ATT_HEADS * ATT_HEAD_DIM
IDX_HEADS = 4
IDX_DIM = 64
IDX_W = IDX_HEADS * IDX_DIM
TOPK_MAX = 256
HG_KDIM = 128
HG_HEADS = 4
HG_W = HG_HEADS * HG_KDIM
ROPE_THETA = 10000.0
LN_EPS = 1e-5
RMS_EPS = 1e-6

LANES = 128
SUBLANES = 8
BF16_ROWS = 16
VMEM_LIMIT_BYTES = 56 * 1024 * 1024

PROJ_ROWS = 512
DSA_QB = 256
DSA_KB = 512
V_ROWS = ATT_HEAD_DIM + BF16_ROWS
COUNT_ACCS = 4
CAP_GROUPS = 2
CAP_ROWS = 2 * CAP_GROUPS * BF16_ROWS
CAP_CHUNKS_PER_BLOCK = DSA_KB // CAP_ROWS
FOLD_GROUPS = CAP_ROWS // BF16_ROWS
REFINE_BITS = 6
FOLD_MIN_CHUNKS = 14
FOLD_LONG_CHUNKS = 22
HG_ROWS = 512
HG_CHUNK = 64
HG_SUB = 8
HG_SUB_BOUNDED = 16
HG_MAX_DECAY = 80.0
FFN_ROWS = 512
FFN_COLS = 1024

_F32 = jnp.float32
_BF16 = jnp.bfloat16
_I32 = jnp.int32
_I16 = jnp.int16
_INT_MIN = -(2 ** 31)
_I16_MIN = -(2 ** 15)
_I16_MAX = 2 ** 15 - 1
_LOW_SIGN = 1 << 15
_NEG = -1e30


def _nt_dot(a, b):
    return lax.dot_general(a, b, (((1,), (1,)), ((), ())), preferred_element_type=_F32)


def _tn_dot(a, b):
    return lax.dot_general(a, b, (((0,), (0,)), ((), ())), preferred_element_type=_F32)


def _dot(a, b):
    return jnp.dot(a, b, preferred_element_type=_F32)


def _tree_sum(parts):
    while len(parts) > 1:
        parts = [parts[n] + parts[n + 1] for n in range(0, len(parts) - 1, 2)] + (
            [parts[-1]] if len(parts) % 2 else [])
    return parts[0]


def _rope_group(z, cos, sin_signed):
    lane = lax.broadcasted_iota(_I32, z.shape, 1)
    first_half = (lane % ATT_HEAD_DIM) < (ATT_HEAD_DIM // 2)
    upper = pltpu.roll(z, LANES - ATT_HEAD_DIM // 2, 1)
    lower = pltpu.roll(z, ATT_HEAD_DIM // 2, 1)
    return z * cos + jnp.where(first_half, upper, lower) * sin_signed


def _proj_kernel(x_ref, wa_ref, wh_ref, cos_ref, sin_ref,
                 qt_ref, k_ref, vt_ref, iqt_ref, ik_ref, iwt_ref,
                 hq_ref, hf_ref, hi_ref, hg_ref, *, q_scale, iw_scale):
    xb = x_ref[...].astype(_BF16)
    cos = cos_ref[...]
    sin = sin_ref[...]
    pa = _dot(xb, wa_ref[...])

    def roped(col0, width):
        return [_rope_group(pa[:, col0 + g * LANES: col0 + (g + 1) * LANES], cos, sin)
                for g in range(width // LANES)]

    for g, z in enumerate(roped(0, ATT_W)):
        qt_ref[0, g * LANES:(g + 1) * LANES, :] = (z * q_scale).T.astype(_BF16)
    for g, z in enumerate(roped(ATT_W, ATT_W)):
        k_ref[:, g * LANES:(g + 1) * LANES] = z.astype(_BF16)
    rows = x_ref.shape[0]
    ones_row = lax.broadcasted_iota(_I32, (V_ROWS - ATT_HEAD_DIM, rows), 0) == 0
    for g in range(ATT_W // LANES):
        vt = pa[:, 2 * ATT_W + g * LANES:2 * ATT_W + (g + 1) * LANES].T.astype(_BF16)
        for sub in range(LANES // ATT_HEAD_DIM):
            head = g * (LANES // ATT_HEAD_DIM) + sub
            vt_ref[0, 0, head, :ATT_HEAD_DIM, :] = vt[sub * ATT_HEAD_DIM:(sub + 1) * ATT_HEAD_DIM]
            vt_ref[0, 0, head, ATT_HEAD_DIM:, :] = jnp.where(ones_row, 1.0, 0.0).astype(_BF16)
    for g, z in enumerate(roped(3 * ATT_W, IDX_W)):
        iqt_ref[0, g * LANES:(g + 1) * LANES, :] = z.T.astype(_BF16)
    ik_ref[...] = roped(3 * ATT_W + IDX_W, LANES)[0].astype(_BF16)
    iwt_ref[0] = (pa[:, 3 * ATT_W + IDX_W + LANES:] * iw_scale).T[:SUBLANES]

    ph = _dot(xb, wh_ref[...])
    hq_ref[...] = ph[:, 0 * HG_W:1 * HG_W]
    hf_ref[...] = ph[:, 1 * HG_W:2 * HG_W]
    hi_ref[...] = ph[:, 2 * HG_W:3 * HG_W]
    hg_ref[...] = ph[:, 3 * HG_W:4 * HG_W]


def _rope_tables(seq):
    half = ATT_HEAD_DIM // 2
    inv = np.power(np.float64(ROPE_THETA), -np.arange(half, dtype=np.float64) / half)
    ang = np.arange(seq, dtype=np.float64)[:, None] * inv[None, :]
    cos = np.cos(ang)
    sin = np.sin(ang)
    cos_t = np.tile(np.concatenate([cos, cos], axis=1), (1, LANES // ATT_HEAD_DIM))
    sin_t = np.tile(np.concatenate([-sin, sin], axis=1), (1, LANES // ATT_HEAD_DIM))
    return jnp.asarray(cos_t, _F32), jnp.asarray(sin_t, _F32)


def _project(x2d, w_in, seq):
    rows, d = x2d.shape
    c = 3 * ATT_W + IDX_W
    w_in = w_in.astype(_BF16)
    w_ik = w_in[:, c:c + IDX_DIM]
    w_iw = w_in[:, c + IDX_DIM:c + IDX_DIM + IDX_HEADS]
    wa = jnp.concatenate([w_in[:, :c], w_ik, w_ik,
                          jnp.pad(w_iw, ((0, 0), (0, LANES - IDX_HEADS)))], axis=1)
    wh = w_in[:, c + IDX_DIM + IDX_HEADS:]
    cos_t, sin_t = _rope_tables(seq)
    tm = PROJ_ROWS
    n_seq_blocks = seq // tm
    row_spec = lambda w: pl.BlockSpec((tm, w), lambda i: (i, 0))
    full_spec = lambda a: pl.BlockSpec(a.shape, lambda i: (0, 0), pipeline_mode=pl.Buffered(1))
    pos_spec = pl.BlockSpec((tm, LANES), lambda i: (i % n_seq_blocks, 0))
    bsz = rows // seq
    t_spec = lambda *feat: pl.BlockSpec(
        (1,) + feat + (tm,), lambda i: (i // n_seq_blocks,) + (0,) * len(feat) + (i % n_seq_blocks,))
    out_shapes = [
        jax.ShapeDtypeStruct((bsz, ATT_W, seq), _BF16),
        jax.ShapeDtypeStruct((rows, ATT_W), _BF16),
        jax.ShapeDtypeStruct((bsz, seq // DSA_KB, ATT_HEADS, V_ROWS, DSA_KB), _BF16),
        jax.ShapeDtypeStruct((bsz, IDX_W, seq), _BF16),
        jax.ShapeDtypeStruct((rows, LANES), _BF16),
        jax.ShapeDtypeStruct((bsz, SUBLANES, seq), _F32),
    ] + [jax.ShapeDtypeStruct((rows, HG_W), _F32)] * 4
    tiles_per_chunk = DSA_KB // tm
    vt_spec = pl.BlockSpec(
        (1, 1, ATT_HEADS, V_ROWS, tm),
        lambda i: (i // n_seq_blocks, (i % n_seq_blocks) // tiles_per_chunk, 0, 0, i % tiles_per_chunk))
    out_specs = [t_spec(ATT_W), row_spec(ATT_W), vt_spec, t_spec(IDX_W),
                 row_spec(LANES), t_spec(SUBLANES)] + [row_spec(HG_W)] * 4
    kern = functools.partial(_proj_kernel, q_scale=ATT_HEAD_DIM ** -0.5 * math.log2(math.e),
                             iw_scale=(IDX_HEADS ** -0.5) * (IDX_DIM ** -0.5))
    return pl.pallas_call(
        kern,
        grid=(rows // tm,),
        in_specs=[row_spec(d), full_spec(wa), full_spec(wh), pos_spec, pos_spec],
        out_specs=out_specs,
        out_shape=out_shapes,
        compiler_params=pltpu.CompilerParams(
            dimension_semantics=("arbitrary",), vmem_limit_bytes=VMEM_LIMIT_BYTES),
    )(x2d, wa, wh, cos_t, sin_t)


def _dsa_kernel(qi_ref, kj_ref, ka_ref, vp_ref, vc_ref,
                qt_ref, iqt_ref, iwt_ref, ik_ref, before_ref, k_ref, vt_prev_ref, vt_ref,
                o_ref,
                hi_scr, lo_scr, fold_scr, cap_scr, bias_scr, m_scr, acc_scr,
                s_even, s_odd, bm_even, bm_odd, *, topk):
    del ka_ref, vp_ref, vc_ref
    p = pl.program_id(1)
    i = qi_ref[p]
    j = kj_ref[p]
    qb, kb = DSA_QB, DSA_KB
    n_kb = (i * qb + qb - 1) // kb + 1
    n_steps = (i * qb + qb - 1) // (2 * kb) + 1
    last_pair = 2 * (n_steps - 1)
    has_second = n_kb == 2 * n_steps
    first_head = lax.broadcasted_iota(_I32, (LANES, qb), 0) < ATT_HEAD_DIM

    def one_head(pair_rows, h):
        keep = first_head if h % 2 == 0 else jnp.logical_not(first_head)
        return jnp.where(keep, pair_rows, jnp.zeros_like(pair_rows))

    @pl.when(j == 0)
    def _select():
        iwt = iwt_ref[0]
        qpos = i * qb + lax.broadcasted_iota(_I32, (kb, qb), 1)
        krow = lax.broadcasted_iota(_I32, (kb, qb), 0)
        iq_heads = [one_head(iqt_ref[0, (h // 2) * LANES:(h // 2 + 1) * LANES, :], h)
                    for h in range(IDX_HEADS)]

        def score_chunk(c, causal):
            row0 = pl.multiple_of(c * kb, kb)
            ikc = ik_ref[0, pl.ds(row0, kb), :]
            score = jnp.zeros((kb, qb), _F32)
            for h in range(IDX_HEADS):
                logits = _dot(ikc, iq_heads[h])
                score = score + iwt[h:h + 1, :] * jnp.maximum(logits, 0.0)
            bits = lax.bitcast_convert_type(score, _I32)
            key = bits ^ (((bits >> 31) & 0x7FFFFFFF) ^ _LOW_SIGN)
            if causal:
                key = jnp.where(row0 + krow <= qpos, key, _INT_MIN ^ _LOW_SIGN)
            hi = (key >> 16).astype(_I16)
            hi_scr[c] = hi
            lo_scr[c] = key.astype(_I16)
            n_tiles = kb // BF16_ROWS
            row0_fold = pl.multiple_of((c % CAP_CHUNKS_PER_BLOCK) * CAP_ROWS, CAP_ROWS)
            for g in range(FOLD_GROUPS):
                tiles = [hi[r * BF16_ROWS:(r + 1) * BF16_ROWS] for r in range(g, n_tiles, FOLD_GROUPS)]
                while len(tiles) > 1:
                    tiles = [jnp.where(tiles[n] > tiles[n + 1], tiles[n], tiles[n + 1])
                             for n in range(0, len(tiles), 2)]
                fold_scr[c // CAP_CHUNKS_PER_BLOCK, pl.ds(row0_fold + g * BF16_ROWS, BF16_ROWS), :] = tiles[0]

        def score_pair(c2, carry):
            score_chunk(2 * c2, False)
            score_chunk(2 * c2 + 1, False)
            return carry

        fold_scr[...] = jnp.full_like(fold_scr, _I16_MIN)
        lax.fori_loop(0, n_steps - 1, score_pair, 0)
        score_chunk(last_pair, True)

        @pl.when(has_second)
        def _score_second():
            score_chunk(last_pair + 1, True)

        @pl.when(jnp.logical_not(has_second))
        def _blank_second():
            hi_scr[last_pair + 1] = jnp.full((kb, qb), _I16_MIN, _I16)
            lo_scr[last_pair + 1] = jnp.full((kb, qb), _I16_MIN, _I16)

        tile_rows = lambda r: slice(r * BF16_ROWS, (r + 1) * BF16_ROWS)
        i16_min = jnp.int16(_I16_MIN)

        def count(ref, cand, n_trips, blocks_per_trip, strict=False):
            cand_rows = jnp.broadcast_to(cand.astype(_I16), (BF16_ROWS, qb))
            one, zero = jnp.int16(1), jnp.int16(0)

            def body(t, accs):
                accs = list(accs)
                for u in range(blocks_per_trip):
                    for r in range(kb // BF16_ROWS):
                        tile = ref[blocks_per_trip * t + u, tile_rows(r), :]
                        hit = tile > cand_rows if strict else tile >= cand_rows
                        accs[r % len(accs)] = accs[r % len(accs)] + jnp.where(hit, one, zero)
                return tuple(accs)

            accs = lax.fori_loop(0, n_trips, body, (jnp.zeros((BF16_ROWS, qb), _I16),) * COUNT_ACCS)
            return jnp.sum(_tree_sum(list(accs)).astype(_I32), axis=0, keepdims=True)

        def bisect(ref, n_above, n_trips, blocks_per_trip):
            def body(step, carry):
                val, n_next = carry
                bit = jnp.left_shift(jnp.int32(1), 15 - step)
                cand = jnp.where(step == 0, 0, val | bit)
                cnt = n_above + count(ref, cand, n_trips, blocks_per_trip)
                ok = cnt >= topk
                return jnp.where(ok, cand, val), jnp.where(ok, n_next, cnt)
            return lax.fori_loop(0, 16, body, (jnp.full((1, qb), _I16_MIN, _I32), n_above))

        n_cap = (n_kb + CAP_CHUNKS_PER_BLOCK - 1) // CAP_CHUNKS_PER_BLOCK
        no_keys = jnp.zeros((1, qb), _I32)

        def high_full():
            return bisect(hi_scr, no_keys, n_steps, 2)

        def high_from_fold():
            base, _ = bisect(fold_scr, no_keys, n_cap, 1)
            n_bits = jnp.where(n_kb >= FOLD_LONG_CHUNKS, REFINE_BITS - 1, REFINE_BITS)

            def count_from(cand):
                cnt = count(hi_scr, jnp.minimum(cand, _I16_MAX), n_steps, 2)
                return jnp.where(cand > _I16_MAX, 0, cnt)

            n_beyond = count_from(base + jnp.left_shift(jnp.int32(1), n_bits))

            def refine():
                def body(step, carry):
                    off, n_next = carry
                    cand_off = off | jnp.left_shift(jnp.int32(1), n_bits - 1 - step)
                    cnt = count_from(base + cand_off)
                    ok = cnt >= topk
                    return jnp.where(ok, cand_off, off), jnp.where(ok, n_next, cnt)
                off, n_next = lax.fori_loop(0, n_bits, body, (no_keys, n_beyond))
                return base + off, n_next

            n_outside = jnp.sum(jnp.where(n_beyond >= topk, 1.0, 0.0))
            return lax.cond(n_outside > 0.0, high_full, refine)

        t_hi, n_gt_hi = lax.cond(n_kb >= FOLD_MIN_CHUNKS, high_from_fold, high_full)
        t_hi_tile = jnp.broadcast_to(t_hi.astype(_I16), (BF16_ROWS, qb))

        cap_scr[...] = jnp.full_like(cap_scr, _I16_MIN)

        def capture_block(c, carry):
            tops = [[jnp.full((BF16_ROWS, qb), _I16_MIN, _I16)] * 2 for _ in range(CAP_GROUPS)]
            for r in range(kb // BF16_ROWS):
                x = jnp.where(hi_scr[c, tile_rows(r), :] == t_hi_tile, lo_scr[c, tile_rows(r), :], i16_min)
                lo_scr[c, tile_rows(r), :] = x
                first, second = tops[r % CAP_GROUPS]
                above = x > first
                tops[r % CAP_GROUPS] = [jnp.where(above, x, first),
                                        jnp.where(above, first, jnp.where(x > second, x, second))]
            row0 = pl.multiple_of((c % CAP_CHUNKS_PER_BLOCK) * CAP_ROWS, CAP_ROWS)
            for g in range(CAP_GROUPS):
                for t in range(2):
                    cap_scr[c // CAP_CHUNKS_PER_BLOCK,
                            pl.ds(row0 + (2 * g + t) * BF16_ROWS, BF16_ROWS), :] = tops[g][t]
            return carry

        lax.fori_loop(0, n_kb, capture_block, 0)
        t_lo, _ = bisect(cap_scr, n_gt_hi, n_cap, 1)
        n_gt = n_gt_hi + count(lo_scr, t_lo, n_steps, 2, strict=True)
        n_wrong = jnp.sum(jnp.where(n_gt >= topk, 1.0, 0.0))
        t_lo, n_gt = lax.cond(n_wrong > 0.0,
                              lambda: bisect(lo_scr, n_gt_hi, n_steps, 2),
                              lambda: (t_lo, n_gt))
        t_hi_rows = jnp.broadcast_to(t_hi.astype(_I16), (kb, qb))
        t_lo = jnp.where((t_hi == _I16_MIN) & (t_lo == _I16_MIN), _I16_MIN + 1, t_lo)
        t_lo_rows = jnp.broadcast_to(t_lo.astype(_I16), (kb, qb))
        rem_rows = jnp.broadcast_to((topk - n_gt).astype(_I16), (kb, qb))

        def mask_block(c, seen):
            hi = hi_scr[c]
            lo = lo_scr[c]
            tie = (hi == t_hi_rows) & (lo == t_lo_rows)
            tie_count = jnp.where(tie, jnp.asarray(1, _BF16), jnp.asarray(0, _BF16))

            def tie_at(r):
                hit = (hi[r:r + 1].astype(_I32) == t_hi) & (lo[r:r + 1].astype(_I32) == t_lo)
                return jnp.where(hit, 1.0, 0.0)

            half = kb // 2
            before_top = seen + _dot(before_ref[...], tie_count[:half])
            seen_mid = before_top[half - 1:half] + tie_at(half - 1)
            before_bot = seen_mid + _dot(before_ref[...], tie_count[half:])
            ties_before = jnp.concatenate([before_top, before_bot], axis=0)
            allowed = ties_before.astype(_I32).astype(_I16) < rem_rows
            chosen = (hi > t_hi_rows) | (lo > t_lo_rows) | (tie & allowed)
            bias_scr[c] = jnp.where(chosen, jnp.asarray(0, bias_scr.dtype),
                                    jnp.asarray(_NEG, bias_scr.dtype))
            return before_bot[half - 1:half] + tie_at(kb - 1)

        seen = lax.fori_loop(0, n_steps - 1,
                             lambda c2, seen: mask_block(2 * c2 + 1, mask_block(2 * c2, seen)),
                             jnp.zeros((1, qb), _F32))
        seen = mask_block(last_pair, seen)

        @pl.when(has_second)
        def _mask_second():
            mask_block(last_pair + 1, seen)

        m_scr[...] = jnp.full_like(m_scr, _NEG)
        acc_scr[...] = jnp.zeros_like(acc_scr)

    def stage_b(h, s_read, bm_read, vt_read):
        m_old = m_scr[h]
        m_new = jnp.maximum(m_old, bm_read[h])
        pr = jnp.exp2(s_read[h] - m_new).astype(_BF16)
        acc_scr[h] = jnp.exp2(m_old - m_new) * acc_scr[h] + _dot(vt_read[0, 0, h], pr)
        m_scr[h] = m_new

    def phase(k_rows, chunk, s_write, bm_write, s_read, bm_read, vt_read):
        bias = bias_scr[chunk].astype(_F32)

        def stage_a(h):
            pair = slice((h // 2) * LANES, (h // 2 + 1) * LANES)
            s = _dot(k_ref[0, k_rows, pair], one_head(qt_ref[0, pair, :], h)) + bias
            s_write[h] = s
            bm_write[h] = jnp.max(s, axis=0, keepdims=True)

        stage_a(0)
        for h in range(ATT_HEADS):
            if h + 1 < ATT_HEADS:
                stage_a(h + 1)
            if s_read is not None:
                stage_b(h, s_read, bm_read, vt_read)

    @pl.when(j == 0)
    def _fill():
        phase(slice(0, kb), 0, s_even, bm_even, None, None, None)

    @pl.when((j > 0) & (j < n_steps))
    def _first():
        phase(slice(0, kb), 2 * j, s_even, bm_even, s_odd, bm_odd, vt_prev_ref)

    @pl.when(2 * j + 1 < n_kb)
    def _second():
        phase(slice(kb, 2 * kb), 2 * j + 1, s_odd, bm_odd, s_even, bm_even, vt_ref)

    @pl.when((j == n_steps) & has_second)
    def _drain_odd():
        for h in range(ATT_HEADS):
            stage_b(h, s_odd, bm_odd, vt_prev_ref)

    @pl.when((j == n_steps) & jnp.logical_not(has_second))
    def _drain_even():
        for h in range(ATT_HEADS):
            stage_b(h, s_even, bm_even, vt_prev_ref)

    @pl.when(j == n_steps)
    def _finish():
        for h in range(ATT_HEADS):
            a = acc_scr[h]
            o_ref[0, h * ATT_HEAD_DIM:(h + 1) * ATT_HEAD_DIM, :] = (
                a[:ATT_HEAD_DIM] / a[ATT_HEAD_DIM:ATT_HEAD_DIM + 1]).astype(o_ref.dtype)


def _dsa_attention(qt, k, vt, iqt, ik, iwt):
    bsz, seq, _ = k.shape
    qb, kb = DSA_QB, DSA_KB
    assert seq % (2 * kb) == 0 and seq % qb == 0 and seq <= _I16_MAX, seq
    topk = min(TOPK_MAX, seq // 4)
    n_qb = seq // qb
    n_steps = lambda i: (i * qb + qb - 1) // (2 * kb) + 1
    pairs = [(i, j) for i in range(n_qb) for j in range(n_steps(i) + 1)]
    as_i32 = lambda vals: jnp.asarray(np.array(vals, np.int32))
    qi = as_i32([i for i, j in pairs])
    kj = as_i32([j for i, j in pairs])
    k_pair = as_i32([min(j, n_steps(i) - 1) for i, j in pairs])
    n_kb = lambda i: (i * qb + qb - 1) // kb + 1
    v_prev = as_i32([n_kb(i) - 1 if j == n_steps(i) else max(2 * j - 1, 0) for i, j in pairs])
    v_this = as_i32([min(2 * j, 2 * n_steps(i) - 1) for i, j in pairs])
    before =jnp.asarray(np.tril(np.ones((kb // 2, kb // 2), np.float32), -1), _BF16)
    q_map = lambda b, p, qi_r, *_: (b, 0, qi_r[p])
    vt_spec = lambda which: pl.BlockSpec(
        (1, 1, ATT_HEADS, V_ROWS, kb), lambda b, p, *refs: (b, refs[which][p], 0, 0, 0))
    grid_spec = pltpu.PrefetchScalarGridSpec(
        num_scalar_prefetch=5,
        grid=(bsz, len(pairs)),
        in_specs=[
            pl.BlockSpec((1, ATT_W, qb), q_map),
            pl.BlockSpec((1, IDX_W, qb), q_map),
            pl.BlockSpec((1, SUBLANES, qb), q_map),
            pl.BlockSpec((1, seq, LANES), lambda b, p, *_: (b, 0, 0)),
            pl.BlockSpec(before.shape, lambda b, p, *_: (0, 0)),
            pl.BlockSpec((1, 2 * kb, ATT_W), lambda b, p, *refs: (b, refs[2][p], 0)),
            vt_spec(3),
            vt_spec(4),
        ],
        out_specs=pl.BlockSpec((1, ATT_W, qb), q_map),
        scratch_shapes=[
            pltpu.VMEM((seq // kb, kb, qb), _I16),
            pltpu.VMEM((seq // kb, kb, qb), _I16),
            pltpu.VMEM((-(-(seq // kb) // CAP_CHUNKS_PER_BLOCK), kb, qb), _I16),
            pltpu.VMEM((-(-(seq // kb) // CAP_CHUNKS_PER_BLOCK), kb, qb), _I16),
            pltpu.VMEM((seq // kb, kb, qb), _BF16),
            pltpu.VMEM((ATT_HEADS, 1, qb), _F32),
            pltpu.VMEM((ATT_HEADS, V_ROWS, qb), _F32),
            pltpu.VMEM((ATT_HEADS, kb, qb), _F32),
            pltpu.VMEM((ATT_HEADS, kb, qb), _F32),
            pltpu.VMEM((ATT_HEADS, 1, qb), _F32),
            pltpu.VMEM((ATT_HEADS, 1, qb), _F32),
        ],
    )
    return pl.pallas_call(
        functools.partial(_dsa_kernel, topk=topk),
        grid_spec=grid_spec,
        out_shape=jax.ShapeDtypeStruct((bsz, ATT_W, seq), _BF16),
        compiler_params=pltpu.CompilerParams(
            dimension_semantics=("arbitrary", "arbitrary"), vmem_limit_bytes=VMEM_LIMIT_BYTES),
    )(qi, kj, k_pair, v_prev, v_this, qt, iqt, iwt, ik, before, k, vt, vt)


def _split3(a):
    hi = a.astype(_BF16)
    r1 = a - hi.astype(_F32)
    mid = r1.astype(_BF16)
    lo = (r1 - mid.astype(_F32)).astype(_BF16)
    return hi, mid, lo


def _hgrn_kernel(lbl_ref, g_ref, hq_ref, hf_ref, hi_ref, hg_ref, o_ref,
                 state_scr, kk_scr, b_scr, o_scr, *, layer):
    rows, ch, sb = HG_ROWS, HG_CHUNK, HG_SUB
    n_sub = ch // sb

    @pl.when(pl.program_id(1) == 0)
    def _reset():
        state_scr[...] = jnp.zeros_like(state_scr)

    lbl = lbl_ref[...]
    e = jnp.exp(lbl - jnp.max(lbl, axis=0, keepdims=True))
    lb = jnp.sum(e[:layer + 1], axis=0, keepdims=True) / jnp.sum(e, axis=0, keepdims=True)

    f = lb + (1.0 - lb) * jax.nn.sigmoid(hf_ref[0])
    kk_scr[...] = 1.0 - f
    logf = jnp.log(f)
    r_i = lax.broadcasted_iota(_I32, (ch, ch), 0)
    c_i = lax.broadcasted_iota(_I32, (ch, ch), 1)
    lower = jnp.where(c_i <= r_i, 1.0, 0.0).astype(_BF16)
    for c in range(rows // ch):
        parts = _split3(logf[c * ch:(c + 1) * ch])
        b_scr[c * ch:(c + 1) * ch, :] = sum(_dot(lower, part) for part in parts)

    t_idx = lax.broadcasted_iota(_I32, (sb, 1), 0)
    row_idx = lax.broadcasted_iota(_I32, (ch, 1), 0)

    def chunk(c, carry, bounded):
        r0 = pl.multiple_of(c * ch, ch)
        cs = pl.ds(r0, ch)
        b = b_scr[cs, :]
        kk = kk_scr[cs, :]
        qv = hq_ref[0, cs, :]
        vv = hi_ref[0, cs, :]
        b_last = b[ch - 1:ch]
        q_in = (qv * jnp.exp(b)).astype(_BF16)
        k_out = kk * jnp.exp(b_last - b)
        vb = vv.astype(_BF16)

        a_off = [[] for _ in range(HG_HEADS)]
        sbs = HG_SUB_BOUNDED if bounded else sb
        for s_i in range(ch // sbs):
            if s_i == 0 and not bounded:
                for h in range(HG_HEADS):
                    a_off[h].append(jnp.zeros((sbs, ch), _F32))
                continue
            ref_b = b[s_i * sbs - 1:s_i * sbs] if s_i else jnp.zeros_like(b_last)
            q_s = (qv[s_i * sbs:(s_i + 1) * sbs] * jnp.exp(b[s_i * sbs:(s_i + 1) * sbs] - ref_b))
            n_rows = (s_i + 1) * sbs if bounded else s_i * sbs
            k_s = jnp.where(row_idx < n_rows,
                            kk * jnp.exp(jnp.where(row_idx < n_rows, ref_b - b, 0.0)), 0.0)
            q_s = q_s.astype(_BF16)
            k_s = k_s.astype(_BF16)
            for h in range(HG_HEADS):
                hs = slice(h * HG_KDIM, (h + 1) * HG_KDIM)
                a_off[h].append(_nt_dot(q_s[:, hs], k_s[:, hs]))

        for h in range(HG_HEADS):
            hs = slice(h * HG_KDIM, (h + 1) * HG_KDIM)
            st = state_scr[h]
            o_h = _nt_dot(q_in[:, hs], st.astype(_BF16))
            a_h = jnp.concatenate(a_off[h], axis=0)
            if bounded:
                a_h = jnp.where(c_i <= r_i, a_h, 0.0)
            o_scr[cs, hs] = o_h + _dot(a_h.astype(_BF16), vb[:, hs])
            state_scr[h] = (st * jnp.exp(b_last[:, hs])
                            + _tn_dot(vb[:, hs], k_out[:, hs].astype(_BF16)))
        if bounded:
            return carry

        for s_i in range(n_sub):
            rs = pl.ds(r0 + s_i * sb, sb)
            q_s = hq_ref[0, rs, :]
            b_s = b_scr[rs, :]
            acc = o_scr[rs, :]
            for t in range(sb):
                one = pl.ds(r0 + s_i * sb + t, 1)
                w = q_s * jnp.exp(jnp.minimum(b_s - b_scr[one, :], 0.0)) * kk_scr[one, :]
                v_row = hi_ref[0, one, :]
                parts = []
                for h in range(HG_HEADS):
                    hs = slice(h * HG_KDIM, (h + 1) * HG_KDIM)
                    a = jnp.sum(w[:, hs], axis=1, keepdims=True)
                    parts.append(jnp.where(t_idx >= t, a, 0.0) * v_row[:, hs])
                acc = acc + jnp.concatenate(parts, axis=1)
            o_scr[rs, :] = acc
        return carry

    run = lambda bounded: lambda: lax.fori_loop(
        0, rows // ch, functools.partial(chunk, bounded=bounded), 0)
    lax.cond(jnp.min(logf) >= -HG_MAX_DECAY / HG_SUB_BOUNDED, run(True), run(False))

    o = o_scr[...]
    gate = hg_ref[0]
    gain = g_ref[...]
    for h in range(HG_HEADS):
        hs = slice(h * HG_KDIM, (h + 1) * HG_KDIM)
        oh = o[:, hs]
        oh = oh * lax.rsqrt(jnp.mean(oh * oh, axis=1, keepdims=True) + RMS_EPS) * gain[:, hs]
        gh = gate[:, hs]
        o_ref[0, :, hs] = (oh * (gh * jax.nn.sigmoid(gh))).astype(o_ref.dtype)


def _hgrn2(hq, hf, hi, hg, lb_logits, norm_g, layer):
    bsz, seq, _ = hq.shape
    rows = HG_ROWS
    blk = pl.BlockSpec((1, rows, HG_W), lambda b, t: (b, t, 0))
    lbl = lb_logits.reshape(lb_logits.shape[0], HG_W)
    gain = norm_g.reshape(1, HG_W)
    return pl.pallas_call(
        functools.partial(_hgrn_kernel, layer=layer),
        grid=(bsz, seq // rows),
        in_specs=[pl.BlockSpec(lbl.shape, lambda b, t: (0, 0)),
                  pl.BlockSpec(gain.shape, lambda b, t: (0, 0)),
                  blk, blk, blk, blk],
        out_specs=blk,
        out_shape=jax.ShapeDtypeStruct((bsz, seq, HG_W), _BF16),
        scratch_shapes=[
            pltpu.VMEM((HG_HEADS, HG_KDIM, HG_KDIM), _F32),
            pltpu.VMEM((rows, HG_W), _F32),
            pltpu.VMEM((rows, HG_W), _F32),
            pltpu.VMEM((rows, HG_W), _F32),
        ],
        compiler_params=pltpu.CompilerParams(
            dimension_semantics=("arbitrary", "arbitrary"), vmem_limit_bytes=VMEM_LIMIT_BYTES),
    )(lbl, gain, hq, hf, hi, hg)


def _layer_norm(y, g, b):
    mu = jnp.mean(y, axis=1, keepdims=True)
    yc = y - mu
    var = jnp.mean(yc * yc, axis=1, keepdims=True)
    return yc * lax.rsqrt(var + LN_EPS) * g + b


def _ffn_kernel(x_ref, att_ref, hgo_ref, wo_ref, g1_ref, b1_ref, wu_ref, wd_ref,
                g2_ref, b2_ref, o_ref, *, alpha):
    mix = (_tn_dot(att_ref[0], wo_ref[:ATT_W, :]) + _dot(hgo_ref[...], wo_ref[ATT_W:, :]))
    y1 = _layer_norm(alpha * x_ref[...] + mix, g1_ref[...], b1_ref[...])
    y1b = y1.astype(_BF16)
    h = jnp.zeros_like(y1)
    for c in range(wu_ref.shape[1] // FFN_COLS):
        cs = slice(c * FFN_COLS, (c + 1) * FFN_COLS)
        u = jnp.maximum(_dot(y1b, wu_ref[:, cs]), 0.0)
        h = h + _dot((u * u).astype(_BF16), wd_ref[cs, :])
    o_ref[...] = _layer_norm(alpha * y1 + h, g2_ref[...], b2_ref[...])


def _out_ffn(x2d, att_t, hgo2d, w_o, g1, b1, w_up, w_down, g2, b2, alpha):
    rows, d = x2d.shape
    seq = att_t.shape[2]
    tm = FFN_ROWS
    n_seq_blocks = seq // tm
    row_spec = lambda w: pl.BlockSpec((tm, w), lambda i: (i, 0))
    full_spec = lambda a: pl.BlockSpec(a.shape, lambda i: (0, 0), pipeline_mode=pl.Buffered(1))
    att_spec = pl.BlockSpec((1, ATT_W, tm), lambda i: (i // n_seq_blocks, 0, i % n_seq_blocks))
    vec = lambda a: a.reshape(1, d).astype(_F32)
    args = (x2d, att_t, hgo2d, w_o.astype(_BF16), vec(g1), vec(b1),
            w_up.astype(_BF16), w_down.astype(_BF16), vec(g2), vec(b2))
    in_specs = [row_spec(d), att_spec, row_spec(HG_W)] + [full_spec(a) for a in args[3:]]
    return pl.pallas_call(
        functools.partial(_ffn_kernel, alpha=alpha),
        grid=(rows // tm,),
        in_specs=in_specs,
        out_specs=row_spec(d),
        out_shape=jax.ShapeDtypeStruct((rows, d), _F32),
        compiler_params=pltpu.CompilerParams(
            dimension_semantics=("arbitrary",), vmem_limit_bytes=VMEM_LIMIT_BYTES),
    )(*args)


def kernel(x, w_in, w_o, lb_logits, hg_norm_g, ln1_g, ln1_b, w_up, w_down, ln2_g, ln2_b):
    bsz, seq, d = x.shape
    depth = w_in.shape[0]
    alpha = (2.0 * depth) ** 0.25
    x2d = x.reshape(bsz * seq, d)
    for l in range(depth):
        qt, k, vt, iqt, ik, iwt, hq, hf, hi, hg = _project(x2d, w_in[l], seq)
        r3 = lambda a: a.reshape(bsz, seq, a.shape[-1])
        att_t = _dsa_attention(qt, r3(k), vt, iqt, r3(ik), iwt)
        hgo = _hgrn2(r3(hq), r3(hf), r3(hi), r3(hg), lb_logits, hg_norm_g[l], l)
        x2d = _out_ffn(x2d, att_t, hgo.reshape(bsz * seq, HG_W),
                       w_o[l], ln1_g[l], ln1_b[l], w_up[l], w_down[l], ln2_g[l], ln2_b[l], alpha)
    return x2d.reshape(bsz, seq, d)
```

```python
import functools
import math

import numpy as np
import jax
import jax.numpy as jnp
from jax import lax
from jax.experimental import pallas as pl
from jax.experimental.pallas import tpu as pltpu

ATT_HEAD_DIM = 64
ATT_HEADS = 8
ATT_W = ATT_HEADS * ATT_HEAD_DIM
IDX_HEADS = 4
IDX_DIM = 64
IDX_W = IDX_HEADS * IDX_DIM
TOPK_MAX = 256
HG_KDIM = 128
HG_HEADS = 4
HG_W = HG_HEADS * HG_KDIM
ROPE_THETA = 10000.0
LN_EPS = 1e-5
RMS_EPS = 1e-6

LANES = 128
SUBLANES = 8
BF16_ROWS = 16
VMEM_LIMIT_BYTES = 56 * 1024 * 1024

PROJ_ROWS = 512
DSA_QB = 256
DSA_KB = 512
V_ROWS = ATT_HEAD_DIM + BF16_ROWS
COUNT_ACCS = 4
CAP_GROUPS = 2
CAP_ROWS = 2 * CAP_GROUPS * BF16_ROWS
CAP_CHUNKS_PER_BLOCK = DSA_KB // CAP_ROWS
FOLD_GROUPS = CAP_ROWS // BF16_ROWS
REFINE_BITS = 6
FOLD_MIN_CHUNKS = 14
FOLD_LONG_CHUNKS = 22
HG_ROWS = 512
HG_CHUNK = 64
HG_SUB = 8
HG_SUB_BOUNDED = 16
HG_MAX_DECAY = 80.0
FFN_ROWS = 512
FFN_COLS = 1024

_F32 = jnp.float32
_BF16 = jnp.bfloat16
_I32 = jnp.int32
_I16 = jnp.int16
_INT_MIN = -(2 ** 31)
_I16_MIN = -(2 ** 15)
_I16_MAX = 2 ** 15 - 1
_LOW_SIGN = 1 << 15
_NEG = -1e30


def _nt_dot(a, b):
    return lax.dot_general(a, b, (((1,), (1,)), ((), ())), preferred_element_type=_F32)


def _tn_dot(a, b):
    return lax.dot_general(a, b, (((0,), (0,)), ((), ())), preferred_element_type=_F32)


def _dot(a, b):
    return jnp.dot(a, b, preferred_element_type=_F32)


def _tree_sum(parts):
    while len(parts) > 1:
        parts = [parts[n] + parts[n + 1] for n in range(0, len(parts) - 1, 2)] + (
            [parts[-1]] if len(parts) % 2 else [])
    return parts[0]


def _rope_group(z, cos, sin_signed):
    lane = lax.broadcasted_iota(_I32, z.shape, 1)
    first_half = (lane % ATT_HEAD_DIM) < (ATT_HEAD_DIM // 2)
    upper = pltpu.roll(z, LANES - ATT_HEAD_DIM // 2, 1)
    lower = pltpu.roll(z, ATT_HEAD_DIM // 2, 1)
    return z * cos + jnp.where(first_half, upper, lower) * sin_signed


def _proj_kernel(x_ref, wa_ref, wh_ref, cos_ref, sin_ref,
                 qt_ref, k_ref, vt_ref, iqt_ref, ik_ref, iwt_ref,
                 hq_ref, hf_ref, hi_ref, hg_ref, *, q_scale, iw_scale):
    xb = x_ref[...].astype(_BF16)
    cos = cos_ref[...]
    sin = sin_ref[...]
    pa = _dot(xb, wa_ref[...])

    def roped(col0, width):
        return [_rope_group(pa[:, col0 + g * LANES: col0 + (g + 1) * LANES], cos, sin)
                for g in range(width // LANES)]

    for g, z in enumerate(roped(0, ATT_W)):
        qt_ref[0, g * LANES:(g + 1) * LANES, :] = (z * q_scale).T.astype(_BF16)
    for g, z in enumerate(roped(ATT_W, ATT_W)):
        k_ref[:, g * LANES:(g + 1) * LANES] = z.astype(_BF16)
    rows = x_ref.shape[0]
    ones_row = lax.broadcasted_iota(_I32, (V_ROWS - ATT_HEAD_DIM, rows), 0) == 0
    for g in range(ATT_W // LANES):
        vt = pa[:, 2 * ATT_W + g * LANES:2 * ATT_W + (g + 1) * LANES].T.astype(_BF16)
        for sub in range(LANES // ATT_HEAD_DIM):
            head = g * (LANES // ATT_HEAD_DIM) + sub
            vt_ref[0, 0, head, :ATT_HEAD_DIM, :] = vt[sub * ATT_HEAD_DIM:(sub + 1) * ATT_HEAD_DIM]
            vt_ref[0, 0, head, ATT_HEAD_DIM:, :] = jnp.where(ones_row, 1.0, 0.0).astype(_BF16)
    for g, z in enumerate(roped(3 * ATT_W, IDX_W)):
        iqt_ref[0, g * LANES:(g + 1) * LANES, :] = z.T.astype(_BF16)
    ik_ref[...] = roped(3 * ATT_W + IDX_W, LANES)[0].astype(_BF16)
    iwt_ref[0] = (pa[:, 3 * ATT_W + IDX_W + LANES:] * iw_scale).T[:SUBLANES]

    ph = _dot(xb, wh_ref[...])
    hq_ref[...] = ph[:, 0 * HG_W:1 * HG_W]
    hf_ref[...] = ph[:, 1 * HG_W:2 * HG_W]
    hi_ref[...] = ph[:, 2 * HG_W:3 * HG_W]
    hg_ref[...] = ph[:, 3 * HG_W:4 * HG_W]


def _rope_tables(seq):
    half = ATT_HEAD_DIM // 2
    inv = np.power(np.float64(ROPE_THETA), -np.arange(half, dtype=np.float64) / half)
    ang = np.arange(seq, dtype=np.float64)[:, None] * inv[None, :]
    cos = np.cos(ang)
    sin = np.sin(ang)
    cos_t = np.tile(np.concatenate([cos, cos], axis=1), (1, LANES // ATT_HEAD_DIM))
    sin_t = np.tile(np.concatenate([-sin, sin], axis=1), (1, LANES // ATT_HEAD_DIM))
    return jnp.asarray(cos_t, _F32), jnp.asarray(sin_t, _F32)


def _project(x2d, w_in, seq):
    rows, d = x2d.shape
    c = 3 * ATT_W + IDX_W
    w_in = w_in.astype(_BF16)
    w_ik = w_in[:, c:c + IDX_DIM]
    w_iw = w_in[:, c + IDX_DIM:c + IDX_DIM + IDX_HEADS]
    wa = jnp.concatenate([w_in[:, :c], w_ik, w_ik,
                          jnp.pad(w_iw, ((0, 0), (0, LANES - IDX_HEADS)))], axis=1)
    wh = w_in[:, c + IDX_DIM + IDX_HEADS:]
    cos_t, sin_t = _rope_tables(seq)
    tm = PROJ_ROWS
    n_seq_blocks = seq // tm
    row_spec = lambda w: pl.BlockSpec((tm, w), lambda i: (i, 0))
    full_spec = lambda a: pl.BlockSpec(a.shape, lambda i: (0, 0), pipeline_mode=pl.Buffered(1))
    pos_spec = pl.BlockSpec((tm, LANES), lambda i: (i % n_seq_blocks, 0))
    bsz = rows // seq
    t_spec = lambda *feat: pl.BlockSpec(
        (1,) + feat + (tm,), lambda i: (i // n_seq_blocks,) + (0,) * len(feat) + (i % n_seq_blocks,))
    out_shapes = [
        jax.ShapeDtypeStruct((bsz, ATT_W, seq), _BF16),
        jax.ShapeDtypeStruct((rows, ATT_W), _BF16),
        jax.ShapeDtypeStruct((bsz, seq // DSA_KB, ATT_HEADS, V_ROWS, DSA_KB), _BF16),
        jax.ShapeDtypeStruct((bsz, IDX_W, seq), _BF16),
        jax.ShapeDtypeStruct((rows, LANES), _BF16),
        jax.ShapeDtypeStruct((bsz, SUBLANES, seq), _F32),
    ] + [jax.ShapeDtypeStruct((rows, HG_W), _F32)] * 4
    tiles_per_chunk = DSA_KB // tm
    vt_spec = pl.BlockSpec(
        (1, 1, ATT_HEADS, V_ROWS, tm),
        lambda i: (i // n_seq_blocks, (i % n_seq_blocks) // tiles_per_chunk, 0, 0, i % tiles_per_chunk))
    out_specs = [t_spec(ATT_W), row_spec(ATT_W), vt_spec, t_spec(IDX_W),
                 row_spec(LANES), t_spec(SUBLANES)] + [row_spec(HG_W)] * 4
    kern = functools.partial(_proj_kernel, q_scale=ATT_HEAD_DIM ** -0.5 * math.log2(math.e),
                             iw_scale=(IDX_HEADS ** -0.5) * (IDX_DIM ** -0.5))
    return pl.pallas_call(
        kern,
        grid=(rows // tm,),
        in_specs=[row_spec(d), full_spec(wa), full_spec(wh), pos_spec, pos_spec],
        out_specs=out_specs,
        out_shape=out_shapes,
        compiler_params=pltpu.CompilerParams(
            dimension_semantics=("arbitrary",), vmem_limit_bytes=VMEM_LIMIT_BYTES),
    )(x2d, wa, wh, cos_t, sin_t)


def _dsa_kernel(qi_ref, kj_ref, ka_ref, vp_ref, vc_ref,
                qt_ref, iqt_ref, iwt_ref, ik_ref, before_ref, k_ref, vt_prev_ref, vt_ref,
                o_ref,
                hi_scr, lo_scr, fold_scr, cap_scr, bias_scr, m_scr, acc_scr,
                s_even, s_odd, bm_even, bm_odd, *, topk):
    del ka_ref, vp_ref, vc_ref
    p = pl.program_id(1)
    i = qi_ref[p]
    j = kj_ref[p]
    qb, kb = DSA_QB, DSA_KB
    n_kb = (i * qb + qb - 1) // kb + 1
    n_steps = (i * qb + qb - 1) // (2 * kb) + 1
    last_pair = 2 * (n_steps - 1)
    has_second = n_kb == 2 * n_steps
    first_head = lax.broadcasted_iota(_I32, (LANES, qb), 0) < ATT_HEAD_DIM

    def one_head(pair_rows, h):
        keep = first_head if h % 2 == 0 else jnp.logical_not(first_head)
        return jnp.where(keep, pair_rows, jnp.zeros_like(pair_rows))

    @pl.when(j == 0)
    def _select():
        iwt = iwt_ref[0]
        qpos = i * qb + lax.broadcasted_iota(_I32, (kb, qb), 1)
        krow = lax.broadcasted_iota(_I32, (kb, qb), 0)
        iq_heads = [one_head(iqt_ref[0, (h // 2) * LANES:(h // 2 + 1) * LANES, :], h)
                    for h in range(IDX_HEADS)]

        def score_chunk(c, causal):
            row0 = pl.multiple_of(c * kb, kb)
            ikc = ik_ref[0, pl.ds(row0, kb), :]
            score = jnp.zeros((kb, qb), _F32)
            for h in range(IDX_HEADS):
                logits = _dot(ikc, iq_heads[h])
                score = score + iwt[h:h + 1, :] * jnp.maximum(logits, 0.0)
            bits = lax.bitcast_convert_type(score, _I32)
            key = bits ^ (((bits >> 31) & 0x7FFFFFFF) ^ _LOW_SIGN)
            if causal:
                key = jnp.where(row0 + krow <= qpos, key, _INT_MIN ^ _LOW_SIGN)
            hi = (key >> 16).astype(_I16)
            hi_scr[c] = hi
            lo_scr[c] = key.astype(_I16)
            n_tiles = kb // BF16_ROWS
            row0_fold = pl.multiple_of((c % CAP_CHUNKS_PER_BLOCK) * CAP_ROWS, CAP_ROWS)
            for g in range(FOLD_GROUPS):
                tiles = [hi[r * BF16_ROWS:(r + 1) * BF16_ROWS] for r in range(g, n_tiles, FOLD_GROUPS)]
                while len(tiles) > 1:
                    tiles = [jnp.where(tiles[n] > tiles[n + 1], tiles[n], tiles[n + 1])
                             for n in range(0, len(tiles), 2)]
                fold_scr[c // CAP_CHUNKS_PER_BLOCK, pl.ds(row0_fold + g * BF16_ROWS, BF16_ROWS), :] = tiles[0]

        def score_pair(c2, carry):
            score_chunk(2 * c2, False)
            score_chunk(2 * c2 + 1, False)
            return carry

        fold_scr[...] = jnp.full_like(fold_scr, _I16_MIN)
        def score_quad(c4, carry):
            score_pair(2 * c4, carry)
            return score_pair(2 * c4 + 1, carry)

        n_quads = (n_steps - 1) // 2
        lax.fori_loop(0, n_quads, score_quad, 0)

        @pl.when(2 * n_quads < n_steps - 1)
        def _score_leftover():
            score_pair(2 * n_quads, 0)

        score_chunk(last_pair, True)

        @pl.when(has_second)
        def _score_second():
            score_chunk(last_pair + 1, True)

        @pl.when(jnp.logical_not(has_second))
        def _blank_second():
            hi_scr[last_pair + 1] = jnp.full((kb, qb), _I16_MIN, _I16)
            lo_scr[last_pair + 1] = jnp.full((kb, qb), _I16_MIN, _I16)

        tile_rows = lambda r: slice(r * BF16_ROWS, (r + 1) * BF16_ROWS)
        i16_min = jnp.int16(_I16_MIN)

        def count(ref, cand, n_trips, blocks_per_trip, strict=False):
            cand_rows = jnp.broadcast_to(cand.astype(_I16), (BF16_ROWS, qb))
            one, zero = jnp.int16(1), jnp.int16(0)

            def body(t, accs):
                accs = list(accs)
                for u in range(blocks_per_trip):
                    for r in range(kb // BF16_ROWS):
                        tile = ref[blocks_per_trip * t + u, tile_rows(r), :]
                        hit = tile > cand_rows if strict else tile >= cand_rows
                        accs[r % len(accs)] = accs[r % len(accs)] + jnp.where(hit, one, zero)
                return tuple(accs)

            accs = lax.fori_loop(0, n_trips, body, (jnp.zeros((BF16_ROWS, qb), _I16),) * COUNT_ACCS)
            return jnp.sum(_tree_sum(list(accs)).astype(_I32), axis=0, keepdims=True)

        def bisect(ref, n_above, n_trips, blocks_per_trip):
            def body(step, carry):
                val, n_next = carry
                bit = jnp.left_shift(jnp.int32(1), 15 - step)
                cand = jnp.where(step == 0, 0, val | bit)
                cnt = n_above + count(ref, cand, n_trips, blocks_per_trip)
                ok = cnt >= topk
                return jnp.where(ok, cand, val), jnp.where(ok, n_next, cnt)
            return lax.fori_loop(0, 16, body, (jnp.full((1, qb), _I16_MIN, _I32), n_above))

        n_cap = (n_kb + CAP_CHUNKS_PER_BLOCK - 1) // CAP_CHUNKS_PER_BLOCK
        no_keys = jnp.zeros((1, qb), _I32)

        def high_full():
            return bisect(hi_scr, no_keys, n_steps, 2)

        def high_from_fold():
            base, _ = bisect(fold_scr, no_keys, n_cap, 1)
            n_bits = jnp.where(n_kb >= FOLD_LONG_CHUNKS, REFINE_BITS - 1, REFINE_BITS)

            def count_from(cand):
                cnt = count(hi_scr, jnp.minimum(cand, _I16_MAX), n_steps, 2)
                return jnp.where(cand > _I16_MAX, 0, cnt)

            n_beyond = count_from(base + jnp.left_shift(jnp.int32(1), n_bits))

            def refine():
                def body(step, carry):
                    off, n_next = carry
                    cand_off = off | jnp.left_shift(jnp.int32(1), n_bits - 1 - step)
                    cnt = count_from(base + cand_off)
                    ok = cnt >= topk
                    return jnp.where(ok, cand_off, off), jnp.where(ok, n_next, cnt)
                off, n_next = lax.fori_loop(0, n_bits, body, (no_keys, n_beyond))
                return base + off, n_next

            n_outside = jnp.sum(jnp.where(n_beyond >= topk, 1.0, 0.0))
            return lax.cond(n_outside > 0.0, high_full, refine)

        t_hi, n_gt_hi = lax.cond(n_kb >= FOLD_MIN_CHUNKS, high_from_fold, high_full)
        t_hi_tile = jnp.broadcast_to(t_hi.astype(_I16), (BF16_ROWS, qb))

        cap_scr[...] = jnp.full_like(cap_scr, _I16_MIN)

        def capture_block(c, carry):
            tops = [[jnp.full((BF16_ROWS, qb), _I16_MIN, _I16)] * 2 for _ in range(CAP_GROUPS)]
            for r in range(kb // BF16_ROWS):
                x = jnp.where(hi_scr[c, tile_rows(r), :] == t_hi_tile, lo_scr[c, tile_rows(r), :], i16_min)
                lo_scr[c, tile_rows(r), :] = x
                first, second = tops[r % CAP_GROUPS]
                above = x > first
                tops[r % CAP_GROUPS] = [jnp.where(above, x, first),
                                        jnp.where(above, first, jnp.where(x > second, x, second))]
            row0 = pl.multiple_of((c % CAP_CHUNKS_PER_BLOCK) * CAP_ROWS, CAP_ROWS)
            for g in range(CAP_GROUPS):
                for t in range(2):
                    cap_scr[c // CAP_CHUNKS_PER_BLOCK,
                            pl.ds(row0 + (2 * g + t) * BF16_ROWS, BF16_ROWS), :] = tops[g][t]
            return carry

        lax.fori_loop(0, n_kb, capture_block, 0)
        t_lo, _ = bisect(cap_scr, n_gt_hi, n_cap, 1)
        n_gt = n_gt_hi + count(lo_scr, t_lo, n_steps, 2, strict=True)
        n_wrong = jnp.sum(jnp.where(n_gt >= topk, 1.0, 0.0))
        t_lo, n_gt = lax.cond(n_wrong > 0.0,
                              lambda: bisect(lo_scr, n_gt_hi, n_steps, 2),
                              lambda: (t_lo, n_gt))
        t_hi_rows = jnp.broadcast_to(t_hi.astype(_I16), (kb, qb))
        t_lo = jnp.where((t_hi == _I16_MIN) & (t_lo == _I16_MIN), _I16_MIN + 1, t_lo)
        t_lo_rows = jnp.broadcast_to(t_lo.astype(_I16), (kb, qb))
        rem_rows = jnp.broadcast_to((topk - n_gt).astype(_I16), (kb, qb))

        def mask_block(c, seen):
            hi = hi_scr[c]
            lo = lo_scr[c]
            tie = (hi == t_hi_rows) & (lo == t_lo_rows)
            tie_count = jnp.where(tie, jnp.asarray(1, _BF16), jnp.asarray(0, _BF16))

            def tie_at(r):
                hit = (hi[r:r + 1].astype(_I32) == t_hi) & (lo[r:r + 1].astype(_I32) == t_lo)
                return jnp.where(hit, 1.0, 0.0)

            half = kb // 2
            before_top = seen + _dot(before_ref[...], tie_count[:half])
            seen_mid = before_top[half - 1:half] + tie_at(half - 1)
            before_bot = seen_mid + _dot(before_ref[...], tie_count[half:])
            ties_before = jnp.concatenate([before_top, before_bot], axis=0)
            allowed = ties_before.astype(_I32).astype(_I16) < rem_rows
            chosen = (hi > t_hi_rows) | (lo > t_lo_rows) | (tie & allowed)
            bias_scr[c] = jnp.where(chosen, jnp.asarray(0, bias_scr.dtype),
                                    jnp.asarray(_NEG, bias_scr.dtype))
            return before_bot[half - 1:half] + tie_at(kb - 1)

        seen = lax.fori_loop(0, n_steps - 1,
                             lambda c2, seen: mask_block(2 * c2 + 1, mask_block(2 * c2, seen)),
                             jnp.zeros((1, qb), _F32))
        seen = mask_block(last_pair, seen)

        @pl.when(has_second)
        def _mask_second():
            mask_block(last_pair + 1, seen)

        m_scr[...] = jnp.full_like(m_scr, _NEG)
        acc_scr[...] = jnp.zeros_like(acc_scr)

    def stage_b(h, s_read, bm_read, vt_read):
        m_old = m_scr[h]
        m_new = jnp.maximum(m_old, bm_read[h])
        pr = jnp.exp2(s_read[h] - m_new).astype(_BF16)
        acc_scr[h] = jnp.exp2(m_old - m_new) * acc_scr[h] + _dot(vt_read[0, 0, h], pr)
        m_scr[h] = m_new

    def phase(k_rows, chunk, s_write, bm_write, s_read, bm_read, vt_read):
        bias = bias_scr[chunk].astype(_F32)

        def stage_a(h):
            pair = slice((h // 2) * LANES, (h // 2 + 1) * LANES)
            s = _dot(k_ref[0, k_rows, pair], one_head(qt_ref[0, pair, :], h)) + bias
            s_write[h] = s
            bm_write[h] = jnp.max(s, axis=0, keepdims=True)

        stage_a(0)
        for h in range(ATT_HEADS):
            if h + 1 < ATT_HEADS:
                stage_a(h + 1)
            if s_read is not None:
                stage_b(h, s_read, bm_read, vt_read)

    @pl.when(j == 0)
    def _fill():
        phase(slice(0, kb), 0, s_even, bm_even, None, None, None)

    @pl.when((j > 0) & (j < n_steps))
    def _first():
        phase(slice(0, kb), 2 * j, s_even, bm_even, s_odd, bm_odd, vt_prev_ref)

    @pl.when(2 * j + 1 < n_kb)
    def _second():
        phase(slice(kb, 2 * kb), 2 * j + 1, s_odd, bm_odd, s_even, bm_even, vt_ref)

    @pl.when((j == n_steps) & has_second)
    def _drain_odd():
        for h in range(ATT_HEADS):
            stage_b(h, s_odd, bm_odd, vt_prev_ref)

    @pl.when((j == n_steps) & jnp.logical_not(has_second))
    def _drain_even():
        for h in range(ATT_HEADS):
            stage_b(h, s_even, bm_even, vt_prev_ref)

    @pl.when(j == n_steps)
    def _finish():
        for h in range(ATT_HEADS):
            a = acc_scr[h]
            o_ref[0, h * ATT_HEAD_DIM:(h + 1) * ATT_HEAD_DIM, :] = (
                a[:ATT_HEAD_DIM] / a[ATT_HEAD_DIM:ATT_HEAD_DIM + 1]).astype(o_ref.dtype)


def _dsa_attention(qt, k, vt, iqt, ik, iwt):
    bsz, seq, _ = k.shape
    qb, kb = DSA_QB, DSA_KB
    topk = min(TOPK_MAX, seq // 4)
    n_qb = seq // qb
    n_steps = lambda i: (i * qb + qb - 1) // (2 * kb) + 1
    pairs = [(i, j) for i in range(n_qb) for j in range(n_steps(i) + 1)]
    as_i32 = lambda vals: jnp.asarray(np.array(vals, np.int32))
    qi = as_i32([i for i, j in pairs])
    kj = as_i32([j for i, j in pairs])
    k_pair = as_i32([min(j, n_steps(i) - 1) for i, j in pairs])
    n_kb = lambda i: (i * qb + qb - 1) // kb + 1
    v_prev = as_i32([n_kb(i) - 1 if j == n_steps(i) else max(2 * j - 1, 0) for i, j in pairs])
    v_this = as_i32([min(2 * j, 2 * n_steps(i) - 1) for i, j in pairs])
    before =jnp.asarray(np.tril(np.ones((kb // 2, kb // 2), np.float32), -1), _BF16)
    q_map = lambda b, p, qi_r, *_: (b, 0, qi_r[p])
    vt_spec = lambda which: pl.BlockSpec(
        (1, 1, ATT_HEADS, V_ROWS, kb), lambda b, p, *refs: (b, refs[which][p], 0, 0, 0))
    grid_spec = pltpu.PrefetchScalarGridSpec(
        num_scalar_prefetch=5,
        grid=(bsz, len(pairs)),
        in_specs=[
            pl.BlockSpec((1, ATT_W, qb), q_map),
            pl.BlockSpec((1, IDX_W, qb), q_map),
            pl.BlockSpec((1, SUBLANES, qb), q_map),
            pl.BlockSpec((1, seq, LANES), lambda b, p, *_: (b, 0, 0)),
            pl.BlockSpec(before.shape, lambda b, p, *_: (0, 0)),
            pl.BlockSpec((1, 2 * kb, ATT_W), lambda b, p, *refs: (b, refs[2][p], 0)),
            vt_spec(3),
            vt_spec(4),
        ],
        out_specs=pl.BlockSpec((1, ATT_W, qb), q_map),
        scratch_shapes=[
            pltpu.VMEM((seq // kb, kb, qb), _I16),
            pltpu.VMEM((seq // kb, kb, qb), _I16),
            pltpu.VMEM((-(-(seq // kb) // CAP_CHUNKS_PER_BLOCK), kb, qb), _I16),
            pltpu.VMEM((-(-(seq // kb) // CAP_CHUNKS_PER_BLOCK), kb, qb), _I16),
            pltpu.VMEM((seq // kb, kb, qb), _BF16),
            pltpu.VMEM((ATT_HEADS, 1, qb), _F32),
            pltpu.VMEM((ATT_HEADS, V_ROWS, qb), _F32),
            pltpu.VMEM((ATT_HEADS, kb, qb), _F32),
            pltpu.VMEM((ATT_HEADS, kb, qb), _F32),
            pltpu.VMEM((ATT_HEADS, 1, qb), _F32),
            pltpu.VMEM((ATT_HEADS, 1, qb), _F32),
        ],
    )
    return pl.pallas_call(
        functools.partial(_dsa_kernel, topk=topk),
        grid_spec=grid_spec,
        out_shape=jax.ShapeDtypeStruct((bsz, ATT_W, seq), _BF16),
        compiler_params=pltpu.CompilerParams(
            dimension_semantics=("arbitrary", "arbitrary"), vmem_limit_bytes=VMEM_LIMIT_BYTES),
    )(qi, kj, k_pair, v_prev, v_this, qt, iqt, iwt, ik, before, k, vt, vt)


def _split3(a):
    hi = a.astype(_BF16)
    r1 = a - hi.astype(_F32)
    mid = r1.astype(_BF16)
    lo = (r1 - mid.astype(_F32)).astype(_BF16)
    return hi, mid, lo


def _hgrn_kernel(lbl_ref, g_ref, hq_ref, hf_ref, hi_ref, hg_ref, o_ref,
                 state_scr, kk_scr, b_scr, o_scr, *, layer):
    rows, ch, sb = HG_ROWS, HG_CHUNK, HG_SUB
    n_sub = ch // sb

    @pl.when(pl.program_id(1) == 0)
    def _reset():
        state_scr[...] = jnp.zeros_like(state_scr)

    lbl = lbl_ref[...]
    e = jnp.exp(lbl - jnp.max(lbl, axis=0, keepdims=True))
    lb = jnp.sum(e[:layer + 1], axis=0, keepdims=True) / jnp.sum(e, axis=0, keepdims=True)

    f = lb + (1.0 - lb) * jax.nn.sigmoid(hf_ref[0])
    kk_scr[...] = 1.0 - f
    logf = jnp.log(f)
    r_i = lax.broadcasted_iota(_I32, (ch, ch), 0)
    c_i = lax.broadcasted_iota(_I32, (ch, ch), 1)
    lower = jnp.where(c_i <= r_i, 1.0, 0.0).astype(_BF16)
    for c in range(rows // ch):
        parts = _split3(logf[c * ch:(c + 1) * ch])
        b_scr[c * ch:(c + 1) * ch, :] = sum(_dot(lower, part) for part in parts)

    t_idx = lax.broadcasted_iota(_I32, (sb, 1), 0)
    row_idx = lax.broadcasted_iota(_I32, (ch, 1), 0)

    def chunk(c, carry, bounded):
        r0 = pl.multiple_of(c * ch, ch)
        cs = pl.ds(r0, ch)
        b = b_scr[cs, :]
        kk = kk_scr[cs, :]
        qv = hq_ref[0, cs, :]
        vv = hi_ref[0, cs, :]
        b_last = b[ch - 1:ch]
        q_in = (qv * jnp.exp(b)).astype(_BF16)
        k_out = kk * jnp.exp(b_last - b)
        vb = vv.astype(_BF16)

        a_off = [[] for _ in range(HG_HEADS)]
        sbs = HG_SUB_BOUNDED if bounded else sb
        for s_i in range(ch // sbs):
            if s_i == 0 and not bounded:
                for h in range(HG_HEADS):
                    a_off[h].append(jnp.zeros((sbs, ch), _F32))
                continue
            ref_b = b[s_i * sbs - 1:s_i * sbs] if s_i else jnp.zeros_like(b_last)
            q_s = (qv[s_i * sbs:(s_i + 1) * sbs] * jnp.exp(b[s_i * sbs:(s_i + 1) * sbs] - ref_b))
            n_rows = (s_i + 1) * sbs if bounded else s_i * sbs
            k_s = jnp.where(row_idx < n_rows,
                            kk * jnp.exp(jnp.where(row_idx < n_rows, ref_b - b, 0.0)), 0.0)
            q_s = q_s.astype(_BF16)
            k_s = k_s.astype(_BF16)
            for h in range(HG_HEADS):
                hs = slice(h * HG_KDIM, (h + 1) * HG_KDIM)
                a_off[h].append(_nt_dot(q_s[:, hs], k_s[:, hs]))

        for h in range(HG_HEADS):
            hs = slice(h * HG_KDIM, (h + 1) * HG_KDIM)
            st = state_scr[h]
            o_h = _nt_dot(q_in[:, hs], st.astype(_BF16))
            a_h = jnp.concatenate(a_off[h], axis=0)
            if bounded:
                a_h = jnp.where(c_i <= r_i, a_h, 0.0)
            o_scr[cs, hs] = o_h + _dot(a_h.astype(_BF16), vb[:, hs])
            state_scr[h] = (st * jnp.exp(b_last[:, hs])
                            + _tn_dot(vb[:, hs], k_out[:, hs].astype(_BF16)))
        if bounded:
            return carry

        for s_i in range(n_sub):
            rs = pl.ds(r0 + s_i * sb, sb)
            q_s = hq_ref[0, rs, :]
            b_s = b_scr[rs, :]
            acc = o_scr[rs, :]
            for t in range(sb):
                one = pl.ds(r0 + s_i * sb + t, 1)
                w = q_s * jnp.exp(jnp.minimum(b_s - b_scr[one, :], 0.0)) * kk_scr[one, :]
                v_row = hi_ref[0, one, :]
                parts = []
                for h in range(HG_HEADS):
                    hs = slice(h * HG_KDIM, (h + 1) * HG_KDIM)
                    a = jnp.sum(w[:, hs], axis=1, keepdims=True)
                    parts.append(jnp.where(t_idx >= t, a, 0.0) * v_row[:, hs])
                acc = acc + jnp.concatenate(parts, axis=1)
            o_scr[rs, :] = acc
        return carry

    run = lambda bounded: lambda: lax.fori_loop(
        0, rows // ch, functools.partial(chunk, bounded=bounded), 0)
    lax.cond(jnp.min(logf) >= -HG_MAX_DECAY / HG_SUB_BOUNDED, run(True), run(False))

    o = o_scr[...]
    gate = hg_ref[0]
    gain = g_ref[...]
    for h in range(HG_HEADS):
        hs = slice(h * HG_KDIM, (h + 1) * HG_KDIM)
        oh = o[:, hs]
        oh = oh * lax.rsqrt(jnp.mean(oh * oh, axis=1, keepdims=True) + RMS_EPS) * gain[:, hs]
        gh = gate[:, hs]
        o_ref[0, :, hs] = (oh * (gh * jax.nn.sigmoid(gh))).astype(o_ref.dtype)


def _hgrn2(hq, hf, hi, hg, lb_logits, norm_g, layer):
    bsz, seq, _ = hq.shape
    rows = HG_ROWS
    blk = pl.BlockSpec((1, rows, HG_W), lambda b, t: (b, t, 0))
    lbl = lb_logits.reshape(lb_logits.shape[0], HG_W)
    gain = norm_g.reshape(1, HG_W)
    return pl.pallas_call(
        functools.partial(_hgrn_kernel, layer=layer),
        grid=(bsz, seq // rows),
        in_specs=[pl.BlockSpec(lbl.shape, lambda b, t: (0, 0)),
                  pl.BlockSpec(gain.shape, lambda b, t: (0, 0)),
                  blk, blk, blk, blk],
        out_specs=blk,
        out_shape=jax.ShapeDtypeStruct((bsz, seq, HG_W), _BF16),
        scratch_shapes=[
            pltpu.VMEM((HG_HEADS, HG_KDIM, HG_KDIM), _F32),
            pltpu.VMEM((rows, HG_W), _F32),
            pltpu.VMEM((rows, HG_W), _F32),
            pltpu.VMEM((rows, HG_W), _F32),
        ],
        compiler_params=pltpu.CompilerParams(
            dimension_semantics=("arbitrary", "arbitrary"), vmem_limit_bytes=VMEM_LIMIT_BYTES),
    )(lbl, gain, hq, hf, hi, hg)


def _layer_norm(y, g, b):
    mu = jnp.mean(y, axis=1, keepdims=True)
    yc = y - mu
    var = jnp.mean(yc * yc, axis=1, keepdims=True)
    return yc * lax.rsqrt(var + LN_EPS) * g + b


def _ffn_kernel(x_ref, att_ref, hgo_ref, wo_ref, g1_ref, b1_ref, wu_ref, wd_ref,
                g2_ref, b2_ref, o_ref, *, alpha):
    mix = (_tn_dot(att_ref[0], wo_ref[:ATT_W, :]) + _dot(hgo_ref[...], wo_ref[ATT_W:, :]))
    y1 = _layer_norm(alpha * x_ref[...] + mix, g1_ref[...], b1_ref[...])
    y1b = y1.astype(_BF16)
    h = jnp.zeros_like(y1)
    for c in range(wu_ref.shape[1] // FFN_COLS):
        cs = slice(c * FFN_COLS, (c + 1) * FFN_COLS)
        u = jnp.maximum(_dot(y1b, wu_ref[:, cs]), 0.0)
        h = h + _dot((u * u).astype(_BF16), wd_ref[cs, :])
    o_ref[...] = _layer_norm(alpha * y1 + h, g2_ref[...], b2_ref[...])


def _out_ffn(x2d, att_t, hgo2d, w_o, g1, b1, w_up, w_down, g2, b2, alpha):
    rows, d = x2d.shape
    seq = att_t.shape[2]
    tm = FFN_ROWS
    n_seq_blocks = seq // tm
    row_spec = lambda w: pl.BlockSpec((tm, w), lambda i: (i, 0))
    full_spec = lambda a: pl.BlockSpec(a.shape, lambda i: (0, 0), pipeline_mode=pl.Buffered(1))
    att_spec = pl.BlockSpec((1, ATT_W, tm), lambda i: (i // n_seq_blocks, 0, i % n_seq_blocks))
    vec = lambda a: a.reshape(1, d).astype(_F32)
    args = (x2d, att_t, hgo2d, w_o.astype(_BF16), vec(g1), vec(b1),
            w_up.astype(_BF16), w_down.astype(_BF16), vec(g2), vec(b2))
    in_specs = [row_spec(d), att_spec, row_spec(HG_W)] + [full_spec(a) for a in args[3:]]
    return pl.pallas_call(
        functools.partial(_ffn_kernel, alpha=alpha),
        grid=(rows // tm,),
        in_specs=in_specs,
        out_specs=row_spec(d),
        out_shape=jax.ShapeDtypeStruct((rows, d), _F32),
        compiler_params=pltpu.CompilerParams(
            dimension_semantics=("arbitrary",), vmem_limit_bytes=VMEM_LIMIT_BYTES),
    )(*args)


def kernel(x, w_in, w_o, lb_logits, hg_norm_g, ln1_g, ln1_b, w_up, w_down, ln2_g, ln2_b):
    bsz, seq, d = x.shape
    depth = w_in.shape[0]
    alpha = (2.0 * depth) ** 0.25
    x2d = x.reshape(bsz * seq, d)
    for l in range(depth):
        qt, k, vt, iqt, ik, iwt, hq, hf, hi, hg = _project(x2d, w_in[l], seq)
        r3 = lambda a: a.reshape(bsz, seq, a.shape[-1])
        att_t = _dsa_attention(qt, r3(k), vt, iqt, r3(ik), iwt)
        hgo = _hgrn2(r3(hq), r3(hf), r3(hi), r3(hg), lb_logits, hg_norm_g[l], l)
        x2d = _out_ffn(x2d, att_t, hgo.reshape(bsz * seq, HG_W),
                       w_o[l], ln1_g[l], ln1_b[l], w_up[l], w_down[l], ln2_g[l], ln2_b[l], alpha)
    return x2d.reshape(bsz, seq, d)
```

```python
import functools
import math

import numpy as np
import jax
import jax.numpy as jnp
from jax import lax
from jax.experimental import pallas as pl
from jax.experimental.pallas import tpu as pltpu

ATT_HEAD_DIM = 64
ATT_HEADS = 8
ATT_W = ATT_HEADS * ATT_HEAD_DIM
IDX_HEADS = 4
IDX_DIM = 64
IDX_W = IDX_HEADS * IDX_DIM
TOPK_MAX = 256
HG_KDIM = 128
HG_HEADS = 4
HG_W = HG_HEADS * HG_KDIM
ROPE_THETA = 10000.0
LN_EPS = 1e-5
RMS_EPS = 1e-6

LANES = 128
SUBLANES = 8
BF16_ROWS = 16
VMEM_LIMIT_BYTES = 56 * 1024 * 1024

PROJ_ROWS = 512
DSA_QB = 256
DSA_KB = 512
V_ROWS = ATT_HEAD_DIM + BF16_ROWS
COUNT_ACCS = 4
CAP_GROUPS = 2
CAP_ROWS = 2 * CAP_GROUPS * BF16_ROWS
CAP_CHUNKS_PER_BLOCK = DSA_KB // CAP_ROWS
FOLD_GROUPS = CAP_ROWS // BF16_ROWS
REFINE_BITS = 6
FOLD_MIN_CHUNKS = 14
FOLD_LONG_CHUNKS = 22
HG_ROWS = 512
HG_CHUNK = 64
HG_SUB = 8
HG_SUB_BOUNDED = 16
HG_MAX_DECAY = 80.0
FFN_ROWS = 512
FFN_COLS = 1024

_F32 = jnp.float32
_BF16 = jnp.bfloat16
_I32 = jnp.int32
_I16 = jnp.int16
_INT_MIN = -(2 ** 31)
_I16_MIN = -(2 ** 15)
_I16_MAX = 2 ** 15 - 1
_LOW_SIGN = 1 << 15
_NEG = -1e30


def _nt_dot(a, b):
    return lax.dot_general(a, b, (((1,), (1,)), ((), ())), preferred_element_type=_F32)


def _tn_dot(a, b):
    return lax.dot_general(a, b, (((0,), (0,)), ((), ())), preferred_element_type=_F32)


def _dot(a, b):
    return jnp.dot(a, b, preferred_element_type=_F32)


def _tree_sum(parts):
    while len(parts) > 1:
        parts = [parts[n] + parts[n + 1] for n in range(0, len(parts) - 1, 2)] + (
            [parts[-1]] if len(parts) % 2 else [])
    return parts[0]


def _rope_group(z, cos, sin_signed):
    lane = lax.broadcasted_iota(_I32, z.shape, 1)
    first_half = (lane % ATT_HEAD_DIM) < (ATT_HEAD_DIM // 2)
    upper = pltpu.roll(z, LANES - ATT_HEAD_DIM // 2, 1)
    lower = pltpu.roll(z, ATT_HEAD_DIM // 2, 1)
    return z * cos + jnp.where(first_half, upper, lower) * sin_signed


def _proj_kernel(x_ref, wa_ref, wh_ref, cos_ref, sin_ref,
                 qt_ref, k_ref, vt_ref, iqt_ref, ik_ref, iwt_ref,
                 hq_ref, hf_ref, hi_ref, hg_ref, *, q_scale, iw_scale):
    xb = x_ref[...].astype(_BF16)
    cos = cos_ref[...]
    sin = sin_ref[...]
    pa = _dot(xb, wa_ref[...])

    def roped(col0, width):
        return [_rope_group(pa[:, col0 + g * LANES: col0 + (g + 1) * LANES], cos, sin)
                for g in range(width // LANES)]

    for g, z in enumerate(roped(0, ATT_W)):
        qt_ref[0, g * LANES:(g + 1) * LANES, :] = (z * q_scale).T.astype(_BF16)
    for g, z in enumerate(roped(ATT_W, ATT_W)):
        k_ref[:, g * LANES:(g + 1) * LANES] = z.astype(_BF16)
    rows = x_ref.shape[0]
    ones_row = lax.broadcasted_iota(_I32, (V_ROWS - ATT_HEAD_DIM, rows), 0) == 0
    for g in range(ATT_W // LANES):
        vt = pa[:, 2 * ATT_W + g * LANES:2 * ATT_W + (g + 1) * LANES].T.astype(_BF16)
        for sub in range(LANES // ATT_HEAD_DIM):
            head = g * (LANES // ATT_HEAD_DIM) + sub
            vt_ref[0, 0, head, :ATT_HEAD_DIM, :] = vt[sub * ATT_HEAD_DIM:(sub + 1) * ATT_HEAD_DIM]
            vt_ref[0, 0, head, ATT_HEAD_DIM:, :] = jnp.where(ones_row, 1.0, 0.0).astype(_BF16)
    for g, z in enumerate(roped(3 * ATT_W, IDX_W)):
        iqt_ref[0, g * LANES:(g + 1) * LANES, :] = z.T.astype(_BF16)
    ik_ref[...] = roped(3 * ATT_W + IDX_W, LANES)[0].astype(_BF16)
    iwt_ref[0] = (pa[:, 3 * ATT_W + IDX_W + LANES:] * iw_scale).T[:SUBLANES]

    ph = _dot(xb, wh_ref[...])
    hq_ref[...] = ph[:, 0 * HG_W:1 * HG_W]
    hf_ref[...] = ph[:, 1 * HG_W:2 * HG_W]
    hi_ref[...] = ph[:, 2 * HG_W:3 * HG_W]
    hg_ref[...] = ph[:, 3 * HG_W:4 * HG_W]


def _rope_tables(seq):
    half = ATT_HEAD_DIM // 2
    inv = np.power(np.float64(ROPE_THETA), -np.arange(half, dtype=np.float64) / half)
    ang = np.arange(seq, dtype=np.float64)[:, None] * inv[None, :]
    cos = np.cos(ang)
    sin = np.sin(ang)
    cos_t = np.tile(np.concatenate([cos, cos], axis=1), (1, LANES // ATT_HEAD_DIM))
    sin_t = np.tile(np.concatenate([-sin, sin], axis=1), (1, LANES // ATT_HEAD_DIM))
    return jnp.asarray(cos_t, _F32), jnp.asarray(sin_t, _F32)


def _project(x2d, w_in, seq):
    rows, d = x2d.shape
    c = 3 * ATT_W + IDX_W
    w_in = w_in.astype(_BF16)
    w_ik = w_in[:, c:c + IDX_DIM]
    w_iw = w_in[:, c + IDX_DIM:c + IDX_DIM + IDX_HEADS]
    wa = jnp.concatenate([w_in[:, :c], w_ik, w_ik,
                          jnp.pad(w_iw, ((0, 0), (0, LANES - IDX_HEADS)))], axis=1)
    wh = w_in[:, c + IDX_DIM + IDX_HEADS:]
    cos_t, sin_t = _rope_tables(seq)
    tm = PROJ_ROWS
    n_seq_blocks = seq // tm
    row_spec = lambda w: pl.BlockSpec((tm, w), lambda i: (i, 0))
    full_spec = lambda a: pl.BlockSpec(a.shape, lambda i: (0, 0), pipeline_mode=pl.Buffered(1))
    pos_spec = pl.BlockSpec((tm, LANES), lambda i: (i % n_seq_blocks, 0))
    bsz = rows // seq
    t_spec = lambda *feat: pl.BlockSpec(
        (1,) + feat + (tm,), lambda i: (i // n_seq_blocks,) + (0,) * len(feat) + (i % n_seq_blocks,))
    out_shapes = [
        jax.ShapeDtypeStruct((bsz, ATT_W, seq), _BF16),
        jax.ShapeDtypeStruct((rows, ATT_W), _BF16),
        jax.ShapeDtypeStruct((bsz, seq // DSA_KB, ATT_HEADS, V_ROWS, DSA_KB), _BF16),
        jax.ShapeDtypeStruct((bsz, IDX_W, seq), _BF16),
        jax.ShapeDtypeStruct((rows, LANES), _BF16),
        jax.ShapeDtypeStruct((bsz, SUBLANES, seq), _F32),
    ] + [jax.ShapeDtypeStruct((rows, HG_W), _F32)] * 4
    tiles_per_chunk = DSA_KB // tm
    vt_spec = pl.BlockSpec(
        (1, 1, ATT_HEADS, V_ROWS, tm),
        lambda i: (i // n_seq_blocks, (i % n_seq_blocks) // tiles_per_chunk, 0, 0, i % tiles_per_chunk))
    out_specs = [t_spec(ATT_W), row_spec(ATT_W), vt_spec, t_spec(IDX_W),
                 row_spec(LANES), t_spec(SUBLANES)] + [row_spec(HG_W)] * 4
    kern = functools.partial(_proj_kernel, q_scale=ATT_HEAD_DIM ** -0.5 * math.log2(math.e),
                             iw_scale=(IDX_HEADS ** -0.5) * (IDX_DIM ** -0.5))
    return pl.pallas_call(
        kern,
        grid=(rows // tm,),
        in_specs=[row_spec(d), full_spec(wa), full_spec(wh), pos_spec, pos_spec],
        out_specs=out_specs,
        out_shape=out_shapes,
        compiler_params=pltpu.CompilerParams(
            dimension_semantics=("arbitrary",), vmem_limit_bytes=VMEM_LIMIT_BYTES),
    )(x2d, wa, wh, cos_t, sin_t)


def _dsa_kernel(qi_ref, kj_ref, ka_ref, vp_ref, vc_ref,
                qt_ref, iqt_ref, iwt_ref, ik_ref, before_ref, k_ref, vt_prev_ref, vt_ref,
                o_ref,
                hi_scr, lo_scr, fold_scr, cap_scr, bias_scr, m_scr, acc_scr,
                s_even, s_odd, bm_even, bm_odd, *, topk):
    del ka_ref, vp_ref, vc_ref
    p = pl.program_id(1)
    i = qi_ref[p]
    j = kj_ref[p]
    qb, kb = DSA_QB, DSA_KB
    n_kb = (i * qb + qb - 1) // kb + 1
    n_steps = (i * qb + qb - 1) // (2 * kb) + 1
    last_pair = 2 * (n_steps - 1)
    has_second = n_kb == 2 * n_steps
    first_head = lax.broadcasted_iota(_I32, (LANES, qb), 0) < ATT_HEAD_DIM

    def one_head(pair_rows, h):
        keep = first_head if h % 2 == 0 else jnp.logical_not(first_head)
        return jnp.where(keep, pair_rows, jnp.zeros_like(pair_rows))

    @pl.when(j == 0)
    def _select():
        iwt = iwt_ref[0]
        qpos = i * qb + lax.broadcasted_iota(_I32, (kb, qb), 1)
        krow = lax.broadcasted_iota(_I32, (kb, qb), 0)
        iq_heads = [one_head(iqt_ref[0, (h // 2) * LANES:(h // 2 + 1) * LANES, :], h)
                    for h in range(IDX_HEADS)]

        def score_chunk(c, causal):
            row0 = pl.multiple_of(c * kb, kb)
            ikc = ik_ref[0, pl.ds(row0, kb), :]
            score = jnp.zeros((kb, qb), _F32)
            for h in range(IDX_HEADS):
                logits = _dot(ikc, iq_heads[h])
                score = score + iwt[h:h + 1, :] * jnp.maximum(logits, 0.0)
            bits = lax.bitcast_convert_type(score, _I32)
            key = bits ^ (((bits >> 31) & 0x7FFFFFFF) ^ _LOW_SIGN)
            if causal:
                key = jnp.where(row0 + krow <= qpos, key, _INT_MIN ^ _LOW_SIGN)
            hi = (key >> 16).astype(_I16)
            hi_scr[c] = hi
            lo_scr[c] = key.astype(_I16)
            n_tiles = kb // BF16_ROWS
            row0_fold = pl.multiple_of((c % CAP_CHUNKS_PER_BLOCK) * CAP_ROWS, CAP_ROWS)
            for g in range(FOLD_GROUPS):
                tiles = [hi[r * BF16_ROWS:(r + 1) * BF16_ROWS] for r in range(g, n_tiles, FOLD_GROUPS)]
                while len(tiles) > 1:
                    tiles = [jnp.where(tiles[n] > tiles[n + 1], tiles[n], tiles[n + 1])
                             for n in range(0, len(tiles), 2)]
                fold_scr[c // CAP_CHUNKS_PER_BLOCK, pl.ds(row0_fold + g * BF16_ROWS, BF16_ROWS), :] = tiles[0]

        def score_pair(c2, carry):
            score_chunk(2 * c2, False)
            score_chunk(2 * c2 + 1, False)
            return carry

        fold_scr[...] = jnp.full_like(fold_scr, _I16_MIN)
        def score_quad(c4, carry):
            score_pair(2 * c4, carry)
            return score_pair(2 * c4 + 1, carry)

        n_quads = (n_steps - 1) // 2
        lax.fori_loop(0, n_quads, score_quad, 0)

        @pl.when(2 * n_quads < n_steps - 1)
        def _score_leftover():
            score_pair(2 * n_quads, 0)

        score_chunk(last_pair, True)

        @pl.when(has_second)
        def _score_second():
            score_chunk(last_pair + 1, True)

        @pl.when(jnp.logical_not(has_second))
        def _blank_second():
            hi_scr[last_pair + 1] = jnp.full((kb, qb), _I16_MIN, _I16)
            lo_scr[last_pair + 1] = jnp.full((kb, qb), _I16_MIN, _I16)

        tile_rows = lambda r: slice(r * BF16_ROWS, (r + 1) * BF16_ROWS)
        i16_min = jnp.int16(_I16_MIN)

        def count(ref, cand, n_trips, blocks_per_trip, strict=False):
            cand_rows = jnp.broadcast_to(cand.astype(_I16), (BF16_ROWS, qb))
            one, zero = jnp.int16(1), jnp.int16(0)

            def body(t, accs):
                accs = list(accs)
                for u in range(blocks_per_trip):
                    for r in range(kb // BF16_ROWS):
                        tile = ref[blocks_per_trip * t + u, tile_rows(r), :]
                        hit = tile > cand_rows if strict else tile >= cand_rows
                        accs[r % len(accs)] = accs[r % len(accs)] + jnp.where(hit, one, zero)
                return tuple(accs)

            accs = lax.fori_loop(0, n_trips, body, (jnp.zeros((BF16_ROWS, qb), _I16),) * COUNT_ACCS)
            return jnp.sum(_tree_sum(list(accs)).astype(_I32), axis=0, keepdims=True)

        def bisect(ref, n_above, n_trips, blocks_per_trip):
            def body(step, carry):
                val, n_next = carry
                bit = jnp.left_shift(jnp.int32(1), 15 - step)
                cand = jnp.where(step == 0, 0, val | bit)
                cnt = n_above + count(ref, cand, n_trips, blocks_per_trip)
                ok = cnt >= topk
                return jnp.where(ok, cand, val), jnp.where(ok, n_next, cnt)
            return lax.fori_loop(0, 16, body, (jnp.full((1, qb), _I16_MIN, _I32), n_above))

        n_cap = (n_kb + CAP_CHUNKS_PER_BLOCK - 1) // CAP_CHUNKS_PER_BLOCK
        no_keys = jnp.zeros((1, qb), _I32)

        def high_full():
            return bisect(hi_scr, no_keys, n_steps, 2)

        def high_from_fold():
            base, _ = bisect(fold_scr, no_keys, n_cap, 1)
            n_bits = jnp.where(n_kb >= FOLD_LONG_CHUNKS, REFINE_BITS - 1, REFINE_BITS)

            def count_from(cand):
                cnt = count(hi_scr, jnp.minimum(cand, _I16_MAX), n_steps, 2)
                return jnp.where(cand > _I16_MAX, 0, cnt)

            n_beyond = count_from(base + jnp.left_shift(jnp.int32(1), n_bits))

            def refine():
                def body(step, carry):
                    off, n_next = carry
                    cand_off = off | jnp.left_shift(jnp.int32(1), n_bits - 1 - step)
                    cnt = count_from(base + cand_off)
                    ok = cnt >= topk
                    return jnp.where(ok, cand_off, off), jnp.where(ok, n_next, cnt)
                off, n_next = lax.fori_loop(0, n_bits, body, (no_keys, n_beyond))
                return base + off, n_next

            n_outside = jnp.sum(jnp.where(n_beyond >= topk, 1.0, 0.0))
            return lax.cond(n_outside > 0.0, high_full, refine)

        t_hi, n_gt_hi = lax.cond(n_kb >= FOLD_MIN_CHUNKS, high_from_fold, high_full)
        t_hi_tile = jnp.broadcast_to(t_hi.astype(_I16), (BF16_ROWS, qb))

        cap_scr[...] = jnp.full_like(cap_scr, _I16_MIN)

        def capture_block(c, carry):
            tops = [[jnp.full((BF16_ROWS, qb), _I16_MIN, _I16)] * 2 for _ in range(CAP_GROUPS)]
            for r in range(kb // BF16_ROWS):
                x = jnp.where(hi_scr[c, tile_rows(r), :] == t_hi_tile, lo_scr[c, tile_rows(r), :], i16_min)
                lo_scr[c, tile_rows(r), :] = x
                first, second = tops[r % CAP_GROUPS]
                above = x > first
                tops[r % CAP_GROUPS] = [jnp.where(above, x, first),
                                        jnp.where(above, first, jnp.where(x > second, x, second))]
            row0 = pl.multiple_of((c % CAP_CHUNKS_PER_BLOCK) * CAP_ROWS, CAP_ROWS)
            for g in range(CAP_GROUPS):
                for t in range(2):
                    cap_scr[c // CAP_CHUNKS_PER_BLOCK,
                            pl.ds(row0 + (2 * g + t) * BF16_ROWS, BF16_ROWS), :] = tops[g][t]
            return carry

        lax.fori_loop(0, n_kb, capture_block, 0)
        t_lo, _ = bisect(cap_scr, n_gt_hi, n_cap, 1)
        n_gt = n_gt_hi + count(lo_scr, t_lo, n_steps, 2, strict=True)
        n_wrong = jnp.sum(jnp.where(n_gt >= topk, 1.0, 0.0))
        t_lo, n_gt = lax.cond(n_wrong > 0.0,
                              lambda: bisect(lo_scr, n_gt_hi, n_steps, 2),
                              lambda: (t_lo, n_gt))
        t_hi_rows = jnp.broadcast_to(t_hi.astype(_I16), (kb, qb))
        t_lo = jnp.where((t_hi == _I16_MIN) & (t_lo == _I16_MIN), _I16_MIN + 1, t_lo)
        t_lo_rows = jnp.broadcast_to(t_lo.astype(_I16), (kb, qb))
        rem_rows = jnp.broadcast_to((topk - n_gt).astype(_I16), (kb, qb))

        def mask_block(c, seen):
            hi = hi_scr[c]
            lo = lo_scr[c]
            tie = (hi == t_hi_rows) & (lo == t_lo_rows)
            tie_count = jnp.where(tie, jnp.asarray(1, _BF16), jnp.asarray(0, _BF16))

            def tie_at(r):
                hit = (hi[r:r + 1].astype(_I32) == t_hi) & (lo[r:r + 1].astype(_I32) == t_lo)
                return jnp.where(hit, 1.0, 0.0)

            half = kb // 2
            before_top = seen + _dot(before_ref[...], tie_count[:half])
            seen_mid = before_top[half - 1:half] + tie_at(half - 1)
            before_bot = seen_mid + _dot(before_ref[...], tie_count[half:])
            ties_before = jnp.concatenate([before_top, before_bot], axis=0)
            allowed = ties_before.astype(_I32).astype(_I16) < rem_rows
            chosen = (hi > t_hi_rows) | (lo > t_lo_rows) | (tie & allowed)
            bias_scr[c] = jnp.where(chosen, jnp.asarray(0, bias_scr.dtype),
                                    jnp.asarray(_NEG, bias_scr.dtype))
            return before_bot[half - 1:half] + tie_at(kb - 1)

        mask_pair = lambda c2, seen: mask_block(2 * c2 + 1, mask_block(2 * c2, seen))
        seen = lax.fori_loop(0, n_quads,
                             lambda c4, seen: mask_pair(2 * c4 + 1, mask_pair(2 * c4, seen)),
                             jnp.zeros((1, qb), _F32))
        seen = lax.cond(2 * n_quads < n_steps - 1,
                        lambda seen: mask_pair(2 * n_quads, seen), lambda seen: seen, seen)
        seen = mask_block(last_pair, seen)

        @pl.when(has_second)
        def _mask_second():
            mask_block(last_pair + 1, seen)

        m_scr[...] = jnp.full_like(m_scr, _NEG)
        acc_scr[...] = jnp.zeros_like(acc_scr)

    def stage_b(h, s_read, bm_read, vt_read):
        m_old = m_scr[h]
        m_new = jnp.maximum(m_old, bm_read[h])
        pr = jnp.exp2(s_read[h] - m_new).astype(_BF16)
        acc_scr[h] = jnp.exp2(m_old - m_new) * acc_scr[h] + _dot(vt_read[0, 0, h], pr)
        m_scr[h] = m_new

    def phase(k_rows, chunk, s_write, bm_write, s_read, bm_read, vt_read):
        bias = bias_scr[chunk].astype(_F32)

        def stage_a(h):
            pair = slice((h // 2) * LANES, (h // 2 + 1) * LANES)
            s = _dot(k_ref[0, k_rows, pair], one_head(qt_ref[0, pair, :], h)) + bias
            s_write[h] = s
            bm_write[h] = jnp.max(s, axis=0, keepdims=True)

        stage_a(0)
        for h in range(ATT_HEADS):
            if h + 1 < ATT_HEADS:
                stage_a(h + 1)
            if s_read is not None:
                stage_b(h, s_read, bm_read, vt_read)

    @pl.when(j == 0)
    def _fill():
        phase(slice(0, kb), 0, s_even, bm_even, None, None, None)

    @pl.when((j > 0) & (j < n_steps))
    def _first():
        phase(slice(0, kb), 2 * j, s_even, bm_even, s_odd, bm_odd, vt_prev_ref)

    @pl.when(2 * j + 1 < n_kb)
    def _second():
        phase(slice(kb, 2 * kb), 2 * j + 1, s_odd, bm_odd, s_even, bm_even, vt_ref)

    @pl.when((j == n_steps) & has_second)
    def _drain_odd():
        for h in range(ATT_HEADS):
            stage_b(h, s_odd, bm_odd, vt_prev_ref)

    @pl.when((j == n_steps) & jnp.logical_not(has_second))
    def _drain_even():
        for h in range(ATT_HEADS):
            stage_b(h, s_even, bm_even, vt_prev_ref)

    @pl.when(j == n_steps)
    def _finish():
        for h in range(ATT_HEADS):
            a = acc_scr[h]
            o_ref[0, h * ATT_HEAD_DIM:(h + 1) * ATT_HEAD_DIM, :] = (
                a[:ATT_HEAD_DIM] / a[ATT_HEAD_DIM:ATT_HEAD_DIM + 1]).astype(o_ref.dtype)


def _dsa_attention(qt, k, vt, iqt, ik, iwt):
    bsz, seq, _ = k.shape
    qb, kb = DSA_QB, DSA_KB
    topk = min(TOPK_MAX, seq // 4)
    n_qb = seq // qb
    n_steps = lambda i: (i * qb + qb - 1) // (2 * kb) + 1
    pairs = [(i, j) for i in range(n_qb) for j in range(n_steps(i) + 1)]
    as_i32 = lambda vals: jnp.asarray(np.array(vals, np.int32))
    qi = as_i32([i for i, j in pairs])
    kj = as_i32([j for i, j in pairs])
    k_pair = as_i32([min(j, n_steps(i) - 1) for i, j in pairs])
    n_kb = lambda i: (i * qb + qb - 1) // kb + 1
    v_prev = as_i32([n_kb(i) - 1 if j == n_steps(i) else max(2 * j - 1, 0) for i, j in pairs])
    v_this = as_i32([min(2 * j, 2 * n_steps(i) - 1) for i, j in pairs])
    before =jnp.asarray(np.tril(np.ones((kb // 2, kb // 2), np.float32), -1), _BF16)
    q_map = lambda b, p, qi_r, *_: (b, 0, qi_r[p])
    vt_spec = lambda which: pl.BlockSpec(
        (1, 1, ATT_HEADS, V_ROWS, kb), lambda b, p, *refs: (b, refs[which][p], 0, 0, 0))
    grid_spec = pltpu.PrefetchScalarGridSpec(
        num_scalar_prefetch=5,
        grid=(bsz, len(pairs)),
        in_specs=[
            pl.BlockSpec((1, ATT_W, qb), q_map),
            pl.BlockSpec((1, IDX_W, qb), q_map),
            pl.BlockSpec((1, SUBLANES, qb), q_map),
            pl.BlockSpec((1, seq, LANES), lambda b, p, *_: (b, 0, 0)),
            pl.BlockSpec(before.shape, lambda b, p, *_: (0, 0)),
            pl.BlockSpec((1, 2 * kb, ATT_W), lambda b, p, *refs: (b, refs[2][p], 0)),
            vt_spec(3),
            vt_spec(4),
        ],
        out_specs=pl.BlockSpec((1, ATT_W, qb), q_map),
        scratch_shapes=[
            pltpu.VMEM((seq // kb, kb, qb), _I16),
            pltpu.VMEM((seq // kb, kb, qb), _I16),
            pltpu.VMEM((-(-(seq // kb) // CAP_CHUNKS_PER_BLOCK), kb, qb), _I16),
            pltpu.VMEM((-(-(seq // kb) // CAP_CHUNKS_PER_BLOCK), kb, qb), _I16),
            pltpu.VMEM((seq // kb, kb, qb), _BF16),
            pltpu.VMEM((ATT_HEADS, 1, qb), _F32),
            pltpu.VMEM((ATT_HEADS, V_ROWS, qb), _F32),
            pltpu.VMEM((ATT_HEADS, kb, qb), _F32),
            pltpu.VMEM((ATT_HEADS, kb, qb), _F32),
            pltpu.VMEM((ATT_HEADS, 1, qb), _F32),
            pltpu.VMEM((ATT_HEADS, 1, qb), _F32),
        ],
    )
    return pl.pallas_call(
        functools.partial(_dsa_kernel, topk=topk),
        grid_spec=grid_spec,
        out_shape=jax.ShapeDtypeStruct((bsz, ATT_W, seq), _BF16),
        compiler_params=pltpu.CompilerParams(
            dimension_semantics=("arbitrary", "arbitrary"), vmem_limit_bytes=VMEM_LIMIT_BYTES),
    )(qi, kj, k_pair, v_prev, v_this, qt, iqt, iwt, ik, before, k, vt, vt)


def _split3(a):
    hi = a.astype(_BF16)
    r1 = a - hi.astype(_F32)
    mid = r1.astype(_BF16)
    lo = (r1 - mid.astype(_F32)).astype(_BF16)
    return hi, mid, lo


def _hgrn_kernel(lbl_ref, g_ref, hq_ref, hf_ref, hi_ref, hg_ref, o_ref,
                 state_scr, kk_scr, b_scr, o_scr, *, layer):
    rows, ch, sb = HG_ROWS, HG_CHUNK, HG_SUB
    n_sub = ch // sb

    @pl.when(pl.program_id(1) == 0)
    def _reset():
        state_scr[...] = jnp.zeros_like(state_scr)

    lbl = lbl_ref[...]
    e = jnp.exp(lbl - jnp.max(lbl, axis=0, keepdims=True))
    lb = jnp.sum(e[:layer + 1], axis=0, keepdims=True) / jnp.sum(e, axis=0, keepdims=True)

    f = lb + (1.0 - lb) * jax.nn.sigmoid(hf_ref[0])
    kk_scr[...] = 1.0 - f
    logf = jnp.log(f)
    r_i = lax.broadcasted_iota(_I32, (ch, ch), 0)
    c_i = lax.broadcasted_iota(_I32, (ch, ch), 1)
    lower = jnp.where(c_i <= r_i, 1.0, 0.0).astype(_BF16)
    for c in range(rows // ch):
        parts = _split3(logf[c * ch:(c + 1) * ch])
        b_scr[c * ch:(c + 1) * ch, :] = sum(_dot(lower, part) for part in parts)

    t_idx = lax.broadcasted_iota(_I32, (sb, 1), 0)
    row_idx = lax.broadcasted_iota(_I32, (ch, 1), 0)

    def chunk(c, carry, bounded):
        r0 = pl.multiple_of(c * ch, ch)
        cs = pl.ds(r0, ch)
        b = b_scr[cs, :]
        kk = kk_scr[cs, :]
        qv = hq_ref[0, cs, :]
        vv = hi_ref[0, cs, :]
        b_last = b[ch - 1:ch]
        q_in = (qv * jnp.exp(b)).astype(_BF16)
        k_out = kk * jnp.exp(b_last - b)
        vb = vv.astype(_BF16)

        a_off = [[] for _ in range(HG_HEADS)]
        sbs = HG_SUB_BOUNDED if bounded else sb
        for s_i in range(ch // sbs):
            if s_i == 0 and not bounded:
                for h in range(HG_HEADS):
                    a_off[h].append(jnp.zeros((sbs, ch), _F32))
                continue
            ref_b = b[s_i * sbs - 1:s_i * sbs] if s_i else jnp.zeros_like(b_last)
            q_s = (qv[s_i * sbs:(s_i + 1) * sbs] * jnp.exp(b[s_i * sbs:(s_i + 1) * sbs] - ref_b))
            n_rows = (s_i + 1) * sbs if bounded else s_i * sbs
            k_s = jnp.where(row_idx < n_rows,
                            kk * jnp.exp(jnp.where(row_idx < n_rows, ref_b - b, 0.0)), 0.0)
            q_s = q_s.astype(_BF16)
            k_s = k_s.astype(_BF16)
            for h in range(HG_HEADS):
                hs = slice(h * HG_KDIM, (h + 1) * HG_KDIM)
                a_off[h].append(_nt_dot(q_s[:, hs], k_s[:, hs]))

        for h in range(HG_HEADS):
            hs = slice(h * HG_KDIM, (h + 1) * HG_KDIM)
            st = state_scr[h]
            o_h = _nt_dot(q_in[:, hs], st.astype(_BF16))
            a_h = jnp.concatenate(a_off[h], axis=0)
            if bounded:
                a_h = jnp.where(c_i <= r_i, a_h, 0.0)
            o_scr[cs, hs] = o_h + _dot(a_h.astype(_BF16), vb[:, hs])
            state_scr[h] = (st * jnp.exp(b_last[:, hs])
                            + _tn_dot(vb[:, hs], k_out[:, hs].astype(_BF16)))
        if bounded:
            return carry

        for s_i in range(n_sub):
            rs = pl.ds(r0 + s_i * sb, sb)
            q_s = hq_ref[0, rs, :]
            b_s = b_scr[rs, :]
            acc = o_scr[rs, :]
            for t in range(sb):
                one = pl.ds(r0 + s_i * sb + t, 1)
                w = q_s * jnp.exp(jnp.minimum(b_s - b_scr[one, :], 0.0)) * kk_scr[one, :]
                v_row = hi_ref[0, one, :]
                parts = []
                for h in range(HG_HEADS):
                    hs = slice(h * HG_KDIM, (h + 1) * HG_KDIM)
                    a = jnp.sum(w[:, hs], axis=1, keepdims=True)
                    parts.append(jnp.where(t_idx >= t, a, 0.0) * v_row[:, hs])
                acc = acc + jnp.concatenate(parts, axis=1)
            o_scr[rs, :] = acc
        return carry

    run = lambda bounded: lambda: lax.fori_loop(
        0, rows // ch, functools.partial(chunk, bounded=bounded), 0)
    lax.cond(jnp.min(logf) >= -HG_MAX_DECAY / HG_SUB_BOUNDED, run(True), run(False))

    o = o_scr[...]
    gate = hg_ref[0]
    gain = g_ref[...]
    for h in range(HG_HEADS):
        hs = slice(h * HG_KDIM, (h + 1) * HG_KDIM)
        oh = o[:, hs]
        oh = oh * lax.rsqrt(jnp.mean(oh * oh, axis=1, keepdims=True) + RMS_EPS) * gain[:, hs]
        gh = gate[:, hs]
        o_ref[0, :, hs] = (oh * (gh * jax.nn.sigmoid(gh))).astype(o_ref.dtype)


def _hgrn2(hq, hf, hi, hg, lb_logits, norm_g, layer):
    bsz, seq, _ = hq.shape
    rows = HG_ROWS
    blk = pl.BlockSpec((1, rows, HG_W), lambda b, t: (b, t, 0))
    lbl = lb_logits.reshape(lb_logits.shape[0], HG_W)
    gain = norm_g.reshape(1, HG_W)
    return pl.pallas_call(
        functools.partial(_hgrn_kernel, layer=layer),
        grid=(bsz, seq // rows),
        in_specs=[pl.BlockSpec(lbl.shape, lambda b, t: (0, 0)),
                  pl.BlockSpec(gain.shape, lambda b, t: (0, 0)),
                  blk, blk, blk, blk],
        out_specs=blk,
        out_shape=jax.ShapeDtypeStruct((bsz, seq, HG_W), _BF16),
        scratch_shapes=[
            pltpu.VMEM((HG_HEADS, HG_KDIM, HG_KDIM), _F32),
            pltpu.VMEM((rows, HG_W), _F32),
            pltpu.VMEM((rows, HG_W), _F32),
            pltpu.VMEM((rows, HG_W), _F32),
        ],
        compiler_params=pltpu.CompilerParams(
            dimension_semantics=("arbitrary", "arbitrary"), vmem_limit_bytes=VMEM_LIMIT_BYTES),
    )(lbl, gain, hq, hf, hi, hg)


def _layer_norm(y, g, b):
    mu = jnp.mean(y, axis=1, keepdims=True)
    yc = y - mu
    var = jnp.mean(yc * yc, axis=1, keepdims=True)
    return yc * lax.rsqrt(var + LN_EPS) * g + b


def _ffn_kernel(x_ref, att_ref, hgo_ref, wo_ref, g1_ref, b1_ref, wu_ref, wd_ref,
                g2_ref, b2_ref, o_ref, *, alpha):
    mix = (_tn_dot(att_ref[0], wo_ref[:ATT_W, :]) + _dot(hgo_ref[...], wo_ref[ATT_W:, :]))
    y1 = _layer_norm(alpha * x_ref[...] + mix, g1_ref[...], b1_ref[...])
    y1b = y1.astype(_BF16)
    h = jnp.zeros_like(y1)
    for c in range(wu_ref.shape[1] // FFN_COLS):
        cs = slice(c * FFN_COLS, (c + 1) * FFN_COLS)
        u = jnp.maximum(_dot(y1b, wu_ref[:, cs]), 0.0)
        h = h + _dot((u * u).astype(_BF16), wd_ref[cs, :])
    o_ref[...] = _layer_norm(alpha * y1 + h, g2_ref[...], b2_ref[...])


def _out_ffn(x2d, att_t, hgo2d, w_o, g1, b1, w_up, w_down, g2, b2, alpha):
    rows, d = x2d.shape
    seq = att_t.shape[2]
    tm = FFN_ROWS
    n_seq_blocks = seq // tm
    row_spec = lambda w: pl.BlockSpec((tm, w), lambda i: (i, 0))
    full_spec = lambda a: pl.BlockSpec(a.shape, lambda i: (0, 0), pipeline_mode=pl.Buffered(1))
    att_spec = pl.BlockSpec((1, ATT_W, tm), lambda i: (i // n_seq_blocks, 0, i % n_seq_blocks))
    vec = lambda a: a.reshape(1, d).astype(_F32)
    args = (x2d, att_t, hgo2d, w_o.astype(_BF16), vec(g1), vec(b1),
            w_up.astype(_BF16), w_down.astype(_BF16), vec(g2), vec(b2))
    in_specs = [row_spec(d), att_spec, row_spec(HG_W)] + [full_spec(a) for a in args[3:]]
    return pl.pallas_call(
        functools.partial(_ffn_kernel, alpha=alpha),
        grid=(rows // tm,),
        in_specs=in_specs,
        out_specs=row_spec(d),
        out_shape=jax.ShapeDtypeStruct((rows, d), _F32),
        compiler_params=pltpu.CompilerParams(
            dimension_semantics=("arbitrary",), vmem_limit_bytes=VMEM_LIMIT_BYTES),
    )(*args)


def kernel(x, w_in, w_o, lb_logits, hg_norm_g, ln1_g, ln1_b, w_up, w_down, ln2_g, ln2_b):
    bsz, seq, d = x.shape
    depth = w_in.shape[0]
    alpha = (2.0 * depth) ** 0.25
    x2d = x.reshape(bsz * seq, d)
    for l in range(depth):
        qt, k, vt, iqt, ik, iwt, hq, hf, hi, hg = _project(x2d, w_in[l], seq)
        r3 = lambda a: a.reshape(bsz, seq, a.shape[-1])
        att_t = _dsa_attention(qt, r3(k), vt, iqt, r3(ik), iwt)
        hgo = _hgrn2(r3(hq), r3(hf), r3(hi), r3(hg), lb_logits, hg_norm_g[l], l)
        x2d = _out_ffn(x2d, att_t, hgo.reshape(bsz * seq, HG_W),
                       w_o[l], ln1_g[l], ln1_b[l], w_up[l], w_down[l], ln2_g[l], ln2_b[l], alpha)
    return x2d.reshape(bsz, seq, d)
```
